```python
import jax, jax.numpy as jnp
from jax import lax
import numpy as np

D_MODEL = 1024
BATCH = 16
SEQ = 256
DEPTH = 4
DEC_BATCH = 2
DEC_SEQ = 4096
PAST_LEN = 512

GRID_W = 64
MLA_HEADS = 4
QK_NOPE_DIM = 128
QK_ROPE_DIM = 64
V_HEAD_DIM = 128
Q_LORA_RANK = 384
KV_LORA_RANK = 256
ROPE_THETA = 10000.0
AXIS_ROPE_DIM = QK_ROPE_DIM // 2
ROPE_HALF = QK_ROPE_DIM // 2
ATTN_SCALE = (QK_NOPE_DIM + QK_ROPE_DIM) ** -0.5
Q_BLOCK = 128
MLA_WIDTH = MLA_HEADS * V_HEAD_DIM
POOL_WINDOWS = (2, 4, 8, 16)
N_POOL_GROUPS = 4
POOL_GROUP = 64
POOL_WIDTH = N_POOL_GROUPS * POOL_GROUP
CONV_WIDTH = 256
CONV_K = 3
MIX_WIDTH = MLA_WIDTH + POOL_WIDTH + CONV_WIDTH
IN_SIZES = (Q_LORA_RANK, KV_LORA_RANK, QK_ROPE_DIM, POOL_WIDTH, CONV_WIDTH, CONV_WIDTH, CONV_WIDTH)
D_IN = Q_LORA_RANK + KV_LORA_RANK + QK_ROPE_DIM + POOL_WIDTH + 3 * CONV_WIDTH
N_EXPERTS = 16
EXPERT_DIM = 512
CAPACITY_FACTOR = 2
ALPHA = (2 * DEPTH) ** 0.25
BETA = (8 * DEPTH) ** -0.25
RMS_EPS = 1e-6
LN_EPS = 1e-5

kernel_name = 'hybrid_mla_pool_conv_expert_choice_dit_step'


def rms_norm(x, g):
    xf = x.astype(jnp.float32)
    y = xf * lax.rsqrt(jnp.mean(xf * xf, axis=-1, keepdims=True) + RMS_EPS)
    return (y * g.astype(jnp.float32)).astype(x.dtype)


def layer_norm(x, g, b):
    xf = x.astype(jnp.float32)
    mu = jnp.mean(xf, axis=-1, keepdims=True)
    var = jnp.mean(jnp.square(xf - mu), axis=-1, keepdims=True)
    y = (xf - mu) * lax.rsqrt(var + LN_EPS)
    return (y * g.astype(jnp.float32) + b.astype(jnp.float32)).astype(x.dtype)


def split_cols(x, sizes):
    out, off = [], 0
    for s in sizes:
        out.append(x[..., off:off + s])
        off += s
    return out


def axial_rope_tables(n_tokens):
    rows_n = n_tokens // GRID_W
    r, cl = jnp.meshgrid(jnp.arange(rows_n, dtype=jnp.float32),
                         jnp.arange(GRID_W, dtype=jnp.float32), indexing='ij')
    inv = ROPE_THETA ** (-jnp.arange(0, AXIS_ROPE_DIM, 2, dtype=jnp.float32) / AXIS_ROPE_DIM)
    ang = jnp.concatenate([r.reshape(-1)[:, None] * inv, cl.reshape(-1)[:, None] * inv], axis=-1)
    return jnp.cos(ang), jnp.sin(ang)


def apply_rope(x, cos, sin):
    xf = x.astype(jnp.float32)
    x1, x2 = xf[..., :ROPE_HALF], xf[..., ROPE_HALF:]
    return jnp.concatenate([x1 * cos - x2 * sin, x2 * cos + x1 * sin], axis=-1).astype(x.dtype)


def blocked_attention(q_nope, q_rope, k_nope, k_rope, v):
    B, T, H, _ = q_nope.shape
    nb = T // Q_BLOCK

    def one_block(args):
        qn, qr = args
        s = (jnp.einsum('bqhd,bkhd->bhqk', qn, k_nope, preferred_element_type=jnp.float32)
             + jnp.einsum('bqhr,bkr->bhqk', qr, k_rope, preferred_element_type=jnp.float32))
        p = jax.nn.softmax(s * ATTN_SCALE, axis=-1).astype(v.dtype)
        return jnp.einsum('bhqk,bkhd->bqhd', p, v)

    qn_b = q_nope.reshape(B, nb, Q_BLOCK, H, QK_NOPE_DIM).swapaxes(0, 1)
    qr_b = q_rope.reshape(B, nb, Q_BLOCK, H, QK_ROPE_DIM).swapaxes(0, 1)
    o = lax.map(one_block, (qn_b, qr_b))
    return o.swapaxes(0, 1).reshape(B, T, H * V_HEAD_DIM)


def mla_mixer(q_c, kv_c, k_r, q_norm, w_uq, kv_norm, w_uk, w_uv, rope, ctx):
    B, T, _ = q_c.shape
    c_kv = rms_norm(kv_c, kv_norm)
    q = (rms_norm(q_c, q_norm) @ w_uq).reshape(B, T, MLA_HEADS, QK_NOPE_DIM + QK_ROPE_DIM)
    q_nope, q_rope = q[..., :QK_NOPE_DIM], q[..., QK_NOPE_DIM:]
    k_rope = k_r
    if rope is not None:
        cos, sin = rope
        q_rope = apply_rope(q_rope, cos[:, None, :], sin[:, None, :])
        k_rope = apply_rope(k_r, cos, sin)
    kv_all, kr_all = c_kv, k_rope
    if ctx is not None:
        ckv_ctx, kr_ctx = ctx
        kv_all = jnp.concatenate([ckv_ctx.astype(c_kv.dtype), c_kv], axis=1)
        kr_all = jnp.concatenate([kr_ctx.astype(k_rope.dtype), k_rope], axis=1)
    S = kv_all.shape[1]
    k_nope = (kv_all @ w_uk).reshape(B, S, MLA_HEADS, QK_NOPE_DIM)
    v = (kv_all @ w_uv).reshape(B, S, MLA_HEADS, V_HEAD_DIM)
    return blocked_attention(q_nope, q_rope, k_nope, kr_all, v), c_kv


def pool_mixer(p, pool_w, pool_scale):
    B, T, _ = p.shape
    pg = p.reshape(B, T, N_POOL_GROUPS, POOL_GROUP)
    cs = jnp.concatenate([jnp.zeros((B, 1, N_POOL_GROUPS, POOL_GROUP), jnp.float32),
                          jnp.cumsum(pg.astype(jnp.float32), axis=1)], axis=1)
    t = jnp.arange(T)
    means = []
    for g, w in enumerate(POOL_WINDOWS):
        lo = jnp.clip(t - w // 2, 0, T)
        hi = jnp.clip(t + w - w // 2, 0, T)
        cnt = (hi - lo).astype(jnp.float32)[None, :, None]
        means.append((cs[:, hi, g] - cs[:, lo, g]) / cnt)
    pooled = jnp.stack(means, axis=2).astype(p.dtype) - pg
    mixed = jnp.einsum('btgc,gcd->btgd', pooled, pool_w).reshape(B, T, POOL_WIDTH)
    return mixed * pool_scale


def short_conv(y, conv_w):
    return lax.conv_general_dilated(y, conv_w[:, None, :].astype(y.dtype), window_strides=(1,),
                                    padding='SAME', dimension_numbers=('NWC', 'WIO', 'NWC'),
                                    feature_group_count=y.shape[-1])


def expert_choice_ffn(u, w_router, w_gate, w_up, w_down):
    B, T, D = u.shape
    cap = CAPACITY_FACTOR * T // N_EXPERTS
    aff = jax.nn.softmax((u @ w_router).astype(jnp.float32), axis=-1)
    g, idx = lax.top_k(aff.swapaxes(1, 2), cap)
    xs = jax.vmap(lambda ub, ib: ub[ib])(u, idx)
    hdn = (jax.nn.silu(jnp.einsum('becd,edf->becf', xs, w_gate))
           * jnp.einsum('becd,edf->becf', xs, w_up))
    ye = jnp.einsum('becf,efd->becd', hdn, w_down) * g[..., None].astype(u.dtype)

    def scatter_one(ib, yb):
        return jnp.zeros((T, D), yb.dtype).at[ib.reshape(-1)].add(yb.reshape(-1, D))

    return jax.vmap(scatter_one)(idx, ye)


def trunk_layer(x, mod, w_in, q_norm, w_uq, kv_norm, w_uk, w_uv, pool_w, pool_scale, conv_w,
                w_out, ln1_g, ln1_b, ln2_g, ln2_b, w_router, w_gate, w_up, w_down, rope, ctx):
    shift1, scale1, gate1, shift2, scale2, gate2 = jnp.split(mod.astype(x.dtype), 6, axis=-1)
    u = x * (1 + scale1) + shift1
    q_c, kv_c, k_r, p_in, g_b, g_c, h_in = split_cols(u @ w_in, IN_SIZES)
    a_out, c_kv = mla_mixer(q_c, kv_c, k_r, q_norm, w_uq, kv_norm, w_uk, w_uv, rope, ctx)
    b_out = pool_mixer(p_in, pool_w, pool_scale)
    c_out = g_b * short_conv(g_c * h_in, conv_w)
    mix = jnp.concatenate([a_out, b_out, c_out], axis=-1) @ w_out
    x = layer_norm(ALPHA * x + gate1 * mix, ln1_g, ln1_b)
    u = x * (1 + scale2) + shift2
    x = layer_norm(ALPHA * x + gate2 * expert_choice_ffn(u, w_router, w_gate, w_up, w_down), ln2_g, ln2_b)
    return x, c_kv, k_r


def setup_inputs(seed: int = 0) -> dict:
    key = jax.random.key(seed)
    ks = jax.random.split(key, 32)

    def nrm(k, shape, scale):
        return jax.random.normal(k, shape, jnp.float32) * scale

    return {
        'x_prompt': nrm(ks[0], (BATCH, SEQ, D_MODEL), 1.0),
        'x_sample': nrm(ks[1], (DEC_BATCH, DEC_SEQ, D_MODEL), 1.0),
        'cache_ckv': nrm(ks[2], (DEC_BATCH, DEPTH, PAST_LEN, KV_LORA_RANK), 1.0),
        'cache_krope': nrm(ks[3], (DEC_BATCH, DEPTH, PAST_LEN, QK_ROPE_DIM), 1.0),
        'c': nrm(ks[4], (DEC_BATCH, D_MODEL), 1.0),
        'c_ctx': nrm(ks[5], (D_MODEL,), 1.0),
        'w_in': nrm(ks[6], (DEPTH, D_MODEL, D_IN), D_MODEL ** -0.5),
        'q_norm': 1.0 + nrm(ks[7], (DEPTH, Q_LORA_RANK), 0.02),
        'w_uq': nrm(ks[8], (DEPTH, Q_LORA_RANK, MLA_HEADS * (QK_NOPE_DIM + QK_ROPE_DIM)), Q_LORA_RANK ** -0.5),
        'kv_norm': 1.0 + nrm(ks[9], (DEPTH, KV_LORA_RANK), 0.02),
        'w_uk': nrm(ks[10], (DEPTH, KV_LORA_RANK, MLA_HEADS * QK_NOPE_DIM), KV_LORA_RANK ** -0.5),
        'w_uv': nrm(ks[11], (DEPTH, KV_LORA_RANK, MLA_HEADS * V_HEAD_DIM), BETA * KV_LORA_RANK ** -0.5),
        'pool_w': nrm(ks[12], (DEPTH, N_POOL_GROUPS, POOL_GROUP, POOL_GROUP), POOL_GROUP ** -0.5),
        'pool_scale': 1.0 + nrm(ks[13], (DEPTH, POOL_WIDTH), 0.1),
        'conv_w': nrm(ks[14], (DEPTH, CONV_K, CONV_WIDTH), CONV_K ** -0.5),
        'w_out': nrm(ks[15], (DEPTH, MIX_WIDTH, D_MODEL), BETA * MIX_WIDTH ** -0.5),
        'w_ada': nrm(ks[16], (DEPTH, D_MODEL, 6 * D_MODEL), 0.5 * D_MODEL ** -0.5),
        'b_ada': nrm(ks[17], (DEPTH, 6 * D_MODEL), 0.02),
        'ln1_g': 1.0 + nrm(ks[18], (DEPTH, D_MODEL), 0.02),
        'ln1_b': nrm(ks[19], (DEPTH, D_MODEL), 0.02),
        'ln2_g': 1.0 + nrm(ks[20], (DEPTH, D_MODEL), 0.02),
        'ln2_b': nrm(ks[21], (DEPTH, D_MODEL), 0.02),
        'w_router': nrm(ks[22], (DEPTH, D_MODEL, N_EXPERTS), D_MODEL ** -0.5),
        'w_gate': nrm(ks[23], (DEPTH, N_EXPERTS, D_MODEL, EXPERT_DIM), D_MODEL ** -0.5),
        'w_up': nrm(ks[24], (DEPTH, N_EXPERTS, D_MODEL, EXPERT_DIM), D_MODEL ** -0.5),
        'w_down': nrm(ks[25], (DEPTH, N_EXPERTS, EXPERT_DIM, D_MODEL), BETA * EXPERT_DIM ** -0.5),
    }


def reference(x_prompt, x_sample, cache_ckv, cache_krope, c, c_ctx, w_in, q_norm, w_uq, kv_norm,
              w_uk, w_uv, pool_w, pool_scale, conv_w, w_out, w_ada, b_ada, ln1_g, ln1_b, ln2_g,
              ln2_b, w_router, w_gate, w_up, w_down):
    def layer_params(l):
        return (w_in[l], q_norm[l], w_uq[l], kv_norm[l], w_uk[l], w_uv[l], pool_w[l], pool_scale[l],
                conv_w[l], w_out[l], ln1_g[l], ln1_b[l], ln2_g[l], ln2_b[l], w_router[l], w_gate[l],
                w_up[l], w_down[l])

    x = x_prompt
    ckv_list, kr_list = [], []
    for l in range(DEPTH):
        mod = (jax.nn.silu(c_ctx) @ w_ada[l] + b_ada[l])[None, None, :]
        x, ckv, kr = trunk_layer(x, mod, *layer_params(l), rope=None, ctx=None)
        ckv_list.append(ckv)
        kr_list.append(kr)
    y_prompt = x
    new_ckv = jnp.stack(ckv_list, axis=1)
    new_krope = jnp.stack(kr_list, axis=1)

    cos, sin = axial_rope_tables(x_sample.shape[1])
    x = x_sample
    for l in range(DEPTH):
        mod = (jax.nn.silu(c) @ w_ada[l] + b_ada[l])[:, None, :]
        x, _, _ = trunk_layer(x, mod, *layer_params(l), rope=(cos, sin),
                              ctx=(cache_ckv[:, l], cache_krope[:, l]))
    y_sample = x
    return (y_prompt, y_sample, new_ckv, new_krope)
```

```python
import functools

import jax
import jax.numpy as jnp
from jax import lax
from jax.experimental import pallas as pl
from jax.experimental.pallas import tpu as pltpu

F32 = jnp.float32
BF16 = jnp.bfloat16
I32 = jnp.int32

D_MODEL = 1024
CTX_B, CTX_T = 16, 256
LAT_B, LAT_T = 2, 4096
DEPTH = 4
PAST_LEN = 512
GRID_W = 64
HEADS = 4
NOPE, ROPE, VDIM = 128, 64, 128
Q_RANK, KV_RANK = 384, 256
POOL_WINDOWS = (2, 4, 8, 16)
POOL_GROUP = 64
MIXW = 256
N_EXPERTS = 16
EXPERT_DIM = 512
ROPE_THETA = 10000.0
ATTN_SCALE = (NOPE + ROPE) ** -0.5
LOG2E = 1.4426950408889634
Q_SCALE = ATTN_SCALE * LOG2E
ALPHA = (2 * DEPTH) ** 0.25
RMS_EPS = 1e-6
LN_EPS = 1e-5

N_CTX = CTX_B * CTX_T
N_LAT = LAT_B * LAT_T
N_TOK = N_CTX + N_LAT
CTX_CAP = 2 * CTX_T // N_EXPERTS
LAT_CAP = 2 * LAT_T // N_EXPERTS

LANES = 128
SUBLANES = 8
CHUNKS = D_MODEL // LANES
HEAD_BLOCK = 2 * LANES
VMEM_LIMIT = 56 * 1024 * 1024

TM = 512
SEQ_TILE = 256
HALO = 8
TQ = 256
KV_CHUNK = 512
PASS_TOK = 4096
N_PASS = N_TOK // PASS_TOK
ROWS = 512
TILE_PITCH = ROWS + 8
SCATTER_UNROLL = 8


def _params(n_axes, arbitrary=False):
    sem = ("arbitrary" if arbitrary else "parallel",) * n_axes
    return pltpu.CompilerParams(dimension_semantics=sem, vmem_limit_bytes=VMEM_LIMIT)


def _mod_row(i, tile):
    start = i * tile
    return jnp.where(start < N_CTX, 0, 1 + (start - N_CTX) // LAT_T)


def _full(shape):
    nd = len(shape)
    return pl.BlockSpec(shape, lambda *_: (0,) * nd)


def _pair_specs(tile, width, lat_row0):
    nct = N_CTX // tile
    off = lat_row0 // tile
    return [pl.BlockSpec((tile, width), lambda i: (jnp.minimum(i, nct - 1), 0)),
            pl.BlockSpec((tile, width), lambda i: (jnp.maximum(i - nct, 0) + off, 0))]


def _pair_load(tile, ctx_ref, lat_ref):
    return jnp.where(pl.program_id(0) < N_CTX // tile, ctx_ref[...], lat_ref[...])


ADA_TN = 1536


def _ada_body(cond_ref, w_ref, b_ref, o_ref):
    c = cond_ref[...]
    a = c * jax.nn.sigmoid(c)
    a_hi = a.astype(BF16)
    a_lo = (a - a_hi.astype(F32)).astype(BF16)
    w = w_ref[0]
    w_hi = w.astype(BF16)
    w_lo = (w - w_hi.astype(F32)).astype(BF16)
    acc = jnp.dot(a_hi, w_hi, preferred_element_type=F32)
    acc += jnp.dot(a_lo, w_hi, preferred_element_type=F32)
    acc += jnp.dot(a_hi, w_lo, preferred_element_type=F32)
    o_ref[0] = acc + b_ref[0]


def _ada(cond, w_ada, b_ada):
    n = 6 * D_MODEL
    return pl.pallas_call(
        _ada_body,
        grid=(DEPTH, n // ADA_TN),
        in_specs=[
            _full((SUBLANES, D_MODEL)),
            pl.BlockSpec((1, D_MODEL, ADA_TN), lambda l, j: (l, 0, j)),
            pl.BlockSpec((1, 1, ADA_TN), lambda l, j: (l, 0, j)),
        ],
        out_specs=pl.BlockSpec((1, SUBLANES, ADA_TN), lambda l, j: (l, 0, j)),
        out_shape=jax.ShapeDtypeStruct((DEPTH, SUBLANES, n), F32),
        compiler_params=_params(2),
        name="ada_mod",
    )(cond, w_ada, b_ada.reshape(DEPTH, 1, n))


IN_EXT = Q_RANK + KV_RANK + 4 * MIXW + 2 * LANES
UQ_EXT = HEADS * NOPE + 2 * HEADS * LANES


def _rms(x, g):
    return x * lax.rsqrt(jnp.mean(x * x, axis=-1, keepdims=True) + RMS_EPS) * g


def _inproj_body(xc_ref, xl_ref, sh_ref, sc_ref, win_ref, qn_ref, kvn_ref, wuq_ref, wuk_ref, wuv_ref,
                 cos_ref, sin_ref, ckv_ref, kr_ref, q_ref, k_ref, v_ref, mix_ref):
    u = _pair_load(TM, xc_ref, xl_ref) * (1.0 + sc_ref[0]) + sh_ref[0]
    h = jnp.dot(u.astype(BF16), win_ref[...], preferred_element_type=F32)
    o = 0
    q_c = h[:, o:o + Q_RANK]; o += Q_RANK
    kv_c = h[:, o:o + KV_RANK]; o += KV_RANK
    mix_ref[...] = h[:, o:o + 4 * MIXW]; o += 4 * MIXW
    kr_a = h[:, o:o + LANES]; o += LANES
    kr_b = h[:, o:o + LANES]
    kr_ref[...] = kr_a
    cos_a = cos_ref[...]
    sin_a = sin_ref[...]

    ckv = _rms(kv_c, kvn_ref[...])
    ckv_ref[...] = ckv
    ckv_b = ckv.astype(BF16)
    k_nope = jnp.dot(ckv_b, wuk_ref[...], preferred_element_type=F32)
    v_ref[...] = jnp.dot(ckv_b, wuv_ref[...], preferred_element_type=F32).astype(BF16)
    k_rope = kr_a * cos_a + kr_b * sin_a
    parts = []
    for hd in range(HEADS):
        parts += [k_nope[:, hd * NOPE:(hd + 1) * NOPE], k_rope]
    k_ref[...] = jnp.concatenate(parts, axis=-1).astype(BF16)

    qn = _rms(q_c, qn_ref[...]).astype(BF16)
    qq = jnp.dot(qn, wuq_ref[...], preferred_element_type=F32)
    ra = HEADS * NOPE
    rb = ra + HEADS * LANES
    parts = []
    for hd in range(HEADS):
        q_rope = (qq[:, ra + hd * LANES:ra + (hd + 1) * LANES] * cos_a
                  + qq[:, rb + hd * LANES:rb + (hd + 1) * LANES] * sin_a)
        parts += [qq[:, hd * NOPE:(hd + 1) * NOPE] * Q_SCALE, q_rope * Q_SCALE]
    q_ref[...] = jnp.concatenate(parts, axis=-1).astype(BF16)


def _inproj(xc, xl, lat_row0, mod3, w_in_ext, q_norm, kv_norm, w_uq_ext, w_uk, w_uv, cos_a, sin_a):
    row = lambda i: (i, 0)
    modspec = lambda k: pl.BlockSpec((1, 1, D_MODEL), lambda i: (_mod_row(i, TM), 0, k))
    outs = (
        jax.ShapeDtypeStruct((N_TOK, KV_RANK), F32),
        jax.ShapeDtypeStruct((N_TOK, LANES), F32),
        jax.ShapeDtypeStruct((N_TOK, HEADS * HEAD_BLOCK), BF16),
        jax.ShapeDtypeStruct((N_TOK, HEADS * HEAD_BLOCK), BF16),
        jax.ShapeDtypeStruct((N_TOK, HEADS * VDIM), BF16),
        jax.ShapeDtypeStruct((N_TOK, 4 * MIXW), F32),
    )
    return pl.pallas_call(
        _inproj_body,
        grid=(N_TOK // TM,),
        in_specs=_pair_specs(TM, D_MODEL, lat_row0) + [
            modspec(0), modspec(1),
            _full((D_MODEL, IN_EXT)),
            _full((1, Q_RANK)), _full((1, KV_RANK)),
            _full((Q_RANK, UQ_EXT)),
            _full((KV_RANK, HEADS * NOPE)), _full((KV_RANK, HEADS * VDIM)),
            pl.BlockSpec((TM, LANES), row), pl.BlockSpec((TM, LANES), row),
        ],
        out_specs=tuple(pl.BlockSpec((TM, s.shape[1]), row) for s in outs),
        out_shape=outs,
        compiler_params=_params(1),
        name="inproj",
    )(xc, xl, mod3, mod3, w_in_ext, q_norm, kv_norm, w_uq_ext, w_uk, w_uv, cos_a, sin_a)


def _cachekv_body(ckv_ref, kr_ref, wuk_ref, wuv_ref, k_ref, v_ref):
    c = ckv_ref[0, 0].astype(BF16)
    k_nope = jnp.dot(c, wuk_ref[0], preferred_element_type=F32)
    v_ref[0, 0] = jnp.dot(c, wuv_ref[0], preferred_element_type=F32).astype(BF16)
    kr = kr_ref[0, 0]
    parts = []
    for hd in range(HEADS):
        parts += [k_nope[:, hd * NOPE:(hd + 1) * NOPE], kr]
    k_ref[0, 0] = jnp.concatenate(parts, axis=-1).astype(BF16)


def _cachekv(cache_ckv, kr_pad, w_uk, w_uv):
    bl = lambda b, l: (b, l, 0, 0)
    wl = lambda b, l: (l, 0, 0)
    return pl.pallas_call(
        _cachekv_body,
        grid=(LAT_B, DEPTH),
        in_specs=[
            pl.BlockSpec((1, 1, PAST_LEN, KV_RANK), bl),
            pl.BlockSpec((1, 1, PAST_LEN, LANES), bl),
            pl.BlockSpec((1, KV_RANK, HEADS * NOPE), wl),
            pl.BlockSpec((1, KV_RANK, HEADS * VDIM), wl),
        ],
        out_specs=(pl.BlockSpec((1, 1, PAST_LEN, HEADS * HEAD_BLOCK), bl),
                   pl.BlockSpec((1, 1, PAST_LEN, HEADS * VDIM), bl)),
        out_shape=(jax.ShapeDtypeStruct((LAT_B, DEPTH, PAST_LEN, HEADS * HEAD_BLOCK), BF16),
                   jax.ShapeDtypeStruct((LAT_B, DEPTH, PAST_LEN, HEADS * VDIM), BF16)),
        compiler_params=_params(2),
        name="cache_kv",
    )(cache_ckv, kr_pad, w_uk, w_uv)


EXT = SEQ_TILE + 2 * HALO


def _shift_up(x, k):
    return pltpu.roll(x, x.shape[0] - k, axis=0)


def _shift_down(x, k):
    return pltpu.roll(x, k, axis=0)


def _seqmix_body(main_ref, prev_ref, next_ref, pw_ref, ps_ref, cw_ref, o_ref):
    j = pl.program_id(0)
    n_ctx_tiles = N_CTX // SEQ_TILE
    tiles_per_lat = LAT_T // SEQ_TILE
    is_ctx = j < n_ctx_tiles
    jj = jnp.where(is_ctx, 0, (j - n_ctx_tiles) % tiles_per_lat)
    first = jj == 0
    last = jnp.where(is_ctx, True, jj == tiles_per_lat - 1)
    t_seq = jnp.where(is_ctx, CTX_T, LAT_T)

    main = main_ref[...]
    prev = jnp.where(first, 0.0, prev_ref[...])
    nxt = jnp.where(last, 0.0, next_ref[...])
    ext = jnp.concatenate([prev, main, nxt], axis=0)
    p = ext[:, 0:MIXW]
    g_b = main[:, MIXW:2 * MIXW]
    g_c = ext[:, 2 * MIXW:3 * MIXW]
    h_in = ext[:, 3 * MIXW:4 * MIXW]

    sums = {}
    b = p
    for w in POOL_WINDOWS:
        b = b + _shift_up(b, w // 2)
        sums[w] = _shift_down(b, w // 2)[HALO:HALO + SEQ_TILE]
    tpos = jj * SEQ_TILE + lax.broadcasted_iota(I32, (SEQ_TILE, 1), 0)
    lane = lax.broadcasted_iota(I32, (1, MIXW), 1)
    num = None
    den = None
    for g, w in enumerate(POOL_WINDOWS):
        lo = jnp.maximum(tpos - w // 2, 0)
        hi = jnp.minimum(tpos + w - w // 2, t_seq)
        cnt = (hi - lo).astype(F32)
        if num is None:
            num, den = sums[w], jnp.broadcast_to(cnt, (SEQ_TILE, MIXW))
        else:
            sel = lane >= g * POOL_GROUP
            num = jnp.where(sel, sums[w], num)
            den = jnp.where(sel, cnt, den)
    pooled = num / den - p[HALO:HALO + SEQ_TILE]
    b_out = jnp.dot(pooled.astype(BF16), pw_ref[...], preferred_element_type=F32) * ps_ref[...]

    y = g_c * h_in
    cw = cw_ref[...]
    conv = (_shift_down(y, 1) * cw[0:1] + y * cw[1:2] + _shift_up(y, 1) * cw[2:3])
    c_out = g_b * conv[HALO:HALO + SEQ_TILE]
    o_ref[...] = jnp.concatenate([b_out, c_out], axis=-1).astype(BF16)


def _seqmix(mix, pool_bd, pool_scale, conv_w):
    per = SEQ_TILE // HALO
    nblk = N_TOK // HALO
    return pl.pallas_call(
        _seqmix_body,
        grid=(N_TOK // SEQ_TILE,),
        in_specs=[
            pl.BlockSpec((SEQ_TILE, 4 * MIXW), lambda j: (j, 0)),
            pl.BlockSpec((HALO, 4 * MIXW), lambda j: (jnp.maximum(j * per - 1, 0), 0)),
            pl.BlockSpec((HALO, 4 * MIXW), lambda j: (jnp.minimum((j + 1) * per, nblk - 1), 0)),
            _full((MIXW, MIXW)), _full((1, MIXW)), _full((3, MIXW)),
        ],
        out_specs=pl.BlockSpec((SEQ_TILE, 2 * MIXW), lambda j: (j, 0)),
        out_shape=jax.ShapeDtypeStruct((N_TOK, 2 * MIXW), BF16),
        compiler_params=_params(1),
        name="seqmix",
    )(mix, mix, mix, pool_bd, pool_scale, conv_w)


_NT = (((1,), (1,)), ((), ()))


def _v_ext(v):
    return jnp.concatenate([v, jnp.ones(v.shape, v.dtype)], axis=-1)


def _attn_ctx_body(q_ref, k_ref, v_ref, o_ref):
    outs = []
    for hd in range(HEADS):
        qh = q_ref[:, hd * HEAD_BLOCK:(hd + 1) * HEAD_BLOCK]
        kh = k_ref[:, hd * HEAD_BLOCK:(hd + 1) * HEAD_BLOCK]
        vh = v_ref[:, hd * VDIM:(hd + 1) * VDIM]
        s = lax.dot_general(qh, kh, _NT, preferred_element_type=F32)
        m = jnp.max(s, axis=-1, keepdims=True)
        p = jnp.exp2(s - m)
        acc = jnp.dot(p.astype(BF16), _v_ext(vh), preferred_element_type=F32)
        outs.append(acc[:, :VDIM] / acc[:, VDIM:])
    o_ref[...] = jnp.concatenate(outs, axis=-1).astype(BF16)


def _attn_ctx(q, k, v):
    row = lambda b: (b, 0)
    return pl.pallas_call(
        _attn_ctx_body,
        grid=(CTX_B,),
        in_specs=[pl.BlockSpec((CTX_T, HEADS * HEAD_BLOCK), row),
                  pl.BlockSpec((CTX_T, HEADS * HEAD_BLOCK), row),
                  pl.BlockSpec((CTX_T, HEADS * VDIM), row)],
        out_specs=pl.BlockSpec((CTX_T, HEADS * VDIM), row),
        out_shape=jax.ShapeDtypeStruct((N_CTX, HEADS * VDIM), BF16),
        compiler_params=_params(1),
        name="attn_ctx",
    )(q, k, v)


def _attn_lat_body(q_ref, kc_ref, vc_ref, ko_ref, vo_ref, o_ref, m_ref, acc_ref):
    reps = KV_CHUNK // LANES

    def update(hd, k, v, first):
        qh = q_ref[:, hd * HEAD_BLOCK:(hd + 1) * HEAD_BLOCK]
        s = lax.dot_general(qh, k, _NT, preferred_element_type=F32)
        mx = jnp.max(s, axis=-1, keepdims=True)
        if first:
            m_new = jnp.broadcast_to(mx, (TQ, LANES))
        else:
            m_old = m_ref[hd]
            m_new = jnp.maximum(m_old, mx)
        p = jnp.exp2(s - jnp.concatenate([m_new] * reps, axis=-1))
        pv = jnp.dot(p.astype(BF16), _v_ext(v), preferred_element_type=F32)
        if first:
            acc_ref[hd] = pv
        else:
            a = jnp.exp2(m_old - m_new)
            acc_ref[hd] = acc_ref[hd] * jnp.concatenate([a, a], axis=-1) + pv
        m_ref[hd] = m_new

    def head_slices(hd):
        return slice(hd * HEAD_BLOCK, (hd + 1) * HEAD_BLOCK), slice(hd * VDIM, (hd + 1) * VDIM)

    for hd in range(HEADS):
        ks, vs = head_slices(hd)
        update(hd, kc_ref[0, 0, :, ks], vc_ref[0, 0, :, vs], True)

    def body(c, _):
        rows = pl.ds(pl.multiple_of(c * KV_CHUNK, KV_CHUNK), KV_CHUNK)
        for hd in range(HEADS):
            ks, vs = head_slices(hd)
            update(hd, ko_ref[rows, ks], vo_ref[rows, vs], False)
        return 0

    lax.fori_loop(0, LAT_T // KV_CHUNK, body, 0)
    outs = [acc_ref[hd][:, :VDIM] / acc_ref[hd][:, VDIM:] for hd in range(HEADS)]
    o_ref[...] = jnp.concatenate(outs, axis=-1).astype(BF16)


def _attn_lat(q, k, v, kc, vc, layer):
    qt = LAT_T // TQ
    ctx_tiles = N_CTX // TQ
    ctx_blocks = N_CTX // LAT_T
    return pl.pallas_call(
        _attn_lat_body,
        grid=(LAT_B, qt),
        in_specs=[
            pl.BlockSpec((TQ, HEADS * HEAD_BLOCK), lambda b, i: (ctx_tiles + b * qt + i, 0)),
            pl.BlockSpec((1, 1, PAST_LEN, HEADS * HEAD_BLOCK), lambda b, i: (b, layer, 0, 0)),
            pl.BlockSpec((1, 1, PAST_LEN, HEADS * VDIM), lambda b, i: (b, layer, 0, 0)),
            pl.BlockSpec((LAT_T, HEADS * HEAD_BLOCK), lambda b, i: (ctx_blocks + b, 0)),
            pl.BlockSpec((LAT_T, HEADS * VDIM), lambda b, i: (ctx_blocks + b, 0)),
        ],
        out_specs=pl.BlockSpec((TQ, HEADS * VDIM), lambda b, i: (b * qt + i, 0)),
        out_shape=jax.ShapeDtypeStruct((N_LAT, HEADS * VDIM), BF16),
        scratch_shapes=[pltpu.VMEM((HEADS, TQ, LANES), F32),
                        pltpu.VMEM((HEADS, TQ, 2 * VDIM), F32)],
        compiler_params=_params(2),
        name="attn_lat",
    )(q, kc, vc, k, v)


def _layer_norm(y, g, b):
    mu = jnp.mean(y, axis=-1, keepdims=True)
    d = y - mu
    var = jnp.mean(d * d, axis=-1, keepdims=True)
    return d * lax.rsqrt(var + LN_EPS) * g + b


def _postmix_body(actx_ref, alat_ref, bc_ref, xc_ref, xl_ref, g1_ref, sh2_ref, sc2_ref, wa_ref, wbc_ref,
                  lg_ref, lb_ref, wr_ref, x1_ref, u2f_ref, aff_ref):
    a = _pair_load(TM, actx_ref, alat_ref)
    mix = jnp.dot(a, wa_ref[...], preferred_element_type=F32)
    mix += jnp.dot(bc_ref[...], wbc_ref[...], preferred_element_type=F32)
    x = _pair_load(TM, xc_ref, xl_ref)
    x1 = _layer_norm(ALPHA * x + g1_ref[0] * mix, lg_ref[...], lb_ref[...])
    x1_ref[...] = x1
    u2 = x1 * (1.0 + sc2_ref[0]) + sh2_ref[0]
    for c in range(CHUNKS):
        u2f_ref[pl.ds(c, TM, stride=CHUNKS), :] = u2[:, c * LANES:(c + 1) * LANES]
    logits = jnp.dot(u2.astype(BF16), wr_ref[...], preferred_element_type=F32)
    lane = lax.broadcasted_iota(I32, (1, LANES), 1)
    logits = jnp.where(lane < N_EXPERTS, logits, -jnp.inf)
    m = jnp.max(logits, axis=-1, keepdims=True)
    e = jnp.exp(logits - m)
    aff_ref[...] = e / jnp.sum(e, axis=-1, keepdims=True)


def _postmix(a_ctx, a_lat, bc, xc, xl, lat_row0, mod3, w_out_a, w_out_bc, ln_g, ln_b, w_router_pad):
    row = lambda i: (i, 0)
    modspec = lambda k: pl.BlockSpec((1, 1, D_MODEL), lambda i: (_mod_row(i, TM), 0, k))
    return pl.pallas_call(
        _postmix_body,
        grid=(N_TOK // TM,),
        in_specs=_pair_specs(TM, HEADS * VDIM, 0) + [
            pl.BlockSpec((TM, 2 * MIXW), row),
        ] + _pair_specs(TM, D_MODEL, lat_row0) + [
            modspec(2), modspec(3), modspec(4),
            _full((HEADS * VDIM, D_MODEL)), _full((2 * MIXW, D_MODEL)),
            _full((1, D_MODEL)), _full((1, D_MODEL)),
            _full((D_MODEL, LANES)),
        ],
        out_specs=(pl.BlockSpec((TM, D_MODEL), row),
                   pl.BlockSpec((TM * CHUNKS, LANES), row),
                   pl.BlockSpec((TM, LANES), row)),
        out_shape=(jax.ShapeDtypeStruct((N_TOK, D_MODEL), F32),
                   jax.ShapeDtypeStruct((N_TOK * CHUNKS, LANES), F32),
                   jax.ShapeDtypeStruct((N_TOK, LANES), F32)),
        compiler_params=_params(1),
        name="postmix",
    )(a_ctx, a_lat, bc, xc, xl, mod3, mod3, mod3, w_out_a, w_out_bc, ln_g, ln_b, w_router_pad)


PREFIX_CHUNK = 256


def _route_body(aff_ref, idx_ref, cnt_ref, *, seq, cap):
    n_chunks = seq // PREFIX_CHUNK
    aff = aff_ref[...]

    def search(i, thr):
        cand = thr | jnp.left_shift(jnp.int32(1), 30 - i)
        n = jnp.sum((aff >= pltpu.bitcast(cand, F32)).astype(I32), axis=0, keepdims=True)
        return jnp.where(n >= cap, cand, thr)

    thr_bits = lax.fori_loop(0, 31, search, jnp.zeros((1, LANES), I32))
    thr = pltpu.bitcast(thr_bits, F32)
    above = pltpu.bitcast(thr_bits + 1, F32)
    n_gt = jnp.sum(((aff > thr) & (aff >= above)).astype(I32), axis=0, keepdims=True)
    need = (cap - n_gt).astype(F32)

    ri = lax.broadcasted_iota(I32, (PREFIX_CHUNK, PREFIX_CHUNK), 0)
    ci = lax.broadcasted_iota(I32, (PREFIX_CHUNK, PREFIX_CHUNK), 1)
    tri = jnp.where(ci <= ri, 1.0, 0.0).astype(BF16)
    carry_eq = jnp.zeros((1, LANES), F32)
    carry_sel = jnp.zeros((1, LANES), F32)
    for c in range(n_chunks):
        rows = slice(c * PREFIX_CHUNK, (c + 1) * PREFIX_CHUNK)
        a = aff_ref[rows, :]
        gt = (a > thr) & (a >= above)
        eq = (a >= thr) & jnp.logical_not(gt)
        p_eq = jnp.dot(tri, jnp.where(eq, 1.0, 0.0).astype(BF16),
                       preferred_element_type=F32) + carry_eq
        sel = jnp.where(gt, 1.0, jnp.where(eq & (p_eq <= need), 1.0, 0.0))
        p_sel = jnp.dot(tri, sel.astype(BF16), preferred_element_type=F32) + carry_sel
        cnt_ref[rows, :] = p_sel
        carry_eq = p_eq[PREFIX_CHUNK - 1:PREFIX_CHUNK, :]
        carry_sel = p_sel[PREFIX_CHUNK - 1:PREFIX_CHUNK, :]

    rank = lax.broadcasted_iota(I32, (1, cap), 1).astype(F32)
    for e in range(N_EXPERTS):
        def count(c, acc, e=e):
            rows = pl.ds(pl.multiple_of(c * PREFIX_CHUNK, PREFIX_CHUNK), PREFIX_CHUNK)
            col = cnt_ref[rows, e:e + 1]
            le = jnp.where(col <= rank, 1.0, 0.0)
            return acc + jnp.sum(le.reshape(PREFIX_CHUNK // SUBLANES, SUBLANES, cap), axis=0)

        acc = lax.fori_loop(0, n_chunks, count, jnp.zeros((SUBLANES, cap), F32))
        idx_ref[0, e:e + 1, :] = jnp.sum(acc, axis=0, keepdims=True).astype(I32)


def _route(aff, first_block, n_req, seq, cap):
    return pl.pallas_call(
        functools.partial(_route_body, seq=seq, cap=cap),
        grid=(n_req,),
        in_specs=[pl.BlockSpec((seq, LANES), lambda b: (first_block + b, 0))],
        out_specs=pl.BlockSpec((1, N_EXPERTS, cap), lambda b: (b, 0, 0)),
        out_shape=jax.ShapeDtypeStruct((n_req, N_EXPERTS, cap), I32),
        scratch_shapes=[pltpu.VMEM((seq, LANES), F32)],
        compiler_params=_params(1),
        name=f"route_{seq}",
    )(aff)


def _moe_body(idx_ref, u2f_ref, aff_ref, wg_ref, wu_ref, wd_ref, acc_ref, tile_ref, gate_ref, y_ref):
    p = pl.program_id(0)
    e = pl.program_id(1)

    @pl.when(e == 0)
    def _():
        acc_ref[...] = jnp.zeros_like(acc_ref)

    base = (p * N_EXPERTS + e) * ROWS

    def gather(c, _):
        for i in range(SUBLANES):
            r = c * SUBLANES + i
            t = idx_ref[base + r]
            slab = u2f_ref[pl.ds(pl.multiple_of(t * CHUNKS, CHUNKS), CHUNKS), :]
            tile_ref[pl.ds(r, CHUNKS, stride=TILE_PITCH), :] = slab
            gate_ref[pl.ds(r, 1), :] = aff_ref[pl.ds(t, 1), :]
        return 0

    lax.fori_loop(0, ROWS // SUBLANES, gather, 0)
    x = jnp.concatenate([tile_ref[pl.ds(c * TILE_PITCH, ROWS), :] for c in range(CHUNKS)],
                        axis=-1).astype(BF16)
    lane = lax.broadcasted_iota(I32, (1, LANES), 1)
    gate = jnp.sum(jnp.where(lane == e, gate_ref[...], 0.0), axis=-1, keepdims=True)

    hg = jnp.dot(x, wg_ref[0, 0].astype(BF16), preferred_element_type=F32)
    hu = jnp.dot(x, wu_ref[0, 0].astype(BF16), preferred_element_type=F32)
    hidden = (hg * jax.nn.sigmoid(hg) * hu).astype(BF16)
    y = jnp.dot(hidden, wd_ref[0, 0].astype(BF16), preferred_element_type=F32) * gate
    for c in range(CHUNKS):
        y_ref[pl.ds(c, ROWS, stride=CHUNKS), :] = y[:, c * LANES:(c + 1) * LANES]

    def scatter(c, _):
        r0 = c * SCATTER_UNROLL
        dst = [pl.ds(pl.multiple_of(idx_ref[base + r0 + i] * CHUNKS, CHUNKS), CHUNKS)
               for i in range(SCATTER_UNROLL)]
        vals = [acc_ref[dst[i], :]
                + y_ref[pl.ds(pl.multiple_of((r0 + i) * CHUNKS, CHUNKS), CHUNKS), :]
                for i in range(SCATTER_UNROLL)]
        for i in range(SCATTER_UNROLL):
            acc_ref[dst[i], :] = vals[i]
        return 0

    lax.fori_loop(0, ROWS // SCATTER_UNROLL, scatter, 0)


def _moe(idx_flat, u2f, aff, w_gate, w_up, w_down, layer):
    one = pl.Buffered(1)
    grid_spec = pltpu.PrefetchScalarGridSpec(
        num_scalar_prefetch=1,
        grid=(N_PASS, N_EXPERTS),
        in_specs=[
            pl.BlockSpec((PASS_TOK * CHUNKS, LANES), lambda p, e, idx: (p, 0), pipeline_mode=one),
            pl.BlockSpec((PASS_TOK, LANES), lambda p, e, idx: (p, 0), pipeline_mode=one),
            pl.BlockSpec((1, 1, D_MODEL, EXPERT_DIM), lambda p, e, idx: (layer, e, 0, 0)),
            pl.BlockSpec((1, 1, D_MODEL, EXPERT_DIM), lambda p, e, idx: (layer, e, 0, 0)),
            pl.BlockSpec((1, 1, EXPERT_DIM, D_MODEL), lambda p, e, idx: (layer, e, 0, 0)),
        ],
        out_specs=pl.BlockSpec((PASS_TOK * CHUNKS, LANES), lambda p, e, idx: (p, 0),
                               pipeline_mode=one),
        scratch_shapes=[
            pltpu.VMEM((CHUNKS * TILE_PITCH, LANES), F32),
            pltpu.VMEM((ROWS, LANES), F32),
            pltpu.VMEM((ROWS * CHUNKS, LANES), F32),
        ],
    )
    return pl.pallas_call(
        _moe_body,
        grid_spec=grid_spec,
        out_shape=jax.ShapeDtypeStruct((N_TOK * CHUNKS, LANES), F32),
        compiler_params=_params(2, arbitrary=True),
        name="moe",
    )(idx_flat, u2f, aff, w_gate, w_up, w_down)


def _final_body(x1_ref, ff_ref, g2_ref, lg_ref, lb_ref, o_ref):
    ffn = jnp.concatenate([ff_ref[pl.ds(c, TM, stride=CHUNKS), :] for c in range(CHUNKS)], axis=-1)
    o_ref[...] = _layer_norm(ALPHA * x1_ref[...] + g2_ref[0] * ffn, lg_ref[...], lb_ref[...])


def _final(x1, ff, mod3, ln_g, ln_b, row0=0, n_rows=N_TOK):
    t0 = row0 // TM
    row = lambda i: (i + t0, 0)
    return pl.pallas_call(
        _final_body,
        grid=(n_rows // TM,),
        in_specs=[
            pl.BlockSpec((TM, D_MODEL), row),
            pl.BlockSpec((TM * CHUNKS, LANES), row),
            pl.BlockSpec((1, 1, D_MODEL), lambda i: (_mod_row(i + t0, TM), 0, 5)),
            _full((1, D_MODEL)), _full((1, D_MODEL)),
        ],
        out_specs=pl.BlockSpec((TM, D_MODEL), lambda i: (i, 0)),
        out_shape=jax.ShapeDtypeStruct((n_rows, D_MODEL), F32),
        compiler_params=_params(1),
        name="final_ln",
    )(x1, ff, mod3, ln_g, ln_b)


def _rot_cols(w):
    half = ROPE // 2
    return jnp.concatenate([-w[..., half:], w[..., :half]], axis=-1)


def _pad_lanes(w):
    pad = [(0, 0)] * (w.ndim - 1) + [(0, LANES - w.shape[-1])]
    return jnp.pad(w, pad)


def _rope_tables():
    rows_n = LAT_T // GRID_W
    r, cl = jnp.meshgrid(jnp.arange(rows_n, dtype=F32), jnp.arange(GRID_W, dtype=F32), indexing="ij")
    inv = ROPE_THETA ** (-jnp.arange(0, ROPE // 2, 2, dtype=F32) / (ROPE // 2))
    ang = jnp.concatenate([r.reshape(-1)[:, None] * inv, cl.reshape(-1)[:, None] * inv], axis=-1)
    cos, sin = jnp.cos(ang), jnp.sin(ang)
    cos_lat = _pad_lanes(jnp.concatenate([cos, cos], axis=-1))
    sin_lat = _pad_lanes(jnp.concatenate([sin, sin], axis=-1))
    cos_ctx = _pad_lanes(jnp.ones((N_CTX, ROPE), F32))
    sin_ctx = jnp.zeros((N_CTX, LANES), F32)
    cos_a = jnp.concatenate([cos_ctx] + [cos_lat] * LAT_B, axis=0)
    sin_a = jnp.concatenate([sin_ctx] + [sin_lat] * LAT_B, axis=0)
    return cos_a, sin_a


def kernel(x_prompt, x_sample, cache_ckv, cache_krope, c, c_ctx, w_in, q_norm, w_uq, kv_norm, w_uk, w_uv,
           pool_w, pool_scale, conv_w, w_out, w_ada, b_ada, ln1_g, ln1_b, ln2_g, ln2_b, w_router,
           w_gate, w_up, w_down):
    o_q, o_kv, o_kr, o_mix = 0, Q_RANK, Q_RANK + KV_RANK, Q_RANK + KV_RANK + ROPE
    w_kr = w_in[:, :, o_kr:o_kr + ROPE]
    w_in_ext = jnp.concatenate(
        [w_in[:, :, o_q:o_kr], w_in[:, :, o_mix:], _pad_lanes(w_kr), _pad_lanes(_rot_cols(w_kr))],
        axis=-1).astype(BF16)
    uq = w_uq.reshape(DEPTH, Q_RANK, HEADS, NOPE + ROPE)
    uq_rope = uq[..., NOPE:]
    w_uq_ext = jnp.concatenate(
        [uq[..., :NOPE].reshape(DEPTH, Q_RANK, HEADS * NOPE),
         _pad_lanes(uq_rope).reshape(DEPTH, Q_RANK, HEADS * LANES),
         _pad_lanes(_rot_cols(uq_rope)).reshape(DEPTH, Q_RANK, HEADS * LANES)], axis=-1).astype(BF16)
    w_uk_b = w_uk.astype(BF16)
    w_uv_b = w_uv.astype(BF16)
    eye = jnp.eye(len(POOL_WINDOWS), dtype=F32)
    pool_bd = (pool_w[:, :, :, None, :] * eye[None, :, None, :, None]).reshape(DEPTH, MIXW, MIXW).astype(BF16)
    w_out_b = w_out.astype(BF16)
    w_router_pad = _pad_lanes(w_router).astype(BF16)
    cos_a, sin_a = _rope_tables()

    cond = jnp.concatenate([c_ctx[None, :], c, jnp.zeros((SUBLANES - 1 - LAT_B, D_MODEL), F32)], axis=0)
    mod = _ada(cond, w_ada, b_ada)
    kc, vc = _cachekv(cache_ckv, _pad_lanes(cache_krope), w_uk_b, w_uv_b)

    xs = (x_prompt.reshape(N_CTX, D_MODEL), x_sample.reshape(N_LAT, D_MODEL), 0)
    ckv_layers, kr_layers = [], []
    ctx_off = (jnp.arange(CTX_B, dtype=I32) * CTX_T)[:, None, None]
    for l in range(DEPTH):
        mod3 = mod[l].reshape(SUBLANES, 1, 6 * D_MODEL)
        ckv, kr, q, k, v, mix = _inproj(*xs, mod3, w_in_ext[l], q_norm[l][None], kv_norm[l][None],
                                        w_uq_ext[l], w_uk_b[l], w_uv_b[l], cos_a, sin_a)
        ckv_layers.append(ckv[:N_CTX].reshape(CTX_B, CTX_T, KV_RANK))
        kr_layers.append(kr[:N_CTX, :ROPE].reshape(CTX_B, CTX_T, ROPE))
        bc = _seqmix(mix, pool_bd[l], pool_scale[l][None], conv_w[l])
        a_ctx = _attn_ctx(q, k, v)
        a_lat = _attn_lat(q, k, v, kc, vc, l)
        x1, u2f, aff = _postmix(a_ctx, a_lat, bc, *xs, mod3, w_out_b[l, :HEADS * VDIM],
                                w_out_b[l, HEADS * VDIM:], ln1_g[l][None], ln1_b[l][None],
                                w_router_pad[l])
        idx_ctx = _route(aff, 0, CTX_B, CTX_T, CTX_CAP)
        idx_lat = _route(aff, N_CTX // LAT_T, LAT_B, LAT_T, LAT_CAP)
        idx_ctx = (idx_ctx + ctx_off).transpose(1, 0, 2).reshape(1, N_EXPERTS, ROWS)
        idx_flat = jnp.concatenate([idx_ctx, idx_lat], axis=0).reshape(-1)
        ff = _moe(idx_flat, u2f, aff, w_gate, w_up, w_down, l)
        if l + 1 < DEPTH:
            x = _final(x1, ff, mod3, ln2_g[l][None], ln2_b[l][None])
            xs = (x, x, N_CTX)
        else:
            y_ctx = _final(x1, ff, mod3, ln2_g[l][None], ln2_b[l][None], 0, N_CTX)
            y_lat = _final(x1, ff, mod3, ln2_g[l][None], ln2_b[l][None], N_CTX, N_LAT)

    y_prompt = y_ctx.reshape(CTX_B, CTX_T, D_MODEL)
    y_sample = y_lat.reshape(LAT_B, LAT_T, D_MODEL)
    new_ckv = jnp.stack(ckv_layers, axis=1)
    new_krope = jnp.stack(kr_layers, axis=1)
    return (y_prompt, y_sample, new_ckv, new_krope)
```

```python
import functools

import jax
import jax.numpy as jnp
from jax import lax
from jax.experimental import pallas as pl
from jax.experimental.pallas import tpu as pltpu

F32 = jnp.float32
BF16 = jnp.bfloat16
I32 = jnp.int32

D_MODEL = 1024
CTX_B, CTX_T = 16, 256
LAT_B, LAT_T = 2, 4096
DEPTH = 4
PAST_LEN = 512
GRID_W = 64
HEADS = 4
NOPE, ROPE, VDIM = 128, 64, 128
Q_RANK, KV_RANK = 384, 256
POOL_WINDOWS = (2, 4, 8, 16)
POOL_GROUP = 64
MIXW = 256
N_EXPERTS = 16
EXPERT_DIM = 512
ROPE_THETA = 10000.0
ATTN_SCALE = (NOPE + ROPE) ** -0.5
LOG2E = 1.4426950408889634
Q_SCALE = ATTN_SCALE * LOG2E
ALPHA = (2 * DEPTH) ** 0.25
RMS_EPS = 1e-6
LN_EPS = 1e-5

N_CTX = CTX_B * CTX_T
N_LAT = LAT_B * LAT_T
N_TOK = N_CTX + N_LAT
CTX_CAP = 2 * CTX_T // N_EXPERTS
LAT_CAP = 2 * LAT_T // N_EXPERTS

LANES = 128
SUBLANES = 8
CHUNKS = D_MODEL // LANES
HEAD_BLOCK = 2 * LANES
VMEM_LIMIT = 56 * 1024 * 1024

TM = 512
SEQ_TILE = 256
HALO = 8
TQ = 512
KV_CHUNK = 512
CHUNK_UNROLL = 8
PASS_TOK = 4096
N_PASS = N_TOK // PASS_TOK
ROWS = 512
TILE_PITCH = ROWS + 8
SCATTER_UNROLL = 8


def _params(n_axes, arbitrary=False):
    sem = ("arbitrary" if arbitrary else "parallel",) * n_axes
    return pltpu.CompilerParams(dimension_semantics=sem, vmem_limit_bytes=VMEM_LIMIT)


def _mod_row(i, tile):
    start = i * tile
    return jnp.where(start < N_CTX, 0, 1 + (start - N_CTX) // LAT_T)


def _full(shape):
    nd = len(shape)
    return pl.BlockSpec(shape, lambda *_: (0,) * nd)


def _pair_specs(tile, width, lat_row0):
    nct = N_CTX // tile
    off = lat_row0 // tile
    return [pl.BlockSpec((tile, width), lambda i: (jnp.minimum(i, nct - 1), 0)),
            pl.BlockSpec((tile, width), lambda i: (jnp.maximum(i - nct, 0) + off, 0))]


def _pair_load(tile, ctx_ref, lat_ref):
    return jnp.where(pl.program_id(0) < N_CTX // tile, ctx_ref[...], lat_ref[...])


ADA_TN = 1536


def _ada_body(cond_ref, w_ref, b_ref, o_ref):
    c = cond_ref[...]
    a = c * jax.nn.sigmoid(c)
    a_hi = a.astype(BF16)
    a_lo = (a - a_hi.astype(F32)).astype(BF16)
    w = w_ref[0]
    w_hi = w.astype(BF16)
    w_lo = (w - w_hi.astype(F32)).astype(BF16)
    acc = jnp.dot(a_hi, w_hi, preferred_element_type=F32)
    acc += jnp.dot(a_lo, w_hi, preferred_element_type=F32)
    acc += jnp.dot(a_hi, w_lo, preferred_element_type=F32)
    o_ref[0] = acc + b_ref[0]


def _ada(cond, w_ada, b_ada):
    n = 6 * D_MODEL
    return pl.pallas_call(
        _ada_body,
        grid=(DEPTH, n // ADA_TN),
        in_specs=[
            _full((SUBLANES, D_MODEL)),
            pl.BlockSpec((1, D_MODEL, ADA_TN), lambda l, j: (l, 0, j)),
            pl.BlockSpec((1, 1, ADA_TN), lambda l, j: (l, 0, j)),
        ],
        out_specs=pl.BlockSpec((1, SUBLANES, ADA_TN), lambda l, j: (l, 0, j)),
        out_shape=jax.ShapeDtypeStruct((DEPTH, SUBLANES, n), F32),
        compiler_params=_params(2),
        name="ada_mod",
    )(cond, w_ada, b_ada.reshape(DEPTH, 1, n))


IN_EXT = Q_RANK + KV_RANK + 4 * MIXW + 2 * LANES
UQ_EXT = HEADS * NOPE + 2 * HEADS * LANES


def _rms(x, g):
    return x * lax.rsqrt(jnp.mean(x * x, axis=-1, keepdims=True) + RMS_EPS) * g


def _inproj_body(xc_ref, xl_ref, sh_ref, sc_ref, win_ref, qn_ref, kvn_ref, wuq_ref, wuk_ref, wuv_ref,
                 cos_ref, sin_ref, ckv_ref, kr_ref, q_ref, k_ref, v_ref, mix_ref):
    u = _pair_load(TM, xc_ref, xl_ref) * (1.0 + sc_ref[0]) + sh_ref[0]
    h = jnp.dot(u.astype(BF16), win_ref[...], preferred_element_type=F32)
    o = 0
    q_c = h[:, o:o + Q_RANK]; o += Q_RANK
    kv_c = h[:, o:o + KV_RANK]; o += KV_RANK
    mix_ref[...] = h[:, o:o + 4 * MIXW]; o += 4 * MIXW
    kr_a = h[:, o:o + LANES]; o += LANES
    kr_b = h[:, o:o + LANES]
    kr_ref[...] = kr_a
    cos_a = cos_ref[...]
    sin_a = sin_ref[...]

    ckv = _rms(kv_c, kvn_ref[...])
    ckv_ref[...] = ckv
    ckv_b = ckv.astype(BF16)
    k_nope = jnp.dot(ckv_b, wuk_ref[...], preferred_element_type=F32)
    v_ref[...] = jnp.dot(ckv_b, wuv_ref[...], preferred_element_type=F32).astype(BF16)
    k_rope = kr_a * cos_a + kr_b * sin_a
    parts = []
    for hd in range(HEADS):
        parts += [k_nope[:, hd * NOPE:(hd + 1) * NOPE], k_rope]
    k_ref[...] = jnp.concatenate(parts, axis=-1).astype(BF16)

    qn = _rms(q_c, qn_ref[...]).astype(BF16)
    qq = jnp.dot(qn, wuq_ref[...], preferred_element_type=F32)
    ra = HEADS * NOPE
    rb = ra + HEADS * LANES
    parts = []
    for hd in range(HEADS):
        q_rope = (qq[:, ra + hd * LANES:ra + (hd + 1) * LANES] * cos_a
                  + qq[:, rb + hd * LANES:rb + (hd + 1) * LANES] * sin_a)
        parts += [qq[:, hd * NOPE:(hd + 1) * NOPE] * Q_SCALE, q_rope * Q_SCALE]
    q_ref[...] = jnp.concatenate(parts, axis=-1).astype(BF16)


def _inproj(xc, xl, lat_row0, mod3, w_in_ext, q_norm, kv_norm, w_uq_ext, w_uk, w_uv, cos_a, sin_a):
    row = lambda i: (i, 0)
    modspec = lambda k: pl.BlockSpec((1, 1, D_MODEL), lambda i: (_mod_row(i, TM), 0, k))
    outs = (
        jax.ShapeDtypeStruct((N_TOK, KV_RANK), F32),
        jax.ShapeDtypeStruct((N_TOK, LANES), F32),
        jax.ShapeDtypeStruct((N_TOK, HEADS * HEAD_BLOCK), BF16),
        jax.ShapeDtypeStruct((N_TOK, HEADS * HEAD_BLOCK), BF16),
        jax.ShapeDtypeStruct((N_TOK, HEADS * VDIM), BF16),
        jax.ShapeDtypeStruct((N_TOK, 4 * MIXW), F32),
    )
    return pl.pallas_call(
        _inproj_body,
        grid=(N_TOK // TM,),
        in_specs=_pair_specs(TM, D_MODEL, lat_row0) + [
            modspec(0), modspec(1),
            _full((D_MODEL, IN_EXT)),
            _full((1, Q_RANK)), _full((1, KV_RANK)),
            _full((Q_RANK, UQ_EXT)),
            _full((KV_RANK, HEADS * NOPE)), _full((KV_RANK, HEADS * VDIM)),
            pl.BlockSpec((TM, LANES), row), pl.BlockSpec((TM, LANES), row),
        ],
        out_specs=tuple(pl.BlockSpec((TM, s.shape[1]), row) for s in outs),
        out_shape=outs,
        compiler_params=_params(1),
        name="inproj",
    )(xc, xl, mod3, mod3, w_in_ext, q_norm, kv_norm, w_uq_ext, w_uk, w_uv, cos_a, sin_a)


def _cachekv_body(ckv_ref, kr_ref, wuk_ref, wuv_ref, k_ref, v_ref):
    c = ckv_ref[0, 0].astype(BF16)
    k_nope = jnp.dot(c, wuk_ref[0], preferred_element_type=F32)
    v_ref[0, 0] = jnp.dot(c, wuv_ref[0], preferred_element_type=F32).astype(BF16)
    kr = kr_ref[0, 0]
    parts = []
    for hd in range(HEADS):
        parts += [k_nope[:, hd * NOPE:(hd + 1) * NOPE], kr]
    k_ref[0, 0] = jnp.concatenate(parts, axis=-1).astype(BF16)


def _cachekv(cache_ckv, kr_pad, w_uk, w_uv):
    bl = lambda b, l: (b, l, 0, 0)
    wl = lambda b, l: (l, 0, 0)
    return pl.pallas_call(
        _cachekv_body,
        grid=(LAT_B, DEPTH),
        in_specs=[
            pl.BlockSpec((1, 1, PAST_LEN, KV_RANK), bl),
            pl.BlockSpec((1, 1, PAST_LEN, LANES), bl),
            pl.BlockSpec((1, KV_RANK, HEADS * NOPE), wl),
            pl.BlockSpec((1, KV_RANK, HEADS * VDIM), wl),
        ],
        out_specs=(pl.BlockSpec((1, 1, PAST_LEN, HEADS * HEAD_BLOCK), bl),
                   pl.BlockSpec((1, 1, PAST_LEN, HEADS * VDIM), bl)),
        out_shape=(jax.ShapeDtypeStruct((LAT_B, DEPTH, PAST_LEN, HEADS * HEAD_BLOCK), BF16),
                   jax.ShapeDtypeStruct((LAT_B, DEPTH, PAST_LEN, HEADS * VDIM), BF16)),
        compiler_params=_params(2),
        name="cache_kv",
    )(cache_ckv, kr_pad, w_uk, w_uv)


EXT = SEQ_TILE + 2 * HALO


def _shift_up(x, k):
    return pltpu.roll(x, x.shape[0] - k, axis=0)


def _shift_down(x, k):
    return pltpu.roll(x, k, axis=0)


def _seqmix_body(main_ref, prev_ref, next_ref, pw_ref, ps_ref, cw_ref, o_ref):
    j = pl.program_id(0)
    n_ctx_tiles = N_CTX // SEQ_TILE
    tiles_per_lat = LAT_T // SEQ_TILE
    is_ctx = j < n_ctx_tiles
    jj = jnp.where(is_ctx, 0, (j - n_ctx_tiles) % tiles_per_lat)
    first = jj == 0
    last = jnp.where(is_ctx, True, jj == tiles_per_lat - 1)
    t_seq = jnp.where(is_ctx, CTX_T, LAT_T)

    main = main_ref[...]
    prev = jnp.where(first, 0.0, prev_ref[...])
    nxt = jnp.where(last, 0.0, next_ref[...])
    ext = jnp.concatenate([prev, main, nxt], axis=0)
    p = ext[:, 0:MIXW]
    g_b = main[:, MIXW:2 * MIXW]
    g_c = ext[:, 2 * MIXW:3 * MIXW]
    h_in = ext[:, 3 * MIXW:4 * MIXW]

    sums = {}
    b = p
    for w in POOL_WINDOWS:
        b = b + _shift_up(b, w // 2)
        sums[w] = _shift_down(b, w // 2)[HALO:HALO + SEQ_TILE]
    tpos = jj * SEQ_TILE + lax.broadcasted_iota(I32, (SEQ_TILE, 1), 0)
    lane = lax.broadcasted_iota(I32, (1, MIXW), 1)
    num = None
    den = None
    for g, w in enumerate(POOL_WINDOWS):
        lo = jnp.maximum(tpos - w // 2, 0)
        hi = jnp.minimum(tpos + w - w // 2, t_seq)
        cnt = (hi - lo).astype(F32)
        if num is None:
            num, den = sums[w], jnp.broadcast_to(cnt, (SEQ_TILE, MIXW))
        else:
            sel = lane >= g * POOL_GROUP
            num = jnp.where(sel, sums[w], num)
            den = jnp.where(sel, cnt, den)
    pooled = num / den - p[HALO:HALO + SEQ_TILE]
    b_out = jnp.dot(pooled.astype(BF16), pw_ref[...], preferred_element_type=F32) * ps_ref[...]

    y = g_c * h_in
    cw = cw_ref[...]
    conv = (_shift_down(y, 1) * cw[0:1] + y * cw[1:2] + _shift_up(y, 1) * cw[2:3])
    c_out = g_b * conv[HALO:HALO + SEQ_TILE]
    o_ref[...] = jnp.concatenate([b_out, c_out], axis=-1).astype(BF16)


def _seqmix(mix, pool_bd, pool_scale, conv_w):
    per = SEQ_TILE // HALO
    nblk = N_TOK // HALO
    return pl.pallas_call(
        _seqmix_body,
        grid=(N_TOK // SEQ_TILE,),
        in_specs=[
            pl.BlockSpec((SEQ_TILE, 4 * MIXW), lambda j: (j, 0)),
            pl.BlockSpec((HALO, 4 * MIXW), lambda j: (jnp.maximum(j * per - 1, 0), 0)),
            pl.BlockSpec((HALO, 4 * MIXW), lambda j: (jnp.minimum((j + 1) * per, nblk - 1), 0)),
            _full((MIXW, MIXW)), _full((1, MIXW)), _full((3, MIXW)),
        ],
        out_specs=pl.BlockSpec((SEQ_TILE, 2 * MIXW), lambda j: (j, 0)),
        out_shape=jax.ShapeDtypeStruct((N_TOK, 2 * MIXW), BF16),
        compiler_params=_params(1),
        name="seqmix",
    )(mix, mix, mix, pool_bd, pool_scale, conv_w)


_NT = (((1,), (1,)), ((), ()))


def _v_ext(v):
    return jnp.concatenate([v, jnp.ones(v.shape, v.dtype)], axis=-1)


def _attn_ctx_body(q_ref, k_ref, v_ref, o_ref):
    outs = []
    for hd in range(HEADS):
        qh = q_ref[:, hd * HEAD_BLOCK:(hd + 1) * HEAD_BLOCK]
        kh = k_ref[:, hd * HEAD_BLOCK:(hd + 1) * HEAD_BLOCK]
        vh = v_ref[:, hd * VDIM:(hd + 1) * VDIM]
        s = lax.dot_general(qh, kh, _NT, preferred_element_type=F32)
        m = jnp.max(s, axis=-1, keepdims=True)
        p = jnp.exp2(s - m)
        acc = jnp.dot(p.astype(BF16), _v_ext(vh), preferred_element_type=F32)
        outs.append(acc[:, :VDIM] / acc[:, VDIM:])
    o_ref[...] = jnp.concatenate(outs, axis=-1).astype(BF16)


def _attn_ctx(q, k, v):
    row = lambda b: (b, 0)
    return pl.pallas_call(
        _attn_ctx_body,
        grid=(CTX_B,),
        in_specs=[pl.BlockSpec((CTX_T, HEADS * HEAD_BLOCK), row),
                  pl.BlockSpec((CTX_T, HEADS * HEAD_BLOCK), row),
                  pl.BlockSpec((CTX_T, HEADS * VDIM), row)],
        out_specs=pl.BlockSpec((CTX_T, HEADS * VDIM), row),
        out_shape=jax.ShapeDtypeStruct((N_CTX, HEADS * VDIM), BF16),
        compiler_params=_params(1),
        name="attn_ctx",
    )(q, k, v)


def _attn_lat_body(q_ref, kc_ref, vc_ref, ko_ref, vo_ref, o_ref, m_ref, acc_ref):
    reps = KV_CHUNK // LANES

    def update(hd, k, v, first):
        qh = q_ref[:, hd * HEAD_BLOCK:(hd + 1) * HEAD_BLOCK]
        s = lax.dot_general(qh, k, _NT, preferred_element_type=F32)
        mx = jnp.max(s, axis=-1, keepdims=True)
        if first:
            m_new = jnp.broadcast_to(mx, (TQ, LANES))
        else:
            m_old = m_ref[hd]
            m_new = jnp.maximum(m_old, mx)
        p = jnp.exp2(s - jnp.concatenate([m_new] * reps, axis=-1))
        pv = jnp.dot(p.astype(BF16), _v_ext(v), preferred_element_type=F32)
        if first:
            acc_ref[hd] = pv
        else:
            a = jnp.exp2(m_old - m_new)
            acc_ref[hd] = acc_ref[hd] * jnp.concatenate([a, a], axis=-1) + pv
        m_ref[hd] = m_new

    def head_slices(hd):
        return slice(hd * HEAD_BLOCK, (hd + 1) * HEAD_BLOCK), slice(hd * VDIM, (hd + 1) * VDIM)

    for hd in range(HEADS):
        ks, vs = head_slices(hd)
        update(hd, kc_ref[0, 0, :, ks], vc_ref[0, 0, :, vs], True)

    def body(c, _):
        for j in range(CHUNK_UNROLL):
            rows = pl.ds(pl.multiple_of((c * CHUNK_UNROLL + j) * KV_CHUNK, KV_CHUNK), KV_CHUNK)
            for hd in range(HEADS):
                ks, vs = head_slices(hd)
                update(hd, ko_ref[rows, ks], vo_ref[rows, vs], False)
        return 0

    lax.fori_loop(0, LAT_T // (KV_CHUNK * CHUNK_UNROLL), body, 0)
    outs = [acc_ref[hd][:, :VDIM] / acc_ref[hd][:, VDIM:] for hd in range(HEADS)]
    o_ref[...] = jnp.concatenate(outs, axis=-1).astype(BF16)


def _attn_lat(q, k, v, kc, vc, layer):
    qt = LAT_T // TQ
    ctx_tiles = N_CTX // TQ
    ctx_blocks = N_CTX // LAT_T
    return pl.pallas_call(
        _attn_lat_body,
        grid=(LAT_B, qt),
        in_specs=[
            pl.BlockSpec((TQ, HEADS * HEAD_BLOCK), lambda b, i: (ctx_tiles + b * qt + i, 0)),
            pl.BlockSpec((1, 1, PAST_LEN, HEADS * HEAD_BLOCK), lambda b, i: (b, layer, 0, 0)),
            pl.BlockSpec((1, 1, PAST_LEN, HEADS * VDIM), lambda b, i: (b, layer, 0, 0)),
            pl.BlockSpec((LAT_T, HEADS * HEAD_BLOCK), lambda b, i: (ctx_blocks + b, 0)),
            pl.BlockSpec((LAT_T, HEADS * VDIM), lambda b, i: (ctx_blocks + b, 0)),
        ],
        out_specs=pl.BlockSpec((TQ, HEADS * VDIM), lambda b, i: (b * qt + i, 0)),
        out_shape=jax.ShapeDtypeStruct((N_LAT, HEADS * VDIM), BF16),
        scratch_shapes=[pltpu.VMEM((HEADS, TQ, LANES), F32),
                        pltpu.VMEM((HEADS, TQ, 2 * VDIM), F32)],
        compiler_params=_params(2),
        name="attn_lat",
    )(q, kc, vc, k, v)


def _layer_norm(y, g, b):
    mu = jnp.mean(y, axis=-1, keepdims=True)
    d = y - mu
    var = jnp.mean(d * d, axis=-1, keepdims=True)
    return d * lax.rsqrt(var + LN_EPS) * g + b


def _postmix_body(actx_ref, alat_ref, bc_ref, xc_ref, xl_ref, g1_ref, sh2_ref, sc2_ref, wa_ref, wbc_ref,
                  lg_ref, lb_ref, wr_ref, x1_ref, u2f_ref, aff_ref):
    a = _pair_load(TM, actx_ref, alat_ref)
    mix = jnp.dot(a, wa_ref[...], preferred_element_type=F32)
    mix += jnp.dot(bc_ref[...], wbc_ref[...], preferred_element_type=F32)
    x = _pair_load(TM, xc_ref, xl_ref)
    x1 = _layer_norm(ALPHA * x + g1_ref[0] * mix, lg_ref[...], lb_ref[...])
    x1_ref[...] = x1
    u2 = x1 * (1.0 + sc2_ref[0]) + sh2_ref[0]
    for c in range(CHUNKS):
        u2f_ref[pl.ds(c, TM, stride=CHUNKS), :] = u2[:, c * LANES:(c + 1) * LANES]
    logits = jnp.dot(u2.astype(BF16), wr_ref[...], preferred_element_type=F32)
    lane = lax.broadcasted_iota(I32, (1, LANES), 1)
    logits = jnp.where(lane < N_EXPERTS, logits, -jnp.inf)
    m = jnp.max(logits, axis=-1, keepdims=True)
    e = jnp.exp(logits - m)
    aff_ref[...] = e / jnp.sum(e, axis=-1, keepdims=True)


def _postmix(a_ctx, a_lat, bc, xc, xl, lat_row0, mod3, w_out_a, w_out_bc, ln_g, ln_b, w_router_pad):
    row = lambda i: (i, 0)
    modspec = lambda k: pl.BlockSpec((1, 1, D_MODEL), lambda i: (_mod_row(i, TM), 0, k))
    return pl.pallas_call(
        _postmix_body,
        grid=(N_TOK // TM,),
        in_specs=_pair_specs(TM, HEADS * VDIM, 0) + [
            pl.BlockSpec((TM, 2 * MIXW), row),
        ] + _pair_specs(TM, D_MODEL, lat_row0) + [
            modspec(2), modspec(3), modspec(4),
            _full((HEADS * VDIM, D_MODEL)), _full((2 * MIXW, D_MODEL)),
            _full((1, D_MODEL)), _full((1, D_MODEL)),
            _full((D_MODEL, LANES)),
        ],
        out_specs=(pl.BlockSpec((TM, D_MODEL), row),
                   pl.BlockSpec((TM * CHUNKS, LANES), row),
                   pl.BlockSpec((TM, LANES), row)),
        out_shape=(jax.ShapeDtypeStruct((N_TOK, D_MODEL), F32),
                   jax.ShapeDtypeStruct((N_TOK * CHUNKS, LANES), F32),
                   jax.ShapeDtypeStruct((N_TOK, LANES), F32)),
        compiler_params=_params(1),
        name="postmix",
    )(a_ctx, a_lat, bc, xc, xl, mod3, mod3, mod3, w_out_a, w_out_bc, ln_g, ln_b, w_router_pad)


PREFIX_CHUNK = 256


def _route_body(aff_ref, idx_ref, cnt_ref, *, seq, cap):
    n_chunks = seq // PREFIX_CHUNK
    aff = aff_ref[...]

    def search(i, thr):
        cand = thr | jnp.left_shift(jnp.int32(1), 30 - i)
        n = jnp.sum((aff >= pltpu.bitcast(cand, F32)).astype(I32), axis=0, keepdims=True)
        return jnp.where(n >= cap, cand, thr)

    thr_bits = lax.fori_loop(0, 31, search, jnp.zeros((1, LANES), I32))
    thr = pltpu.bitcast(thr_bits, F32)
    above = pltpu.bitcast(thr_bits + 1, F32)
    n_gt = jnp.sum(((aff > thr) & (aff >= above)).astype(I32), axis=0, keepdims=True)
    need = (cap - n_gt).astype(F32)

    ri = lax.broadcasted_iota(I32, (PREFIX_CHUNK, PREFIX_CHUNK), 0)
    ci = lax.broadcasted_iota(I32, (PREFIX_CHUNK, PREFIX_CHUNK), 1)
    tri = jnp.where(ci <= ri, 1.0, 0.0).astype(BF16)
    carry_eq = jnp.zeros((1, LANES), F32)
    carry_sel = jnp.zeros((1, LANES), F32)
    for c in range(n_chunks):
        rows = slice(c * PREFIX_CHUNK, (c + 1) * PREFIX_CHUNK)
        a = aff_ref[rows, :]
        gt = (a > thr) & (a >= above)
        eq = (a >= thr) & jnp.logical_not(gt)
        p_eq = jnp.dot(tri, jnp.where(eq, 1.0, 0.0).astype(BF16),
                       preferred_element_type=F32) + carry_eq
        sel = jnp.where(gt, 1.0, jnp.where(eq & (p_eq <= need), 1.0, 0.0))
        p_sel = jnp.dot(tri, sel.astype(BF16), preferred_element_type=F32) + carry_sel
        cnt_ref[rows, :] = p_sel
        carry_eq = p_eq[PREFIX_CHUNK - 1:PREFIX_CHUNK, :]
        carry_sel = p_sel[PREFIX_CHUNK - 1:PREFIX_CHUNK, :]

    rank = lax.broadcasted_iota(I32, (1, cap), 1).astype(F32)
    for e in range(N_EXPERTS):
        def count(c, acc, e=e):
            rows = pl.ds(pl.multiple_of(c * PREFIX_CHUNK, PREFIX_CHUNK), PREFIX_CHUNK)
            col = cnt_ref[rows, e:e + 1]
            le = jnp.where(col <= rank, 1.0, 0.0)
            return acc + jnp.sum(le.reshape(PREFIX_CHUNK // SUBLANES, SUBLANES, cap), axis=0)

        acc = lax.fori_loop(0, n_chunks, count, jnp.zeros((SUBLANES, cap), F32))
        idx_ref[0, e:e + 1, :] = jnp.sum(acc, axis=0, keepdims=True).astype(I32)


def _route(aff, first_block, n_req, seq, cap):
    return pl.pallas_call(
        functools.partial(_route_body, seq=seq, cap=cap),
        grid=(n_req,),
        in_specs=[pl.BlockSpec((seq, LANES), lambda b: (first_block + b, 0))],
        out_specs=pl.BlockSpec((1, N_EXPERTS, cap), lambda b: (b, 0, 0)),
        out_shape=jax.ShapeDtypeStruct((n_req, N_EXPERTS, cap), I32),
        scratch_shapes=[pltpu.VMEM((seq, LANES), F32)],
        compiler_params=_params(1),
        name=f"route_{seq}",
    )(aff)


RANK_TILE = LANES


def _expert_row(col):
    full = jnp.concatenate([jnp.broadcast_to(col, (N_EXPERTS, LANES)),
                            jnp.zeros((LANES - N_EXPERTS, LANES), col.dtype)], axis=0)
    return full.T[0:1]


def _route_count_body(aff_ref, cnt_ref, cend_ref, *, seq, cap):
    n_chunks = seq // PREFIX_CHUNK
    dense = aff_ref[...].T[:N_EXPERTS]

    def search(i, thr):
        cand = thr | jnp.left_shift(jnp.int32(1), 30 - i)
        n = jnp.sum((dense >= pltpu.bitcast(cand, F32)).astype(I32), axis=1, keepdims=True)
        return jnp.where(n >= cap, cand, thr)

    thr_bits = lax.fori_loop(0, 31, search, jnp.zeros((N_EXPERTS, 1), I32))
    thr_col = pltpu.bitcast(thr_bits, F32)
    above_col = pltpu.bitcast(thr_bits + 1, F32)
    n_gt = jnp.sum(((dense > thr_col) & (dense >= above_col)).astype(I32), axis=1, keepdims=True)
    thr = _expert_row(thr_col)
    above = _expert_row(above_col)
    need = _expert_row((cap - n_gt).astype(F32))

    ri = lax.broadcasted_iota(I32, (PREFIX_CHUNK, PREFIX_CHUNK), 0)
    ci = lax.broadcasted_iota(I32, (PREFIX_CHUNK, PREFIX_CHUNK), 1)
    tri = jnp.where(ci <= ri, 1.0, 0.0).astype(BF16)
    carry_eq = jnp.zeros((1, LANES), F32)
    carry_sel = jnp.zeros((1, LANES), F32)
    per = PREFIX_CHUNK // RANK_TILE
    for c in range(n_chunks):
        rows = slice(c * PREFIX_CHUNK, (c + 1) * PREFIX_CHUNK)
        a = aff_ref[rows, :]
        gt = (a > thr) & (a >= above)
        eq = (a >= thr) & jnp.logical_not(gt)
        p_eq = jnp.dot(tri, jnp.where(eq, 1.0, 0.0).astype(BF16),
                       preferred_element_type=F32) + carry_eq
        sel = jnp.where(gt, 1.0, jnp.where(eq & (p_eq <= need), 1.0, 0.0))
        p_sel = jnp.dot(tri, sel.astype(BF16), preferred_element_type=F32) + carry_sel
        dense_cnt = p_sel.T[:N_EXPERTS]
        for k in range(per):
            last = (k + 1) * RANK_TILE - 1
            cnt_ref[c * per + k] = dense_cnt[:, k * RANK_TILE:(k + 1) * RANK_TILE]
            cend_ref[0, c * per + k:c * per + k + 1, :] = p_sel[last:last + 1, :].astype(I32)
        carry_eq = p_eq[PREFIX_CHUNK - 1:PREFIX_CHUNK, :]
        carry_sel = p_sel[PREFIX_CHUNK - 1:PREFIX_CHUNK, :]


def _route_count(aff, first_block, n_req, seq, cap):
    n_tiles = seq // RANK_TILE
    return pl.pallas_call(
        functools.partial(_route_count_body, seq=seq, cap=cap),
        grid=(n_req,),
        in_specs=[pl.BlockSpec((seq, LANES), lambda b: (first_block + b, 0))],
        out_specs=(pl.BlockSpec((n_tiles, N_EXPERTS, RANK_TILE), lambda b: (b, 0, 0)),
                   pl.BlockSpec((1, n_tiles, LANES), lambda b: (b, 0, 0))),
        out_shape=(jax.ShapeDtypeStruct((n_req * n_tiles, N_EXPERTS, RANK_TILE), F32),
                   jax.ShapeDtypeStruct((n_req, n_tiles, LANES), I32)),
        compiler_params=_params(1),
        name=f"route_count_{seq}",
    )(aff)


def _route_rank_body(cend_ref, cnt_ref, idx_ref, *bufs, seq, cap):
    b = pl.program_id(0)
    n_tiles = seq // RANK_TILE
    rank = lax.broadcasted_iota(I32, (RANK_TILE, RANK_TILE), 0).astype(F32)
    ones = jnp.ones((RANK_TILE, LANES), BF16)

    def tile(j, _):
        counts = cnt_ref[j]
        for e in range(N_EXPERTS):
            prev = (b * N_EXPERTS + e) * n_tiles + j - 1
            start = jnp.where(j > 0, cend_ref[jnp.maximum(prev, 0)], 0)
            local = counts[e:e + 1, :] - start.astype(F32)
            le = jnp.where(local <= rank, 1.0, 0.0).astype(BF16)
            pos = jnp.dot(le, ones, preferred_element_type=F32) + (j * RANK_TILE).astype(F32)
            bufs[e][pl.ds(start, RANK_TILE), :] = pos
        return 0

    lax.fori_loop(0, n_tiles, tile, 0)
    lane = lax.broadcasted_iota(I32, (1, LANES), 1)
    out = jnp.zeros((cap, LANES), F32)
    for e in range(N_EXPERTS):
        out = jnp.where(lane == e, bufs[e][0:cap, :], out)
    idx_ref[0] = out.astype(I32)


def _route_rank(cend_flat, cnt, n_req, seq, cap):
    n_tiles = seq // RANK_TILE
    grid_spec = pltpu.PrefetchScalarGridSpec(
        num_scalar_prefetch=1,
        grid=(n_req,),
        in_specs=[pl.BlockSpec((n_tiles, N_EXPERTS, RANK_TILE), lambda b, c: (b, 0, 0))],
        out_specs=pl.BlockSpec((1, cap, LANES), lambda b, c: (b, 0, 0)),
        scratch_shapes=[pltpu.VMEM((cap + RANK_TILE, LANES), F32)] * N_EXPERTS,
    )
    return pl.pallas_call(
        functools.partial(_route_rank_body, seq=seq, cap=cap),
        grid_spec=grid_spec,
        out_shape=jax.ShapeDtypeStruct((n_req, cap, LANES), I32),
        compiler_params=_params(1),
        name=f"route_rank_{seq}",
    )(cend_flat, cnt)


def _moe_body(idx_ref, u2f_ref, aff_ref, wg_ref, wu_ref, wd_ref, acc_ref, tile_ref, gate_ref, y_ref):
    p = pl.program_id(0)
    e = pl.program_id(1)

    @pl.when(e == 0)
    def _():
        acc_ref[...] = jnp.zeros_like(acc_ref)

    base = (p * N_EXPERTS + e) * ROWS

    def gather(c, _):
        for i in range(SUBLANES):
            r = c * SUBLANES + i
            t = idx_ref[base + r]
            slab = u2f_ref[pl.ds(pl.multiple_of(t * CHUNKS, CHUNKS), CHUNKS), :]
            tile_ref[pl.ds(r, CHUNKS, stride=TILE_PITCH), :] = slab
            gate_ref[pl.ds(r, 1), :] = aff_ref[pl.ds(t, 1), :]
        return 0

    lax.fori_loop(0, ROWS // SUBLANES, gather, 0)
    x = jnp.concatenate([tile_ref[pl.ds(c * TILE_PITCH, ROWS), :] for c in range(CHUNKS)],
                        axis=-1).astype(BF16)
    lane = lax.broadcasted_iota(I32, (1, LANES), 1)
    gate = jnp.sum(jnp.where(lane == e, gate_ref[...], 0.0), axis=-1, keepdims=True)

    hg = jnp.dot(x, wg_ref[0, 0].astype(BF16), preferred_element_type=F32)
    hu = jnp.dot(x, wu_ref[0, 0].astype(BF16), preferred_element_type=F32)
    hidden = (hg * jax.nn.sigmoid(hg) * hu).astype(BF16)
    y = jnp.dot(hidden, wd_ref[0, 0].astype(BF16), preferred_element_type=F32) * gate
    for c in range(CHUNKS):
        y_ref[pl.ds(c, ROWS, stride=CHUNKS), :] = y[:, c * LANES:(c + 1) * LANES]

    def scatter(c, _):
        r0 = c * SCATTER_UNROLL
        dst = [pl.ds(pl.multiple_of(idx_ref[base + r0 + i] * CHUNKS, CHUNKS), CHUNKS)
               for i in range(SCATTER_UNROLL)]
        vals = [acc_ref[dst[i], :]
                + y_ref[pl.ds(pl.multiple_of((r0 + i) * CHUNKS, CHUNKS), CHUNKS), :]
                for i in range(SCATTER_UNROLL)]
        for i in range(SCATTER_UNROLL):
            acc_ref[dst[i], :] = vals[i]
        return 0

    lax.fori_loop(0, ROWS // SCATTER_UNROLL, scatter, 0)


def _moe(idx_flat, u2f, aff, w_gate, w_up, w_down, layer):
    one = pl.Buffered(1)
    grid_spec = pltpu.PrefetchScalarGridSpec(
        num_scalar_prefetch=1,
        grid=(N_PASS, N_EXPERTS),
        in_specs=[
            pl.BlockSpec((PASS_TOK * CHUNKS, LANES), lambda p, e, idx: (p, 0), pipeline_mode=one),
            pl.BlockSpec((PASS_TOK, LANES), lambda p, e, idx: (p, 0), pipeline_mode=one),
            pl.BlockSpec((1, 1, D_MODEL, EXPERT_DIM), lambda p, e, idx: (layer, e, 0, 0)),
            pl.BlockSpec((1, 1, D_MODEL, EXPERT_DIM), lambda p, e, idx: (layer, e, 0, 0)),
            pl.BlockSpec((1, 1, EXPERT_DIM, D_MODEL), lambda p, e, idx: (layer, e, 0, 0)),
        ],
        out_specs=pl.BlockSpec((PASS_TOK * CHUNKS, LANES), lambda p, e, idx: (p, 0),
                               pipeline_mode=one),
        scratch_shapes=[
            pltpu.VMEM((CHUNKS * TILE_PITCH, LANES), F32),
            pltpu.VMEM((ROWS, LANES), F32),
            pltpu.VMEM((ROWS * CHUNKS, LANES), F32),
        ],
    )
    return pl.pallas_call(
        _moe_body,
        grid_spec=grid_spec,
        out_shape=jax.ShapeDtypeStruct((N_TOK * CHUNKS, LANES), F32),
        compiler_params=_params(2, arbitrary=True),
        name="moe",
    )(idx_flat, u2f, aff, w_gate, w_up, w_down)


def _final_body(x1_ref, ff_ref, g2_ref, lg_ref, lb_ref, o_ref):
    ffn = jnp.concatenate([ff_ref[pl.ds(c, TM, stride=CHUNKS), :] for c in range(CHUNKS)], axis=-1)
    o_ref[...] = _layer_norm(ALPHA * x1_ref[...] + g2_ref[0] * ffn, lg_ref[...], lb_ref[...])


def _final(x1, ff, mod3, ln_g, ln_b, row0=0, n_rows=N_TOK):
    t0 = row0 // TM
    row = lambda i: (i + t0, 0)
    return pl.pallas_call(
        _final_body,
        grid=(n_rows // TM,),
        in_specs=[
            pl.BlockSpec((TM, D_MODEL), row),
            pl.BlockSpec((TM * CHUNKS, LANES), row),
            pl.BlockSpec((1, 1, D_MODEL), lambda i: (_mod_row(i + t0, TM), 0, 5)),
            _full((1, D_MODEL)), _full((1, D_MODEL)),
        ],
        out_specs=pl.BlockSpec((TM, D_MODEL), lambda i: (i, 0)),
        out_shape=jax.ShapeDtypeStruct((n_rows, D_MODEL), F32),
        compiler_params=_params(1),
        name="final_ln",
    )(x1, ff, mod3, ln_g, ln_b)


def _rot_cols(w):
    half = ROPE // 2
    return jnp.concatenate([-w[..., half:], w[..., :half]], axis=-1)


def _pad_lanes(w):
    pad = [(0, 0)] * (w.ndim - 1) + [(0, LANES - w.shape[-1])]
    return jnp.pad(w, pad)


def _rope_tables():
    rows_n = LAT_T // GRID_W
    r, cl = jnp.meshgrid(jnp.arange(rows_n, dtype=F32), jnp.arange(GRID_W, dtype=F32), indexing="ij")
    inv = ROPE_THETA ** (-jnp.arange(0, ROPE // 2, 2, dtype=F32) / (ROPE // 2))
    ang = jnp.concatenate([r.reshape(-1)[:, None] * inv, cl.reshape(-1)[:, None] * inv], axis=-1)
    cos, sin = jnp.cos(ang), jnp.sin(ang)
    cos_lat = _pad_lanes(jnp.concatenate([cos, cos], axis=-1))
    sin_lat = _pad_lanes(jnp.concatenate([sin, sin], axis=-1))
    cos_ctx = _pad_lanes(jnp.ones((N_CTX, ROPE), F32))
    sin_ctx = jnp.zeros((N_CTX, LANES), F32)
    cos_a = jnp.concatenate([cos_ctx] + [cos_lat] * LAT_B, axis=0)
    sin_a = jnp.concatenate([sin_ctx] + [sin_lat] * LAT_B, axis=0)
    return cos_a, sin_a


def kernel(x_prompt, x_sample, cache_ckv, cache_krope, c, c_ctx, w_in, q_norm, w_uq, kv_norm, w_uk, w_uv,
           pool_w, pool_scale, conv_w, w_out, w_ada, b_ada, ln1_g, ln1_b, ln2_g, ln2_b, w_router,
           w_gate, w_up, w_down):
    o_q, o_kv, o_kr, o_mix = 0, Q_RANK, Q_RANK + KV_RANK, Q_RANK + KV_RANK + ROPE
    w_kr = w_in[:, :, o_kr:o_kr + ROPE]
    w_in_ext = jnp.concatenate(
        [w_in[:, :, o_q:o_kr], w_in[:, :, o_mix:], _pad_lanes(w_kr), _pad_lanes(_rot_cols(w_kr))],
        axis=-1).astype(BF16)
    uq = w_uq.reshape(DEPTH, Q_RANK, HEADS, NOPE + ROPE)
    uq_rope = uq[..., NOPE:]
    w_uq_ext = jnp.concatenate(
        [uq[..., :NOPE].reshape(DEPTH, Q_RANK, HEADS * NOPE),
         _pad_lanes(uq_rope).reshape(DEPTH, Q_RANK, HEADS * LANES),
         _pad_lanes(_rot_cols(uq_rope)).reshape(DEPTH, Q_RANK, HEADS * LANES)], axis=-1).astype(BF16)
    w_uk_b = w_uk.astype(BF16)
    w_uv_b = w_uv.astype(BF16)
    eye = jnp.eye(len(POOL_WINDOWS), dtype=F32)
    pool_bd = (pool_w[:, :, :, None, :] * eye[None, :, None, :, None]).reshape(DEPTH, MIXW, MIXW).astype(BF16)
    w_out_b = w_out.astype(BF16)
    w_router_pad = _pad_lanes(w_router).astype(BF16)
    cos_a, sin_a = _rope_tables()

    cond = jnp.concatenate([c_ctx[None, :], c, jnp.zeros((SUBLANES - 1 - LAT_B, D_MODEL), F32)], axis=0)
    mod = _ada(cond, w_ada, b_ada)
    kc, vc = _cachekv(cache_ckv, _pad_lanes(cache_krope), w_uk_b, w_uv_b)

    xs = (x_prompt.reshape(N_CTX, D_MODEL), x_sample.reshape(N_LAT, D_MODEL), 0)
    ckv_layers, kr_layers = [], []
    ctx_off = (jnp.arange(CTX_B, dtype=I32) * CTX_T)[:, None, None]
    for l in range(DEPTH):
        mod3 = mod[l].reshape(SUBLANES, 1, 6 * D_MODEL)
        ckv, kr, q, k, v, mix = _inproj(*xs, mod3, w_in_ext[l], q_norm[l][None], kv_norm[l][None],
                                        w_uq_ext[l], w_uk_b[l], w_uv_b[l], cos_a, sin_a)
        ckv_layers.append(ckv[:N_CTX].reshape(CTX_B, CTX_T, KV_RANK))
        kr_layers.append(kr[:N_CTX, :ROPE].reshape(CTX_B, CTX_T, ROPE))
        bc = _seqmix(mix, pool_bd[l], pool_scale[l][None], conv_w[l])
        a_ctx = _attn_ctx(q, k, v)
        a_lat = _attn_lat(q, k, v, kc, vc, l)
        x1, u2f, aff = _postmix(a_ctx, a_lat, bc, *xs, mod3, w_out_b[l, :HEADS * VDIM],
                                w_out_b[l, HEADS * VDIM:], ln1_g[l][None], ln1_b[l][None],
                                w_router_pad[l])
        idx_ctx = _route(aff, 0, CTX_B, CTX_T, CTX_CAP)
        cnt_lat, cend = _route_count(aff, N_CTX // LAT_T, LAT_B, LAT_T, LAT_CAP)
        cend_flat = cend[:, :, :N_EXPERTS].transpose(0, 2, 1).reshape(-1)
        idx_lat = _route_rank(cend_flat, cnt_lat, LAT_B, LAT_T, LAT_CAP)
        idx_lat = idx_lat[:, :, :N_EXPERTS].transpose(0, 2, 1)
        idx_ctx = (idx_ctx + ctx_off).transpose(1, 0, 2).reshape(1, N_EXPERTS, ROWS)
        idx_flat = jnp.concatenate([idx_ctx, idx_lat], axis=0).reshape(-1)
        ff = _moe(idx_flat, u2f, aff, w_gate, w_up, w_down, l)
        if l + 1 < DEPTH:
            x = _final(x1, ff, mod3, ln2_g[l][None], ln2_b[l][None])
            xs = (x, x, N_CTX)
        else:
            y_ctx = _final(x1, ff, mod3, ln2_g[l][None], ln2_b[l][None], 0, N_CTX)
            y_lat = _final(x1, ff, mod3, ln2_g[l][None], ln2_b[l][None], N_CTX, N_LAT)

    y_prompt = y_ctx.reshape(CTX_B, CTX_T, D_MODEL)
    y_sample = y_lat.reshape(LAT_B, LAT_T, D_MODEL)
    new_ckv = jnp.stack(ckv_layers, axis=1)
    new_krope = jnp.stack(kr_layers, axis=1)
    return (y_prompt, y_sample, new_ckv, new_krope)
```

```python
import functools

import jax
import jax.numpy as jnp
from jax import lax
from jax.experimental import pallas as pl
from jax.experimental.pallas import tpu as pltpu

F32 = jnp.float32
BF16 = jnp.bfloat16
I32 = jnp.int32

D_MODEL = 1024
CTX_B, CTX_T = 16, 256
LAT_B, LAT_T = 2, 4096
DEPTH = 4
PAST_LEN = 512
GRID_W = 64
HEADS = 4
NOPE, ROPE, VDIM = 128, 64, 128
Q_RANK, KV_RANK = 384, 256
POOL_WINDOWS = (2, 4, 8, 16)
POOL_GROUP = 64
MIXW = 256
N_EXPERTS = 16
EXPERT_DIM = 512
ROPE_THETA = 10000.0
ATTN_SCALE = (NOPE + ROPE) ** -0.5
LOG2E = 1.4426950408889634
Q_SCALE = ATTN_SCALE * LOG2E
ALPHA = (2 * DEPTH) ** 0.25
RMS_EPS = 1e-6
LN_EPS = 1e-5

N_CTX = CTX_B * CTX_T
N_LAT = LAT_B * LAT_T
N_TOK = N_CTX + N_LAT
CTX_CAP = 2 * CTX_T // N_EXPERTS
LAT_CAP = 2 * LAT_T // N_EXPERTS

LANES = 128
SUBLANES = 8
CHUNKS = D_MODEL // LANES
HEAD_BLOCK = 2 * LANES
VMEM_LIMIT = 56 * 1024 * 1024

TM = 512
SEQ_TILE = 256
HALO = 8
TQ = 512
KV_CHUNK = 512
CHUNK_UNROLL = 8
PASS_TOK = 4096
N_PASS = N_TOK // PASS_TOK
ROWS = 512
TILE_PITCH = ROWS + 8
SCATTER_UNROLL = 8
WORDS = CHUNKS // 2
ROW_BLOCK = 256


def _params(n_axes, arbitrary=False):
    sem = ("arbitrary" if arbitrary else "parallel",) * n_axes
    return pltpu.CompilerParams(dimension_semantics=sem, vmem_limit_bytes=VMEM_LIMIT)


def _mod_row(i, tile):
    start = i * tile
    return jnp.where(start < N_CTX, 0, 1 + (start - N_CTX) // LAT_T)


def _full(shape):
    nd = len(shape)
    return pl.BlockSpec(shape, lambda *_: (0,) * nd)


def _pair_specs(tile, width, lat_row0):
    nct = N_CTX // tile
    off = lat_row0 // tile
    return [pl.BlockSpec((tile, width), lambda i: (jnp.minimum(i, nct - 1), 0)),
            pl.BlockSpec((tile, width), lambda i: (jnp.maximum(i - nct, 0) + off, 0))]


def _pair_load(tile, ctx_ref, lat_ref):
    return jnp.where(pl.program_id(0) < N_CTX // tile, ctx_ref[...], lat_ref[...])


ADA_TN = 1536


def _ada_body(cond_ref, w_ref, b_ref, o_ref):
    c = cond_ref[...]
    a = c * jax.nn.sigmoid(c)
    a_hi = a.astype(BF16)
    a_lo = (a - a_hi.astype(F32)).astype(BF16)
    w = w_ref[0]
    w_hi = w.astype(BF16)
    w_lo = (w - w_hi.astype(F32)).astype(BF16)
    acc = jnp.dot(a_hi, w_hi, preferred_element_type=F32)
    acc += jnp.dot(a_lo, w_hi, preferred_element_type=F32)
    acc += jnp.dot(a_hi, w_lo, preferred_element_type=F32)
    o_ref[0] = acc + b_ref[0]


def _ada(cond, w_ada, b_ada):
    n = 6 * D_MODEL
    return pl.pallas_call(
        _ada_body,
        grid=(DEPTH, n // ADA_TN),
        in_specs=[
            _full((SUBLANES, D_MODEL)),
            pl.BlockSpec((1, D_MODEL, ADA_TN), lambda l, j: (l, 0, j)),
            pl.BlockSpec((1, 1, ADA_TN), lambda l, j: (l, 0, j)),
        ],
        out_specs=pl.BlockSpec((1, SUBLANES, ADA_TN), lambda l, j: (l, 0, j)),
        out_shape=jax.ShapeDtypeStruct((DEPTH, SUBLANES, n), F32),
        compiler_params=_params(2),
        name="ada_mod",
    )(cond, w_ada, b_ada.reshape(DEPTH, 1, n))


IN_EXT = Q_RANK + KV_RANK + 4 * MIXW + 2 * LANES
UQ_EXT = HEADS * NOPE + 2 * HEADS * LANES


def _rms(x, g):
    return x * lax.rsqrt(jnp.mean(x * x, axis=-1, keepdims=True) + RMS_EPS) * g


def _inproj_body(xc_ref, xl_ref, sh_ref, sc_ref, win_ref, qn_ref, kvn_ref, wuq_ref, wuk_ref, wuv_ref,
                 cos_ref, sin_ref, ckv_ref, kr_ref, q_ref, k_ref, v_ref, mix_ref):
    u = _pair_load(TM, xc_ref, xl_ref) * (1.0 + sc_ref[0]) + sh_ref[0]
    h = jnp.dot(u.astype(BF16), win_ref[...], preferred_element_type=F32)
    o = 0
    q_c = h[:, o:o + Q_RANK]; o += Q_RANK
    kv_c = h[:, o:o + KV_RANK]; o += KV_RANK
    mix_ref[...] = h[:, o:o + 4 * MIXW]; o += 4 * MIXW
    kr_a = h[:, o:o + LANES]; o += LANES
    kr_b = h[:, o:o + LANES]
    kr_ref[...] = kr_a
    cos_a = cos_ref[...]
    sin_a = sin_ref[...]

    ckv = _rms(kv_c, kvn_ref[...])
    ckv_ref[...] = ckv
    ckv_b = ckv.astype(BF16)
    k_nope = jnp.dot(ckv_b, wuk_ref[...], preferred_element_type=F32)
    v_ref[...] = jnp.dot(ckv_b, wuv_ref[...], preferred_element_type=F32).astype(BF16)
    k_rope = kr_a * cos_a + kr_b * sin_a
    parts = []
    for hd in range(HEADS):
        parts += [k_nope[:, hd * NOPE:(hd + 1) * NOPE], k_rope]
    k_ref[...] = jnp.concatenate(parts, axis=-1).astype(BF16)

    qn = _rms(q_c, qn_ref[...]).astype(BF16)
    qq = jnp.dot(qn, wuq_ref[...], preferred_element_type=F32)
    ra = HEADS * NOPE
    rb = ra + HEADS * LANES
    parts = []
    for hd in range(HEADS):
        q_rope = (qq[:, ra + hd * LANES:ra + (hd + 1) * LANES] * cos_a
                  + qq[:, rb + hd * LANES:rb + (hd + 1) * LANES] * sin_a)
        parts += [qq[:, hd * NOPE:(hd + 1) * NOPE] * Q_SCALE, q_rope * Q_SCALE]
    q_ref[...] = jnp.concatenate(parts, axis=-1).astype(BF16)


def _inproj(xc, xl, lat_row0, mod3, w_in_ext, q_norm, kv_norm, w_uq_ext, w_uk, w_uv, cos_a, sin_a):
    row = lambda i: (i, 0)
    modspec = lambda k: pl.BlockSpec((1, 1, D_MODEL), lambda i: (_mod_row(i, TM), 0, k))
    outs = (
        jax.ShapeDtypeStruct((N_TOK, KV_RANK), F32),
        jax.ShapeDtypeStruct((N_TOK, LANES), F32),
        jax.ShapeDtypeStruct((N_TOK, HEADS * HEAD_BLOCK), BF16),
        jax.ShapeDtypeStruct((N_TOK, HEADS * HEAD_BLOCK), BF16),
        jax.ShapeDtypeStruct((N_TOK, HEADS * VDIM), BF16),
        jax.ShapeDtypeStruct((N_TOK, 4 * MIXW), F32),
    )
    return pl.pallas_call(
        _inproj_body,
        grid=(N_TOK // TM,),
        in_specs=_pair_specs(TM, D_MODEL, lat_row0) + [
            modspec(0), modspec(1),
            _full((D_MODEL, IN_EXT)),
            _full((1, Q_RANK)), _full((1, KV_RANK)),
            _full((Q_RANK, UQ_EXT)),
            _full((KV_RANK, HEADS * NOPE)), _full((KV_RANK, HEADS * VDIM)),
            pl.BlockSpec((TM, LANES), row), pl.BlockSpec((TM, LANES), row),
        ],
        out_specs=tuple(pl.BlockSpec((TM, s.shape[1]), row) for s in outs),
        out_shape=outs,
        compiler_params=_params(1),
        name="inproj",
    )(xc, xl, mod3, mod3, w_in_ext, q_norm, kv_norm, w_uq_ext, w_uk, w_uv, cos_a, sin_a)


def _cachekv_body(ckv_ref, kr_ref, wuk_ref, wuv_ref, k_ref, v_ref):
    c = ckv_ref[0, 0].astype(BF16)
    k_nope = jnp.dot(c, wuk_ref[0], preferred_element_type=F32)
    v_ref[0, 0] = jnp.dot(c, wuv_ref[0], preferred_element_type=F32).astype(BF16)
    kr = kr_ref[0, 0]
    parts = []
    for hd in range(HEADS):
        parts += [k_nope[:, hd * NOPE:(hd + 1) * NOPE], kr]
    k_ref[0, 0] = jnp.concatenate(parts, axis=-1).astype(BF16)


def _cachekv(cache_ckv, kr_pad, w_uk, w_uv):
    bl = lambda b, l: (b, l, 0, 0)
    wl = lambda b, l: (l, 0, 0)
    return pl.pallas_call(
        _cachekv_body,
        grid=(LAT_B, DEPTH),
        in_specs=[
            pl.BlockSpec((1, 1, PAST_LEN, KV_RANK), bl),
            pl.BlockSpec((1, 1, PAST_LEN, LANES), bl),
            pl.BlockSpec((1, KV_RANK, HEADS * NOPE), wl),
            pl.BlockSpec((1, KV_RANK, HEADS * VDIM), wl),
        ],
        out_specs=(pl.BlockSpec((1, 1, PAST_LEN, HEADS * HEAD_BLOCK), bl),
                   pl.BlockSpec((1, 1, PAST_LEN, HEADS * VDIM), bl)),
        out_shape=(jax.ShapeDtypeStruct((LAT_B, DEPTH, PAST_LEN, HEADS * HEAD_BLOCK), BF16),
                   jax.ShapeDtypeStruct((LAT_B, DEPTH, PAST_LEN, HEADS * VDIM), BF16)),
        compiler_params=_params(2),
        name="cache_kv",
    )(cache_ckv, kr_pad, w_uk, w_uv)


EXT = SEQ_TILE + 2 * HALO


def _shift_up(x, k):
    return pltpu.roll(x, x.shape[0] - k, axis=0)


def _shift_down(x, k):
    return pltpu.roll(x, k, axis=0)


def _seqmix_body(main_ref, prev_ref, next_ref, pw_ref, ps_ref, cw_ref, o_ref):
    j = pl.program_id(0)
    n_ctx_tiles = N_CTX // SEQ_TILE
    tiles_per_lat = LAT_T // SEQ_TILE
    is_ctx = j < n_ctx_tiles
    jj = jnp.where(is_ctx, 0, (j - n_ctx_tiles) % tiles_per_lat)
    first = jj == 0
    last = jnp.where(is_ctx, True, jj == tiles_per_lat - 1)
    t_seq = jnp.where(is_ctx, CTX_T, LAT_T)

    main = main_ref[...]
    prev = jnp.where(first, 0.0, prev_ref[...])
    nxt = jnp.where(last, 0.0, next_ref[...])
    ext = jnp.concatenate([prev, main, nxt], axis=0)
    p = ext[:, 0:MIXW]
    g_b = main[:, MIXW:2 * MIXW]
    g_c = ext[:, 2 * MIXW:3 * MIXW]
    h_in = ext[:, 3 * MIXW:4 * MIXW]

    sums = {}
    b = p
    for w in POOL_WINDOWS:
        b = b + _shift_up(b, w // 2)
        sums[w] = _shift_down(b, w // 2)[HALO:HALO + SEQ_TILE]
    tpos = jj * SEQ_TILE + lax.broadcasted_iota(I32, (SEQ_TILE, 1), 0)
    lane = lax.broadcasted_iota(I32, (1, MIXW), 1)
    num = None
    den = None
    for g, w in enumerate(POOL_WINDOWS):
        lo = jnp.maximum(tpos - w // 2, 0)
        hi = jnp.minimum(tpos + w - w // 2, t_seq)
        cnt = (hi - lo).astype(F32)
        if num is None:
            num, den = sums[w], jnp.broadcast_to(cnt, (SEQ_TILE, MIXW))
        else:
            sel = lane >= g * POOL_GROUP
            num = jnp.where(sel, sums[w], num)
            den = jnp.where(sel, cnt, den)
    pooled = num / den - p[HALO:HALO + SEQ_TILE]
    b_out = jnp.dot(pooled.astype(BF16), pw_ref[...], preferred_element_type=F32) * ps_ref[...]

    y = g_c * h_in
    cw = cw_ref[...]
    conv = (_shift_down(y, 1) * cw[0:1] + y * cw[1:2] + _shift_up(y, 1) * cw[2:3])
    c_out = g_b * conv[HALO:HALO + SEQ_TILE]
    o_ref[...] = jnp.concatenate([b_out, c_out], axis=-1).astype(BF16)


def _seqmix(mix, pool_bd, pool_scale, conv_w):
    per = SEQ_TILE // HALO
    nblk = N_TOK // HALO
    return pl.pallas_call(
        _seqmix_body,
        grid=(N_TOK // SEQ_TILE,),
        in_specs=[
            pl.BlockSpec((SEQ_TILE, 4 * MIXW), lambda j: (j, 0)),
            pl.BlockSpec((HALO, 4 * MIXW), lambda j: (jnp.maximum(j * per - 1, 0), 0)),
            pl.BlockSpec((HALO, 4 * MIXW), lambda j: (jnp.minimum((j + 1) * per, nblk - 1), 0)),
            _full((MIXW, MIXW)), _full((1, MIXW)), _full((3, MIXW)),
        ],
        out_specs=pl.BlockSpec((SEQ_TILE, 2 * MIXW), lambda j: (j, 0)),
        out_shape=jax.ShapeDtypeStruct((N_TOK, 2 * MIXW), BF16),
        compiler_params=_params(1),
        name="seqmix",
    )(mix, mix, mix, pool_bd, pool_scale, conv_w)


_NT = (((1,), (1,)), ((), ()))


def _v_ext(v):
    return jnp.concatenate([v, jnp.ones(v.shape, v.dtype)], axis=-1)


def _attn_ctx_body(q_ref, k_ref, v_ref, o_ref):
    outs = []
    for hd in range(HEADS):
        qh = q_ref[:, hd * HEAD_BLOCK:(hd + 1) * HEAD_BLOCK]
        kh = k_ref[:, hd * HEAD_BLOCK:(hd + 1) * HEAD_BLOCK]
        vh = v_ref[:, hd * VDIM:(hd + 1) * VDIM]
        s = lax.dot_general(qh, kh, _NT, preferred_element_type=F32)
        m = jnp.max(s, axis=-1, keepdims=True)
        p = jnp.exp2(s - m)
        acc = jnp.dot(p.astype(BF16), _v_ext(vh), preferred_element_type=F32)
        outs.append(acc[:, :VDIM] / acc[:, VDIM:])
    o_ref[...] = jnp.concatenate(outs, axis=-1).astype(BF16)


def _attn_ctx(q, k, v):
    row = lambda b: (b, 0)
    return pl.pallas_call(
        _attn_ctx_body,
        grid=(CTX_B,),
        in_specs=[pl.BlockSpec((CTX_T, HEADS * HEAD_BLOCK), row),
                  pl.BlockSpec((CTX_T, HEADS * HEAD_BLOCK), row),
                  pl.BlockSpec((CTX_T, HEADS * VDIM), row)],
        out_specs=pl.BlockSpec((CTX_T, HEADS * VDIM), row),
        out_shape=jax.ShapeDtypeStruct((N_CTX, HEADS * VDIM), BF16),
        compiler_params=_params(1),
        name="attn_ctx",
    )(q, k, v)


def _attn_lat_body(q_ref, kc_ref, vc_ref, ko_ref, vo_ref, o_ref, m_ref, acc_ref):
    reps = KV_CHUNK // LANES

    def update(hd, k, v, first):
        qh = q_ref[:, hd * HEAD_BLOCK:(hd + 1) * HEAD_BLOCK]
        s = lax.dot_general(qh, k, _NT, preferred_element_type=F32)
        mx = jnp.max(s, axis=-1, keepdims=True)
        if first:
            m_new = jnp.broadcast_to(mx, (TQ, LANES))
        else:
            m_old = m_ref[hd]
            m_new = jnp.maximum(m_old, mx)
        p = jnp.exp2(s - jnp.concatenate([m_new] * reps, axis=-1))
        pv = jnp.dot(p.astype(BF16), _v_ext(v), preferred_element_type=F32)
        if first:
            acc_ref[hd] = pv
        else:
            a = jnp.exp2(m_old - m_new)
            acc_ref[hd] = acc_ref[hd] * jnp.concatenate([a, a], axis=-1) + pv
        m_ref[hd] = m_new

    def head_slices(hd):
        return slice(hd * HEAD_BLOCK, (hd + 1) * HEAD_BLOCK), slice(hd * VDIM, (hd + 1) * VDIM)

    for hd in range(HEADS):
        ks, vs = head_slices(hd)
        update(hd, kc_ref[0, 0, :, ks], vc_ref[0, 0, :, vs], True)

    def body(c, _):
        for j in range(CHUNK_UNROLL):
            rows = pl.ds(pl.multiple_of((c * CHUNK_UNROLL + j) * KV_CHUNK, KV_CHUNK), KV_CHUNK)
            for hd in range(HEADS):
                ks, vs = head_slices(hd)
                update(hd, ko_ref[rows, ks], vo_ref[rows, vs], False)
        return 0

    lax.fori_loop(0, LAT_T // (KV_CHUNK * CHUNK_UNROLL), body, 0)
    outs = [acc_ref[hd][:, :VDIM] / acc_ref[hd][:, VDIM:] for hd in range(HEADS)]
    o_ref[...] = jnp.concatenate(outs, axis=-1).astype(BF16)


def _attn_lat(q, k, v, kc, vc, layer):
    qt = LAT_T // TQ
    ctx_tiles = N_CTX // TQ
    ctx_blocks = N_CTX // LAT_T
    return pl.pallas_call(
        _attn_lat_body,
        grid=(LAT_B, qt),
        in_specs=[
            pl.BlockSpec((TQ, HEADS * HEAD_BLOCK), lambda b, i: (ctx_tiles + b * qt + i, 0)),
            pl.BlockSpec((1, 1, PAST_LEN, HEADS * HEAD_BLOCK), lambda b, i: (b, layer, 0, 0)),
            pl.BlockSpec((1, 1, PAST_LEN, HEADS * VDIM), lambda b, i: (b, layer, 0, 0)),
            pl.BlockSpec((LAT_T, HEADS * HEAD_BLOCK), lambda b, i: (ctx_blocks + b, 0)),
            pl.BlockSpec((LAT_T, HEADS * VDIM), lambda b, i: (ctx_blocks + b, 0)),
        ],
        out_specs=pl.BlockSpec((TQ, HEADS * VDIM), lambda b, i: (b * qt + i, 0)),
        out_shape=jax.ShapeDtypeStruct((N_LAT, HEADS * VDIM), BF16),
        scratch_shapes=[pltpu.VMEM((HEADS, TQ, LANES), F32),
                        pltpu.VMEM((HEADS, TQ, 2 * VDIM), F32)],
        compiler_params=_params(2),
        name="attn_lat",
    )(q, kc, vc, k, v)


def _layer_norm(y, g, b):
    mu = jnp.mean(y, axis=-1, keepdims=True)
    d = y - mu
    var = jnp.mean(d * d, axis=-1, keepdims=True)
    return d * lax.rsqrt(var + LN_EPS) * g + b


def _postmix_body(actx_ref, alat_ref, bc_ref, xc_ref, xl_ref, g1_ref, sh2_ref, sc2_ref, wa_ref, wbc_ref,
                  lg_ref, lb_ref, wr_ref, x1_ref, u2p_ref, aff_ref):
    a = _pair_load(TM, actx_ref, alat_ref)
    mix = jnp.dot(a, wa_ref[...], preferred_element_type=F32)
    mix += jnp.dot(bc_ref[...], wbc_ref[...], preferred_element_type=F32)
    x = _pair_load(TM, xc_ref, xl_ref)
    x1 = _layer_norm(ALPHA * x + g1_ref[0] * mix, lg_ref[...], lb_ref[...])
    x1_ref[...] = x1
    u2 = x1 * (1.0 + sc2_ref[0]) + sh2_ref[0]
    for c in range(WORDS):
        lo = u2[:, c * LANES:(c + 1) * LANES]
        hi = u2[:, (c + WORDS) * LANES:(c + WORDS + 1) * LANES]
        u2p_ref[pl.ds(c, TM, stride=WORDS), :] = pltpu.pack_elementwise([lo, hi], packed_dtype=BF16)
    logits = jnp.dot(u2.astype(BF16), wr_ref[...], preferred_element_type=F32)
    lane = lax.broadcasted_iota(I32, (1, LANES), 1)
    logits = jnp.where(lane < N_EXPERTS, logits, -jnp.inf)
    m = jnp.max(logits, axis=-1, keepdims=True)
    e = jnp.exp(logits - m)
    aff_ref[...] = e / jnp.sum(e, axis=-1, keepdims=True)


def _postmix(a_ctx, a_lat, bc, xc, xl, lat_row0, mod3, w_out_a, w_out_bc, ln_g, ln_b, w_router_pad):
    row = lambda i: (i, 0)
    modspec = lambda k: pl.BlockSpec((1, 1, D_MODEL), lambda i: (_mod_row(i, TM), 0, k))
    return pl.pallas_call(
        _postmix_body,
        grid=(N_TOK // TM,),
        in_specs=_pair_specs(TM, HEADS * VDIM, 0) + [
            pl.BlockSpec((TM, 2 * MIXW), row),
        ] + _pair_specs(TM, D_MODEL, lat_row0) + [
            modspec(2), modspec(3), modspec(4),
            _full((HEADS * VDIM, D_MODEL)), _full((2 * MIXW, D_MODEL)),
            _full((1, D_MODEL)), _full((1, D_MODEL)),
            _full((D_MODEL, LANES)),
        ],
        out_specs=(pl.BlockSpec((TM, D_MODEL), row),
                   pl.BlockSpec((TM * WORDS, LANES), row),
                   pl.BlockSpec((TM, LANES), row)),
        out_shape=(jax.ShapeDtypeStruct((N_TOK, D_MODEL), F32),
                   jax.ShapeDtypeStruct((N_TOK * WORDS, LANES), jnp.uint32),
                   jax.ShapeDtypeStruct((N_TOK, LANES), F32)),
        compiler_params=_params(1),
        name="postmix",
    )(a_ctx, a_lat, bc, xc, xl, mod3, mod3, mod3, w_out_a, w_out_bc, ln_g, ln_b, w_router_pad)


PREFIX_CHUNK = 256


def _route_body(aff_ref, idx_ref, cnt_ref, *, seq, cap):
    n_chunks = seq // PREFIX_CHUNK
    aff = aff_ref[...]

    def search(i, thr):
        cand = thr | jnp.left_shift(jnp.int32(1), 30 - i)
        n = jnp.sum((aff >= pltpu.bitcast(cand, F32)).astype(I32), axis=0, keepdims=True)
        return jnp.where(n >= cap, cand, thr)

    thr_bits = lax.fori_loop(0, 31, search, jnp.zeros((1, LANES), I32))
    thr = pltpu.bitcast(thr_bits, F32)
    above = pltpu.bitcast(thr_bits + 1, F32)
    n_gt = jnp.sum(((aff > thr) & (aff >= above)).astype(I32), axis=0, keepdims=True)
    need = (cap - n_gt).astype(F32)

    ri = lax.broadcasted_iota(I32, (PREFIX_CHUNK, PREFIX_CHUNK), 0)
    ci = lax.broadcasted_iota(I32, (PREFIX_CHUNK, PREFIX_CHUNK), 1)
    tri = jnp.where(ci <= ri, 1.0, 0.0).astype(BF16)
    carry_eq = jnp.zeros((1, LANES), F32)
    carry_sel = jnp.zeros((1, LANES), F32)
    for c in range(n_chunks):
        rows = slice(c * PREFIX_CHUNK, (c + 1) * PREFIX_CHUNK)
        a = aff_ref[rows, :]
        gt = (a > thr) & (a >= above)
        eq = (a >= thr) & jnp.logical_not(gt)
        p_eq = jnp.dot(tri, jnp.where(eq, 1.0, 0.0).astype(BF16),
                       preferred_element_type=F32) + carry_eq
        sel = jnp.where(gt, 1.0, jnp.where(eq & (p_eq <= need), 1.0, 0.0))
        p_sel = jnp.dot(tri, sel.astype(BF16), preferred_element_type=F32) + carry_sel
        cnt_ref[rows, :] = p_sel
        carry_eq = p_eq[PREFIX_CHUNK - 1:PREFIX_CHUNK, :]
        carry_sel = p_sel[PREFIX_CHUNK - 1:PREFIX_CHUNK, :]

    rank = lax.broadcasted_iota(I32, (1, cap), 1).astype(F32)
    for e in range(N_EXPERTS):
        def count(c, acc, e=e):
            rows = pl.ds(pl.multiple_of(c * PREFIX_CHUNK, PREFIX_CHUNK), PREFIX_CHUNK)
            col = cnt_ref[rows, e:e + 1]
            le = jnp.where(col <= rank, 1.0, 0.0)
            return acc + jnp.sum(le.reshape(PREFIX_CHUNK // SUBLANES, SUBLANES, cap), axis=0)

        acc = lax.fori_loop(0, n_chunks, count, jnp.zeros((SUBLANES, cap), F32))
        idx_ref[0, e:e + 1, :] = jnp.sum(acc, axis=0, keepdims=True).astype(I32)


def _route(aff, first_block, n_req, seq, cap):
    return pl.pallas_call(
        functools.partial(_route_body, seq=seq, cap=cap),
        grid=(n_req,),
        in_specs=[pl.BlockSpec((seq, LANES), lambda b: (first_block + b, 0))],
        out_specs=pl.BlockSpec((1, N_EXPERTS, cap), lambda b: (b, 0, 0)),
        out_shape=jax.ShapeDtypeStruct((n_req, N_EXPERTS, cap), I32),
        scratch_shapes=[pltpu.VMEM((seq, LANES), F32)],
        compiler_params=_params(1),
        name=f"route_{seq}",
    )(aff)


RANK_TILE = LANES


def _expert_row(col):
    full = jnp.concatenate([jnp.broadcast_to(col, (N_EXPERTS, LANES)),
                            jnp.zeros((LANES - N_EXPERTS, LANES), col.dtype)], axis=0)
    return full.T[0:1]


def _route_count_body(aff_ref, cnt_ref, cend_ref, *, seq, cap):
    n_chunks = seq // PREFIX_CHUNK
    dense = aff_ref[...].T[:N_EXPERTS]

    def search(i, thr):
        cand = thr | jnp.left_shift(jnp.int32(1), 30 - i)
        n = jnp.sum((dense >= pltpu.bitcast(cand, F32)).astype(I32), axis=1, keepdims=True)
        return jnp.where(n >= cap, cand, thr)

    thr_bits = lax.fori_loop(0, 31, search, jnp.zeros((N_EXPERTS, 1), I32))
    thr_col = pltpu.bitcast(thr_bits, F32)
    above_col = pltpu.bitcast(thr_bits + 1, F32)
    n_gt = jnp.sum(((dense > thr_col) & (dense >= above_col)).astype(I32), axis=1, keepdims=True)
    thr = _expert_row(thr_col)
    above = _expert_row(above_col)
    need = _expert_row((cap - n_gt).astype(F32))

    ri = lax.broadcasted_iota(I32, (PREFIX_CHUNK, PREFIX_CHUNK), 0)
    ci = lax.broadcasted_iota(I32, (PREFIX_CHUNK, PREFIX_CHUNK), 1)
    tri = jnp.where(ci <= ri, 1.0, 0.0).astype(BF16)
    carry_eq = jnp.zeros((1, LANES), F32)
    carry_sel = jnp.zeros((1, LANES), F32)
    per = PREFIX_CHUNK // RANK_TILE
    for c in range(n_chunks):
        rows = slice(c * PREFIX_CHUNK, (c + 1) * PREFIX_CHUNK)
        a = aff_ref[rows, :]
        gt = (a > thr) & (a >= above)
        eq = (a >= thr) & jnp.logical_not(gt)
        p_eq = jnp.dot(tri, jnp.where(eq, 1.0, 0.0).astype(BF16),
                       preferred_element_type=F32) + carry_eq
        sel = jnp.where(gt, 1.0, jnp.where(eq & (p_eq <= need), 1.0, 0.0))
        p_sel = jnp.dot(tri, sel.astype(BF16), preferred_element_type=F32) + carry_sel
        dense_cnt = p_sel.T[:N_EXPERTS]
        for k in range(per):
            last = (k + 1) * RANK_TILE - 1
            cnt_ref[c * per + k] = dense_cnt[:, k * RANK_TILE:(k + 1) * RANK_TILE]
            cend_ref[0, c * per + k:c * per + k + 1, :] = p_sel[last:last + 1, :].astype(I32)
        carry_eq = p_eq[PREFIX_CHUNK - 1:PREFIX_CHUNK, :]
        carry_sel = p_sel[PREFIX_CHUNK - 1:PREFIX_CHUNK, :]


def _route_count(aff, first_block, n_req, seq, cap):
    n_tiles = seq // RANK_TILE
    return pl.pallas_call(
        functools.partial(_route_count_body, seq=seq, cap=cap),
        grid=(n_req,),
        in_specs=[pl.BlockSpec((seq, LANES), lambda b: (first_block + b, 0))],
        out_specs=(pl.BlockSpec((n_tiles, N_EXPERTS, RANK_TILE), lambda b: (b, 0, 0)),
                   pl.BlockSpec((1, n_tiles, LANES), lambda b: (b, 0, 0))),
        out_shape=(jax.ShapeDtypeStruct((n_req * n_tiles, N_EXPERTS, RANK_TILE), F32),
                   jax.ShapeDtypeStruct((n_req, n_tiles, LANES), I32)),
        compiler_params=_params(1),
        name=f"route_count_{seq}",
    )(aff)


def _route_rank_body(cend_ref, cnt_ref, idx_ref, *bufs, seq, cap):
    b = pl.program_id(0)
    n_tiles = seq // RANK_TILE
    rank = lax.broadcasted_iota(I32, (RANK_TILE, RANK_TILE), 0).astype(F32)
    ones = jnp.ones((RANK_TILE, LANES), BF16)

    def tile(j, _):
        counts = cnt_ref[j]
        for e in range(N_EXPERTS):
            prev = (b * N_EXPERTS + e) * n_tiles + j - 1
            start = jnp.where(j > 0, cend_ref[jnp.maximum(prev, 0)], 0)
            local = counts[e:e + 1, :] - jnp.asarray(start, F32)
            le = jnp.where(local <= rank, 1.0, 0.0).astype(BF16)
            pos = jnp.dot(le, ones, preferred_element_type=F32) + jnp.asarray(j * RANK_TILE, F32)
            bufs[e][pl.ds(start, RANK_TILE), :] = pos
        return 0

    lax.fori_loop(0, n_tiles, tile, 0)
    lane = lax.broadcasted_iota(I32, (1, LANES), 1)
    out = jnp.zeros((cap, LANES), F32)
    for e in range(N_EXPERTS):
        out = jnp.where(lane == e, bufs[e][0:cap, :], out)
    idx_ref[0] = out.astype(I32)


def _route_rank(cend_flat, cnt, n_req, seq, cap):
    n_tiles = seq // RANK_TILE
    grid_spec = pltpu.PrefetchScalarGridSpec(
        num_scalar_prefetch=1,
        grid=(n_req,),
        in_specs=[pl.BlockSpec((n_tiles, N_EXPERTS, RANK_TILE), lambda b, c: (b, 0, 0))],
        out_specs=pl.BlockSpec((1, cap, LANES), lambda b, c: (b, 0, 0)),
        scratch_shapes=[pltpu.VMEM((cap + RANK_TILE, LANES), F32)] * N_EXPERTS,
    )
    return pl.pallas_call(
        functools.partial(_route_rank_body, seq=seq, cap=cap),
        grid_spec=grid_spec,
        out_shape=jax.ShapeDtypeStruct((n_req, cap, LANES), I32),
        compiler_params=_params(1),
        name=f"route_rank_{seq}",
    )(cend_flat, cnt)


def _gather_rows(idx_ref, base, u2p_ref, aff_ref, tile_ref, gate_ref, r0, n):
    for i in range(n):
        r = r0 + i
        t = idx_ref[base + r]
        slab = u2p_ref[pl.ds(pl.multiple_of(t * WORDS, WORDS), WORDS), :]
        tile_ref[pl.ds(r, WORDS, stride=TILE_PITCH), :] = slab
        gate_ref[pl.ds(r, 1), :] = aff_ref[pl.ds(t, 1), :]


def _scatter_rows(idx_ref, base, y_ref, acc_ref, r0, n):
    for g0 in range(0, n, SCATTER_UNROLL):
        rows = [r0 + g0 + i for i in range(SCATTER_UNROLL)]
        dst = [pl.ds(pl.multiple_of(idx_ref[base + r] * CHUNKS, CHUNKS), CHUNKS) for r in rows]
        vals = [acc_ref[d, :] + y_ref[pl.ds(pl.multiple_of(r * CHUNKS, CHUNKS), CHUNKS), :]
                for d, r in zip(dst, rows)]
        for d, v in zip(dst, vals):
            acc_ref[d, :] = v


def _expert_rows(e, tile_ref, gate_ref, wg, wu, wd, y_ref, r0, n):
    halves = [[], []]
    for c in range(WORDS):
        w = tile_ref[pl.ds(c * TILE_PITCH + r0, n), :]
        for k in range(2):
            halves[k].append(pltpu.unpack_elementwise(w, index=k, packed_dtype=BF16,
                                                      unpacked_dtype=F32))
    x = jnp.concatenate(halves[0] + halves[1], axis=-1).astype(BF16)
    lane = lax.broadcasted_iota(I32, (1, LANES), 1)
    gate = jnp.sum(jnp.where(lane == e, gate_ref[pl.ds(r0, n), :], 0.0), axis=-1, keepdims=True)
    hg = jnp.dot(x, wg, preferred_element_type=F32)
    hu = jnp.dot(x, wu, preferred_element_type=F32)
    hidden = (hg * jax.nn.sigmoid(hg) * hu).astype(BF16)
    y = jnp.dot(hidden, wd, preferred_element_type=F32) * gate
    for c in range(CHUNKS):
        y_ref[pl.ds(r0 * CHUNKS + c, n, stride=CHUNKS), :] = y[:, c * LANES:(c + 1) * LANES]


def _moe_body(idx_ref, u2p_ref, aff_ref, wg_ref, wu_ref, wd_ref, acc_ref,
              tile_a, tile_b, gate_a, gate_b, y_a, y_b):
    p = pl.program_id(0)
    e = pl.program_id(1)
    step = p * N_EXPERTS + e
    base = step * ROWS
    base_next = jnp.minimum(step + 1, N_PASS * N_EXPERTS - 1) * ROWS
    base_prev = jnp.where(e == 0, base, base - ROWS)

    @pl.when(e == 0)
    def _():
        acc_ref[...] = jnp.zeros_like(acc_ref)
        y_b[...] = jnp.zeros_like(y_b)

        def gather(c, _):
            _gather_rows(idx_ref, base, u2p_ref, aff_ref, tile_a, gate_a, c * SUBLANES, SUBLANES)
            return 0

        lax.fori_loop(0, ROWS // SUBLANES, gather, 0)

    def run(tile_cur, gate_cur, y_cur, tile_nxt, gate_nxt, y_prv):
        wg = wg_ref[0, 0].astype(BF16)
        wu = wu_ref[0, 0].astype(BF16)
        wd = wd_ref[0, 0].astype(BF16)
        for blk in range(ROWS // ROW_BLOCK):
            r0 = blk * ROW_BLOCK
            _gather_rows(idx_ref, base_next, u2p_ref, aff_ref, tile_nxt, gate_nxt, r0, ROW_BLOCK)
            _expert_rows(e, tile_cur, gate_cur, wg, wu, wd, y_cur, r0, ROW_BLOCK)
            _scatter_rows(idx_ref, base_prev, y_prv, acc_ref, r0, ROW_BLOCK)

    @pl.when(e % 2 == 0)
    def _():
        run(tile_a, gate_a, y_a, tile_b, gate_b, y_b)

    @pl.when(e % 2 == 1)
    def _():
        run(tile_b, gate_b, y_b, tile_a, gate_a, y_a)

    @pl.when(e == N_EXPERTS - 1)
    def _():
        def scatter(c, _):
            _scatter_rows(idx_ref, base, y_b, acc_ref, c * SCATTER_UNROLL, SCATTER_UNROLL)
            return 0

        lax.fori_loop(0, ROWS // SCATTER_UNROLL, scatter, 0)


def _moe(idx_flat, u2p, aff, w_gate, w_up, w_down, layer):
    one = pl.Buffered(1)
    grid_spec = pltpu.PrefetchScalarGridSpec(
        num_scalar_prefetch=1,
        grid=(N_PASS, N_EXPERTS),
        in_specs=[
            pl.BlockSpec((PASS_TOK * WORDS, LANES), lambda p, e, idx: (p, 0), pipeline_mode=one),
            pl.BlockSpec((PASS_TOK, LANES), lambda p, e, idx: (p, 0), pipeline_mode=one),
            pl.BlockSpec((1, 1, D_MODEL, EXPERT_DIM), lambda p, e, idx: (layer, e, 0, 0)),
            pl.BlockSpec((1, 1, D_MODEL, EXPERT_DIM), lambda p, e, idx: (layer, e, 0, 0)),
            pl.BlockSpec((1, 1, EXPERT_DIM, D_MODEL), lambda p, e, idx: (layer, e, 0, 0)),
        ],
        out_specs=pl.BlockSpec((PASS_TOK * CHUNKS, LANES), lambda p, e, idx: (p, 0),
                               pipeline_mode=one),
        scratch_shapes=[
            pltpu.VMEM((WORDS * TILE_PITCH, LANES), jnp.uint32),
            pltpu.VMEM((WORDS * TILE_PITCH, LANES), jnp.uint32),
            pltpu.VMEM((ROWS, LANES), F32),
            pltpu.VMEM((ROWS, LANES), F32),
            pltpu.VMEM((ROWS * CHUNKS, LANES), F32),
            pltpu.VMEM((ROWS * CHUNKS, LANES), F32),
        ],
    )
    return pl.pallas_call(
        _moe_body,
        grid_spec=grid_spec,
        out_shape=jax.ShapeDtypeStruct((N_TOK * CHUNKS, LANES), F32),
        compiler_params=_params(2, arbitrary=True),
        name="moe",
    )(idx_flat, u2p, aff, w_gate, w_up, w_down)


def _final_body(x1_ref, ff_ref, g2_ref, lg_ref, lb_ref, o_ref):
    ffn = jnp.concatenate([ff_ref[pl.ds(c, TM, stride=CHUNKS), :] for c in range(CHUNKS)], axis=-1)
    o_ref[...] = _layer_norm(ALPHA * x1_ref[...] + g2_ref[0] * ffn, lg_ref[...], lb_ref[...])


def _final(x1, ff, mod3, ln_g, ln_b, row0=0, n_rows=N_TOK):
    t0 = row0 // TM
    row = lambda i: (i + t0, 0)
    return pl.pallas_call(
        _final_body,
        grid=(n_rows // TM,),
        in_specs=[
            pl.BlockSpec((TM, D_MODEL), row),
            pl.BlockSpec((TM * CHUNKS, LANES), row),
            pl.BlockSpec((1, 1, D_MODEL), lambda i: (_mod_row(i + t0, TM), 0, 5)),
            _full((1, D_MODEL)), _full((1, D_MODEL)),
        ],
        out_specs=pl.BlockSpec((TM, D_MODEL), lambda i: (i, 0)),
        out_shape=jax.ShapeDtypeStruct((n_rows, D_MODEL), F32),
        compiler_params=_params(1),
        name="final_ln",
    )(x1, ff, mod3, ln_g, ln_b)


def _rot_cols(w):
    half = ROPE // 2
    return jnp.concatenate([-w[..., half:], w[..., :half]], axis=-1)


def _pad_lanes(w):
    pad = [(0, 0)] * (w.ndim - 1) + [(0, LANES - w.shape[-1])]
    return jnp.pad(w, pad)


def _rope_tables():
    rows_n = LAT_T // GRID_W
    r, cl = jnp.meshgrid(jnp.arange(rows_n, dtype=F32), jnp.arange(GRID_W, dtype=F32), indexing="ij")
    inv = ROPE_THETA ** (-jnp.arange(0, ROPE // 2, 2, dtype=F32) / (ROPE // 2))
    ang = jnp.concatenate([r.reshape(-1)[:, None] * inv, cl.reshape(-1)[:, None] * inv], axis=-1)
    cos, sin = jnp.cos(ang), jnp.sin(ang)
    cos_lat = _pad_lanes(jnp.concatenate([cos, cos], axis=-1))
    sin_lat = _pad_lanes(jnp.concatenate([sin, sin], axis=-1))
    cos_ctx = _pad_lanes(jnp.ones((N_CTX, ROPE), F32))
    sin_ctx = jnp.zeros((N_CTX, LANES), F32)
    cos_a = jnp.concatenate([cos_ctx] + [cos_lat] * LAT_B, axis=0)
    sin_a = jnp.concatenate([sin_ctx] + [sin_lat] * LAT_B, axis=0)
    return cos_a, sin_a


def kernel(x_prompt, x_sample, cache_ckv, cache_krope, c, c_ctx, w_in, q_norm, w_uq, kv_norm, w_uk, w_uv,
           pool_w, pool_scale, conv_w, w_out, w_ada, b_ada, ln1_g, ln1_b, ln2_g, ln2_b, w_router,
           w_gate, w_up, w_down):
    o_q, o_kv, o_kr, o_mix = 0, Q_RANK, Q_RANK + KV_RANK, Q_RANK + KV_RANK + ROPE
    w_kr = w_in[:, :, o_kr:o_kr + ROPE]
    w_in_ext = jnp.concatenate(
        [w_in[:, :, o_q:o_kr], w_in[:, :, o_mix:], _pad_lanes(w_kr), _pad_lanes(_rot_cols(w_kr))],
        axis=-1).astype(BF16)
    uq = w_uq.reshape(DEPTH, Q_RANK, HEADS, NOPE + ROPE)
    uq_rope = uq[..., NOPE:]
    w_uq_ext = jnp.concatenate(
        [uq[..., :NOPE].reshape(DEPTH, Q_RANK, HEADS * NOPE),
         _pad_lanes(uq_rope).reshape(DEPTH, Q_RANK, HEADS * LANES),
         _pad_lanes(_rot_cols(uq_rope)).reshape(DEPTH, Q_RANK, HEADS * LANES)], axis=-1).astype(BF16)
    w_uk_b = w_uk.astype(BF16)
    w_uv_b = w_uv.astype(BF16)
    eye = jnp.eye(len(POOL_WINDOWS), dtype=F32)
    pool_bd = (pool_w[:, :, :, None, :] * eye[None, :, None, :, None]).reshape(DEPTH, MIXW, MIXW).astype(BF16)
    w_out_b = w_out.astype(BF16)
    w_router_pad = _pad_lanes(w_router).astype(BF16)
    cos_a, sin_a = _rope_tables()

    cond = jnp.concatenate([c_ctx[None, :], c, jnp.zeros((SUBLANES - 1 - LAT_B, D_MODEL), F32)], axis=0)
    mod = _ada(cond, w_ada, b_ada)
    kc, vc = _cachekv(cache_ckv, _pad_lanes(cache_krope), w_uk_b, w_uv_b)

    xs = (x_prompt.reshape(N_CTX, D_MODEL), x_sample.reshape(N_LAT, D_MODEL), 0)
    ckv_layers, kr_layers = [], []
    ctx_off = (jnp.arange(CTX_B, dtype=I32) * CTX_T)[:, None, None]
    for l in range(DEPTH):
        mod3 = mod[l].reshape(SUBLANES, 1, 6 * D_MODEL)
        ckv, kr, q, k, v, mix = _inproj(*xs, mod3, w_in_ext[l], q_norm[l][None], kv_norm[l][None],
                                        w_uq_ext[l], w_uk_b[l], w_uv_b[l], cos_a, sin_a)
        ckv_layers.append(ckv[:N_CTX].reshape(CTX_B, CTX_T, KV_RANK))
        kr_layers.append(kr[:N_CTX, :ROPE].reshape(CTX_B, CTX_T, ROPE))
        bc = _seqmix(mix, pool_bd[l], pool_scale[l][None], conv_w[l])
        a_ctx = _attn_ctx(q, k, v)
        a_lat = _attn_lat(q, k, v, kc, vc, l)
        x1, u2p, aff = _postmix(a_ctx, a_lat, bc, *xs, mod3, w_out_b[l, :HEADS * VDIM],
                                w_out_b[l, HEADS * VDIM:], ln1_g[l][None], ln1_b[l][None],
                                w_router_pad[l])
        idx_ctx = _route(aff, 0, CTX_B, CTX_T, CTX_CAP)
        cnt_lat, cend = _route_count(aff, N_CTX // LAT_T, LAT_B, LAT_T, LAT_CAP)
        cend_flat = cend[:, :, :N_EXPERTS].transpose(0, 2, 1).reshape(-1)
        idx_lat = _route_rank(cend_flat, cnt_lat, LAT_B, LAT_T, LAT_CAP)
        idx_lat = idx_lat[:, :, :N_EXPERTS].transpose(0, 2, 1)
        idx_ctx = (idx_ctx + ctx_off).transpose(1, 0, 2).reshape(1, N_EXPERTS, ROWS)
        idx_flat = jnp.concatenate([idx_ctx, idx_lat], axis=0).reshape(-1)
        ff = _moe(idx_flat, u2p, aff, w_gate, w_up, w_down, l)
        if l + 1 < DEPTH:
            x = _final(x1, ff, mod3, ln2_g[l][None], ln2_b[l][None])
            xs = (x, x, N_CTX)
        else:
            y_ctx = _final(x1, ff, mod3, ln2_g[l][None], ln2_b[l][None], 0, N_CTX)
            y_lat = _final(x1, ff, mod3, ln2_g[l][None], ln2_b[l][None], N_CTX, N_LAT)

    y_prompt = y_ctx.reshape(CTX_B, CTX_T, D_MODEL)
    y_sample = y_lat.reshape(LAT_B, LAT_T, D_MODEL)
    new_ckv = jnp.stack(ckv_layers, axis=1)
    new_krope = jnp.stack(kr_layers, axis=1)
    return (y_prompt, y_sample, new_ckv, new_krope)
```

```python
import functools

import jax
import jax.numpy as jnp
from jax import lax
from jax.experimental import pallas as pl
from jax.experimental.pallas import tpu as pltpu

F32 = jnp.float32
BF16 = jnp.bfloat16
I32 = jnp.int32

D_MODEL = 1024
CTX_B, CTX_T = 16, 256
LAT_B, LAT_T = 2, 4096
DEPTH = 4
PAST_LEN = 512
GRID_W = 64
HEADS = 4
NOPE, ROPE, VDIM = 128, 64, 128
Q_RANK, KV_RANK = 384, 256
POOL_WINDOWS = (2, 4, 8, 16)
POOL_GROUP = 64
MIXW = 256
N_EXPERTS = 16
EXPERT_DIM = 512
ROPE_THETA = 10000.0
ATTN_SCALE = (NOPE + ROPE) ** -0.5
LOG2E = 1.4426950408889634
Q_SCALE = ATTN_SCALE * LOG2E
ALPHA = (2 * DEPTH) ** 0.25
RMS_EPS = 1e-6
LN_EPS = 1e-5

N_CTX = CTX_B * CTX_T
N_LAT = LAT_B * LAT_T
N_TOK = N_CTX + N_LAT
CTX_CAP = 2 * CTX_T // N_EXPERTS
LAT_CAP = 2 * LAT_T // N_EXPERTS

LANES = 128
SUBLANES = 8
CHUNKS = D_MODEL // LANES
HEAD_BLOCK = 2 * LANES
VMEM_LIMIT = 56 * 1024 * 1024

TM = 512
SEQ_TILE = 256
HALO = 8
TQ = 512
KV_CHUNK = 512
CHUNK_UNROLL = 8
PASS_TOK = 4096
N_PASS = N_TOK // PASS_TOK
ROWS = 512
TILE_PITCH = ROWS + 8
SCATTER_UNROLL = 8
WORDS = CHUNKS // 2
ROW_BLOCK = 256


def _params(n_axes, arbitrary=False):
    sem = ("arbitrary" if arbitrary else "parallel",) * n_axes
    return pltpu.CompilerParams(dimension_semantics=sem, vmem_limit_bytes=VMEM_LIMIT)


def _mod_row(i, tile):
    start = i * tile
    return jnp.where(start < N_CTX, 0, 1 + (start - N_CTX) // LAT_T)


def _full(shape):
    nd = len(shape)
    return pl.BlockSpec(shape, lambda *_: (0,) * nd)


def _pair_specs(tile, width, lat_row0):
    nct = N_CTX // tile
    off = lat_row0 // tile
    return [pl.BlockSpec((tile, width), lambda i: (jnp.minimum(i, nct - 1), 0)),
            pl.BlockSpec((tile, width), lambda i: (jnp.maximum(i - nct, 0) + off, 0))]


def _pair_load(tile, ctx_ref, lat_ref):
    return jnp.where(pl.program_id(0) < N_CTX // tile, ctx_ref[...], lat_ref[...])


ADA_TN = 1536


def _ada_body(cond_ref, w_ref, b_ref, o_ref):
    c = cond_ref[...]
    a = c * jax.nn.sigmoid(c)
    a_hi = a.astype(BF16)
    a_lo = (a - a_hi.astype(F32)).astype(BF16)
    w = w_ref[0]
    w_hi = w.astype(BF16)
    w_lo = (w - w_hi.astype(F32)).astype(BF16)
    acc = jnp.dot(a_hi, w_hi, preferred_element_type=F32)
    acc += jnp.dot(a_lo, w_hi, preferred_element_type=F32)
    acc += jnp.dot(a_hi, w_lo, preferred_element_type=F32)
    o_ref[0] = acc + b_ref[0]


def _ada(cond, w_ada, b_ada):
    n = 6 * D_MODEL
    return pl.pallas_call(
        _ada_body,
        grid=(DEPTH, n // ADA_TN),
        in_specs=[
            _full((SUBLANES, D_MODEL)),
            pl.BlockSpec((1, D_MODEL, ADA_TN), lambda l, j: (l, 0, j)),
            pl.BlockSpec((1, 1, ADA_TN), lambda l, j: (l, 0, j)),
        ],
        out_specs=pl.BlockSpec((1, SUBLANES, ADA_TN), lambda l, j: (l, 0, j)),
        out_shape=jax.ShapeDtypeStruct((DEPTH, SUBLANES, n), F32),
        compiler_params=_params(2),
        name="ada_mod",
    )(cond, w_ada, b_ada.reshape(DEPTH, 1, n))


IN_EXT = Q_RANK + KV_RANK + 4 * MIXW + 2 * LANES
UQ_EXT = HEADS * NOPE + 2 * HEADS * LANES


def _rms(x, g):
    return x * lax.rsqrt(jnp.mean(x * x, axis=-1, keepdims=True) + RMS_EPS) * g


MIX_COL0 = Q_RANK + KV_RANK


def _ffn_rows(ff_ref, n):
    return jnp.concatenate([ff_ref[pl.ds(c, n, stride=CHUNKS), :] for c in range(CHUNKS)], axis=-1)


def _front_body(*refs, fused_ln):
    if fused_ln:
        (x1_ref, ff_ref, x1p_ref, x1n_ref, ffp_ref, ffn_ref, g2_ref, lg_ref, lb_ref,
         sh_ref, sc_ref, win_ref, qn_ref, kvn_ref, wuq_ref, wuk_ref, wuv_ref, cos_ref, sin_ref,
         pw_ref, ps_ref, cw_ref,
         ckv_ref, kr_ref, q_ref, k_ref, v_ref, bc_ref, x_ref) = refs

        def norm2(x1, ffn):
            return _layer_norm(ALPHA * x1 + g2_ref[0] * ffn, lg_ref[...], lb_ref[...])

        x = norm2(x1_ref[...], _ffn_rows(ff_ref, TM))
        x_ref[...] = x
        x_halo = jnp.concatenate([norm2(x1p_ref[...], _ffn_rows(ffp_ref, HALO)),
                                  norm2(x1n_ref[...], _ffn_rows(ffn_ref, HALO))], axis=0)
    else:
        (xc_ref, xl_ref, xp_ref, xn_ref,
         sh_ref, sc_ref, win_ref, qn_ref, kvn_ref, wuq_ref, wuk_ref, wuv_ref, cos_ref, sin_ref,
         pw_ref, ps_ref, cw_ref,
         ckv_ref, kr_ref, q_ref, k_ref, v_ref, bc_ref) = refs
        x = _pair_load(TM, xc_ref, xl_ref)
        x_halo = jnp.concatenate([xp_ref[...], xn_ref[...]], axis=0)

    u = jnp.concatenate([x, x_halo], axis=0) * (1.0 + sc_ref[0]) + sh_ref[0]
    h_all = jnp.dot(u.astype(BF16), win_ref[...], preferred_element_type=F32)
    h = h_all[:TM]
    mix_halo = h_all[TM:, MIX_COL0:MIX_COL0 + 4 * MIXW]
    o = 0
    q_c = h[:, o:o + Q_RANK]; o += Q_RANK
    kv_c = h[:, o:o + KV_RANK]; o += KV_RANK
    mix = h[:, o:o + 4 * MIXW]; o += 4 * MIXW
    kr_a = h[:, o:o + LANES]; o += LANES
    kr_b = h[:, o:o + LANES]
    kr_ref[...] = kr_a
    cos_a = cos_ref[...]
    sin_a = sin_ref[...]

    n_sub = TM // SEQ_TILE
    for s in range(n_sub):
        lo = s * SEQ_TILE
        prev = mix_halo[:HALO] if s == 0 else mix[lo - HALO:lo]
        nxt = mix_halo[HALO:] if s == n_sub - 1 else mix[lo + SEQ_TILE:lo + SEQ_TILE + HALO]
        bc_ref[lo:lo + SEQ_TILE, :] = _seqmix_math(
            pl.program_id(0) * n_sub + s, mix[lo:lo + SEQ_TILE], prev, nxt,
            pw_ref[...], ps_ref[...], cw_ref[...])

    ckv = _rms(kv_c, kvn_ref[...])
    ckv_ref[...] = ckv
    ckv_b = ckv.astype(BF16)
    k_nope = jnp.dot(ckv_b, wuk_ref[...], preferred_element_type=F32)
    v_ref[...] = jnp.dot(ckv_b, wuv_ref[...], preferred_element_type=F32).astype(BF16)
    k_rope = kr_a * cos_a + kr_b * sin_a
    parts = []
    for hd in range(HEADS):
        parts += [k_nope[:, hd * NOPE:(hd + 1) * NOPE], k_rope]
    k_ref[...] = jnp.concatenate(parts, axis=-1).astype(BF16)

    qn = _rms(q_c, qn_ref[...]).astype(BF16)
    qq = jnp.dot(qn, wuq_ref[...], preferred_element_type=F32)
    ra = HEADS * NOPE
    rb = ra + HEADS * LANES
    parts = []
    for hd in range(HEADS):
        q_rope = (qq[:, ra + hd * LANES:ra + (hd + 1) * LANES] * cos_a
                  + qq[:, rb + hd * LANES:rb + (hd + 1) * LANES] * sin_a)
        parts += [qq[:, hd * NOPE:(hd + 1) * NOPE] * Q_SCALE, q_rope * Q_SCALE]
    q_ref[...] = jnp.concatenate(parts, axis=-1).astype(BF16)


def _halo_specs(rows_per_token, width, lat_row0, n_rows):
    nct = N_CTX // TM
    nblk = n_rows // HALO

    def first_block(i):
        return (jnp.maximum(i - nct, 0) * TM + lat_row0) // HALO

    shape = (HALO * rows_per_token, width)
    return [pl.BlockSpec(shape, lambda i: (jnp.maximum(first_block(i) - 1, 0), 0)),
            pl.BlockSpec(shape, lambda i: (jnp.minimum(first_block(i) + TM // HALO, nblk - 1), 0))]


def _front(prev, mod3, w_in_ext, q_norm, kv_norm, w_uq_ext, w_uk, w_uv, cos_a, sin_a,
           pool_bd, pool_scale, conv_w):
    row = lambda i: (i, 0)
    modspec = lambda k: pl.BlockSpec((1, 1, D_MODEL), lambda i: (_mod_row(i, TM), 0, k))
    fused_ln = len(prev) == 5
    outs = [
        jax.ShapeDtypeStruct((N_TOK, KV_RANK), F32),
        jax.ShapeDtypeStruct((N_TOK, LANES), F32),
        jax.ShapeDtypeStruct((N_TOK, HEADS * HEAD_BLOCK), BF16),
        jax.ShapeDtypeStruct((N_TOK, HEADS * HEAD_BLOCK), BF16),
        jax.ShapeDtypeStruct((N_TOK, HEADS * VDIM), BF16),
        jax.ShapeDtypeStruct((N_TOK, 2 * MIXW), BF16),
    ]
    if fused_ln:
        x1, ff, mod3_prev, ln_g, ln_b = prev
        outs.append(jax.ShapeDtypeStruct((N_TOK, D_MODEL), F32))
        lead_specs = ([pl.BlockSpec((TM, D_MODEL), row), pl.BlockSpec((TM * CHUNKS, LANES), row)]
                      + _halo_specs(1, D_MODEL, N_CTX, N_TOK)
                      + _halo_specs(CHUNKS, LANES, N_CTX, N_TOK)
                      + [pl.BlockSpec((1, 1, D_MODEL), lambda i: (_mod_row(i, TM), 0, 5)),
                         _full((1, D_MODEL)), _full((1, D_MODEL))])
        lead_args = (x1, ff, x1, x1, ff, ff, mod3_prev, ln_g, ln_b)
    else:
        xc, xl = prev
        lead_specs = _pair_specs(TM, D_MODEL, 0) + _halo_specs(1, D_MODEL, 0, N_LAT)
        lead_args = (xc, xl, xl, xl)
    return pl.pallas_call(
        functools.partial(_front_body, fused_ln=fused_ln),
        grid=(N_TOK // TM,),
        in_specs=lead_specs + [
            modspec(0), modspec(1),
            _full((D_MODEL, IN_EXT)),
            _full((1, Q_RANK)), _full((1, KV_RANK)),
            _full((Q_RANK, UQ_EXT)),
            _full((KV_RANK, HEADS * NOPE)), _full((KV_RANK, HEADS * VDIM)),
            pl.BlockSpec((TM, LANES), row), pl.BlockSpec((TM, LANES), row),
            _full((MIXW, MIXW)), _full((1, MIXW)), _full((3, MIXW)),
        ],
        out_specs=tuple(pl.BlockSpec((TM, s.shape[1]), row) for s in outs),
        out_shape=tuple(outs),
        compiler_params=_params(1),
        name="front_ln" if fused_ln else "front",
    )(*lead_args, mod3, mod3, w_in_ext, q_norm, kv_norm, w_uq_ext, w_uk, w_uv, cos_a, sin_a,
      pool_bd, pool_scale, conv_w)


def _cachekv_body(ckv_ref, kr_ref, wuk_ref, wuv_ref, k_ref, v_ref):
    c = ckv_ref[0, 0].astype(BF16)
    k_nope = jnp.dot(c, wuk_ref[0], preferred_element_type=F32)
    v_ref[0, 0] = jnp.dot(c, wuv_ref[0], preferred_element_type=F32).astype(BF16)
    kr = kr_ref[0, 0]
    parts = []
    for hd in range(HEADS):
        parts += [k_nope[:, hd * NOPE:(hd + 1) * NOPE], kr]
    k_ref[0, 0] = jnp.concatenate(parts, axis=-1).astype(BF16)


def _cachekv(cache_ckv, kr_pad, w_uk, w_uv):
    bl = lambda b, l: (b, l, 0, 0)
    wl = lambda b, l: (l, 0, 0)
    return pl.pallas_call(
        _cachekv_body,
        grid=(LAT_B, DEPTH),
        in_specs=[
            pl.BlockSpec((1, 1, PAST_LEN, KV_RANK), bl),
            pl.BlockSpec((1, 1, PAST_LEN, LANES), bl),
            pl.BlockSpec((1, KV_RANK, HEADS * NOPE), wl),
            pl.BlockSpec((1, KV_RANK, HEADS * VDIM), wl),
        ],
        out_specs=(pl.BlockSpec((1, 1, PAST_LEN, HEADS * HEAD_BLOCK), bl),
                   pl.BlockSpec((1, 1, PAST_LEN, HEADS * VDIM), bl)),
        out_shape=(jax.ShapeDtypeStruct((LAT_B, DEPTH, PAST_LEN, HEADS * HEAD_BLOCK), BF16),
                   jax.ShapeDtypeStruct((LAT_B, DEPTH, PAST_LEN, HEADS * VDIM), BF16)),
        compiler_params=_params(2),
        name="cache_kv",
    )(cache_ckv, kr_pad, w_uk, w_uv)


EXT = SEQ_TILE + 2 * HALO


def _shift_up(x, k):
    return pltpu.roll(x, x.shape[0] - k, axis=0)


def _shift_down(x, k):
    return pltpu.roll(x, k, axis=0)


def _seqmix_math(j, main, prev, nxt, pw, ps, cw):
    n_ctx_tiles = N_CTX // SEQ_TILE
    tiles_per_lat = LAT_T // SEQ_TILE
    is_ctx = j < n_ctx_tiles
    jj = jnp.where(is_ctx, 0, (j - n_ctx_tiles) % tiles_per_lat)
    first = jj == 0
    last = jnp.where(is_ctx, True, jj == tiles_per_lat - 1)
    t_seq = jnp.where(is_ctx, CTX_T, LAT_T)

    prev = jnp.where(first, 0.0, prev)
    nxt = jnp.where(last, 0.0, nxt)
    ext = jnp.concatenate([prev, main, nxt], axis=0)
    p = ext[:, 0:MIXW]
    g_b = main[:, MIXW:2 * MIXW]
    g_c = ext[:, 2 * MIXW:3 * MIXW]
    h_in = ext[:, 3 * MIXW:4 * MIXW]

    sums = {}
    b = p
    for w in POOL_WINDOWS:
        b = b + _shift_up(b, w // 2)
        sums[w] = _shift_down(b, w // 2)[HALO:HALO + SEQ_TILE]
    tpos = jj * SEQ_TILE + lax.broadcasted_iota(I32, (SEQ_TILE, 1), 0)
    lane = lax.broadcasted_iota(I32, (1, MIXW), 1)
    num = None
    den = None
    for g, w in enumerate(POOL_WINDOWS):
        lo = jnp.maximum(tpos - w // 2, 0)
        hi = jnp.minimum(tpos + w - w // 2, t_seq)
        cnt = (hi - lo).astype(F32)
        if num is None:
            num, den = sums[w], jnp.broadcast_to(cnt, (SEQ_TILE, MIXW))
        else:
            sel = lane >= g * POOL_GROUP
            num = jnp.where(sel, sums[w], num)
            den = jnp.where(sel, cnt, den)
    pooled = num / den - p[HALO:HALO + SEQ_TILE]
    b_out = jnp.dot(pooled.astype(BF16), pw, preferred_element_type=F32) * ps

    y = g_c * h_in
    conv = (_shift_down(y, 1) * cw[0:1] + y * cw[1:2] + _shift_up(y, 1) * cw[2:3])
    c_out = g_b * conv[HALO:HALO + SEQ_TILE]
    return jnp.concatenate([b_out, c_out], axis=-1).astype(BF16)


_NT = (((1,), (1,)), ((), ()))


def _v_ext(v):
    return jnp.concatenate([v, jnp.ones(v.shape, v.dtype)], axis=-1)


def _attn_ctx_body(q_ref, k_ref, v_ref, o_ref):
    outs = []
    for hd in range(HEADS):
        qh = q_ref[:, hd * HEAD_BLOCK:(hd + 1) * HEAD_BLOCK]
        kh = k_ref[:, hd * HEAD_BLOCK:(hd + 1) * HEAD_BLOCK]
        vh = v_ref[:, hd * VDIM:(hd + 1) * VDIM]
        s = lax.dot_general(qh, kh, _NT, preferred_element_type=F32)
        m = jnp.max(s, axis=-1, keepdims=True)
        p = jnp.exp2(s - m)
        acc = jnp.dot(p.astype(BF16), _v_ext(vh), preferred_element_type=F32)
        outs.append(acc[:, :VDIM] / acc[:, VDIM:])
    o_ref[...] = jnp.concatenate(outs, axis=-1).astype(BF16)


def _attn_ctx(q, k, v):
    row = lambda b: (b, 0)
    return pl.pallas_call(
        _attn_ctx_body,
        grid=(CTX_B,),
        in_specs=[pl.BlockSpec((CTX_T, HEADS * HEAD_BLOCK), row),
                  pl.BlockSpec((CTX_T, HEADS * HEAD_BLOCK), row),
                  pl.BlockSpec((CTX_T, HEADS * VDIM), row)],
        out_specs=pl.BlockSpec((CTX_T, HEADS * VDIM), row),
        out_shape=jax.ShapeDtypeStruct((N_CTX, HEADS * VDIM), BF16),
        compiler_params=_params(1),
        name="attn_ctx",
    )(q, k, v)


def _attn_lat_body(q_ref, kc_ref, vc_ref, ko_ref, vo_ref, o_ref, m_ref, acc_ref):
    reps = KV_CHUNK // LANES

    def update(hd, k, v, first):
        qh = q_ref[:, hd * HEAD_BLOCK:(hd + 1) * HEAD_BLOCK]
        s = lax.dot_general(qh, k, _NT, preferred_element_type=F32)
        mx = jnp.max(s, axis=-1, keepdims=True)
        if first:
            m_new = jnp.broadcast_to(mx, (TQ, LANES))
        else:
            m_old = m_ref[hd]
            m_new = jnp.maximum(m_old, mx)
        p = jnp.exp2(s - jnp.concatenate([m_new] * reps, axis=-1))
        pv = jnp.dot(p.astype(BF16), _v_ext(v), preferred_element_type=F32)
        if first:
            acc_ref[hd] = pv
        else:
            a = jnp.exp2(m_old - m_new)
            acc_ref[hd] = acc_ref[hd] * jnp.concatenate([a, a], axis=-1) + pv
        m_ref[hd] = m_new

    def head_slices(hd):
        return slice(hd * HEAD_BLOCK, (hd + 1) * HEAD_BLOCK), slice(hd * VDIM, (hd + 1) * VDIM)

    for hd in range(HEADS):
        ks, vs = head_slices(hd)
        update(hd, kc_ref[0, 0, :, ks], vc_ref[0, 0, :, vs], True)

    def body(c, _):
        for j in range(CHUNK_UNROLL):
            rows = pl.ds(pl.multiple_of((c * CHUNK_UNROLL + j) * KV_CHUNK, KV_CHUNK), KV_CHUNK)
            for hd in range(HEADS):
                ks, vs = head_slices(hd)
                update(hd, ko_ref[rows, ks], vo_ref[rows, vs], False)
        return 0

    lax.fori_loop(0, LAT_T // (KV_CHUNK * CHUNK_UNROLL), body, 0)
    outs = [acc_ref[hd][:, :VDIM] / acc_ref[hd][:, VDIM:] for hd in range(HEADS)]
    o_ref[...] = jnp.concatenate(outs, axis=-1).astype(BF16)


def _attn_lat(q, k, v, kc, vc, layer):
    qt = LAT_T // TQ
    ctx_tiles = N_CTX // TQ
    ctx_blocks = N_CTX // LAT_T
    return pl.pallas_call(
        _attn_lat_body,
        grid=(LAT_B, qt),
        in_specs=[
            pl.BlockSpec((TQ, HEADS * HEAD_BLOCK), lambda b, i: (ctx_tiles + b * qt + i, 0)),
            pl.BlockSpec((1, 1, PAST_LEN, HEADS * HEAD_BLOCK), lambda b, i: (b, layer, 0, 0)),
            pl.BlockSpec((1, 1, PAST_LEN, HEADS * VDIM), lambda b, i: (b, layer, 0, 0)),
            pl.BlockSpec((LAT_T, HEADS * HEAD_BLOCK), lambda b, i: (ctx_blocks + b, 0)),
            pl.BlockSpec((LAT_T, HEADS * VDIM), lambda b, i: (ctx_blocks + b, 0)),
        ],
        out_specs=pl.BlockSpec((TQ, HEADS * VDIM), lambda b, i: (b * qt + i, 0)),
        out_shape=jax.ShapeDtypeStruct((N_LAT, HEADS * VDIM), BF16),
        scratch_shapes=[pltpu.VMEM((HEADS, TQ, LANES), F32),
                        pltpu.VMEM((HEADS, TQ, 2 * VDIM), F32)],
        compiler_params=_params(2),
        name="attn_lat",
    )(q, kc, vc, k, v)


def _layer_norm(y, g, b):
    mu = jnp.mean(y, axis=-1, keepdims=True)
    d = y - mu
    var = jnp.mean(d * d, axis=-1, keepdims=True)
    return d * lax.rsqrt(var + LN_EPS) * g + b


def _postmix_body(actx_ref, alat_ref, bc_ref, xc_ref, xl_ref, g1_ref, sh2_ref, sc2_ref, wa_ref, wbc_ref,
                  lg_ref, lb_ref, wr_ref, x1_ref, u2p_ref, aff_ref):
    a = _pair_load(TM, actx_ref, alat_ref)
    mix = jnp.dot(a, wa_ref[...], preferred_element_type=F32)
    mix += jnp.dot(bc_ref[...], wbc_ref[...], preferred_element_type=F32)
    x = _pair_load(TM, xc_ref, xl_ref)
    x1 = _layer_norm(ALPHA * x + g1_ref[0] * mix, lg_ref[...], lb_ref[...])
    x1_ref[...] = x1
    u2 = x1 * (1.0 + sc2_ref[0]) + sh2_ref[0]
    for c in range(WORDS):
        lo = u2[:, c * LANES:(c + 1) * LANES]
        hi = u2[:, (c + WORDS) * LANES:(c + WORDS + 1) * LANES]
        u2p_ref[pl.ds(c, TM, stride=WORDS), :] = pltpu.pack_elementwise([lo, hi], packed_dtype=BF16)
    logits = jnp.dot(u2.astype(BF16), wr_ref[...], preferred_element_type=F32)
    lane = lax.broadcasted_iota(I32, (1, LANES), 1)
    logits = jnp.where(lane < N_EXPERTS, logits, -jnp.inf)
    m = jnp.max(logits, axis=-1, keepdims=True)
    e = jnp.exp(logits - m)
    aff_ref[...] = e / jnp.sum(e, axis=-1, keepdims=True)


def _postmix(a_ctx, a_lat, bc, xc, xl, lat_row0, mod3, w_out_a, w_out_bc, ln_g, ln_b, w_router_pad):
    row = lambda i: (i, 0)
    modspec = lambda k: pl.BlockSpec((1, 1, D_MODEL), lambda i: (_mod_row(i, TM), 0, k))
    return pl.pallas_call(
        _postmix_body,
        grid=(N_TOK // TM,),
        in_specs=_pair_specs(TM, HEADS * VDIM, 0) + [
            pl.BlockSpec((TM, 2 * MIXW), row),
        ] + _pair_specs(TM, D_MODEL, lat_row0) + [
            modspec(2), modspec(3), modspec(4),
            _full((HEADS * VDIM, D_MODEL)), _full((2 * MIXW, D_MODEL)),
            _full((1, D_MODEL)), _full((1, D_MODEL)),
            _full((D_MODEL, LANES)),
        ],
        out_specs=(pl.BlockSpec((TM, D_MODEL), row),
                   pl.BlockSpec((TM * WORDS, LANES), row),
                   pl.BlockSpec((TM, LANES), row)),
        out_shape=(jax.ShapeDtypeStruct((N_TOK, D_MODEL), F32),
                   jax.ShapeDtypeStruct((N_TOK * WORDS, LANES), jnp.uint32),
                   jax.ShapeDtypeStruct((N_TOK, LANES), F32)),
        compiler_params=_params(1),
        name="postmix",
    )(a_ctx, a_lat, bc, xc, xl, mod3, mod3, mod3, w_out_a, w_out_bc, ln_g, ln_b, w_router_pad)


PREFIX_CHUNK = 256


def _route_body(aff_ref, idx_ref, cnt_ref, *, seq, cap):
    n_chunks = seq // PREFIX_CHUNK
    aff = aff_ref[...]

    def search(i, thr):
        cand = thr | jnp.left_shift(jnp.int32(1), 30 - i)
        n = jnp.sum((aff >= pltpu.bitcast(cand, F32)).astype(I32), axis=0, keepdims=True)
        return jnp.where(n >= cap, cand, thr)

    thr_bits = lax.fori_loop(0, 31, search, jnp.zeros((1, LANES), I32))
    thr = pltpu.bitcast(thr_bits, F32)
    above = pltpu.bitcast(thr_bits + 1, F32)
    n_gt = jnp.sum(((aff > thr) & (aff >= above)).astype(I32), axis=0, keepdims=True)
    need = (cap - n_gt).astype(F32)

    ri = lax.broadcasted_iota(I32, (PREFIX_CHUNK, PREFIX_CHUNK), 0)
    ci = lax.broadcasted_iota(I32, (PREFIX_CHUNK, PREFIX_CHUNK), 1)
    tri = jnp.where(ci <= ri, 1.0, 0.0).astype(BF16)
    carry_eq = jnp.zeros((1, LANES), F32)
    carry_sel = jnp.zeros((1, LANES), F32)
    for c in range(n_chunks):
        rows = slice(c * PREFIX_CHUNK, (c + 1) * PREFIX_CHUNK)
        a = aff_ref[rows, :]
        gt = (a > thr) & (a >= above)
        eq = (a >= thr) & jnp.logical_not(gt)
        p_eq = jnp.dot(tri, jnp.where(eq, 1.0, 0.0).astype(BF16),
                       preferred_element_type=F32) + carry_eq
        sel = jnp.where(gt, 1.0, jnp.where(eq & (p_eq <= need), 1.0, 0.0))
        p_sel = jnp.dot(tri, sel.astype(BF16), preferred_element_type=F32) + carry_sel
        cnt_ref[rows, :] = p_sel
        carry_eq = p_eq[PREFIX_CHUNK - 1:PREFIX_CHUNK, :]
        carry_sel = p_sel[PREFIX_CHUNK - 1:PREFIX_CHUNK, :]

    rank = lax.broadcasted_iota(I32, (1, cap), 1).astype(F32)
    for e in range(N_EXPERTS):
        def count(c, acc, e=e):
            rows = pl.ds(pl.multiple_of(c * PREFIX_CHUNK, PREFIX_CHUNK), PREFIX_CHUNK)
            col = cnt_ref[rows, e:e + 1]
            le = jnp.where(col <= rank, 1.0, 0.0)
            return acc + jnp.sum(le.reshape(PREFIX_CHUNK // SUBLANES, SUBLANES, cap), axis=0)

        acc = lax.fori_loop(0, n_chunks, count, jnp.zeros((SUBLANES, cap), F32))
        idx_ref[0, e:e + 1, :] = jnp.sum(acc, axis=0, keepdims=True).astype(I32)


def _route(aff, first_block, n_req, seq, cap):
    return pl.pallas_call(
        functools.partial(_route_body, seq=seq, cap=cap),
        grid=(n_req,),
        in_specs=[pl.BlockSpec((seq, LANES), lambda b: (first_block + b, 0))],
        out_specs=pl.BlockSpec((1, N_EXPERTS, cap), lambda b: (b, 0, 0)),
        out_shape=jax.ShapeDtypeStruct((n_req, N_EXPERTS, cap), I32),
        scratch_shapes=[pltpu.VMEM((seq, LANES), F32)],
        compiler_params=_params(1),
        name=f"route_{seq}",
    )(aff)


RANK_TILE = LANES


def _expert_row(col):
    full = jnp.concatenate([jnp.broadcast_to(col, (N_EXPERTS, LANES)),
                            jnp.zeros((LANES - N_EXPERTS, LANES), col.dtype)], axis=0)
    return full.T[0:1]


def _route_count_body(aff_ref, cnt_ref, cend_ref, *, seq, cap):
    n_chunks = seq // PREFIX_CHUNK
    dense = aff_ref[...].T[:N_EXPERTS]

    def search(i, thr):
        cand = thr | jnp.left_shift(jnp.int32(1), 30 - i)
        n = jnp.sum((dense >= pltpu.bitcast(cand, F32)).astype(I32), axis=1, keepdims=True)
        return jnp.where(n >= cap, cand, thr)

    thr_bits = lax.fori_loop(0, 31, search, jnp.zeros((N_EXPERTS, 1), I32))
    thr_col = pltpu.bitcast(thr_bits, F32)
    above_col = pltpu.bitcast(thr_bits + 1, F32)
    n_gt = jnp.sum(((dense > thr_col) & (dense >= above_col)).astype(I32), axis=1, keepdims=True)
    thr = _expert_row(thr_col)
    above = _expert_row(above_col)
    need = _expert_row((cap - n_gt).astype(F32))

    ri = lax.broadcasted_iota(I32, (PREFIX_CHUNK, PREFIX_CHUNK), 0)
    ci = lax.broadcasted_iota(I32, (PREFIX_CHUNK, PREFIX_CHUNK), 1)
    tri = jnp.where(ci <= ri, 1.0, 0.0).astype(BF16)
    carry_eq = jnp.zeros((1, LANES), F32)
    carry_sel = jnp.zeros((1, LANES), F32)
    per = PREFIX_CHUNK // RANK_TILE
    for c in range(n_chunks):
        rows = slice(c * PREFIX_CHUNK, (c + 1) * PREFIX_CHUNK)
        a = aff_ref[rows, :]
        gt = (a > thr) & (a >= above)
        eq = (a >= thr) & jnp.logical_not(gt)
        p_eq = jnp.dot(tri, jnp.where(eq, 1.0, 0.0).astype(BF16),
                       preferred_element_type=F32) + carry_eq
        sel = jnp.where(gt, 1.0, jnp.where(eq & (p_eq <= need), 1.0, 0.0))
        p_sel = jnp.dot(tri, sel.astype(BF16), preferred_element_type=F32) + carry_sel
        dense_cnt = p_sel.T[:N_EXPERTS]
        for k in range(per):
            last = (k + 1) * RANK_TILE - 1
            cnt_ref[c * per + k] = dense_cnt[:, k * RANK_TILE:(k + 1) * RANK_TILE]
            cend_ref[0, c * per + k:c * per + k + 1, :] = p_sel[last:last + 1, :].astype(I32)
        carry_eq = p_eq[PREFIX_CHUNK - 1:PREFIX_CHUNK, :]
        carry_sel = p_sel[PREFIX_CHUNK - 1:PREFIX_CHUNK, :]


def _route_count(aff, first_block, n_req, seq, cap):
    n_tiles = seq // RANK_TILE
    return pl.pallas_call(
        functools.partial(_route_count_body, seq=seq, cap=cap),
        grid=(n_req,),
        in_specs=[pl.BlockSpec((seq, LANES), lambda b: (first_block + b, 0))],
        out_specs=(pl.BlockSpec((n_tiles, N_EXPERTS, RANK_TILE), lambda b: (b, 0, 0)),
                   pl.BlockSpec((1, n_tiles, LANES), lambda b: (b, 0, 0))),
        out_shape=(jax.ShapeDtypeStruct((n_req * n_tiles, N_EXPERTS, RANK_TILE), F32),
                   jax.ShapeDtypeStruct((n_req, n_tiles, LANES), I32)),
        compiler_params=_params(1),
        name=f"route_count_{seq}",
    )(aff)


def _route_rank_body(cend_ref, cnt_ref, idx_ref, *bufs, seq, cap):
    b = pl.program_id(0)
    n_tiles = seq // RANK_TILE
    rank = lax.broadcasted_iota(I32, (RANK_TILE, RANK_TILE), 0).astype(F32)
    ones = jnp.ones((RANK_TILE, LANES), BF16)

    def tile(j, _):
        counts = cnt_ref[j]
        for e in range(N_EXPERTS):
            prev = (b * N_EXPERTS + e) * n_tiles + j - 1
            start = jnp.where(j > 0, cend_ref[jnp.maximum(prev, 0)], 0)
            local = counts[e:e + 1, :] - jnp.asarray(start, F32)
            le = jnp.where(local <= rank, 1.0, 0.0).astype(BF16)
            pos = jnp.dot(le, ones, preferred_element_type=F32) + jnp.asarray(j * RANK_TILE, F32)
            bufs[e][pl.ds(start, RANK_TILE), :] = pos
        return 0

    lax.fori_loop(0, n_tiles, tile, 0)
    lane = lax.broadcasted_iota(I32, (1, LANES), 1)
    out = jnp.zeros((cap, LANES), F32)
    for e in range(N_EXPERTS):
        out = jnp.where(lane == e, bufs[e][0:cap, :], out)
    idx_ref[0] = out.astype(I32)


def _route_rank(cend_flat, cnt, n_req, seq, cap):
    n_tiles = seq // RANK_TILE
    grid_spec = pltpu.PrefetchScalarGridSpec(
        num_scalar_prefetch=1,
        grid=(n_req,),
        in_specs=[pl.BlockSpec((n_tiles, N_EXPERTS, RANK_TILE), lambda b, c: (b, 0, 0))],
        out_specs=pl.BlockSpec((1, cap, LANES), lambda b, c: (b, 0, 0)),
        scratch_shapes=[pltpu.VMEM((cap + RANK_TILE, LANES), F32)] * N_EXPERTS,
    )
    return pl.pallas_call(
        functools.partial(_route_rank_body, seq=seq, cap=cap),
        grid_spec=grid_spec,
        out_shape=jax.ShapeDtypeStruct((n_req, cap, LANES), I32),
        compiler_params=_params(1),
        name=f"route_rank_{seq}",
    )(cend_flat, cnt)


def _gather_rows(idx_ref, base, u2p_ref, aff_ref, tile_ref, gate_ref, r0, n):
    for i in range(n):
        r = r0 + i
        t = idx_ref[base + r]
        slab = u2p_ref[pl.ds(pl.multiple_of(t * WORDS, WORDS), WORDS), :]
        tile_ref[pl.ds(r, WORDS, stride=TILE_PITCH), :] = slab
        gate_ref[pl.ds(r, 1), :] = aff_ref[pl.ds(t, 1), :]


def _scatter_rows(idx_ref, base, y_ref, acc_ref, r0, n):
    for g0 in range(0, n, SCATTER_UNROLL):
        rows = [r0 + g0 + i for i in range(SCATTER_UNROLL)]
        dst = [pl.ds(pl.multiple_of(idx_ref[base + r] * CHUNKS, CHUNKS), CHUNKS) for r in rows]
        vals = [acc_ref[d, :] + y_ref[pl.ds(pl.multiple_of(r * CHUNKS, CHUNKS), CHUNKS), :]
                for d, r in zip(dst, rows)]
        for d, v in zip(dst, vals):
            acc_ref[d, :] = v


def _expert_rows(e, tile_ref, gate_ref, wg, wu, wd, y_ref, r0, n):
    halves = [[], []]
    for c in range(WORDS):
        w = tile_ref[pl.ds(c * TILE_PITCH + r0, n), :]
        for k in range(2):
            halves[k].append(pltpu.unpack_elementwise(w, index=k, packed_dtype=BF16,
                                                      unpacked_dtype=F32))
    x = jnp.concatenate(halves[0] + halves[1], axis=-1).astype(BF16)
    lane = lax.broadcasted_iota(I32, (1, LANES), 1)
    gate = jnp.sum(jnp.where(lane == e, gate_ref[pl.ds(r0, n), :], 0.0), axis=-1, keepdims=True)
    hg = jnp.dot(x, wg, preferred_element_type=F32)
    hu = jnp.dot(x, wu, preferred_element_type=F32)
    hidden = (hg * jax.nn.sigmoid(hg) * hu).astype(BF16)
    y = jnp.dot(hidden, wd, preferred_element_type=F32) * gate
    for c in range(CHUNKS):
        y_ref[pl.ds(r0 * CHUNKS + c, n, stride=CHUNKS), :] = y[:, c * LANES:(c + 1) * LANES]


def _moe_body(idx_ref, u2p_ref, aff_ref, wg_ref, wu_ref, wd_ref, acc_ref,
              tile_a, tile_b, gate_a, gate_b, y_a, y_b):
    p = pl.program_id(0)
    e = pl.program_id(1)
    step = p * N_EXPERTS + e
    base = step * ROWS
    base_next = jnp.minimum(step + 1, N_PASS * N_EXPERTS - 1) * ROWS
    base_prev = jnp.where(e == 0, base, base - ROWS)

    @pl.when(e == 0)
    def _():
        acc_ref[...] = jnp.zeros_like(acc_ref)
        y_b[...] = jnp.zeros_like(y_b)

        def gather(c, _):
            _gather_rows(idx_ref, base, u2p_ref, aff_ref, tile_a, gate_a, c * SUBLANES, SUBLANES)
            return 0

        lax.fori_loop(0, ROWS // SUBLANES, gather, 0)

    def run(tile_cur, gate_cur, y_cur, tile_nxt, gate_nxt, y_prv):
        wg = wg_ref[0, 0].astype(BF16)
        wu = wu_ref[0, 0].astype(BF16)
        wd = wd_ref[0, 0].astype(BF16)
        for blk in range(ROWS // ROW_BLOCK):
            r0 = blk * ROW_BLOCK
            _gather_rows(idx_ref, base_next, u2p_ref, aff_ref, tile_nxt, gate_nxt, r0, ROW_BLOCK)
            _expert_rows(e, tile_cur, gate_cur, wg, wu, wd, y_cur, r0, ROW_BLOCK)
            _scatter_rows(idx_ref, base_prev, y_prv, acc_ref, r0, ROW_BLOCK)

    @pl.when(e % 2 == 0)
    def _():
        run(tile_a, gate_a, y_a, tile_b, gate_b, y_b)

    @pl.when(e % 2 == 1)
    def _():
        run(tile_b, gate_b, y_b, tile_a, gate_a, y_a)

    @pl.when(e == N_EXPERTS - 1)
    def _():
        def scatter(c, _):
            _scatter_rows(idx_ref, base, y_b, acc_ref, c * SCATTER_UNROLL, SCATTER_UNROLL)
            return 0

        lax.fori_loop(0, ROWS // SCATTER_UNROLL, scatter, 0)


def _moe(idx_flat, u2p, aff, w_gate, w_up, w_down, layer):
    one = pl.Buffered(1)
    grid_spec = pltpu.PrefetchScalarGridSpec(
        num_scalar_prefetch=1,
        grid=(N_PASS, N_EXPERTS),
        in_specs=[
            pl.BlockSpec((PASS_TOK * WORDS, LANES), lambda p, e, idx: (p, 0), pipeline_mode=one),
            pl.BlockSpec((PASS_TOK, LANES), lambda p, e, idx: (p, 0), pipeline_mode=one),
            pl.BlockSpec((1, 1, D_MODEL, EXPERT_DIM), lambda p, e, idx: (layer, e, 0, 0)),
            pl.BlockSpec((1, 1, D_MODEL, EXPERT_DIM), lambda p, e, idx: (layer, e, 0, 0)),
            pl.BlockSpec((1, 1, EXPERT_DIM, D_MODEL), lambda p, e, idx: (layer, e, 0, 0)),
        ],
        out_specs=pl.BlockSpec((PASS_TOK * CHUNKS, LANES), lambda p, e, idx: (p, 0),
                               pipeline_mode=one),
        scratch_shapes=[
            pltpu.VMEM((WORDS * TILE_PITCH, LANES), jnp.uint32),
            pltpu.VMEM((WORDS * TILE_PITCH, LANES), jnp.uint32),
            pltpu.VMEM((ROWS, LANES), F32),
            pltpu.VMEM((ROWS, LANES), F32),
            pltpu.VMEM((ROWS * CHUNKS, LANES), F32),
            pltpu.VMEM((ROWS * CHUNKS, LANES), F32),
        ],
    )
    return pl.pallas_call(
        _moe_body,
        grid_spec=grid_spec,
        out_shape=jax.ShapeDtypeStruct((N_TOK * CHUNKS, LANES), F32),
        compiler_params=_params(2, arbitrary=True),
        name="moe",
    )(idx_flat, u2p, aff, w_gate, w_up, w_down)


def _final_body(x1_ref, ff_ref, g2_ref, lg_ref, lb_ref, o_ref):
    ffn = jnp.concatenate([ff_ref[pl.ds(c, TM, stride=CHUNKS), :] for c in range(CHUNKS)], axis=-1)
    o_ref[...] = _layer_norm(ALPHA * x1_ref[...] + g2_ref[0] * ffn, lg_ref[...], lb_ref[...])


def _final(x1, ff, mod3, ln_g, ln_b, row0=0, n_rows=N_TOK):
    t0 = row0 // TM
    row = lambda i: (i + t0, 0)
    return pl.pallas_call(
        _final_body,
        grid=(n_rows // TM,),
        in_specs=[
            pl.BlockSpec((TM, D_MODEL), row),
            pl.BlockSpec((TM * CHUNKS, LANES), row),
            pl.BlockSpec((1, 1, D_MODEL), lambda i: (_mod_row(i + t0, TM), 0, 5)),
            _full((1, D_MODEL)), _full((1, D_MODEL)),
        ],
        out_specs=pl.BlockSpec((TM, D_MODEL), lambda i: (i, 0)),
        out_shape=jax.ShapeDtypeStruct((n_rows, D_MODEL), F32),
        compiler_params=_params(1),
        name="final_ln",
    )(x1, ff, mod3, ln_g, ln_b)


def _rot_cols(w):
    half = ROPE // 2
    return jnp.concatenate([-w[..., half:], w[..., :half]], axis=-1)


def _pad_lanes(w):
    pad = [(0, 0)] * (w.ndim - 1) + [(0, LANES - w.shape[-1])]
    return jnp.pad(w, pad)


def _rope_tables():
    rows_n = LAT_T // GRID_W
    r, cl = jnp.meshgrid(jnp.arange(rows_n, dtype=F32), jnp.arange(GRID_W, dtype=F32), indexing="ij")
    inv = ROPE_THETA ** (-jnp.arange(0, ROPE // 2, 2, dtype=F32) / (ROPE // 2))
    ang = jnp.concatenate([r.reshape(-1)[:, None] * inv, cl.reshape(-1)[:, None] * inv], axis=-1)
    cos, sin = jnp.cos(ang), jnp.sin(ang)
    cos_lat = _pad_lanes(jnp.concatenate([cos, cos], axis=-1))
    sin_lat = _pad_lanes(jnp.concatenate([sin, sin], axis=-1))
    cos_ctx = _pad_lanes(jnp.ones((N_CTX, ROPE), F32))
    sin_ctx = jnp.zeros((N_CTX, LANES), F32)
    cos_a = jnp.concatenate([cos_ctx] + [cos_lat] * LAT_B, axis=0)
    sin_a = jnp.concatenate([sin_ctx] + [sin_lat] * LAT_B, axis=0)
    return cos_a, sin_a


def kernel(x_prompt, x_sample, cache_ckv, cache_krope, c, c_ctx, w_in, q_norm, w_uq, kv_norm, w_uk, w_uv,
           pool_w, pool_scale, conv_w, w_out, w_ada, b_ada, ln1_g, ln1_b, ln2_g, ln2_b, w_router,
           w_gate, w_up, w_down):
    o_q, o_kv, o_kr, o_mix = 0, Q_RANK, Q_RANK + KV_RANK, Q_RANK + KV_RANK + ROPE
    w_kr = w_in[:, :, o_kr:o_kr + ROPE]
    w_in_ext = jnp.concatenate(
        [w_in[:, :, o_q:o_kr], w_in[:, :, o_mix:], _pad_lanes(w_kr), _pad_lanes(_rot_cols(w_kr))],
        axis=-1).astype(BF16)
    uq = w_uq.reshape(DEPTH, Q_RANK, HEADS, NOPE + ROPE)
    uq_rope = uq[..., NOPE:]
    w_uq_ext = jnp.concatenate(
        [uq[..., :NOPE].reshape(DEPTH, Q_RANK, HEADS * NOPE),
         _pad_lanes(uq_rope).reshape(DEPTH, Q_RANK, HEADS * LANES),
         _pad_lanes(_rot_cols(uq_rope)).reshape(DEPTH, Q_RANK, HEADS * LANES)], axis=-1).astype(BF16)
    w_uk_b = w_uk.astype(BF16)
    w_uv_b = w_uv.astype(BF16)
    eye = jnp.eye(len(POOL_WINDOWS), dtype=F32)
    pool_bd = (pool_w[:, :, :, None, :] * eye[None, :, None, :, None]).reshape(DEPTH, MIXW, MIXW).astype(BF16)
    w_out_b = w_out.astype(BF16)
    w_router_pad = _pad_lanes(w_router).astype(BF16)
    cos_a, sin_a = _rope_tables()

    cond = jnp.concatenate([c_ctx[None, :], c, jnp.zeros((SUBLANES - 1 - LAT_B, D_MODEL), F32)], axis=0)
    mod = _ada(cond, w_ada, b_ada)
    kc, vc = _cachekv(cache_ckv, _pad_lanes(cache_krope), w_uk_b, w_uv_b)

    prev = (x_prompt.reshape(N_CTX, D_MODEL), x_sample.reshape(N_LAT, D_MODEL))
    xs = prev + (0,)
    ckv_layers, kr_layers = [], []
    ctx_off = (jnp.arange(CTX_B, dtype=I32) * CTX_T)[:, None, None]
    for l in range(DEPTH):
        mod3 = mod[l].reshape(SUBLANES, 1, 6 * D_MODEL)
        outs = _front(prev, mod3, w_in_ext[l], q_norm[l][None], kv_norm[l][None],
                      w_uq_ext[l], w_uk_b[l], w_uv_b[l], cos_a, sin_a,
                      pool_bd[l], pool_scale[l][None], conv_w[l])
        ckv, kr, q, k, v, bc = outs[:6]
        if l > 0:
            xs = (outs[6], outs[6], N_CTX)
        ckv_layers.append(ckv[:N_CTX].reshape(CTX_B, CTX_T, KV_RANK))
        kr_layers.append(kr[:N_CTX, :ROPE].reshape(CTX_B, CTX_T, ROPE))
        a_ctx = _attn_ctx(q, k, v)
        a_lat = _attn_lat(q, k, v, kc, vc, l)
        x1, u2p, aff = _postmix(a_ctx, a_lat, bc, *xs, mod3, w_out_b[l, :HEADS * VDIM],
                                w_out_b[l, HEADS * VDIM:], ln1_g[l][None], ln1_b[l][None],
                                w_router_pad[l])
        idx_ctx = _route(aff, 0, CTX_B, CTX_T, CTX_CAP)
        cnt_lat, cend = _route_count(aff, N_CTX // LAT_T, LAT_B, LAT_T, LAT_CAP)
        cend_flat = cend[:, :, :N_EXPERTS].transpose(0, 2, 1).reshape(-1)
        idx_lat = _route_rank(cend_flat, cnt_lat, LAT_B, LAT_T, LAT_CAP)
        idx_lat = idx_lat[:, :, :N_EXPERTS].transpose(0, 2, 1)
        idx_ctx = (idx_ctx + ctx_off).transpose(1, 0, 2).reshape(1, N_EXPERTS, ROWS)
        idx_flat = jnp.concatenate([idx_ctx, idx_lat], axis=0).reshape(-1)
        ff = _moe(idx_flat, u2p, aff, w_gate, w_up, w_down, l)
        if l + 1 < DEPTH:
            prev = (x1, ff, mod3, ln2_g[l][None], ln2_b[l][None])
        else:
            y_ctx = _final(x1, ff, mod3, ln2_g[l][None], ln2_b[l][None], 0, N_CTX)
            y_lat = _final(x1, ff, mod3, ln2_g[l][None], ln2_b[l][None], N_CTX, N_LAT)

    y_prompt = y_ctx.reshape(CTX_B, CTX_T, D_MODEL)
    y_sample = y_lat.reshape(LAT_B, LAT_T, D_MODEL)
    new_ckv = jnp.stack(ckv_layers, axis=1)
    new_krope = jnp.stack(kr_layers, axis=1)
    return (y_prompt, y_sample, new_ckv, new_krope)
```

```python
import functools

import jax
import jax.numpy as jnp
from jax import lax
from jax.experimental import pallas as pl
from jax.experimental.pallas import tpu as pltpu

F32 = jnp.float32
BF16 = jnp.bfloat16
I32 = jnp.int32

D_MODEL = 1024
CTX_B, CTX_T = 16, 256
LAT_B, LAT_T = 2, 4096
DEPTH = 4
PAST_LEN = 512
GRID_W = 64
HEADS = 4
NOPE, ROPE, VDIM = 128, 64, 128
Q_RANK, KV_RANK = 384, 256
POOL_WINDOWS = (2, 4, 8, 16)
POOL_GROUP = 64
MIXW = 256
N_EXPERTS = 16
EXPERT_DIM = 512
ROPE_THETA = 10000.0
ATTN_SCALE = (NOPE + ROPE) ** -0.5
LOG2E = 1.4426950408889634
Q_SCALE = ATTN_SCALE * LOG2E
ALPHA = (2 * DEPTH) ** 0.25
RMS_EPS = 1e-6
LN_EPS = 1e-5

N_CTX = CTX_B * CTX_T
N_LAT = LAT_B * LAT_T
N_TOK = N_CTX + N_LAT
CTX_CAP = 2 * CTX_T // N_EXPERTS
LAT_CAP = 2 * LAT_T // N_EXPERTS

LANES = 128
SUBLANES = 8
CHUNKS = D_MODEL // LANES
HEAD_BLOCK = 2 * LANES
VMEM_LIMIT = 56 * 1024 * 1024

TM = 512
SEQ_TILE = 256
HALO = 8
TQ = 512
KV_CHUNK = 512
CHUNK_UNROLL = 8
PASS_TOK = 4096
N_PASS = N_TOK // PASS_TOK
ROWS = 512
TILE_PITCH = ROWS + 8
SCATTER_UNROLL = 8
WORDS = CHUNKS // 2
ROW_BLOCK = 256
POST_BLOCK = 256


def _params(n_axes, arbitrary=False):
    sem = ("arbitrary" if arbitrary else "parallel",) * n_axes
    return pltpu.CompilerParams(dimension_semantics=sem, vmem_limit_bytes=VMEM_LIMIT)


def _mod_row(i, tile):
    start = i * tile
    return jnp.where(start < N_CTX, 0, 1 + (start - N_CTX) // LAT_T)


def _full(shape):
    nd = len(shape)
    return pl.BlockSpec(shape, lambda *_: (0,) * nd)


def _pair_specs(tile, width, lat_row0):
    nct = N_CTX // tile
    off = lat_row0 // tile
    return [pl.BlockSpec((tile, width), lambda i: (jnp.minimum(i, nct - 1), 0)),
            pl.BlockSpec((tile, width), lambda i: (jnp.maximum(i - nct, 0) + off, 0))]


def _pair_load(tile, ctx_ref, lat_ref):
    return jnp.where(pl.program_id(0) < N_CTX // tile, ctx_ref[...], lat_ref[...])


ADA_TN = 1536


def _ada_body(cond_ref, w_ref, b_ref, o_ref):
    c = cond_ref[...]
    a = c * jax.nn.sigmoid(c)
    a_hi = a.astype(BF16)
    a_lo = (a - a_hi.astype(F32)).astype(BF16)
    w = w_ref[0]
    w_hi = w.astype(BF16)
    w_lo = (w - w_hi.astype(F32)).astype(BF16)
    acc = jnp.dot(a_hi, w_hi, preferred_element_type=F32)
    acc += jnp.dot(a_lo, w_hi, preferred_element_type=F32)
    acc += jnp.dot(a_hi, w_lo, preferred_element_type=F32)
    o_ref[0] = acc + b_ref[0]


def _ada(cond, w_ada, b_ada):
    n = 6 * D_MODEL
    return pl.pallas_call(
        _ada_body,
        grid=(DEPTH, n // ADA_TN),
        in_specs=[
            _full((SUBLANES, D_MODEL)),
            pl.BlockSpec((1, D_MODEL, ADA_TN), lambda l, j: (l, 0, j)),
            pl.BlockSpec((1, 1, ADA_TN), lambda l, j: (l, 0, j)),
        ],
        out_specs=pl.BlockSpec((1, SUBLANES, ADA_TN), lambda l, j: (l, 0, j)),
        out_shape=jax.ShapeDtypeStruct((DEPTH, SUBLANES, n), F32),
        compiler_params=_params(2),
        name="ada_mod",
    )(cond, w_ada, b_ada.reshape(DEPTH, 1, n))


IN_EXT = Q_RANK + KV_RANK + 4 * MIXW + 2 * LANES
UQ_EXT = HEADS * NOPE + 2 * HEADS * LANES


def _rms(x, g):
    return x * lax.rsqrt(jnp.mean(x * x, axis=-1, keepdims=True) + RMS_EPS) * g


MIX_COL0 = Q_RANK + KV_RANK


def _ffn_rows(ff_ref, n):
    return jnp.concatenate([ff_ref[pl.ds(c, n, stride=CHUNKS), :] for c in range(CHUNKS)], axis=-1)


def _front_body(*refs, fused_ln):
    if fused_ln:
        (x1_ref, ff_ref, x1p_ref, x1n_ref, ffp_ref, ffn_ref, g2_ref, lg_ref, lb_ref,
         sh_ref, sc_ref, win_ref, qn_ref, kvn_ref, wuq_ref, wuk_ref, wuv_ref, cos_ref, sin_ref,
         pw_ref, ps_ref, cw_ref,
         ckv_ref, kr_ref, q_ref, k_ref, v_ref, bc_ref, x_ref) = refs

        def norm2(x1, ffn):
            return _layer_norm(ALPHA * x1 + g2_ref[0] * ffn, lg_ref[...], lb_ref[...])

        x = norm2(x1_ref[...], _ffn_rows(ff_ref, TM))
        x_ref[...] = x
        x_halo = jnp.concatenate([norm2(x1p_ref[...], _ffn_rows(ffp_ref, HALO)),
                                  norm2(x1n_ref[...], _ffn_rows(ffn_ref, HALO))], axis=0)
    else:
        (xc_ref, xl_ref, xp_ref, xn_ref,
         sh_ref, sc_ref, win_ref, qn_ref, kvn_ref, wuq_ref, wuk_ref, wuv_ref, cos_ref, sin_ref,
         pw_ref, ps_ref, cw_ref,
         ckv_ref, kr_ref, q_ref, k_ref, v_ref, bc_ref) = refs
        x = _pair_load(TM, xc_ref, xl_ref)
        x_halo = jnp.concatenate([xp_ref[...], xn_ref[...]], axis=0)

    u = jnp.concatenate([x, x_halo], axis=0) * (1.0 + sc_ref[0]) + sh_ref[0]
    h_all = jnp.dot(u.astype(BF16), win_ref[...], preferred_element_type=F32)
    h = h_all[:TM]
    mix_halo = h_all[TM:, MIX_COL0:MIX_COL0 + 4 * MIXW]
    o = 0
    q_c = h[:, o:o + Q_RANK]; o += Q_RANK
    kv_c = h[:, o:o + KV_RANK]; o += KV_RANK
    mix = h[:, o:o + 4 * MIXW]; o += 4 * MIXW
    kr_a = h[:, o:o + LANES]; o += LANES
    kr_b = h[:, o:o + LANES]
    kr_ref[...] = kr_a
    cos_a = cos_ref[...]
    sin_a = sin_ref[...]

    n_sub = TM // SEQ_TILE
    for s in range(n_sub):
        lo = s * SEQ_TILE
        prev = mix_halo[:HALO] if s == 0 else mix[lo - HALO:lo]
        nxt = mix_halo[HALO:] if s == n_sub - 1 else mix[lo + SEQ_TILE:lo + SEQ_TILE + HALO]
        bc_ref[lo:lo + SEQ_TILE, :] = _seqmix_math(
            pl.program_id(0) * n_sub + s, mix[lo:lo + SEQ_TILE], prev, nxt,
            pw_ref[...], ps_ref[...], cw_ref[...])

    ckv = _rms(kv_c, kvn_ref[...])
    ckv_ref[...] = ckv
    ckv_b = ckv.astype(BF16)
    k_nope = jnp.dot(ckv_b, wuk_ref[...], preferred_element_type=F32)
    v_ref[...] = jnp.dot(ckv_b, wuv_ref[...], preferred_element_type=F32).astype(BF16)
    k_rope = kr_a * cos_a + kr_b * sin_a
    parts = []
    for hd in range(HEADS):
        parts += [k_nope[:, hd * NOPE:(hd + 1) * NOPE], k_rope]
    k_ref[...] = jnp.concatenate(parts, axis=-1).astype(BF16)

    qn = _rms(q_c, qn_ref[...]).astype(BF16)
    qq = jnp.dot(qn, wuq_ref[...], preferred_element_type=F32)
    ra = HEADS * NOPE
    rb = ra + HEADS * LANES
    parts = []
    for hd in range(HEADS):
        q_rope = (qq[:, ra + hd * LANES:ra + (hd + 1) * LANES] * cos_a
                  + qq[:, rb + hd * LANES:rb + (hd + 1) * LANES] * sin_a)
        parts += [qq[:, hd * NOPE:(hd + 1) * NOPE] * Q_SCALE, q_rope * Q_SCALE]
    q_ref[...] = jnp.concatenate(parts, axis=-1).astype(BF16)


def _halo_specs(rows_per_token, width, lat_row0, n_rows):
    nct = N_CTX // TM
    nblk = n_rows // HALO

    def first_block(i):
        return (jnp.maximum(i - nct, 0) * TM + lat_row0) // HALO

    shape = (HALO * rows_per_token, width)
    return [pl.BlockSpec(shape, lambda i: (jnp.maximum(first_block(i) - 1, 0), 0)),
            pl.BlockSpec(shape, lambda i: (jnp.minimum(first_block(i) + TM // HALO, nblk - 1), 0))]


def _front(prev, mod3, w_in_ext, q_norm, kv_norm, w_uq_ext, w_uk, w_uv, cos_a, sin_a,
           pool_bd, pool_scale, conv_w):
    row = lambda i: (i, 0)
    modspec = lambda k: pl.BlockSpec((1, 1, D_MODEL), lambda i: (_mod_row(i, TM), 0, k))
    fused_ln = len(prev) == 5
    outs = [
        jax.ShapeDtypeStruct((N_TOK, KV_RANK), F32),
        jax.ShapeDtypeStruct((N_TOK, LANES), F32),
        jax.ShapeDtypeStruct((N_TOK, HEADS * HEAD_BLOCK), BF16),
        jax.ShapeDtypeStruct((N_TOK, HEADS * HEAD_BLOCK), BF16),
        jax.ShapeDtypeStruct((N_TOK, HEADS * VDIM), BF16),
        jax.ShapeDtypeStruct((N_TOK, 2 * MIXW), BF16),
    ]
    if fused_ln:
        x1, ff, mod3_prev, ln_g, ln_b = prev
        outs.append(jax.ShapeDtypeStruct((N_TOK, D_MODEL), F32))
        lead_specs = ([pl.BlockSpec((TM, D_MODEL), row), pl.BlockSpec((TM * CHUNKS, LANES), row)]
                      + _halo_specs(1, D_MODEL, N_CTX, N_TOK)
                      + _halo_specs(CHUNKS, LANES, N_CTX, N_TOK)
                      + [pl.BlockSpec((1, 1, D_MODEL), lambda i: (_mod_row(i, TM), 0, 5)),
                         _full((1, D_MODEL)), _full((1, D_MODEL))])
        lead_args = (x1, ff, x1, x1, ff, ff, mod3_prev, ln_g, ln_b)
    else:
        xc, xl = prev
        lead_specs = _pair_specs(TM, D_MODEL, 0) + _halo_specs(1, D_MODEL, 0, N_LAT)
        lead_args = (xc, xl, xl, xl)
    return pl.pallas_call(
        functools.partial(_front_body, fused_ln=fused_ln),
        grid=(N_TOK // TM,),
        in_specs=lead_specs + [
            modspec(0), modspec(1),
            _full((D_MODEL, IN_EXT)),
            _full((1, Q_RANK)), _full((1, KV_RANK)),
            _full((Q_RANK, UQ_EXT)),
            _full((KV_RANK, HEADS * NOPE)), _full((KV_RANK, HEADS * VDIM)),
            pl.BlockSpec((TM, LANES), row), pl.BlockSpec((TM, LANES), row),
            _full((MIXW, MIXW)), _full((1, MIXW)), _full((3, MIXW)),
        ],
        out_specs=tuple(pl.BlockSpec((TM, s.shape[1]), row) for s in outs),
        out_shape=tuple(outs),
        compiler_params=_params(1),
        name="front_ln" if fused_ln else "front",
    )(*lead_args, mod3, mod3, w_in_ext, q_norm, kv_norm, w_uq_ext, w_uk, w_uv, cos_a, sin_a,
      pool_bd, pool_scale, conv_w)


def _cachekv_body(ckv_ref, kr_ref, wuk_ref, wuv_ref, k_ref, v_ref):
    c = ckv_ref[0, 0].astype(BF16)
    k_nope = jnp.dot(c, wuk_ref[0], preferred_element_type=F32)
    v_ref[0, 0] = jnp.dot(c, wuv_ref[0], preferred_element_type=F32).astype(BF16)
    kr = kr_ref[0, 0]
    parts = []
    for hd in range(HEADS):
        parts += [k_nope[:, hd * NOPE:(hd + 1) * NOPE], kr]
    k_ref[0, 0] = jnp.concatenate(parts, axis=-1).astype(BF16)


def _cachekv(cache_ckv, kr_pad, w_uk, w_uv):
    bl = lambda b, l: (b, l, 0, 0)
    wl = lambda b, l: (l, 0, 0)
    return pl.pallas_call(
        _cachekv_body,
        grid=(LAT_B, DEPTH),
        in_specs=[
            pl.BlockSpec((1, 1, PAST_LEN, KV_RANK), bl),
            pl.BlockSpec((1, 1, PAST_LEN, LANES), bl),
            pl.BlockSpec((1, KV_RANK, HEADS * NOPE), wl),
            pl.BlockSpec((1, KV_RANK, HEADS * VDIM), wl),
        ],
        out_specs=(pl.BlockSpec((1, 1, PAST_LEN, HEADS * HEAD_BLOCK), bl),
                   pl.BlockSpec((1, 1, PAST_LEN, HEADS * VDIM), bl)),
        out_shape=(jax.ShapeDtypeStruct((LAT_B, DEPTH, PAST_LEN, HEADS * HEAD_BLOCK), BF16),
                   jax.ShapeDtypeStruct((LAT_B, DEPTH, PAST_LEN, HEADS * VDIM), BF16)),
        compiler_params=_params(2),
        name="cache_kv",
    )(cache_ckv, kr_pad, w_uk, w_uv)


EXT = SEQ_TILE + 2 * HALO


def _shift_up(x, k):
    return pltpu.roll(x, x.shape[0] - k, axis=0)


def _shift_down(x, k):
    return pltpu.roll(x, k, axis=0)


def _seqmix_math(j, main, prev, nxt, pw, ps, cw):
    n_ctx_tiles = N_CTX // SEQ_TILE
    tiles_per_lat = LAT_T // SEQ_TILE
    is_ctx = j < n_ctx_tiles
    jj = jnp.where(is_ctx, 0, (j - n_ctx_tiles) % tiles_per_lat)
    first = jj == 0
    last = jnp.where(is_ctx, True, jj == tiles_per_lat - 1)
    t_seq = jnp.where(is_ctx, CTX_T, LAT_T)

    prev = jnp.where(first, 0.0, prev)
    nxt = jnp.where(last, 0.0, nxt)
    ext = jnp.concatenate([prev, main, nxt], axis=0)
    p = ext[:, 0:MIXW]
    g_b = main[:, MIXW:2 * MIXW]
    g_c = ext[:, 2 * MIXW:3 * MIXW]
    h_in = ext[:, 3 * MIXW:4 * MIXW]

    sums = {}
    b = p
    for w in POOL_WINDOWS:
        b = b + _shift_up(b, w // 2)
        sums[w] = _shift_down(b, w // 2)[HALO:HALO + SEQ_TILE]
    tpos = jj * SEQ_TILE + lax.broadcasted_iota(I32, (SEQ_TILE, 1), 0)
    lane = lax.broadcasted_iota(I32, (1, MIXW), 1)
    num = None
    den = None
    for g, w in enumerate(POOL_WINDOWS):
        lo = jnp.maximum(tpos - w // 2, 0)
        hi = jnp.minimum(tpos + w - w // 2, t_seq)
        cnt = (hi - lo).astype(F32)
        if num is None:
            num, den = sums[w], jnp.broadcast_to(cnt, (SEQ_TILE, MIXW))
        else:
            sel = lane >= g * POOL_GROUP
            num = jnp.where(sel, sums[w], num)
            den = jnp.where(sel, cnt, den)
    pooled = num / den - p[HALO:HALO + SEQ_TILE]
    b_out = jnp.dot(pooled.astype(BF16), pw, preferred_element_type=F32) * ps

    y = g_c * h_in
    conv = (_shift_down(y, 1) * cw[0:1] + y * cw[1:2] + _shift_up(y, 1) * cw[2:3])
    c_out = g_b * conv[HALO:HALO + SEQ_TILE]
    return jnp.concatenate([b_out, c_out], axis=-1).astype(BF16)


_NT = (((1,), (1,)), ((), ()))


def _v_ext(v):
    return jnp.concatenate([v, jnp.ones(v.shape, v.dtype)], axis=-1)


def _attn_ctx_body(q_ref, k_ref, v_ref, o_ref):
    outs = []
    for hd in range(HEADS):
        qh = q_ref[:, hd * HEAD_BLOCK:(hd + 1) * HEAD_BLOCK]
        kh = k_ref[:, hd * HEAD_BLOCK:(hd + 1) * HEAD_BLOCK]
        vh = v_ref[:, hd * VDIM:(hd + 1) * VDIM]
        s = lax.dot_general(qh, kh, _NT, preferred_element_type=F32)
        m = jnp.max(s, axis=-1, keepdims=True)
        p = jnp.exp2(s - m)
        acc = jnp.dot(p.astype(BF16), _v_ext(vh), preferred_element_type=F32)
        outs.append(acc[:, :VDIM] / acc[:, VDIM:])
    o_ref[...] = jnp.concatenate(outs, axis=-1).astype(BF16)


def _attn_ctx(q, k, v):
    row = lambda b: (b, 0)
    return pl.pallas_call(
        _attn_ctx_body,
        grid=(CTX_B,),
        in_specs=[pl.BlockSpec((CTX_T, HEADS * HEAD_BLOCK), row),
                  pl.BlockSpec((CTX_T, HEADS * HEAD_BLOCK), row),
                  pl.BlockSpec((CTX_T, HEADS * VDIM), row)],
        out_specs=pl.BlockSpec((CTX_T, HEADS * VDIM), row),
        out_shape=jax.ShapeDtypeStruct((N_CTX, HEADS * VDIM), BF16),
        compiler_params=_params(1),
        name="attn_ctx",
    )(q, k, v)


def _attn_lat_body(q_ref, kc_ref, vc_ref, ko_ref, vo_ref, o_ref, m_ref, acc_ref):
    reps = KV_CHUNK // LANES

    def update(hd, k, v, first):
        qh = q_ref[:, hd * HEAD_BLOCK:(hd + 1) * HEAD_BLOCK]
        s = lax.dot_general(qh, k, _NT, preferred_element_type=F32)
        mx = jnp.max(s, axis=-1, keepdims=True)
        if first:
            m_new = jnp.broadcast_to(mx, (TQ, LANES))
        else:
            m_old = m_ref[hd]
            m_new = jnp.maximum(m_old, mx)
        p = jnp.exp2(s - jnp.concatenate([m_new] * reps, axis=-1))
        pv = jnp.dot(p.astype(BF16), _v_ext(v), preferred_element_type=F32)
        if first:
            acc_ref[hd] = pv
        else:
            a = jnp.exp2(m_old - m_new)
            acc_ref[hd] = acc_ref[hd] * jnp.concatenate([a, a], axis=-1) + pv
        m_ref[hd] = m_new

    def head_slices(hd):
        return slice(hd * HEAD_BLOCK, (hd + 1) * HEAD_BLOCK), slice(hd * VDIM, (hd + 1) * VDIM)

    for hd in range(HEADS):
        ks, vs = head_slices(hd)
        update(hd, kc_ref[0, 0, :, ks], vc_ref[0, 0, :, vs], True)

    def body(c, _):
        for j in range(CHUNK_UNROLL):
            rows = pl.ds(pl.multiple_of((c * CHUNK_UNROLL + j) * KV_CHUNK, KV_CHUNK), KV_CHUNK)
            for hd in range(HEADS):
                ks, vs = head_slices(hd)
                update(hd, ko_ref[rows, ks], vo_ref[rows, vs], False)
        return 0

    lax.fori_loop(0, LAT_T // (KV_CHUNK * CHUNK_UNROLL), body, 0)
    outs = [acc_ref[hd][:, :VDIM] / acc_ref[hd][:, VDIM:] for hd in range(HEADS)]
    o_ref[...] = jnp.concatenate(outs, axis=-1).astype(BF16)


def _attn_lat(q, k, v, kc, vc, layer):
    qt = LAT_T // TQ
    ctx_tiles = N_CTX // TQ
    ctx_blocks = N_CTX // LAT_T
    return pl.pallas_call(
        _attn_lat_body,
        grid=(LAT_B, qt),
        in_specs=[
            pl.BlockSpec((TQ, HEADS * HEAD_BLOCK), lambda b, i: (ctx_tiles + b * qt + i, 0)),
            pl.BlockSpec((1, 1, PAST_LEN, HEADS * HEAD_BLOCK), lambda b, i: (b, layer, 0, 0)),
            pl.BlockSpec((1, 1, PAST_LEN, HEADS * VDIM), lambda b, i: (b, layer, 0, 0)),
            pl.BlockSpec((LAT_T, HEADS * HEAD_BLOCK), lambda b, i: (ctx_blocks + b, 0)),
            pl.BlockSpec((LAT_T, HEADS * VDIM), lambda b, i: (ctx_blocks + b, 0)),
        ],
        out_specs=pl.BlockSpec((TQ, HEADS * VDIM), lambda b, i: (b * qt + i, 0)),
        out_shape=jax.ShapeDtypeStruct((N_LAT, HEADS * VDIM), BF16),
        scratch_shapes=[pltpu.VMEM((HEADS, TQ, LANES), F32),
                        pltpu.VMEM((HEADS, TQ, 2 * VDIM), F32)],
        compiler_params=_params(2),
        name="attn_lat",
    )(q, kc, vc, k, v)


def _layer_norm(y, g, b):
    mu = jnp.mean(y, axis=-1, keepdims=True)
    d = y - mu
    var = jnp.mean(d * d, axis=-1, keepdims=True)
    return d * lax.rsqrt(var + LN_EPS) * g + b


def _postmix_body(actx_ref, alat_ref, bc_ref, xc_ref, xl_ref, g1_ref, sh2_ref, sc2_ref, wa_ref, wbc_ref,
                  lg_ref, lb_ref, wr_ref, x1_ref, u2p_ref, aff_ref):
    is_ctx = pl.program_id(0) < N_CTX // TM
    lane = lax.broadcasted_iota(I32, (1, LANES), 1)
    blocks = [slice(b * POST_BLOCK, (b + 1) * POST_BLOCK) for b in range(TM // POST_BLOCK)]
    mixes = []
    for rows in blocks:
        a = jnp.where(is_ctx, actx_ref[rows, :], alat_ref[rows, :])
        mix = jnp.dot(a, wa_ref[...], preferred_element_type=F32)
        mixes.append(mix + jnp.dot(bc_ref[rows, :], wbc_ref[...], preferred_element_type=F32))
    u2s = []
    for b, rows in enumerate(blocks):
        x = jnp.where(is_ctx, xc_ref[rows, :], xl_ref[rows, :])
        x1 = _layer_norm(ALPHA * x + g1_ref[0] * mixes[b], lg_ref[...], lb_ref[...])
        x1_ref[rows, :] = x1
        u2 = x1 * (1.0 + sc2_ref[0]) + sh2_ref[0]
        for c in range(WORDS):
            lo = u2[:, c * LANES:(c + 1) * LANES]
            hi = u2[:, (c + WORDS) * LANES:(c + WORDS + 1) * LANES]
            u2p_ref[pl.ds(b * POST_BLOCK * WORDS + c, POST_BLOCK, stride=WORDS), :] = (
                pltpu.pack_elementwise([lo, hi], packed_dtype=BF16))
        u2s.append(u2.astype(BF16))
    for b, rows in enumerate(blocks):
        logits = jnp.dot(u2s[b], wr_ref[...], preferred_element_type=F32)
        logits = jnp.where(lane < N_EXPERTS, logits, -jnp.inf)
        m = jnp.max(logits, axis=-1, keepdims=True)
        e = jnp.exp(logits - m)
        aff_ref[rows, :] = e / jnp.sum(e, axis=-1, keepdims=True)


def _postmix(a_ctx, a_lat, bc, xc, xl, lat_row0, mod3, w_out_a, w_out_bc, ln_g, ln_b, w_router_pad):
    row = lambda i: (i, 0)
    modspec = lambda k: pl.BlockSpec((1, 1, D_MODEL), lambda i: (_mod_row(i, TM), 0, k))
    return pl.pallas_call(
        _postmix_body,
        grid=(N_TOK // TM,),
        in_specs=_pair_specs(TM, HEADS * VDIM, 0) + [
            pl.BlockSpec((TM, 2 * MIXW), row),
        ] + _pair_specs(TM, D_MODEL, lat_row0) + [
            modspec(2), modspec(3), modspec(4),
            _full((HEADS * VDIM, D_MODEL)), _full((2 * MIXW, D_MODEL)),
            _full((1, D_MODEL)), _full((1, D_MODEL)),
            _full((D_MODEL, LANES)),
        ],
        out_specs=(pl.BlockSpec((TM, D_MODEL), row),
                   pl.BlockSpec((TM * WORDS, LANES), row),
                   pl.BlockSpec((TM, LANES), row)),
        out_shape=(jax.ShapeDtypeStruct((N_TOK, D_MODEL), F32),
                   jax.ShapeDtypeStruct((N_TOK * WORDS, LANES), jnp.uint32),
                   jax.ShapeDtypeStruct((N_TOK, LANES), F32)),
        compiler_params=_params(1),
        name="postmix",
    )(a_ctx, a_lat, bc, xc, xl, mod3, mod3, mod3, w_out_a, w_out_bc, ln_g, ln_b, w_router_pad)


PREFIX_CHUNK = 256


def _route_ctx_body(aff_ref, idx_ref):
    n_rows = CTX_B * N_EXPERTS
    dense = jnp.concatenate(
        [aff_ref[b * CTX_T:(b + 1) * CTX_T, :].T[:N_EXPERTS] for b in range(CTX_B)], axis=0)

    def search(i, thr):
        cand = thr | jnp.left_shift(jnp.int32(1), 30 - i)
        n = jnp.sum((dense >= pltpu.bitcast(cand, F32)).astype(I32), axis=1, keepdims=True)
        return jnp.where(n >= CTX_CAP, cand, thr)

    thr_bits = lax.fori_loop(0, 31, search, jnp.zeros((n_rows, 1), I32))
    thr = pltpu.bitcast(thr_bits, F32)
    above = pltpu.bitcast(thr_bits + 1, F32)
    gt = (dense > thr) & (dense >= above)
    eq = (dense >= thr) & jnp.logical_not(gt)
    n_gt = jnp.sum(gt.astype(I32), axis=1, keepdims=True)
    need = (CTX_CAP - n_gt).astype(F32)

    ri = lax.broadcasted_iota(I32, (CTX_T, CTX_T), 0)
    ci = lax.broadcasted_iota(I32, (CTX_T, CTX_T), 1)
    tri = jnp.where(ri <= ci, 1.0, 0.0).astype(BF16)
    p_eq = jnp.dot(jnp.where(eq, 1.0, 0.0).astype(BF16), tri, preferred_element_type=F32)
    sel = jnp.where(gt, 1.0, jnp.where(eq & (p_eq <= need), 1.0, 0.0))
    cnt = jnp.dot(sel.astype(BF16), tri, preferred_element_type=F32)

    ones = jnp.ones((CTX_T, LANES), BF16)
    lane = lax.broadcasted_iota(I32, (1, LANES), 1)
    out = jnp.zeros((n_rows, LANES), F32)
    for r in range(CTX_CAP):
        le = jnp.where(cnt <= float(r), 1.0, 0.0).astype(BF16)
        out = jnp.where(lane == r, jnp.dot(le, ones, preferred_element_type=F32), out)
    idx_ref[...] = out.astype(I32)


def _route_ctx(aff):
    idx = pl.pallas_call(
        _route_ctx_body,
        grid=(1,),
        in_specs=[pl.BlockSpec((N_CTX, LANES), lambda i: (0, 0))],
        out_specs=_full((CTX_B * N_EXPERTS, LANES)),
        out_shape=jax.ShapeDtypeStruct((CTX_B * N_EXPERTS, LANES), I32),
        compiler_params=_params(1),
        name="route_ctx",
    )(aff)
    return idx[:, :CTX_CAP].reshape(CTX_B, N_EXPERTS, CTX_CAP)


RANK_TILE = LANES


def _expert_row(col):
    full = jnp.concatenate([jnp.broadcast_to(col, (N_EXPERTS, LANES)),
                            jnp.zeros((LANES - N_EXPERTS, LANES), col.dtype)], axis=0)
    return full.T[0:1]


def _route_count_body(aff_ref, cnt_ref, cend_ref, *, seq, cap):
    n_chunks = seq // PREFIX_CHUNK
    dense = aff_ref[...].T[:N_EXPERTS]

    def search(i, thr):
        cand = thr | jnp.left_shift(jnp.int32(1), 30 - i)
        n = jnp.sum((dense >= pltpu.bitcast(cand, F32)).astype(I32), axis=1, keepdims=True)
        return jnp.where(n >= cap, cand, thr)

    thr_bits = lax.fori_loop(0, 31, search, jnp.zeros((N_EXPERTS, 1), I32))
    thr_col = pltpu.bitcast(thr_bits, F32)
    above_col = pltpu.bitcast(thr_bits + 1, F32)
    n_gt = jnp.sum(((dense > thr_col) & (dense >= above_col)).astype(I32), axis=1, keepdims=True)
    thr = _expert_row(thr_col)
    above = _expert_row(above_col)
    need = _expert_row((cap - n_gt).astype(F32))

    ri = lax.broadcasted_iota(I32, (PREFIX_CHUNK, PREFIX_CHUNK), 0)
    ci = lax.broadcasted_iota(I32, (PREFIX_CHUNK, PREFIX_CHUNK), 1)
    tri = jnp.where(ci <= ri, 1.0, 0.0).astype(BF16)
    carry_eq = jnp.zeros((1, LANES), F32)
    carry_sel = jnp.zeros((1, LANES), F32)
    per = PREFIX_CHUNK // RANK_TILE
    for c in range(n_chunks):
        rows = slice(c * PREFIX_CHUNK, (c + 1) * PREFIX_CHUNK)
        a = aff_ref[rows, :]
        gt = (a > thr) & (a >= above)
        eq = (a >= thr) & jnp.logical_not(gt)
        p_eq = jnp.dot(tri, jnp.where(eq, 1.0, 0.0).astype(BF16),
                       preferred_element_type=F32) + carry_eq
        sel = jnp.where(gt, 1.0, jnp.where(eq & (p_eq <= need), 1.0, 0.0))
        p_sel = jnp.dot(tri, sel.astype(BF16), preferred_element_type=F32) + carry_sel
        dense_cnt = p_sel.T[:N_EXPERTS]
        for k in range(per):
            last = (k + 1) * RANK_TILE - 1
            cnt_ref[c * per + k] = dense_cnt[:, k * RANK_TILE:(k + 1) * RANK_TILE]
            cend_ref[0, c * per + k:c * per + k + 1, :] = p_sel[last:last + 1, :].astype(I32)
        carry_eq = p_eq[PREFIX_CHUNK - 1:PREFIX_CHUNK, :]
        carry_sel = p_sel[PREFIX_CHUNK - 1:PREFIX_CHUNK, :]


def _route_count(aff, first_block, n_req, seq, cap):
    n_tiles = seq // RANK_TILE
    return pl.pallas_call(
        functools.partial(_route_count_body, seq=seq, cap=cap),
        grid=(n_req,),
        in_specs=[pl.BlockSpec((seq, LANES), lambda b: (first_block + b, 0))],
        out_specs=(pl.BlockSpec((n_tiles, N_EXPERTS, RANK_TILE), lambda b: (b, 0, 0)),
                   pl.BlockSpec((1, n_tiles, LANES), lambda b: (b, 0, 0))),
        out_shape=(jax.ShapeDtypeStruct((n_req * n_tiles, N_EXPERTS, RANK_TILE), F32),
                   jax.ShapeDtypeStruct((n_req, n_tiles, LANES), I32)),
        compiler_params=_params(1),
        name=f"route_count_{seq}",
    )(aff)


def _route_rank_body(cend_ref, cnt_ref, idx_ref, *bufs, seq, cap):
    b = pl.program_id(0)
    n_tiles = seq // RANK_TILE
    rank = lax.broadcasted_iota(I32, (RANK_TILE, RANK_TILE), 0).astype(F32)
    ones = jnp.ones((RANK_TILE, LANES), BF16)

    def tile(j, _):
        counts = cnt_ref[j]
        for e in range(N_EXPERTS):
            prev = (b * N_EXPERTS + e) * n_tiles + j - 1
            start = jnp.where(j > 0, cend_ref[jnp.maximum(prev, 0)], 0)
            local = counts[e:e + 1, :] - jnp.asarray(start, F32)
            le = jnp.where(local <= rank, 1.0, 0.0).astype(BF16)
            pos = jnp.dot(le, ones, preferred_element_type=F32) + jnp.asarray(j * RANK_TILE, F32)
            bufs[e][pl.ds(start, RANK_TILE), :] = pos
        return 0

    lax.fori_loop(0, n_tiles, tile, 0)
    lane = lax.broadcasted_iota(I32, (1, LANES), 1)
    out = jnp.zeros((cap, LANES), F32)
    for e in range(N_EXPERTS):
        out = jnp.where(lane == e, bufs[e][0:cap, :], out)
    idx_ref[0] = out.astype(I32)


def _route_rank(cend_flat, cnt, n_req, seq, cap):
    n_tiles = seq // RANK_TILE
    grid_spec = pltpu.PrefetchScalarGridSpec(
        num_scalar_prefetch=1,
        grid=(n_req,),
        in_specs=[pl.BlockSpec((n_tiles, N_EXPERTS, RANK_TILE), lambda b, c: (b, 0, 0))],
        out_specs=pl.BlockSpec((1, cap, LANES), lambda b, c: (b, 0, 0)),
        scratch_shapes=[pltpu.VMEM((cap + RANK_TILE, LANES), F32)] * N_EXPERTS,
    )
    return pl.pallas_call(
        functools.partial(_route_rank_body, seq=seq, cap=cap),
        grid_spec=grid_spec,
        out_shape=jax.ShapeDtypeStruct((n_req, cap, LANES), I32),
        compiler_params=_params(1),
        name=f"route_rank_{seq}",
    )(cend_flat, cnt)


def _gather_rows(idx_ref, base, u2p_ref, aff_ref, tile_ref, gate_ref, r0, n):
    for i in range(n):
        r = r0 + i
        t = idx_ref[base + r]
        slab = u2p_ref[pl.ds(pl.multiple_of(t * WORDS, WORDS), WORDS), :]
        tile_ref[pl.ds(r, WORDS, stride=TILE_PITCH), :] = slab
        gate_ref[pl.ds(r, 1), :] = aff_ref[pl.ds(t, 1), :]


def _scatter_rows(idx_ref, base, y_ref, acc_ref, r0, n):
    for g0 in range(0, n, SCATTER_UNROLL):
        rows = [r0 + g0 + i for i in range(SCATTER_UNROLL)]
        dst = [pl.ds(pl.multiple_of(idx_ref[base + r] * CHUNKS, CHUNKS), CHUNKS) for r in rows]
        vals = [acc_ref[d, :] + y_ref[pl.ds(pl.multiple_of(r * CHUNKS, CHUNKS), CHUNKS), :]
                for d, r in zip(dst, rows)]
        for d, v in zip(dst, vals):
            acc_ref[d, :] = v


def _expert_rows(e, tile_ref, gate_ref, wg, wu, wd, y_ref, r0, n):
    halves = [[], []]
    for c in range(WORDS):
        w = tile_ref[pl.ds(c * TILE_PITCH + r0, n), :]
        for k in range(2):
            halves[k].append(pltpu.unpack_elementwise(w, index=k, packed_dtype=BF16,
                                                      unpacked_dtype=F32))
    x = jnp.concatenate(halves[0] + halves[1], axis=-1).astype(BF16)
    lane = lax.broadcasted_iota(I32, (1, LANES), 1)
    gate = jnp.sum(jnp.where(lane == e, gate_ref[pl.ds(r0, n), :], 0.0), axis=-1, keepdims=True)
    hg = jnp.dot(x, wg, preferred_element_type=F32)
    hu = jnp.dot(x, wu, preferred_element_type=F32)
    hidden = (hg * jax.nn.sigmoid(hg) * hu).astype(BF16)
    y = jnp.dot(hidden, wd, preferred_element_type=F32) * gate
    for c in range(CHUNKS):
        y_ref[pl.ds(r0 * CHUNKS + c, n, stride=CHUNKS), :] = y[:, c * LANES:(c + 1) * LANES]


def _moe_body(idx_ref, u2p_ref, aff_ref, wg_ref, wu_ref, wd_ref, acc_ref,
              tile_a, tile_b, gate_a, gate_b, y_a, y_b):
    p = pl.program_id(0)
    e = pl.program_id(1)
    step = p * N_EXPERTS + e
    base = step * ROWS
    base_next = jnp.minimum(step + 1, N_PASS * N_EXPERTS - 1) * ROWS
    base_prev = jnp.where(e == 0, base, base - ROWS)

    @pl.when(e == 0)
    def _():
        acc_ref[...] = jnp.zeros_like(acc_ref)
        y_b[...] = jnp.zeros_like(y_b)

        def gather(c, _):
            _gather_rows(idx_ref, base, u2p_ref, aff_ref, tile_a, gate_a, c * SUBLANES, SUBLANES)
            return 0

        lax.fori_loop(0, ROWS // SUBLANES, gather, 0)

    def run(tile_cur, gate_cur, y_cur, tile_nxt, gate_nxt, y_prv):
        wg = wg_ref[0, 0].astype(BF16)
        wu = wu_ref[0, 0].astype(BF16)
        wd = wd_ref[0, 0].astype(BF16)
        for blk in range(ROWS // ROW_BLOCK):
            r0 = blk * ROW_BLOCK
            _gather_rows(idx_ref, base_next, u2p_ref, aff_ref, tile_nxt, gate_nxt, r0, ROW_BLOCK)
            _expert_rows(e, tile_cur, gate_cur, wg, wu, wd, y_cur, r0, ROW_BLOCK)
            _scatter_rows(idx_ref, base_prev, y_prv, acc_ref, r0, ROW_BLOCK)

    @pl.when(e % 2 == 0)
    def _():
        run(tile_a, gate_a, y_a, tile_b, gate_b, y_b)

    @pl.when(e % 2 == 1)
    def _():
        run(tile_b, gate_b, y_b, tile_a, gate_a, y_a)

    @pl.when(e == N_EXPERTS - 1)
    def _():
        def scatter(c, _):
            _scatter_rows(idx_ref, base, y_b, acc_ref, c * SCATTER_UNROLL, SCATTER_UNROLL)
            return 0

        lax.fori_loop(0, ROWS // SCATTER_UNROLL, scatter, 0)


def _moe(idx_flat, u2p, aff, w_gate, w_up, w_down, layer):
    one = pl.Buffered(1)
    grid_spec = pltpu.PrefetchScalarGridSpec(
        num_scalar_prefetch=1,
        grid=(N_PASS, N_EXPERTS),
        in_specs=[
            pl.BlockSpec((PASS_TOK * WORDS, LANES), lambda p, e, idx: (p, 0), pipeline_mode=one),
            pl.BlockSpec((PASS_TOK, LANES), lambda p, e, idx: (p, 0), pipeline_mode=one),
            pl.BlockSpec((1, 1, D_MODEL, EXPERT_DIM), lambda p, e, idx: (layer, e, 0, 0)),
            pl.BlockSpec((1, 1, D_MODEL, EXPERT_DIM), lambda p, e, idx: (layer, e, 0, 0)),
            pl.BlockSpec((1, 1, EXPERT_DIM, D_MODEL), lambda p, e, idx: (layer, e, 0, 0)),
        ],
        out_specs=pl.BlockSpec((PASS_TOK * CHUNKS, LANES), lambda p, e, idx: (p, 0),
                               pipeline_mode=one),
        scratch_shapes=[
            pltpu.VMEM((WORDS * TILE_PITCH, LANES), jnp.uint32),
            pltpu.VMEM((WORDS * TILE_PITCH, LANES), jnp.uint32),
            pltpu.VMEM((ROWS, LANES), F32),
            pltpu.VMEM((ROWS, LANES), F32),
            pltpu.VMEM((ROWS * CHUNKS, LANES), F32),
            pltpu.VMEM((ROWS * CHUNKS, LANES), F32),
        ],
    )
    return pl.pallas_call(
        _moe_body,
        grid_spec=grid_spec,
        out_shape=jax.ShapeDtypeStruct((N_TOK * CHUNKS, LANES), F32),
        compiler_params=_params(2, arbitrary=True),
        name="moe",
    )(idx_flat, u2p, aff, w_gate, w_up, w_down)


def _final_body(x1_ref, ff_ref, g2_ref, lg_ref, lb_ref, o_ref):
    ffn = jnp.concatenate([ff_ref[pl.ds(c, TM, stride=CHUNKS), :] for c in range(CHUNKS)], axis=-1)
    o_ref[...] = _layer_norm(ALPHA * x1_ref[...] + g2_ref[0] * ffn, lg_ref[...], lb_ref[...])


def _final(x1, ff, mod3, ln_g, ln_b, row0=0, n_rows=N_TOK):
    t0 = row0 // TM
    row = lambda i: (i + t0, 0)
    return pl.pallas_call(
        _final_body,
        grid=(n_rows // TM,),
        in_specs=[
            pl.BlockSpec((TM, D_MODEL), row),
            pl.BlockSpec((TM * CHUNKS, LANES), row),
            pl.BlockSpec((1, 1, D_MODEL), lambda i: (_mod_row(i + t0, TM), 0, 5)),
            _full((1, D_MODEL)), _full((1, D_MODEL)),
        ],
        out_specs=pl.BlockSpec((TM, D_MODEL), lambda i: (i, 0)),
        out_shape=jax.ShapeDtypeStruct((n_rows, D_MODEL), F32),
        compiler_params=_params(1),
        name="final_ln",
    )(x1, ff, mod3, ln_g, ln_b)


def _rot_cols(w):
    half = ROPE // 2
    return jnp.concatenate([-w[..., half:], w[..., :half]], axis=-1)


def _pad_lanes(w):
    pad = [(0, 0)] * (w.ndim - 1) + [(0, LANES - w.shape[-1])]
    return jnp.pad(w, pad)


def _rope_tables():
    rows_n = LAT_T // GRID_W
    r, cl = jnp.meshgrid(jnp.arange(rows_n, dtype=F32), jnp.arange(GRID_W, dtype=F32), indexing="ij")
    inv = ROPE_THETA ** (-jnp.arange(0, ROPE // 2, 2, dtype=F32) / (ROPE // 2))
    ang = jnp.concatenate([r.reshape(-1)[:, None] * inv, cl.reshape(-1)[:, None] * inv], axis=-1)
    cos, sin = jnp.cos(ang), jnp.sin(ang)
    cos_lat = _pad_lanes(jnp.concatenate([cos, cos], axis=-1))
    sin_lat = _pad_lanes(jnp.concatenate([sin, sin], axis=-1))
    cos_ctx = _pad_lanes(jnp.ones((N_CTX, ROPE), F32))
    sin_ctx = jnp.zeros((N_CTX, LANES), F32)
    cos_a = jnp.concatenate([cos_ctx] + [cos_lat] * LAT_B, axis=0)
    sin_a = jnp.concatenate([sin_ctx] + [sin_lat] * LAT_B, axis=0)
    return cos_a, sin_a


def kernel(x_prompt, x_sample, cache_ckv, cache_krope, c, c_ctx, w_in, q_norm, w_uq, kv_norm, w_uk, w_uv,
           pool_w, pool_scale, conv_w, w_out, w_ada, b_ada, ln1_g, ln1_b, ln2_g, ln2_b, w_router,
           w_gate, w_up, w_down):
    o_q, o_kv, o_kr, o_mix = 0, Q_RANK, Q_RANK + KV_RANK, Q_RANK + KV_RANK + ROPE
    w_kr = w_in[:, :, o_kr:o_kr + ROPE]
    w_in_ext = jnp.concatenate(
        [w_in[:, :, o_q:o_kr], w_in[:, :, o_mix:], _pad_lanes(w_kr), _pad_lanes(_rot_cols(w_kr))],
        axis=-1).astype(BF16)
    hw = NOPE + ROPE
    uq_nope = [w_uq[:, :, h * hw:h * hw + NOPE] for h in range(HEADS)]
    uq_rope = [w_uq[:, :, h * hw + NOPE:(h + 1) * hw] for h in range(HEADS)]
    w_uq_ext = jnp.concatenate(
        uq_nope + [_pad_lanes(w) for w in uq_rope] + [_pad_lanes(_rot_cols(w)) for w in uq_rope],
        axis=-1).astype(BF16)
    w_uk_b = w_uk.astype(BF16)
    w_uv_b = w_uv.astype(BF16)
    eye = jnp.eye(len(POOL_WINDOWS), dtype=F32)
    pool_bd = (pool_w[:, :, :, None, :] * eye[None, :, None, :, None]).reshape(DEPTH, MIXW, MIXW).astype(BF16)
    w_out_b = w_out.astype(BF16)
    w_router_pad = _pad_lanes(w_router).astype(BF16)
    cos_a, sin_a = _rope_tables()

    cond = jnp.concatenate([c_ctx[None, :], c, jnp.zeros((SUBLANES - 1 - LAT_B, D_MODEL), F32)], axis=0)
    mod = _ada(cond, w_ada, b_ada)
    kc, vc = _cachekv(cache_ckv, _pad_lanes(cache_krope), w_uk_b, w_uv_b)

    prev = (x_prompt.reshape(N_CTX, D_MODEL), x_sample.reshape(N_LAT, D_MODEL))
    xs = prev + (0,)
    ckv_layers, kr_layers = [], []
    ctx_off = (jnp.arange(CTX_B, dtype=I32) * CTX_T)[:, None, None]
    for l in range(DEPTH):
        mod3 = mod[l].reshape(SUBLANES, 1, 6 * D_MODEL)
        outs = _front(prev, mod3, w_in_ext[l], q_norm[l][None], kv_norm[l][None],
                      w_uq_ext[l], w_uk_b[l], w_uv_b[l], cos_a, sin_a,
                      pool_bd[l], pool_scale[l][None], conv_w[l])
        ckv, kr, q, k, v, bc = outs[:6]
        if l > 0:
            xs = (outs[6], outs[6], N_CTX)
        ckv_layers.append(ckv[:N_CTX].reshape(CTX_B, CTX_T, KV_RANK))
        kr_layers.append(kr[:N_CTX, :ROPE].reshape(CTX_B, CTX_T, ROPE))
        a_ctx = _attn_ctx(q, k, v)
        a_lat = _attn_lat(q, k, v, kc, vc, l)
        x1, u2p, aff = _postmix(a_ctx, a_lat, bc, *xs, mod3, w_out_b[l, :HEADS * VDIM],
                                w_out_b[l, HEADS * VDIM:], ln1_g[l][None], ln1_b[l][None],
                                w_router_pad[l])
        idx_ctx = _route_ctx(aff)
        cnt_lat, cend = _route_count(aff, N_CTX // LAT_T, LAT_B, LAT_T, LAT_CAP)
        cend_flat = cend[:, :, :N_EXPERTS].transpose(0, 2, 1).reshape(-1)
        idx_lat = _route_rank(cend_flat, cnt_lat, LAT_B, LAT_T, LAT_CAP)
        idx_lat = idx_lat[:, :, :N_EXPERTS].transpose(0, 2, 1)
        idx_ctx = (idx_ctx + ctx_off).transpose(1, 0, 2).reshape(1, N_EXPERTS, ROWS)
        idx_flat = jnp.concatenate([idx_ctx, idx_lat], axis=0).reshape(-1)
        ff = _moe(idx_flat, u2p, aff, w_gate, w_up, w_down, l)
        if l + 1 < DEPTH:
            prev = (x1, ff, mod3, ln2_g[l][None], ln2_b[l][None])
        else:
            y_ctx = _final(x1, ff, mod3, ln2_g[l][None], ln2_b[l][None], 0, N_CTX)
            y_lat = _final(x1, ff, mod3, ln2_g[l][None], ln2_b[l][None], N_CTX, N_LAT)

    y_prompt = y_ctx.reshape(CTX_B, CTX_T, D_MODEL)
    y_sample = y_lat.reshape(LAT_B, LAT_T, D_MODEL)
    new_ckv = jnp.stack(ckv_layers, axis=1)
    new_krope = jnp.stack(kr_layers, axis=1)
    return (y_prompt, y_sample, new_ckv, new_krope)
```

```python
import functools

import jax
import jax.numpy as jnp
from jax import lax
from jax.experimental import pallas as pl
from jax.experimental.pallas import tpu as pltpu

F32 = jnp.float32
BF16 = jnp.bfloat16
I32 = jnp.int32

D_MODEL = 1024
CTX_B, CTX_T = 16, 256
LAT_B, LAT_T = 2, 4096
DEPTH = 4
PAST_LEN = 512
GRID_W = 64
HEADS = 4
NOPE, ROPE, VDIM = 128, 64, 128
Q_RANK, KV_RANK = 384, 256
POOL_WINDOWS = (2, 4, 8, 16)
POOL_GROUP = 64
MIXW = 256
N_EXPERTS = 16
EXPERT_DIM = 512
ROPE_THETA = 10000.0
ATTN_SCALE = (NOPE + ROPE) ** -0.5
LOG2E = 1.4426950408889634
Q_SCALE = ATTN_SCALE * LOG2E
ALPHA = (2 * DEPTH) ** 0.25
RMS_EPS = 1e-6
LN_EPS = 1e-5

N_CTX = CTX_B * CTX_T
N_LAT = LAT_B * LAT_T
N_TOK = N_CTX + N_LAT
CTX_CAP = 2 * CTX_T // N_EXPERTS
LAT_CAP = 2 * LAT_T // N_EXPERTS

LANES = 128
SUBLANES = 8
CHUNKS = D_MODEL // LANES
HEAD_BLOCK = 2 * LANES
VMEM_LIMIT = 56 * 1024 * 1024

TM = 512
SEQ_TILE = 256
HALO = 8
TQ = 512
KV_CHUNK = 512
CHUNK_UNROLL = 8
PASS_TOK = 4096
N_PASS = N_TOK // PASS_TOK
ROWS = 512
TILE_PITCH = ROWS + 8
SCATTER_UNROLL = 8
WORDS = CHUNKS // 2
ROW_BLOCK = 256
POST_BLOCK = 256


def _params(n_axes, arbitrary=False):
    sem = ("arbitrary" if arbitrary else "parallel",) * n_axes
    return pltpu.CompilerParams(dimension_semantics=sem, vmem_limit_bytes=VMEM_LIMIT)


def _mod_row(i, tile):
    start = i * tile
    return jnp.where(start < N_CTX, 0, 1 + (start - N_CTX) // LAT_T)


def _full(shape):
    nd = len(shape)
    return pl.BlockSpec(shape, lambda *_: (0,) * nd)


def _pair_specs(tile, width, lat_row0):
    nct = N_CTX // tile
    off = lat_row0 // tile
    return [pl.BlockSpec((tile, width), lambda i: (jnp.minimum(i, nct - 1), 0)),
            pl.BlockSpec((tile, width), lambda i: (jnp.maximum(i - nct, 0) + off, 0))]


def _pair_load(tile, ctx_ref, lat_ref):
    return jnp.where(pl.program_id(0) < N_CTX // tile, ctx_ref[...], lat_ref[...])


ADA_TN = 1536


def _ada_body(cond_ref, w_ref, b_ref, o_ref):
    c = cond_ref[...]
    a = c * jax.nn.sigmoid(c)
    a_hi = a.astype(BF16)
    a_lo = (a - a_hi.astype(F32)).astype(BF16)
    w = w_ref[0]
    w_hi = w.astype(BF16)
    w_lo = (w - w_hi.astype(F32)).astype(BF16)
    acc = jnp.dot(a_hi, w_hi, preferred_element_type=F32)
    acc += jnp.dot(a_lo, w_hi, preferred_element_type=F32)
    acc += jnp.dot(a_hi, w_lo, preferred_element_type=F32)
    o_ref[0] = acc + b_ref[0]


def _ada(cond, w_ada, b_ada):
    n = 6 * D_MODEL
    return pl.pallas_call(
        _ada_body,
        grid=(DEPTH, n // ADA_TN),
        in_specs=[
            _full((SUBLANES, D_MODEL)),
            pl.BlockSpec((1, D_MODEL, ADA_TN), lambda l, j: (l, 0, j)),
            pl.BlockSpec((1, 1, ADA_TN), lambda l, j: (l, 0, j)),
        ],
        out_specs=pl.BlockSpec((1, SUBLANES, ADA_TN), lambda l, j: (l, 0, j)),
        out_shape=jax.ShapeDtypeStruct((DEPTH, SUBLANES, n), F32),
        compiler_params=_params(2),
        name="ada_mod",
    )(cond, w_ada, b_ada.reshape(DEPTH, 1, n))


D_IN = Q_RANK + KV_RANK + ROPE + 4 * MIXW
IN_EXT = Q_RANK + KV_RANK + 4 * MIXW + 2 * LANES
UQ_EXT = HEADS * NOPE + 2 * HEADS * LANES


def _prep_win_body(w_ref, o_ref):
    w = w_ref[0]
    kr0 = Q_RANK + KV_RANK
    kr = w[:, kr0:kr0 + ROPE]
    half = ROPE // 2
    rot = jnp.concatenate([-kr[:, half:], kr[:, :half]], axis=-1)
    zeros = jnp.zeros((D_MODEL, LANES - ROPE), F32)
    out = jnp.concatenate([w[:, :kr0], w[:, kr0 + ROPE:], kr, zeros, rot, zeros], axis=-1)
    o_ref[0] = out.astype(BF16)


def _prep_win(w_in):
    lyr = lambda l: (l, 0, 0)
    return pl.pallas_call(
        _prep_win_body,
        grid=(DEPTH,),
        in_specs=[pl.BlockSpec((1, D_MODEL, D_IN), lyr)],
        out_specs=pl.BlockSpec((1, D_MODEL, IN_EXT), lyr),
        out_shape=jax.ShapeDtypeStruct((DEPTH, D_MODEL, IN_EXT), BF16),
        compiler_params=_params(1),
        name="prep_w_in",
    )(w_in)


def _rms(x, g):
    return x * lax.rsqrt(jnp.mean(x * x, axis=-1, keepdims=True) + RMS_EPS) * g


MIX_COL0 = Q_RANK + KV_RANK


def _ffn_rows(ff_ref, n):
    return jnp.concatenate([ff_ref[pl.ds(c, n, stride=CHUNKS), :] for c in range(CHUNKS)], axis=-1)


def _front_body(*refs, fused_ln):
    if fused_ln:
        (x1_ref, ff_ref, x1p_ref, x1n_ref, ffp_ref, ffn_ref, g2_ref, lg_ref, lb_ref,
         sh_ref, sc_ref, win_ref, qn_ref, kvn_ref, wuq_ref, wuk_ref, wuv_ref, cos_ref, sin_ref,
         pw_ref, ps_ref, cw_ref,
         ckv_ref, kr_ref, q_ref, k_ref, v_ref, bc_ref, x_ref) = refs

        def norm2(x1, ffn):
            return _layer_norm(ALPHA * x1 + g2_ref[0] * ffn, lg_ref[...], lb_ref[...])

        x = norm2(x1_ref[...], _ffn_rows(ff_ref, TM))
        x_ref[...] = x
        x_halo = jnp.concatenate([norm2(x1p_ref[...], _ffn_rows(ffp_ref, HALO)),
                                  norm2(x1n_ref[...], _ffn_rows(ffn_ref, HALO))], axis=0)
    else:
        (xc_ref, xl_ref, xp_ref, xn_ref,
         sh_ref, sc_ref, win_ref, qn_ref, kvn_ref, wuq_ref, wuk_ref, wuv_ref, cos_ref, sin_ref,
         pw_ref, ps_ref, cw_ref,
         ckv_ref, kr_ref, q_ref, k_ref, v_ref, bc_ref) = refs
        x = _pair_load(TM, xc_ref, xl_ref)
        x_halo = jnp.concatenate([xp_ref[...], xn_ref[...]], axis=0)

    u = jnp.concatenate([x, x_halo], axis=0) * (1.0 + sc_ref[0]) + sh_ref[0]
    h_all = jnp.dot(u.astype(BF16), win_ref[0], preferred_element_type=F32)
    h = h_all[:TM]
    mix_halo = h_all[TM:, MIX_COL0:MIX_COL0 + 4 * MIXW]
    o = 0
    q_c = h[:, o:o + Q_RANK]; o += Q_RANK
    kv_c = h[:, o:o + KV_RANK]; o += KV_RANK
    mix = h[:, o:o + 4 * MIXW]; o += 4 * MIXW
    kr_a = h[:, o:o + LANES]; o += LANES
    kr_b = h[:, o:o + LANES]
    kr_ref[...] = kr_a
    cos_a = cos_ref[...]
    sin_a = sin_ref[...]

    n_sub = TM // SEQ_TILE
    for s in range(n_sub):
        lo = s * SEQ_TILE
        prev = mix_halo[:HALO] if s == 0 else mix[lo - HALO:lo]
        nxt = mix_halo[HALO:] if s == n_sub - 1 else mix[lo + SEQ_TILE:lo + SEQ_TILE + HALO]
        bc_ref[lo:lo + SEQ_TILE, :] = _seqmix_math(
            pl.program_id(0) * n_sub + s, mix[lo:lo + SEQ_TILE], prev, nxt,
            pw_ref[...], ps_ref[...], cw_ref[...])

    ckv = _rms(kv_c, kvn_ref[...])
    ckv_ref[...] = ckv
    ckv_b = ckv.astype(BF16)
    k_nope = jnp.dot(ckv_b, wuk_ref[...], preferred_element_type=F32)
    v_ref[...] = jnp.dot(ckv_b, wuv_ref[...], preferred_element_type=F32).astype(BF16)
    k_rope = kr_a * cos_a + kr_b * sin_a
    parts = []
    for hd in range(HEADS):
        parts += [k_nope[:, hd * NOPE:(hd + 1) * NOPE], k_rope]
    k_ref[...] = jnp.concatenate(parts, axis=-1).astype(BF16)

    qn = _rms(q_c, qn_ref[...]).astype(BF16)
    qq = jnp.dot(qn, wuq_ref[...], preferred_element_type=F32)
    ra = HEADS * NOPE
    rb = ra + HEADS * LANES
    parts = []
    for hd in range(HEADS):
        q_rope = (qq[:, ra + hd * LANES:ra + (hd + 1) * LANES] * cos_a
                  + qq[:, rb + hd * LANES:rb + (hd + 1) * LANES] * sin_a)
        parts += [qq[:, hd * NOPE:(hd + 1) * NOPE] * Q_SCALE, q_rope * Q_SCALE]
    q_ref[...] = jnp.concatenate(parts, axis=-1).astype(BF16)


def _halo_specs(rows_per_token, width, lat_row0, n_rows):
    nct = N_CTX // TM
    nblk = n_rows // HALO

    def first_block(i):
        return (jnp.maximum(i - nct, 0) * TM + lat_row0) // HALO

    shape = (HALO * rows_per_token, width)
    return [pl.BlockSpec(shape, lambda i: (jnp.maximum(first_block(i) - 1, 0), 0)),
            pl.BlockSpec(shape, lambda i: (jnp.minimum(first_block(i) + TM // HALO, nblk - 1), 0))]


def _front(prev, mod3, layer, w_in_ext, q_norm, kv_norm, w_uq_ext, w_uk, w_uv, cos_a, sin_a,
           pool_bd, pool_scale, conv_w):
    row = lambda i: (i, 0)
    lyr = lambda i: (layer, 0, 0)
    modspec = lambda k: pl.BlockSpec((1, 1, D_MODEL), lambda i: (_mod_row(i, TM), 0, k))
    fused_ln = len(prev) == 5
    outs = [
        jax.ShapeDtypeStruct((N_TOK, KV_RANK), F32),
        jax.ShapeDtypeStruct((N_TOK, LANES), F32),
        jax.ShapeDtypeStruct((N_TOK, HEADS * HEAD_BLOCK), BF16),
        jax.ShapeDtypeStruct((N_TOK, HEADS * HEAD_BLOCK), BF16),
        jax.ShapeDtypeStruct((N_TOK, HEADS * VDIM), BF16),
        jax.ShapeDtypeStruct((N_TOK, 2 * MIXW), BF16),
    ]
    if fused_ln:
        x1, ff, mod3_prev, ln_g, ln_b = prev
        outs.append(jax.ShapeDtypeStruct((N_TOK, D_MODEL), F32))
        lead_specs = ([pl.BlockSpec((TM, D_MODEL), row), pl.BlockSpec((TM * CHUNKS, LANES), row)]
                      + _halo_specs(1, D_MODEL, N_CTX, N_TOK)
                      + _halo_specs(CHUNKS, LANES, N_CTX, N_TOK)
                      + [pl.BlockSpec((1, 1, D_MODEL), lambda i: (_mod_row(i, TM), 0, 5)),
                         _full((1, D_MODEL)), _full((1, D_MODEL))])
        lead_args = (x1, ff, x1, x1, ff, ff, mod3_prev, ln_g, ln_b)
    else:
        xc, xl = prev
        lead_specs = _pair_specs(TM, D_MODEL, 0) + _halo_specs(1, D_MODEL, 0, N_LAT)
        lead_args = (xc, xl, xl, xl)
    return pl.pallas_call(
        functools.partial(_front_body, fused_ln=fused_ln),
        grid=(N_TOK // TM,),
        in_specs=lead_specs + [
            modspec(0), modspec(1),
            pl.BlockSpec((1, D_MODEL, IN_EXT), lyr),
            _full((1, Q_RANK)), _full((1, KV_RANK)),
            _full((Q_RANK, UQ_EXT)),
            _full((KV_RANK, HEADS * NOPE)), _full((KV_RANK, HEADS * VDIM)),
            pl.BlockSpec((TM, LANES), row), pl.BlockSpec((TM, LANES), row),
            _full((MIXW, MIXW)), _full((1, MIXW)), _full((3, MIXW)),
        ],
        out_specs=tuple(pl.BlockSpec((TM, s.shape[1]), row) for s in outs),
        out_shape=tuple(outs),
        compiler_params=_params(1),
        name="front_ln" if fused_ln else "front",
    )(*lead_args, mod3, mod3, w_in_ext, q_norm, kv_norm, w_uq_ext, w_uk, w_uv, cos_a, sin_a,
      pool_bd, pool_scale, conv_w)


def _cachekv_body(ckv_ref, kr_ref, wuk_ref, wuv_ref, k_ref, v_ref):
    c = ckv_ref[0, 0].astype(BF16)
    k_nope = jnp.dot(c, wuk_ref[0], preferred_element_type=F32)
    v_ref[0, 0] = jnp.dot(c, wuv_ref[0], preferred_element_type=F32).astype(BF16)
    kr = kr_ref[0, 0]
    parts = []
    for hd in range(HEADS):
        parts += [k_nope[:, hd * NOPE:(hd + 1) * NOPE], kr]
    k_ref[0, 0] = jnp.concatenate(parts, axis=-1).astype(BF16)


def _cachekv(cache_ckv, kr_pad, w_uk, w_uv):
    bl = lambda b, l: (b, l, 0, 0)
    wl = lambda b, l: (l, 0, 0)
    return pl.pallas_call(
        _cachekv_body,
        grid=(LAT_B, DEPTH),
        in_specs=[
            pl.BlockSpec((1, 1, PAST_LEN, KV_RANK), bl),
            pl.BlockSpec((1, 1, PAST_LEN, LANES), bl),
            pl.BlockSpec((1, KV_RANK, HEADS * NOPE), wl),
            pl.BlockSpec((1, KV_RANK, HEADS * VDIM), wl),
        ],
        out_specs=(pl.BlockSpec((1, 1, PAST_LEN, HEADS * HEAD_BLOCK), bl),
                   pl.BlockSpec((1, 1, PAST_LEN, HEADS * VDIM), bl)),
        out_shape=(jax.ShapeDtypeStruct((LAT_B, DEPTH, PAST_LEN, HEADS * HEAD_BLOCK), BF16),
                   jax.ShapeDtypeStruct((LAT_B, DEPTH, PAST_LEN, HEADS * VDIM), BF16)),
        compiler_params=_params(2),
        name="cache_kv",
    )(cache_ckv, kr_pad, w_uk, w_uv)


EXT = SEQ_TILE + 2 * HALO


def _shift_up(x, k):
    return pltpu.roll(x, x.shape[0] - k, axis=0)


def _shift_down(x, k):
    return pltpu.roll(x, k, axis=0)


def _seqmix_math(j, main, prev, nxt, pw, ps, cw):
    n_ctx_tiles = N_CTX // SEQ_TILE
    tiles_per_lat = LAT_T // SEQ_TILE
    is_ctx = j < n_ctx_tiles
    jj = jnp.where(is_ctx, 0, (j - n_ctx_tiles) % tiles_per_lat)
    first = jj == 0
    last = jnp.where(is_ctx, True, jj == tiles_per_lat - 1)
    t_seq = jnp.where(is_ctx, CTX_T, LAT_T)

    prev = jnp.where(first, 0.0, prev)
    nxt = jnp.where(last, 0.0, nxt)
    ext = jnp.concatenate([prev, main, nxt], axis=0)
    p = ext[:, 0:MIXW]
    g_b = main[:, MIXW:2 * MIXW]
    g_c = ext[:, 2 * MIXW:3 * MIXW]
    h_in = ext[:, 3 * MIXW:4 * MIXW]

    sums = {}
    b = p
    for w in POOL_WINDOWS:
        b = b + _shift_up(b, w // 2)
        sums[w] = _shift_down(b, w // 2)[HALO:HALO + SEQ_TILE]
    tpos = jj * SEQ_TILE + lax.broadcasted_iota(I32, (SEQ_TILE, 1), 0)
    lane = lax.broadcasted_iota(I32, (1, MIXW), 1)
    num = None
    den = None
    for g, w in enumerate(POOL_WINDOWS):
        lo = jnp.maximum(tpos - w // 2, 0)
        hi = jnp.minimum(tpos + w - w // 2, t_seq)
        cnt = (hi - lo).astype(F32)
        if num is None:
            num, den = sums[w], jnp.broadcast_to(cnt, (SEQ_TILE, MIXW))
        else:
            sel = lane >= g * POOL_GROUP
            num = jnp.where(sel, sums[w], num)
            den = jnp.where(sel, cnt, den)
    pooled = num / den - p[HALO:HALO + SEQ_TILE]
    b_out = jnp.dot(pooled.astype(BF16), pw, preferred_element_type=F32) * ps

    y = g_c * h_in
    conv = (_shift_down(y, 1) * cw[0:1] + y * cw[1:2] + _shift_up(y, 1) * cw[2:3])
    c_out = g_b * conv[HALO:HALO + SEQ_TILE]
    return jnp.concatenate([b_out, c_out], axis=-1).astype(BF16)


_NT = (((1,), (1,)), ((), ()))


def _v_ext(v):
    return jnp.concatenate([v, jnp.ones(v.shape, v.dtype)], axis=-1)


def _attn_ctx_body(q_ref, k_ref, v_ref, o_ref):
    outs = []
    for hd in range(HEADS):
        qh = q_ref[:, hd * HEAD_BLOCK:(hd + 1) * HEAD_BLOCK]
        kh = k_ref[:, hd * HEAD_BLOCK:(hd + 1) * HEAD_BLOCK]
        vh = v_ref[:, hd * VDIM:(hd + 1) * VDIM]
        s = lax.dot_general(qh, kh, _NT, preferred_element_type=F32)
        m = jnp.max(s, axis=-1, keepdims=True)
        p = jnp.exp2(s - m)
        acc = jnp.dot(p.astype(BF16), _v_ext(vh), preferred_element_type=F32)
        outs.append(acc[:, :VDIM] / acc[:, VDIM:])
    o_ref[...] = jnp.concatenate(outs, axis=-1).astype(BF16)


def _attn_ctx(q, k, v):
    row = lambda b: (b, 0)
    return pl.pallas_call(
        _attn_ctx_body,
        grid=(CTX_B,),
        in_specs=[pl.BlockSpec((CTX_T, HEADS * HEAD_BLOCK), row),
                  pl.BlockSpec((CTX_T, HEADS * HEAD_BLOCK), row),
                  pl.BlockSpec((CTX_T, HEADS * VDIM), row)],
        out_specs=pl.BlockSpec((CTX_T, HEADS * VDIM), row),
        out_shape=jax.ShapeDtypeStruct((N_CTX, HEADS * VDIM), BF16),
        compiler_params=_params(1),
        name="attn_ctx",
    )(q, k, v)


def _attn_lat_body(q_ref, kc_ref, vc_ref, ko_ref, vo_ref, o_ref, m_ref, acc_ref):
    reps = KV_CHUNK // LANES

    def update(hd, k, v, first):
        qh = q_ref[:, hd * HEAD_BLOCK:(hd + 1) * HEAD_BLOCK]
        s = lax.dot_general(qh, k, _NT, preferred_element_type=F32)
        mx = jnp.max(s, axis=-1, keepdims=True)
        if first:
            m_new = jnp.broadcast_to(mx, (TQ, LANES))
        else:
            m_old = m_ref[hd]
            m_new = jnp.maximum(m_old, mx)
        p = jnp.exp2(s - jnp.concatenate([m_new] * reps, axis=-1))
        pv = jnp.dot(p.astype(BF16), _v_ext(v), preferred_element_type=F32)
        if first:
            acc_ref[hd] = pv
        else:
            a = jnp.exp2(m_old - m_new)
            acc_ref[hd] = acc_ref[hd] * jnp.concatenate([a, a], axis=-1) + pv
        m_ref[hd] = m_new

    def head_slices(hd):
        return slice(hd * HEAD_BLOCK, (hd + 1) * HEAD_BLOCK), slice(hd * VDIM, (hd + 1) * VDIM)

    for hd in range(HEADS):
        ks, vs = head_slices(hd)
        update(hd, kc_ref[0, 0, :, ks], vc_ref[0, 0, :, vs], True)

    def body(c, _):
        for j in range(CHUNK_UNROLL):
            rows = pl.ds(pl.multiple_of((c * CHUNK_UNROLL + j) * KV_CHUNK, KV_CHUNK), KV_CHUNK)
            for hd in range(HEADS):
                ks, vs = head_slices(hd)
                update(hd, ko_ref[rows, ks], vo_ref[rows, vs], False)
        return 0

    lax.fori_loop(0, LAT_T // (KV_CHUNK * CHUNK_UNROLL), body, 0)
    outs = [acc_ref[hd][:, :VDIM] / acc_ref[hd][:, VDIM:] for hd in range(HEADS)]
    o_ref[...] = jnp.concatenate(outs, axis=-1).astype(BF16)


def _attn_lat(q, k, v, kc, vc, layer):
    qt = LAT_T // TQ
    ctx_tiles = N_CTX // TQ
    ctx_blocks = N_CTX // LAT_T
    return pl.pallas_call(
        _attn_lat_body,
        grid=(LAT_B, qt),
        in_specs=[
            pl.BlockSpec((TQ, HEADS * HEAD_BLOCK), lambda b, i: (ctx_tiles + b * qt + i, 0)),
            pl.BlockSpec((1, 1, PAST_LEN, HEADS * HEAD_BLOCK), lambda b, i: (b, layer, 0, 0)),
            pl.BlockSpec((1, 1, PAST_LEN, HEADS * VDIM), lambda b, i: (b, layer, 0, 0)),
            pl.BlockSpec((LAT_T, HEADS * HEAD_BLOCK), lambda b, i: (ctx_blocks + b, 0)),
            pl.BlockSpec((LAT_T, HEADS * VDIM), lambda b, i: (ctx_blocks + b, 0)),
        ],
        out_specs=pl.BlockSpec((TQ, HEADS * VDIM), lambda b, i: (b * qt + i, 0)),
        out_shape=jax.ShapeDtypeStruct((N_LAT, HEADS * VDIM), BF16),
        scratch_shapes=[pltpu.VMEM((HEADS, TQ, LANES), F32),
                        pltpu.VMEM((HEADS, TQ, 2 * VDIM), F32)],
        compiler_params=_params(2),
        name="attn_lat",
    )(q, kc, vc, k, v)


def _layer_norm(y, g, b):
    mu = jnp.mean(y, axis=-1, keepdims=True)
    d = y - mu
    var = jnp.mean(d * d, axis=-1, keepdims=True)
    return d * lax.rsqrt(var + LN_EPS) * g + b


def _postmix_body(actx_ref, alat_ref, bc_ref, xc_ref, xl_ref, g1_ref, sh2_ref, sc2_ref, wa_ref, wbc_ref,
                  lg_ref, lb_ref, wr_ref, x1_ref, u2p_ref, aff_ref):
    is_ctx = pl.program_id(0) < N_CTX // TM
    lane = lax.broadcasted_iota(I32, (1, LANES), 1)
    blocks = [slice(b * POST_BLOCK, (b + 1) * POST_BLOCK) for b in range(TM // POST_BLOCK)]
    mixes = []
    for rows in blocks:
        a = jnp.where(is_ctx, actx_ref[rows, :], alat_ref[rows, :])
        mix = jnp.dot(a, wa_ref[0], preferred_element_type=F32)
        mixes.append(mix + jnp.dot(bc_ref[rows, :], wbc_ref[0], preferred_element_type=F32))
    u2s = []
    for b, rows in enumerate(blocks):
        x = jnp.where(is_ctx, xc_ref[rows, :], xl_ref[rows, :])
        x1 = _layer_norm(ALPHA * x + g1_ref[0] * mixes[b], lg_ref[...], lb_ref[...])
        x1_ref[rows, :] = x1
        u2 = x1 * (1.0 + sc2_ref[0]) + sh2_ref[0]
        for c in range(WORDS):
            lo = u2[:, c * LANES:(c + 1) * LANES]
            hi = u2[:, (c + WORDS) * LANES:(c + WORDS + 1) * LANES]
            u2p_ref[pl.ds(b * POST_BLOCK * WORDS + c, POST_BLOCK, stride=WORDS), :] = (
                pltpu.pack_elementwise([lo, hi], packed_dtype=BF16))
        u2s.append(u2.astype(BF16))
    for b, rows in enumerate(blocks):
        logits = jnp.dot(u2s[b], wr_ref[...], preferred_element_type=F32)
        logits = jnp.where(lane < N_EXPERTS, logits, -jnp.inf)
        m = jnp.max(logits, axis=-1, keepdims=True)
        e = jnp.exp(logits - m)
        aff_ref[rows, :] = e / jnp.sum(e, axis=-1, keepdims=True)


def _postmix(a_ctx, a_lat, bc, xc, xl, lat_row0, mod3, layer, w_out_b, ln_g, ln_b, w_router_pad):
    row = lambda i: (i, 0)
    modspec = lambda k: pl.BlockSpec((1, 1, D_MODEL), lambda i: (_mod_row(i, TM), 0, k))
    return pl.pallas_call(
        _postmix_body,
        grid=(N_TOK // TM,),
        in_specs=_pair_specs(TM, HEADS * VDIM, 0) + [
            pl.BlockSpec((TM, 2 * MIXW), row),
        ] + _pair_specs(TM, D_MODEL, lat_row0) + [
            modspec(2), modspec(3), modspec(4),
            pl.BlockSpec((1, HEADS * VDIM, D_MODEL), lambda i: (layer, 0, 0)),
            pl.BlockSpec((1, 2 * MIXW, D_MODEL), lambda i: (layer, 1, 0)),
            _full((1, D_MODEL)), _full((1, D_MODEL)),
            _full((D_MODEL, LANES)),
        ],
        out_specs=(pl.BlockSpec((TM, D_MODEL), row),
                   pl.BlockSpec((TM * WORDS, LANES), row),
                   pl.BlockSpec((TM, LANES), row)),
        out_shape=(jax.ShapeDtypeStruct((N_TOK, D_MODEL), F32),
                   jax.ShapeDtypeStruct((N_TOK * WORDS, LANES), jnp.uint32),
                   jax.ShapeDtypeStruct((N_TOK, LANES), F32)),
        compiler_params=_params(1),
        name="postmix",
    )(a_ctx, a_lat, bc, xc, xl, mod3, mod3, mod3, w_out_b, w_out_b, ln_g, ln_b, w_router_pad)


PREFIX_CHUNK = 256


def _route_ctx_body(aff_ref, idx_ref):
    n_rows = CTX_B * N_EXPERTS
    dense = jnp.concatenate(
        [aff_ref[b * CTX_T:(b + 1) * CTX_T, :].T[:N_EXPERTS] for b in range(CTX_B)], axis=0)

    def search(i, thr):
        cand = thr | jnp.left_shift(jnp.int32(1), 30 - i)
        n = jnp.sum((dense >= pltpu.bitcast(cand, F32)).astype(I32), axis=1, keepdims=True)
        return jnp.where(n >= CTX_CAP, cand, thr)

    thr_bits = lax.fori_loop(0, 31, search, jnp.zeros((n_rows, 1), I32))
    thr = pltpu.bitcast(thr_bits, F32)
    above = pltpu.bitcast(thr_bits + 1, F32)
    gt = (dense > thr) & (dense >= above)
    eq = (dense >= thr) & jnp.logical_not(gt)
    n_gt = jnp.sum(gt.astype(I32), axis=1, keepdims=True)
    need = (CTX_CAP - n_gt).astype(F32)

    ri = lax.broadcasted_iota(I32, (CTX_T, CTX_T), 0)
    ci = lax.broadcasted_iota(I32, (CTX_T, CTX_T), 1)
    tri = jnp.where(ri <= ci, 1.0, 0.0).astype(BF16)
    p_eq = jnp.dot(jnp.where(eq, 1.0, 0.0).astype(BF16), tri, preferred_element_type=F32)
    sel = jnp.where(gt, 1.0, jnp.where(eq & (p_eq <= need), 1.0, 0.0))
    cnt = jnp.dot(sel.astype(BF16), tri, preferred_element_type=F32)

    ones = jnp.ones((CTX_T, LANES), BF16)
    lane = lax.broadcasted_iota(I32, (1, LANES), 1)
    out = jnp.zeros((n_rows, LANES), F32)
    for r in range(CTX_CAP):
        le = jnp.where(cnt <= float(r), 1.0, 0.0).astype(BF16)
        out = jnp.where(lane == r, jnp.dot(le, ones, preferred_element_type=F32), out)
    idx_ref[...] = out.astype(I32)


def _route_ctx(aff):
    idx = pl.pallas_call(
        _route_ctx_body,
        grid=(1,),
        in_specs=[pl.BlockSpec((N_CTX, LANES), lambda i: (0, 0))],
        out_specs=_full((CTX_B * N_EXPERTS, LANES)),
        out_shape=jax.ShapeDtypeStruct((CTX_B * N_EXPERTS, LANES), I32),
        compiler_params=_params(1),
        name="route_ctx",
    )(aff)
    return idx[:, :CTX_CAP].reshape(CTX_B, N_EXPERTS, CTX_CAP)


RANK_TILE = LANES


def _expert_row(col):
    full = jnp.concatenate([jnp.broadcast_to(col, (N_EXPERTS, LANES)),
                            jnp.zeros((LANES - N_EXPERTS, LANES), col.dtype)], axis=0)
    return full.T[0:1]


def _route_count_body(aff_ref, cnt_ref, cend_ref, *, seq, cap):
    n_chunks = seq // PREFIX_CHUNK
    dense = aff_ref[...].T[:N_EXPERTS]

    def search(i, thr):
        cand = thr | jnp.left_shift(jnp.int32(1), 30 - i)
        n = jnp.sum((dense >= pltpu.bitcast(cand, F32)).astype(I32), axis=1, keepdims=True)
        return jnp.where(n >= cap, cand, thr)

    thr_bits = lax.fori_loop(0, 31, search, jnp.zeros((N_EXPERTS, 1), I32))
    thr_col = pltpu.bitcast(thr_bits, F32)
    above_col = pltpu.bitcast(thr_bits + 1, F32)
    n_gt = jnp.sum(((dense > thr_col) & (dense >= above_col)).astype(I32), axis=1, keepdims=True)
    thr = _expert_row(thr_col)
    above = _expert_row(above_col)
    need = _expert_row((cap - n_gt).astype(F32))

    ri = lax.broadcasted_iota(I32, (PREFIX_CHUNK, PREFIX_CHUNK), 0)
    ci = lax.broadcasted_iota(I32, (PREFIX_CHUNK, PREFIX_CHUNK), 1)
    tri = jnp.where(ci <= ri, 1.0, 0.0).astype(BF16)
    carry_eq = jnp.zeros((1, LANES), F32)
    carry_sel = jnp.zeros((1, LANES), F32)
    per = PREFIX_CHUNK // RANK_TILE
    for c in range(n_chunks):
        rows = slice(c * PREFIX_CHUNK, (c + 1) * PREFIX_CHUNK)
        a = aff_ref[rows, :]
        gt = (a > thr) & (a >= above)
        eq = (a >= thr) & jnp.logical_not(gt)
        p_eq = jnp.dot(tri, jnp.where(eq, 1.0, 0.0).astype(BF16),
                       preferred_element_type=F32) + carry_eq
        sel = jnp.where(gt, 1.0, jnp.where(eq & (p_eq <= need), 1.0, 0.0))
        p_sel = jnp.dot(tri, sel.astype(BF16), preferred_element_type=F32) + carry_sel
        dense_cnt = p_sel.T[:N_EXPERTS]
        for k in range(per):
            last = (k + 1) * RANK_TILE - 1
            cnt_ref[c * per + k] = dense_cnt[:, k * RANK_TILE:(k + 1) * RANK_TILE]
            cend_ref[0, c * per + k:c * per + k + 1, :] = p_sel[last:last + 1, :].astype(I32)
        carry_eq = p_eq[PREFIX_CHUNK - 1:PREFIX_CHUNK, :]
        carry_sel = p_sel[PREFIX_CHUNK - 1:PREFIX_CHUNK, :]


def _route_count(aff, first_block, n_req, seq, cap):
    n_tiles = seq // RANK_TILE
    return pl.pallas_call(
        functools.partial(_route_count_body, seq=seq, cap=cap),
        grid=(n_req,),
        in_specs=[pl.BlockSpec((seq, LANES), lambda b: (first_block + b, 0))],
        out_specs=(pl.BlockSpec((n_tiles, N_EXPERTS, RANK_TILE), lambda b: (b, 0, 0)),
                   pl.BlockSpec((1, n_tiles, LANES), lambda b: (b, 0, 0))),
        out_shape=(jax.ShapeDtypeStruct((n_req * n_tiles, N_EXPERTS, RANK_TILE), F32),
                   jax.ShapeDtypeStruct((n_req, n_tiles, LANES), I32)),
        compiler_params=_params(1),
        name=f"route_count_{seq}",
    )(aff)


def _route_rank_body(cend_ref, cnt_ref, idx_ref, *bufs, seq, cap):
    b = pl.program_id(0)
    n_tiles = seq // RANK_TILE
    rank = lax.broadcasted_iota(I32, (RANK_TILE, RANK_TILE), 0).astype(F32)
    ones = jnp.ones((RANK_TILE, LANES), BF16)

    def tile(j, _):
        counts = cnt_ref[j]
        for e in range(N_EXPERTS):
            prev = (b * N_EXPERTS + e) * n_tiles + j - 1
            start = jnp.where(j > 0, cend_ref[jnp.maximum(prev, 0)], 0)
            local = counts[e:e + 1, :] - jnp.asarray(start, F32)
            le = jnp.where(local <= rank, 1.0, 0.0).astype(BF16)
            pos = jnp.dot(le, ones, preferred_element_type=F32) + jnp.asarray(j * RANK_TILE, F32)
            bufs[e][pl.ds(start, RANK_TILE), :] = pos
        return 0

    lax.fori_loop(0, n_tiles, tile, 0)
    lane = lax.broadcasted_iota(I32, (1, LANES), 1)
    out = jnp.zeros((cap, LANES), F32)
    for e in range(N_EXPERTS):
        out = jnp.where(lane == e, bufs[e][0:cap, :], out)
    idx_ref[0] = out.astype(I32)


def _route_rank(cend_flat, cnt, n_req, seq, cap):
    n_tiles = seq // RANK_TILE
    grid_spec = pltpu.PrefetchScalarGridSpec(
        num_scalar_prefetch=1,
        grid=(n_req,),
        in_specs=[pl.BlockSpec((n_tiles, N_EXPERTS, RANK_TILE), lambda b, c: (b, 0, 0))],
        out_specs=pl.BlockSpec((1, cap, LANES), lambda b, c: (b, 0, 0)),
        scratch_shapes=[pltpu.VMEM((cap + RANK_TILE, LANES), F32)] * N_EXPERTS,
    )
    return pl.pallas_call(
        functools.partial(_route_rank_body, seq=seq, cap=cap),
        grid_spec=grid_spec,
        out_shape=jax.ShapeDtypeStruct((n_req, cap, LANES), I32),
        compiler_params=_params(1),
        name=f"route_rank_{seq}",
    )(cend_flat, cnt)


def _gather_rows(idx_ref, base, u2p_ref, aff_ref, tile_ref, gate_ref, r0, n):
    for i in range(n):
        r = r0 + i
        t = idx_ref[base + r]
        slab = u2p_ref[pl.ds(pl.multiple_of(t * WORDS, WORDS), WORDS), :]
        tile_ref[pl.ds(r, WORDS, stride=TILE_PITCH), :] = slab
        gate_ref[pl.ds(r, 1), :] = aff_ref[pl.ds(t, 1), :]


def _scatter_rows(idx_ref, base, y_ref, acc_ref, r0, n):
    for g0 in range(0, n, SCATTER_UNROLL):
        rows = [r0 + g0 + i for i in range(SCATTER_UNROLL)]
        dst = [pl.ds(pl.multiple_of(idx_ref[base + r] * CHUNKS, CHUNKS), CHUNKS) for r in rows]
        vals = [acc_ref[d, :] + y_ref[pl.ds(pl.multiple_of(r * CHUNKS, CHUNKS), CHUNKS), :]
                for d, r in zip(dst, rows)]
        for d, v in zip(dst, vals):
            acc_ref[d, :] = v


def _expert_rows(e, tile_ref, gate_ref, wg, wu, wd, y_ref, r0, n):
    halves = [[], []]
    for c in range(WORDS):
        w = tile_ref[pl.ds(c * TILE_PITCH + r0, n), :]
        for k in range(2):
            halves[k].append(pltpu.unpack_elementwise(w, index=k, packed_dtype=BF16,
                                                      unpacked_dtype=F32))
    x = jnp.concatenate(halves[0] + halves[1], axis=-1).astype(BF16)
    lane = lax.broadcasted_iota(I32, (1, LANES), 1)
    gate = jnp.sum(jnp.where(lane == e, gate_ref[pl.ds(r0, n), :], 0.0), axis=-1, keepdims=True)
    hg = jnp.dot(x, wg, preferred_element_type=F32)
    hu = jnp.dot(x, wu, preferred_element_type=F32)
    hidden = (hg * jax.nn.sigmoid(hg) * hu).astype(BF16)
    y = jnp.dot(hidden, wd, preferred_element_type=F32) * gate
    for c in range(CHUNKS):
        y_ref[pl.ds(r0 * CHUNKS + c, n, stride=CHUNKS), :] = y[:, c * LANES:(c + 1) * LANES]


def _moe_body(idx_ref, u2p_ref, aff_ref, wg_ref, wu_ref, wd_ref, acc_ref,
              tile_a, tile_b, gate_a, gate_b, y_a, y_b):
    p = pl.program_id(0)
    e = pl.program_id(1)
    step = p * N_EXPERTS + e
    base = step * ROWS
    base_next = jnp.minimum(step + 1, N_PASS * N_EXPERTS - 1) * ROWS
    base_prev = jnp.where(e == 0, base, base - ROWS)

    @pl.when(e == 0)
    def _():
        acc_ref[...] = jnp.zeros_like(acc_ref)
        y_b[...] = jnp.zeros_like(y_b)

        def gather(c, _):
            _gather_rows(idx_ref, base, u2p_ref, aff_ref, tile_a, gate_a, c * SUBLANES, SUBLANES)
            return 0

        lax.fori_loop(0, ROWS // SUBLANES, gather, 0)

    def run(tile_cur, gate_cur, y_cur, tile_nxt, gate_nxt, y_prv):
        wg = wg_ref[0, 0].astype(BF16)
        wu = wu_ref[0, 0].astype(BF16)
        wd = wd_ref[0, 0].astype(BF16)
        for blk in range(ROWS // ROW_BLOCK):
            r0 = blk * ROW_BLOCK
            _gather_rows(idx_ref, base_next, u2p_ref, aff_ref, tile_nxt, gate_nxt, r0, ROW_BLOCK)
            _expert_rows(e, tile_cur, gate_cur, wg, wu, wd, y_cur, r0, ROW_BLOCK)
            _scatter_rows(idx_ref, base_prev, y_prv, acc_ref, r0, ROW_BLOCK)

    @pl.when(e % 2 == 0)
    def _():
        run(tile_a, gate_a, y_a, tile_b, gate_b, y_b)

    @pl.when(e % 2 == 1)
    def _():
        run(tile_b, gate_b, y_b, tile_a, gate_a, y_a)

    @pl.when(e == N_EXPERTS - 1)
    def _():
        def scatter(c, _):
            _scatter_rows(idx_ref, base, y_b, acc_ref, c * SCATTER_UNROLL, SCATTER_UNROLL)
            return 0

        lax.fori_loop(0, ROWS // SCATTER_UNROLL, scatter, 0)


def _moe(idx_flat, u2p, aff, w_gate, w_up, w_down, layer):
    one = pl.Buffered(1)
    grid_spec = pltpu.PrefetchScalarGridSpec(
        num_scalar_prefetch=1,
        grid=(N_PASS, N_EXPERTS),
        in_specs=[
            pl.BlockSpec((PASS_TOK * WORDS, LANES), lambda p, e, idx: (p, 0), pipeline_mode=one),
            pl.BlockSpec((PASS_TOK, LANES), lambda p, e, idx: (p, 0), pipeline_mode=one),
            pl.BlockSpec((1, 1, D_MODEL, EXPERT_DIM), lambda p, e, idx: (layer, e, 0, 0)),
            pl.BlockSpec((1, 1, D_MODEL, EXPERT_DIM), lambda p, e, idx: (layer, e, 0, 0)),
            pl.BlockSpec((1, 1, EXPERT_DIM, D_MODEL), lambda p, e, idx: (layer, e, 0, 0)),
        ],
        out_specs=pl.BlockSpec((PASS_TOK * CHUNKS, LANES), lambda p, e, idx: (p, 0),
                               pipeline_mode=one),
        scratch_shapes=[
            pltpu.VMEM((WORDS * TILE_PITCH, LANES), jnp.uint32),
            pltpu.VMEM((WORDS * TILE_PITCH, LANES), jnp.uint32),
            pltpu.VMEM((ROWS, LANES), F32),
            pltpu.VMEM((ROWS, LANES), F32),
            pltpu.VMEM((ROWS * CHUNKS, LANES), F32),
            pltpu.VMEM((ROWS * CHUNKS, LANES), F32),
        ],
    )
    return pl.pallas_call(
        _moe_body,
        grid_spec=grid_spec,
        out_shape=jax.ShapeDtypeStruct((N_TOK * CHUNKS, LANES), F32),
        compiler_params=_params(2, arbitrary=True),
        name="moe",
    )(idx_flat, u2p, aff, w_gate, w_up, w_down)


def _final_body(x1_ref, ff_ref, g2_ref, lg_ref, lb_ref, o_ref):
    ffn = jnp.concatenate([ff_ref[pl.ds(c, TM, stride=CHUNKS), :] for c in range(CHUNKS)], axis=-1)
    o_ref[...] = _layer_norm(ALPHA * x1_ref[...] + g2_ref[0] * ffn, lg_ref[...], lb_ref[...])


def _final(x1, ff, mod3, ln_g, ln_b, row0=0, n_rows=N_TOK):
    t0 = row0 // TM
    row = lambda i: (i + t0, 0)
    return pl.pallas_call(
        _final_body,
        grid=(n_rows // TM,),
        in_specs=[
            pl.BlockSpec((TM, D_MODEL), row),
            pl.BlockSpec((TM * CHUNKS, LANES), row),
            pl.BlockSpec((1, 1, D_MODEL), lambda i: (_mod_row(i + t0, TM), 0, 5)),
            _full((1, D_MODEL)), _full((1, D_MODEL)),
        ],
        out_specs=pl.BlockSpec((TM, D_MODEL), lambda i: (i, 0)),
        out_shape=jax.ShapeDtypeStruct((n_rows, D_MODEL), F32),
        compiler_params=_params(1),
        name="final_ln",
    )(x1, ff, mod3, ln_g, ln_b)


def _rot_cols(w):
    half = ROPE // 2
    return jnp.concatenate([-w[..., half:], w[..., :half]], axis=-1)


def _pad_lanes(w):
    pad = [(0, 0)] * (w.ndim - 1) + [(0, LANES - w.shape[-1])]
    return jnp.pad(w, pad)


def _rope_tables():
    rows_n = LAT_T // GRID_W
    r, cl = jnp.meshgrid(jnp.arange(rows_n, dtype=F32), jnp.arange(GRID_W, dtype=F32), indexing="ij")
    inv = ROPE_THETA ** (-jnp.arange(0, ROPE // 2, 2, dtype=F32) / (ROPE // 2))
    ang = jnp.concatenate([r.reshape(-1)[:, None] * inv, cl.reshape(-1)[:, None] * inv], axis=-1)
    cos, sin = jnp.cos(ang), jnp.sin(ang)
    cos_lat = _pad_lanes(jnp.concatenate([cos, cos], axis=-1))
    sin_lat = _pad_lanes(jnp.concatenate([sin, sin], axis=-1))
    cos_ctx = _pad_lanes(jnp.ones((N_CTX, ROPE), F32))
    sin_ctx = jnp.zeros((N_CTX, LANES), F32)
    cos_a = jnp.concatenate([cos_ctx] + [cos_lat] * LAT_B, axis=0)
    sin_a = jnp.concatenate([sin_ctx] + [sin_lat] * LAT_B, axis=0)
    return cos_a, sin_a


def kernel(x_prompt, x_sample, cache_ckv, cache_krope, c, c_ctx, w_in, q_norm, w_uq, kv_norm, w_uk, w_uv,
           pool_w, pool_scale, conv_w, w_out, w_ada, b_ada, ln1_g, ln1_b, ln2_g, ln2_b, w_router,
           w_gate, w_up, w_down):
    w_in_ext = _prep_win(w_in)
    hw = NOPE + ROPE
    uq_nope = [w_uq[:, :, h * hw:h * hw + NOPE] for h in range(HEADS)]
    uq_rope = [w_uq[:, :, h * hw + NOPE:(h + 1) * hw] for h in range(HEADS)]
    w_uq_ext = jnp.concatenate(
        uq_nope + [_pad_lanes(w) for w in uq_rope] + [_pad_lanes(_rot_cols(w)) for w in uq_rope],
        axis=-1).astype(BF16)
    w_uk_b = w_uk.astype(BF16)
    w_uv_b = w_uv.astype(BF16)
    eye = jnp.eye(len(POOL_WINDOWS), dtype=F32)
    pool_bd = (pool_w[:, :, :, None, :] * eye[None, :, None, :, None]).reshape(DEPTH, MIXW, MIXW).astype(BF16)
    w_out_b = w_out.astype(BF16)
    w_router_pad = _pad_lanes(w_router).astype(BF16)
    cos_a, sin_a = _rope_tables()

    cond = jnp.concatenate([c_ctx[None, :], c, jnp.zeros((SUBLANES - 1 - LAT_B, D_MODEL), F32)], axis=0)
    mod = _ada(cond, w_ada, b_ada)
    kc, vc = _cachekv(cache_ckv, _pad_lanes(cache_krope), w_uk_b, w_uv_b)

    prev = (x_prompt.reshape(N_CTX, D_MODEL), x_sample.reshape(N_LAT, D_MODEL))
    xs = prev + (0,)
    ckv_layers, kr_layers = [], []
    ctx_off = (jnp.arange(CTX_B, dtype=I32) * CTX_T)[:, None, None]
    for l in range(DEPTH):
        mod3 = mod[l].reshape(SUBLANES, 1, 6 * D_MODEL)
        outs = _front(prev, mod3, l, w_in_ext, q_norm[l][None], kv_norm[l][None],
                      w_uq_ext[l], w_uk_b[l], w_uv_b[l], cos_a, sin_a,
                      pool_bd[l], pool_scale[l][None], conv_w[l])
        ckv, kr, q, k, v, bc = outs[:6]
        if l > 0:
            xs = (outs[6], outs[6], N_CTX)
        ckv_layers.append(ckv[:N_CTX].reshape(CTX_B, CTX_T, KV_RANK))
        kr_layers.append(kr[:N_CTX, :ROPE].reshape(CTX_B, CTX_T, ROPE))
        a_ctx = _attn_ctx(q, k, v)
        a_lat = _attn_lat(q, k, v, kc, vc, l)
        x1, u2p, aff = _postmix(a_ctx, a_lat, bc, *xs, mod3, l, w_out_b,
                                ln1_g[l][None], ln1_b[l][None], w_router_pad[l])
        idx_ctx = _route_ctx(aff)
        cnt_lat, cend = _route_count(aff, N_CTX // LAT_T, LAT_B, LAT_T, LAT_CAP)
        cend_flat = cend[:, :, :N_EXPERTS].transpose(0, 2, 1).reshape(-1)
        idx_lat = _route_rank(cend_flat, cnt_lat, LAT_B, LAT_T, LAT_CAP)
        idx_lat = idx_lat[:, :, :N_EXPERTS].transpose(0, 2, 1)
        idx_ctx = (idx_ctx + ctx_off).transpose(1, 0, 2).reshape(1, N_EXPERTS, ROWS)
        idx_flat = jnp.concatenate([idx_ctx, idx_lat], axis=0).reshape(-1)
        ff = _moe(idx_flat, u2p, aff, w_gate, w_up, w_down, l)
        if l + 1 < DEPTH:
            prev = (x1, ff, mod3, ln2_g[l][None], ln2_b[l][None])
        else:
            y_ctx = _final(x1, ff, mod3, ln2_g[l][None], ln2_b[l][None], 0, N_CTX)
            y_lat = _final(x1, ff, mod3, ln2_g[l][None], ln2_b[l][None], N_CTX, N_LAT)

    y_prompt = y_ctx.reshape(CTX_B, CTX_T, D_MODEL)
    y_sample = y_lat.reshape(LAT_B, LAT_T, D_MODEL)
    new_ckv = jnp.stack(ckv_layers, axis=1)
    new_krope = jnp.stack(kr_layers, axis=1)
    return (y_prompt, y_sample, new_ckv, new_krope)
```

```python
import functools

import jax
import jax.numpy as jnp
from jax import lax
from jax.experimental import pallas as pl
from jax.experimental.pallas import tpu as pltpu

F32 = jnp.float32
BF16 = jnp.bfloat16
I32 = jnp.int32

D_MODEL = 1024
CTX_B, CTX_T = 16, 256
LAT_B, LAT_T = 2, 4096
DEPTH = 4
PAST_LEN = 512
GRID_W = 64
HEADS = 4
NOPE, ROPE, VDIM = 128, 64, 128
Q_RANK, KV_RANK = 384, 256
POOL_WINDOWS = (2, 4, 8, 16)
POOL_GROUP = 64
MIXW = 256
N_EXPERTS = 16
EXPERT_DIM = 512
ROPE_THETA = 10000.0
ATTN_SCALE = (NOPE + ROPE) ** -0.5
LOG2E = 1.4426950408889634
Q_SCALE = ATTN_SCALE * LOG2E
ALPHA = (2 * DEPTH) ** 0.25
RMS_EPS = 1e-6
LN_EPS = 1e-5

N_CTX = CTX_B * CTX_T
N_LAT = LAT_B * LAT_T
N_TOK = N_CTX + N_LAT
CTX_CAP = 2 * CTX_T // N_EXPERTS
LAT_CAP = 2 * LAT_T // N_EXPERTS

LANES = 128
SUBLANES = 8
CHUNKS = D_MODEL // LANES
HEAD_BLOCK = 2 * LANES
VMEM_LIMIT = 56 * 1024 * 1024

TM = 512
SEQ_TILE = 256
HALO = 8
TQ = 512
KV_CHUNK = 512
CHUNK_UNROLL = 8
PASS_TOK = 4096
N_PASS = N_TOK // PASS_TOK
ROWS = 512
TILE_PITCH = ROWS + 8
SCATTER_UNROLL = 8
WORDS = CHUNKS // 2
ROW_BLOCK = 256
POST_BLOCK = 256


def _params(n_axes, arbitrary=False):
    sem = ("arbitrary" if arbitrary else "parallel",) * n_axes
    return pltpu.CompilerParams(dimension_semantics=sem, vmem_limit_bytes=VMEM_LIMIT)


def _mod_row(i, tile):
    start = i * tile
    return jnp.where(start < N_CTX, 0, 1 + (start - N_CTX) // LAT_T)


def _full(shape):
    nd = len(shape)
    return pl.BlockSpec(shape, lambda *_: (0,) * nd)


def _pair_specs(tile, width, lat_row0):
    nct = N_CTX // tile
    off = lat_row0 // tile
    return [pl.BlockSpec((tile, width), lambda i: (jnp.minimum(i, nct - 1), 0)),
            pl.BlockSpec((tile, width), lambda i: (jnp.maximum(i - nct, 0) + off, 0))]


def _pair_load(tile, ctx_ref, lat_ref):
    return jnp.where(pl.program_id(0) < N_CTX // tile, ctx_ref[...], lat_ref[...])


ADA_TN = 1536


def _ada_body(cond_ref, w_ref, b_ref, o_ref):
    c = cond_ref[...]
    a = c * jax.nn.sigmoid(c)
    a_hi = a.astype(BF16)
    a_lo = (a - a_hi.astype(F32)).astype(BF16)
    w = w_ref[0]
    w_hi = w.astype(BF16)
    w_lo = (w - w_hi.astype(F32)).astype(BF16)
    acc = jnp.dot(a_hi, w_hi, preferred_element_type=F32)
    acc += jnp.dot(a_lo, w_hi, preferred_element_type=F32)
    acc += jnp.dot(a_hi, w_lo, preferred_element_type=F32)
    o_ref[0] = acc + b_ref[0]


def _ada(cond, w_ada, b_ada):
    n = 6 * D_MODEL
    return pl.pallas_call(
        _ada_body,
        grid=(DEPTH, n // ADA_TN),
        in_specs=[
            _full((SUBLANES, D_MODEL)),
            pl.BlockSpec((1, D_MODEL, ADA_TN), lambda l, j: (l, 0, j)),
            pl.BlockSpec((1, 1, ADA_TN), lambda l, j: (l, 0, j)),
        ],
        out_specs=pl.BlockSpec((1, SUBLANES, ADA_TN), lambda l, j: (l, 0, j)),
        out_shape=jax.ShapeDtypeStruct((DEPTH, SUBLANES, n), F32),
        compiler_params=_params(2),
        name="ada_mod",
    )(cond, w_ada, b_ada.reshape(DEPTH, 1, n))


D_IN = Q_RANK + KV_RANK + ROPE + 4 * MIXW
IN_EXT = Q_RANK + KV_RANK + 4 * MIXW + LANES
UQ_EXT = HEADS * NOPE + HEADS * LANES


def _prep_win_body(w_ref, o_ref):
    w = w_ref[0]
    kr0 = Q_RANK + KV_RANK
    kr = w[:, kr0:kr0 + ROPE]
    half = ROPE // 2
    rot = jnp.concatenate([-kr[:, half:], kr[:, :half]], axis=-1)
    out = jnp.concatenate([w[:, :kr0], w[:, kr0 + ROPE:], kr, rot], axis=-1)
    o_ref[0] = out.astype(BF16)


def _rope_block(blk, cos_a, sin_a):
    return blk * cos_a + pltpu.roll(blk, ROPE, axis=1) * sin_a


def _prep_win(w_in):
    lyr = lambda l: (l, 0, 0)
    return pl.pallas_call(
        _prep_win_body,
        grid=(DEPTH,),
        in_specs=[pl.BlockSpec((1, D_MODEL, D_IN), lyr)],
        out_specs=pl.BlockSpec((1, D_MODEL, IN_EXT), lyr),
        out_shape=jax.ShapeDtypeStruct((DEPTH, D_MODEL, IN_EXT), BF16),
        compiler_params=_params(1),
        name="prep_w_in",
    )(w_in)


def _rms(x, g):
    return x * lax.rsqrt(jnp.mean(x * x, axis=-1, keepdims=True) + RMS_EPS) * g


MIX_COL0 = Q_RANK + KV_RANK


def _ffn_rows(ff_ref, n, tok0=0):
    return jnp.concatenate([ff_ref[pl.ds(tok0 * CHUNKS + c, n, stride=CHUNKS), :]
                            for c in range(CHUNKS)], axis=-1)


def _front_body(*refs, fused_ln):
    if fused_ln:
        (x1_ref, ff_ref, x1p_ref, x1n_ref, ffp_ref, ffn_ref, g2_ref, lg_ref, lb_ref,
         sh_ref, sc_ref, win_ref, qn_ref, kvn_ref, wuq_ref, wuk_ref, wuv_ref, cos_ref, sin_ref,
         pw_ref, ps_ref, cw_ref,
         ckv_ref, kr_ref, q_ref, k_ref, v_ref, bc_ref, x_ref) = refs

        def norm2(x1, ffn):
            return _layer_norm(ALPHA * x1 + g2_ref[0] * ffn, lg_ref[...], lb_ref[...])

    else:
        (xc_ref, xl_ref, xp_ref, xn_ref,
         sh_ref, sc_ref, win_ref, qn_ref, kvn_ref, wuq_ref, wuk_ref, wuv_ref, cos_ref, sin_ref,
         pw_ref, ps_ref, cw_ref,
         ckv_ref, kr_ref, q_ref, k_ref, v_ref, bc_ref) = refs
        is_ctx = pl.program_id(0) < N_CTX // TM

    n_sub = TM // SEQ_TILE
    subs = [slice(s * SEQ_TILE, (s + 1) * SEQ_TILE) for s in range(n_sub)]

    xs = []
    for s, rows in enumerate(subs):
        if fused_ln:
            x = norm2(x1_ref[rows, :], _ffn_rows(ff_ref, SEQ_TILE, s * SEQ_TILE))
            x_ref[rows, :] = x
        else:
            x = jnp.where(is_ctx, xc_ref[rows, :], xl_ref[rows, :])
        xs.append(x)
    if fused_ln:
        x_halo = [norm2(x1p_ref[...], _ffn_rows(ffp_ref, HALO)),
                  norm2(x1n_ref[...], _ffn_rows(ffn_ref, HALO))]
    else:
        x_halo = [xp_ref[...], xn_ref[...]]
    xs[-1] = jnp.concatenate([xs[-1]] + x_halo, axis=0)

    hs = []
    for x in xs:
        u = x * (1.0 + sc_ref[0]) + sh_ref[0]
        hs.append(jnp.dot(u.astype(BF16), win_ref[0], preferred_element_type=F32))
    mix_cols = slice(MIX_COL0, MIX_COL0 + 4 * MIXW)
    mix_halo = hs[-1][SEQ_TILE:, mix_cols]
    mixes = [h[:SEQ_TILE, mix_cols] for h in hs]

    for s, rows in enumerate(subs):
        prev = mix_halo[:HALO] if s == 0 else mixes[s - 1][SEQ_TILE - HALO:]
        nxt = mix_halo[HALO:] if s == n_sub - 1 else mixes[s + 1][:HALO]
        bc_ref[rows, :] = _seqmix_math(pl.program_id(0) * n_sub + s, mixes[s], prev, nxt,
                                       pw_ref[...], ps_ref[...], cw_ref[...])

    kr0 = MIX_COL0 + 4 * MIXW
    ckv_bs = []
    for s, rows in enumerate(subs):
        ckv = _rms(hs[s][:SEQ_TILE, Q_RANK:Q_RANK + KV_RANK], kvn_ref[...])
        ckv_ref[rows, :] = ckv
        ckv_bs.append(ckv.astype(BF16))
    for s, rows in enumerate(subs):
        k_nope = jnp.dot(ckv_bs[s], wuk_ref[...], preferred_element_type=F32)
        v_ref[rows, :] = jnp.dot(ckv_bs[s], wuv_ref[...], preferred_element_type=F32).astype(BF16)
        kr_blk = hs[s][:SEQ_TILE, kr0:kr0 + LANES]
        kr_ref[rows, :] = kr_blk
        k_rope = _rope_block(kr_blk, cos_ref[rows, :], sin_ref[rows, :])
        parts = []
        for hd in range(HEADS):
            parts += [k_nope[:, hd * NOPE:(hd + 1) * NOPE], k_rope]
        k_ref[rows, :] = jnp.concatenate(parts, axis=-1).astype(BF16)
    qns = [_rms(hs[s][:SEQ_TILE, :Q_RANK], qn_ref[...]).astype(BF16) for s in range(n_sub)]
    ra = HEADS * NOPE
    for s, rows in enumerate(subs):
        qq = jnp.dot(qns[s], wuq_ref[...], preferred_element_type=F32)
        parts = []
        for hd in range(HEADS):
            q_rope = _rope_block(qq[:, ra + hd * LANES:ra + (hd + 1) * LANES],
                                 cos_ref[rows, :], sin_ref[rows, :])
            parts += [qq[:, hd * NOPE:(hd + 1) * NOPE] * Q_SCALE, q_rope * Q_SCALE]
        q_ref[rows, :] = jnp.concatenate(parts, axis=-1).astype(BF16)


def _halo_specs(rows_per_token, width, lat_row0, n_rows):
    nct = N_CTX // TM
    nblk = n_rows // HALO

    def first_block(i):
        return (jnp.maximum(i - nct, 0) * TM + lat_row0) // HALO

    shape = (HALO * rows_per_token, width)
    return [pl.BlockSpec(shape, lambda i: (jnp.maximum(first_block(i) - 1, 0), 0)),
            pl.BlockSpec(shape, lambda i: (jnp.minimum(first_block(i) + TM // HALO, nblk - 1), 0))]


def _front(prev, mod3, layer, w_in_ext, q_norm, kv_norm, w_uq_ext, w_uk, w_uv, cos_a, sin_a,
           pool_bd, pool_scale, conv_w):
    row = lambda i: (i, 0)
    lyr = lambda i: (layer, 0, 0)
    modspec = lambda k: pl.BlockSpec((1, 1, D_MODEL), lambda i: (_mod_row(i, TM), 0, k))
    fused_ln = len(prev) == 5
    outs = [
        jax.ShapeDtypeStruct((N_TOK, KV_RANK), F32),
        jax.ShapeDtypeStruct((N_TOK, LANES), F32),
        jax.ShapeDtypeStruct((N_TOK, HEADS * HEAD_BLOCK), BF16),
        jax.ShapeDtypeStruct((N_TOK, HEADS * HEAD_BLOCK), BF16),
        jax.ShapeDtypeStruct((N_TOK, HEADS * VDIM), BF16),
        jax.ShapeDtypeStruct((N_TOK, 2 * MIXW), BF16),
    ]
    if fused_ln:
        x1, ff, mod3_prev, ln_g, ln_b = prev
        outs.append(jax.ShapeDtypeStruct((N_TOK, D_MODEL), F32))
        lead_specs = ([pl.BlockSpec((TM, D_MODEL), row), pl.BlockSpec((TM * CHUNKS, LANES), row)]
                      + _halo_specs(1, D_MODEL, N_CTX, N_TOK)
                      + _halo_specs(CHUNKS, LANES, N_CTX, N_TOK)
                      + [pl.BlockSpec((1, 1, D_MODEL), lambda i: (_mod_row(i, TM), 0, 5)),
                         _full((1, D_MODEL)), _full((1, D_MODEL))])
        lead_args = (x1, ff, x1, x1, ff, ff, mod3_prev, ln_g, ln_b)
    else:
        xc, xl = prev
        lead_specs = _pair_specs(TM, D_MODEL, 0) + _halo_specs(1, D_MODEL, 0, N_LAT)
        lead_args = (xc, xl, xl, xl)
    return pl.pallas_call(
        functools.partial(_front_body, fused_ln=fused_ln),
        grid=(N_TOK // TM,),
        in_specs=lead_specs + [
            modspec(0), modspec(1),
            pl.BlockSpec((1, D_MODEL, IN_EXT), lyr),
            _full((1, Q_RANK)), _full((1, KV_RANK)),
            _full((Q_RANK, UQ_EXT)),
            _full((KV_RANK, HEADS * NOPE)), _full((KV_RANK, HEADS * VDIM)),
            pl.BlockSpec((TM, LANES), row), pl.BlockSpec((TM, LANES), row),
            _full((MIXW, MIXW)), _full((1, MIXW)), _full((3, MIXW)),
        ],
        out_specs=tuple(pl.BlockSpec((TM, s.shape[1]), row) for s in outs),
        out_shape=tuple(outs),
        compiler_params=_params(1),
        name="front_ln" if fused_ln else "front",
    )(*lead_args, mod3, mod3, w_in_ext, q_norm, kv_norm, w_uq_ext, w_uk, w_uv, cos_a, sin_a,
      pool_bd, pool_scale, conv_w)


def _cachekv_body(ckv_ref, kr_ref, wuk_ref, wuv_ref, k_ref, v_ref):
    c = ckv_ref[0, 0].astype(BF16)
    k_nope = jnp.dot(c, wuk_ref[0], preferred_element_type=F32)
    v_ref[0, 0] = jnp.dot(c, wuv_ref[0], preferred_element_type=F32).astype(BF16)
    kr = kr_ref[0, 0]
    parts = []
    for hd in range(HEADS):
        parts += [k_nope[:, hd * NOPE:(hd + 1) * NOPE], kr]
    k_ref[0, 0] = jnp.concatenate(parts, axis=-1).astype(BF16)


def _cachekv(cache_ckv, kr_pad, w_uk, w_uv):
    bl = lambda b, l: (b, l, 0, 0)
    wl = lambda b, l: (l, 0, 0)
    return pl.pallas_call(
        _cachekv_body,
        grid=(LAT_B, DEPTH),
        in_specs=[
            pl.BlockSpec((1, 1, PAST_LEN, KV_RANK), bl),
            pl.BlockSpec((1, 1, PAST_LEN, LANES), bl),
            pl.BlockSpec((1, KV_RANK, HEADS * NOPE), wl),
            pl.BlockSpec((1, KV_RANK, HEADS * VDIM), wl),
        ],
        out_specs=(pl.BlockSpec((1, 1, PAST_LEN, HEADS * HEAD_BLOCK), bl),
                   pl.BlockSpec((1, 1, PAST_LEN, HEADS * VDIM), bl)),
        out_shape=(jax.ShapeDtypeStruct((LAT_B, DEPTH, PAST_LEN, HEADS * HEAD_BLOCK), BF16),
                   jax.ShapeDtypeStruct((LAT_B, DEPTH, PAST_LEN, HEADS * VDIM), BF16)),
        compiler_params=_params(2),
        name="cache_kv",
    )(cache_ckv, kr_pad, w_uk, w_uv)


EXT = SEQ_TILE + 2 * HALO


def _shift_up(x, k):
    return pltpu.roll(x, x.shape[0] - k, axis=0)


def _shift_down(x, k):
    return pltpu.roll(x, k, axis=0)


def _seqmix_math(j, main, prev, nxt, pw, ps, cw):
    n_ctx_tiles = N_CTX // SEQ_TILE
    tiles_per_lat = LAT_T // SEQ_TILE
    is_ctx = j < n_ctx_tiles
    jj = jnp.where(is_ctx, 0, (j - n_ctx_tiles) % tiles_per_lat)
    first = jj == 0
    last = jnp.where(is_ctx, True, jj == tiles_per_lat - 1)
    t_seq = jnp.where(is_ctx, CTX_T, LAT_T)

    prev = jnp.where(first, 0.0, prev)
    nxt = jnp.where(last, 0.0, nxt)
    ext = jnp.concatenate([prev, main, nxt], axis=0)
    p = ext[:, 0:MIXW]
    g_b = main[:, MIXW:2 * MIXW]
    g_c = ext[:, 2 * MIXW:3 * MIXW]
    h_in = ext[:, 3 * MIXW:4 * MIXW]

    sums = {}
    b = p
    for w in POOL_WINDOWS:
        b = b + _shift_up(b, w // 2)
        sums[w] = _shift_down(b, w // 2)[HALO:HALO + SEQ_TILE]
    tpos = jj * SEQ_TILE + lax.broadcasted_iota(I32, (SEQ_TILE, 1), 0)
    lane = lax.broadcasted_iota(I32, (1, MIXW), 1)
    num = None
    den = None
    for g, w in enumerate(POOL_WINDOWS):
        lo = jnp.maximum(tpos - w // 2, 0)
        hi = jnp.minimum(tpos + w - w // 2, t_seq)
        cnt = (hi - lo).astype(F32)
        if num is None:
            num, den = sums[w], jnp.broadcast_to(cnt, (SEQ_TILE, MIXW))
        else:
            sel = lane >= g * POOL_GROUP
            num = jnp.where(sel, sums[w], num)
            den = jnp.where(sel, cnt, den)
    pooled = num / den - p[HALO:HALO + SEQ_TILE]
    b_out = jnp.dot(pooled.astype(BF16), pw, preferred_element_type=F32) * ps

    y = g_c * h_in
    conv = (_shift_down(y, 1) * cw[0:1] + y * cw[1:2] + _shift_up(y, 1) * cw[2:3])
    c_out = g_b * conv[HALO:HALO + SEQ_TILE]
    return jnp.concatenate([b_out, c_out], axis=-1).astype(BF16)


_NT = (((1,), (1,)), ((), ()))


def _v_ext(v):
    return jnp.concatenate([v, jnp.ones(v.shape, v.dtype)], axis=-1)


def _attn_ctx_body(q_ref, k_ref, v_ref, o_ref):
    outs = []
    for hd in range(HEADS):
        qh = q_ref[:, hd * HEAD_BLOCK:(hd + 1) * HEAD_BLOCK]
        kh = k_ref[:, hd * HEAD_BLOCK:(hd + 1) * HEAD_BLOCK]
        vh = v_ref[:, hd * VDIM:(hd + 1) * VDIM]
        s = lax.dot_general(qh, kh, _NT, preferred_element_type=F32)
        m = jnp.max(s, axis=-1, keepdims=True)
        p = jnp.exp2(s - m)
        acc = jnp.dot(p.astype(BF16), _v_ext(vh), preferred_element_type=F32)
        outs.append(acc[:, :VDIM] / acc[:, VDIM:])
    o_ref[...] = jnp.concatenate(outs, axis=-1).astype(BF16)


def _attn_ctx(q, k, v):
    row = lambda b: (b, 0)
    return pl.pallas_call(
        _attn_ctx_body,
        grid=(CTX_B,),
        in_specs=[pl.BlockSpec((CTX_T, HEADS * HEAD_BLOCK), row),
                  pl.BlockSpec((CTX_T, HEADS * HEAD_BLOCK), row),
                  pl.BlockSpec((CTX_T, HEADS * VDIM), row)],
        out_specs=pl.BlockSpec((CTX_T, HEADS * VDIM), row),
        out_shape=jax.ShapeDtypeStruct((N_CTX, HEADS * VDIM), BF16),
        compiler_params=_params(1),
        name="attn_ctx",
    )(q, k, v)


def _attn_lat_body(q_ref, kc_ref, vc_ref, ko_ref, vo_ref, o_ref, m_ref, acc_ref):
    reps = KV_CHUNK // LANES

    def update(hd, k, v, first):
        qh = q_ref[:, hd * HEAD_BLOCK:(hd + 1) * HEAD_BLOCK]
        s = lax.dot_general(qh, k, _NT, preferred_element_type=F32)
        mx = jnp.max(s, axis=-1, keepdims=True)
        if first:
            m_new = jnp.broadcast_to(mx, (TQ, LANES))
        else:
            m_old = m_ref[hd]
            m_new = jnp.maximum(m_old, mx)
        p = jnp.exp2(s - jnp.concatenate([m_new] * reps, axis=-1))
        pv = jnp.dot(p.astype(BF16), _v_ext(v), preferred_element_type=F32)
        if first:
            acc_ref[hd] = pv
        else:
            a = jnp.exp2(m_old - m_new)
            acc_ref[hd] = acc_ref[hd] * jnp.concatenate([a, a], axis=-1) + pv
        m_ref[hd] = m_new

    def head_slices(hd):
        return slice(hd * HEAD_BLOCK, (hd + 1) * HEAD_BLOCK), slice(hd * VDIM, (hd + 1) * VDIM)

    for hd in range(HEADS):
        ks, vs = head_slices(hd)
        update(hd, kc_ref[0, 0, :, ks], vc_ref[0, 0, :, vs], True)

    def body(c, _):
        for j in range(CHUNK_UNROLL):
            rows = pl.ds(pl.multiple_of((c * CHUNK_UNROLL + j) * KV_CHUNK, KV_CHUNK), KV_CHUNK)
            for hd in range(HEADS):
                ks, vs = head_slices(hd)
                update(hd, ko_ref[rows, ks], vo_ref[rows, vs], False)
        return 0

    lax.fori_loop(0, LAT_T // (KV_CHUNK * CHUNK_UNROLL), body, 0)
    outs = [acc_ref[hd][:, :VDIM] / acc_ref[hd][:, VDIM:] for hd in range(HEADS)]
    o_ref[...] = jnp.concatenate(outs, axis=-1).astype(BF16)


def _attn_lat(q, k, v, kc, vc, layer):
    qt = LAT_T // TQ
    ctx_tiles = N_CTX // TQ
    ctx_blocks = N_CTX // LAT_T
    return pl.pallas_call(
        _attn_lat_body,
        grid=(LAT_B, qt),
        in_specs=[
            pl.BlockSpec((TQ, HEADS * HEAD_BLOCK), lambda b, i: (ctx_tiles + b * qt + i, 0)),
            pl.BlockSpec((1, 1, PAST_LEN, HEADS * HEAD_BLOCK), lambda b, i: (b, layer, 0, 0)),
            pl.BlockSpec((1, 1, PAST_LEN, HEADS * VDIM), lambda b, i: (b, layer, 0, 0)),
            pl.BlockSpec((LAT_T, HEADS * HEAD_BLOCK), lambda b, i: (ctx_blocks + b, 0)),
            pl.BlockSpec((LAT_T, HEADS * VDIM), lambda b, i: (ctx_blocks + b, 0)),
        ],
        out_specs=pl.BlockSpec((TQ, HEADS * VDIM), lambda b, i: (b * qt + i, 0)),
        out_shape=jax.ShapeDtypeStruct((N_LAT, HEADS * VDIM), BF16),
        scratch_shapes=[pltpu.VMEM((HEADS, TQ, LANES), F32),
                        pltpu.VMEM((HEADS, TQ, 2 * VDIM), F32)],
        compiler_params=_params(2),
        name="attn_lat",
    )(q, kc, vc, k, v)


def _layer_norm(y, g, b):
    mu = jnp.mean(y, axis=-1, keepdims=True)
    d = y - mu
    var = jnp.mean(d * d, axis=-1, keepdims=True)
    return d * lax.rsqrt(var + LN_EPS) * g + b


def _postmix_body(actx_ref, alat_ref, bc_ref, xc_ref, xl_ref, g1_ref, sh2_ref, sc2_ref, wa_ref, wbc_ref,
                  lg_ref, lb_ref, wr_ref, x1_ref, u2p_ref, aff_ref):
    is_ctx = pl.program_id(0) < N_CTX // TM
    lane = lax.broadcasted_iota(I32, (1, LANES), 1)
    blocks = [slice(b * POST_BLOCK, (b + 1) * POST_BLOCK) for b in range(TM // POST_BLOCK)]
    mixes = []
    for rows in blocks:
        a = jnp.where(is_ctx, actx_ref[rows, :], alat_ref[rows, :])
        mix = jnp.dot(a, wa_ref[0], preferred_element_type=F32)
        mixes.append(mix + jnp.dot(bc_ref[rows, :], wbc_ref[0], preferred_element_type=F32))
    u2s = []
    for b, rows in enumerate(blocks):
        x = jnp.where(is_ctx, xc_ref[rows, :], xl_ref[rows, :])
        x1 = _layer_norm(ALPHA * x + g1_ref[0] * mixes[b], lg_ref[...], lb_ref[...])
        x1_ref[rows, :] = x1
        u2 = x1 * (1.0 + sc2_ref[0]) + sh2_ref[0]
        for c in range(WORDS):
            lo = u2[:, c * LANES:(c + 1) * LANES]
            hi = u2[:, (c + WORDS) * LANES:(c + WORDS + 1) * LANES]
            u2p_ref[pl.ds(b * POST_BLOCK * WORDS + c, POST_BLOCK, stride=WORDS), :] = (
                pltpu.pack_elementwise([lo, hi], packed_dtype=BF16))
        u2s.append(u2.astype(BF16))
    for b, rows in enumerate(blocks):
        logits = jnp.dot(u2s[b], wr_ref[...], preferred_element_type=F32)
        logits = jnp.where(lane < N_EXPERTS, logits, -jnp.inf)
        m = jnp.max(logits, axis=-1, keepdims=True)
        e = jnp.exp(logits - m)
        aff_ref[rows, :] = e / jnp.sum(e, axis=-1, keepdims=True)


def _postmix(a_ctx, a_lat, bc, xc, xl, lat_row0, mod3, layer, w_out_b, ln_g, ln_b, w_router_pad):
    row = lambda i: (i, 0)
    modspec = lambda k: pl.BlockSpec((1, 1, D_MODEL), lambda i: (_mod_row(i, TM), 0, k))
    return pl.pallas_call(
        _postmix_body,
        grid=(N_TOK // TM,),
        in_specs=_pair_specs(TM, HEADS * VDIM, 0) + [
            pl.BlockSpec((TM, 2 * MIXW), row),
        ] + _pair_specs(TM, D_MODEL, lat_row0) + [
            modspec(2), modspec(3), modspec(4),
            pl.BlockSpec((1, HEADS * VDIM, D_MODEL), lambda i: (layer, 0, 0)),
            pl.BlockSpec((1, 2 * MIXW, D_MODEL), lambda i: (layer, 1, 0)),
            _full((1, D_MODEL)), _full((1, D_MODEL)),
            _full((D_MODEL, LANES)),
        ],
        out_specs=(pl.BlockSpec((TM, D_MODEL), row),
                   pl.BlockSpec((TM * WORDS, LANES), row),
                   pl.BlockSpec((TM, LANES), row)),
        out_shape=(jax.ShapeDtypeStruct((N_TOK, D_MODEL), F32),
                   jax.ShapeDtypeStruct((N_TOK * WORDS, LANES), jnp.uint32),
                   jax.ShapeDtypeStruct((N_TOK, LANES), F32)),
        compiler_params=_params(1),
        name="postmix",
    )(a_ctx, a_lat, bc, xc, xl, mod3, mod3, mod3, w_out_b, w_out_b, ln_g, ln_b, w_router_pad)


PREFIX_CHUNK = 256


def _route_ctx_body(aff_ref, idx_ref):
    n_rows = CTX_B * N_EXPERTS
    dense = jnp.concatenate(
        [aff_ref[b * CTX_T:(b + 1) * CTX_T, :].T[:N_EXPERTS] for b in range(CTX_B)], axis=0)

    def search(i, thr):
        cand = thr | jnp.left_shift(jnp.int32(1), 30 - i)
        n = jnp.sum((dense >= pltpu.bitcast(cand, F32)).astype(I32), axis=1, keepdims=True)
        return jnp.where(n >= CTX_CAP, cand, thr)

    thr_bits = lax.fori_loop(0, 31, search, jnp.zeros((n_rows, 1), I32))
    thr = pltpu.bitcast(thr_bits, F32)
    above = pltpu.bitcast(thr_bits + 1, F32)
    gt = (dense > thr) & (dense >= above)
    eq = (dense >= thr) & jnp.logical_not(gt)
    n_gt = jnp.sum(gt.astype(I32), axis=1, keepdims=True)
    need = (CTX_CAP - n_gt).astype(F32)

    ri = lax.broadcasted_iota(I32, (CTX_T, CTX_T), 0)
    ci = lax.broadcasted_iota(I32, (CTX_T, CTX_T), 1)
    tri = jnp.where(ri <= ci, 1.0, 0.0).astype(BF16)
    p_eq = jnp.dot(jnp.where(eq, 1.0, 0.0).astype(BF16), tri, preferred_element_type=F32)
    sel = jnp.where(gt, 1.0, jnp.where(eq & (p_eq <= need), 1.0, 0.0))
    cnt = jnp.dot(sel.astype(BF16), tri, preferred_element_type=F32)

    ones = jnp.ones((CTX_T, LANES), BF16)
    lane = lax.broadcasted_iota(I32, (1, LANES), 1)
    out = jnp.zeros((n_rows, LANES), F32)
    for r in range(CTX_CAP):
        le = jnp.where(cnt <= float(r), 1.0, 0.0).astype(BF16)
        out = jnp.where(lane == r, jnp.dot(le, ones, preferred_element_type=F32), out)
    idx_ref[...] = out.astype(I32)


def _route_ctx(aff):
    idx = pl.pallas_call(
        _route_ctx_body,
        grid=(1,),
        in_specs=[pl.BlockSpec((N_CTX, LANES), lambda i: (0, 0))],
        out_specs=_full((CTX_B * N_EXPERTS, LANES)),
        out_shape=jax.ShapeDtypeStruct((CTX_B * N_EXPERTS, LANES), I32),
        compiler_params=_params(1),
        name="route_ctx",
    )(aff)
    return idx[:, :CTX_CAP].reshape(CTX_B, N_EXPERTS, CTX_CAP)


RANK_TILE = LANES


def _expert_row(col):
    full = jnp.concatenate([jnp.broadcast_to(col, (N_EXPERTS, LANES)),
                            jnp.zeros((LANES - N_EXPERTS, LANES), col.dtype)], axis=0)
    return full.T[0:1]


def _route_count_body(aff_ref, cnt_ref, cend_ref, *, seq, cap):
    n_chunks = seq // PREFIX_CHUNK
    dense = aff_ref[...].T[:N_EXPERTS]

    def search(i, thr):
        cand = thr | jnp.left_shift(jnp.int32(1), 30 - i)
        n = jnp.sum((dense >= pltpu.bitcast(cand, F32)).astype(I32), axis=1, keepdims=True)
        return jnp.where(n >= cap, cand, thr)

    thr_bits = lax.fori_loop(0, 31, search, jnp.zeros((N_EXPERTS, 1), I32))
    thr_col = pltpu.bitcast(thr_bits, F32)
    above_col = pltpu.bitcast(thr_bits + 1, F32)
    n_gt = jnp.sum(((dense > thr_col) & (dense >= above_col)).astype(I32), axis=1, keepdims=True)
    thr = _expert_row(thr_col)
    above = _expert_row(above_col)
    need = _expert_row((cap - n_gt).astype(F32))

    ri = lax.broadcasted_iota(I32, (PREFIX_CHUNK, PREFIX_CHUNK), 0)
    ci = lax.broadcasted_iota(I32, (PREFIX_CHUNK, PREFIX_CHUNK), 1)
    tri = jnp.where(ci <= ri, 1.0, 0.0).astype(BF16)
    carry_eq = jnp.zeros((1, LANES), F32)
    carry_sel = jnp.zeros((1, LANES), F32)
    per = PREFIX_CHUNK // RANK_TILE
    for c in range(n_chunks):
        rows = slice(c * PREFIX_CHUNK, (c + 1) * PREFIX_CHUNK)
        a = aff_ref[rows, :]
        gt = (a > thr) & (a >= above)
        eq = (a >= thr) & jnp.logical_not(gt)
        p_eq = jnp.dot(tri, jnp.where(eq, 1.0, 0.0).astype(BF16),
                       preferred_element_type=F32) + carry_eq
        sel = jnp.where(gt, 1.0, jnp.where(eq & (p_eq <= need), 1.0, 0.0))
        p_sel = jnp.dot(tri, sel.astype(BF16), preferred_element_type=F32) + carry_sel
        dense_cnt = p_sel.T[:N_EXPERTS]
        for k in range(per):
            last = (k + 1) * RANK_TILE - 1
            cnt_ref[c * per + k] = dense_cnt[:, k * RANK_TILE:(k + 1) * RANK_TILE]
            cend_ref[0, c * per + k:c * per + k + 1, :] = p_sel[last:last + 1, :].astype(I32)
        carry_eq = p_eq[PREFIX_CHUNK - 1:PREFIX_CHUNK, :]
        carry_sel = p_sel[PREFIX_CHUNK - 1:PREFIX_CHUNK, :]


def _route_count(aff, first_block, n_req, seq, cap):
    n_tiles = seq // RANK_TILE
    return pl.pallas_call(
        functools.partial(_route_count_body, seq=seq, cap=cap),
        grid=(n_req,),
        in_specs=[pl.BlockSpec((seq, LANES), lambda b: (first_block + b, 0))],
        out_specs=(pl.BlockSpec((n_tiles, N_EXPERTS, RANK_TILE), lambda b: (b, 0, 0)),
                   pl.BlockSpec((1, n_tiles, LANES), lambda b: (b, 0, 0))),
        out_shape=(jax.ShapeDtypeStruct((n_req * n_tiles, N_EXPERTS, RANK_TILE), F32),
                   jax.ShapeDtypeStruct((n_req, n_tiles, LANES), I32)),
        compiler_params=_params(1),
        name=f"route_count_{seq}",
    )(aff)


def _route_rank_body(cend_ref, cnt_ref, idx_ref, *bufs, seq, cap):
    b = pl.program_id(0)
    n_tiles = seq // RANK_TILE
    rank = lax.broadcasted_iota(I32, (RANK_TILE, RANK_TILE), 0).astype(F32)
    ones = jnp.ones((RANK_TILE, LANES), BF16)

    def tile(j, _):
        counts = cnt_ref[j]
        for e in range(N_EXPERTS):
            prev = (b * N_EXPERTS + e) * n_tiles + j - 1
            start = jnp.where(j > 0, cend_ref[jnp.maximum(prev, 0)], 0)
            local = counts[e:e + 1, :] - jnp.asarray(start, F32)
            le = jnp.where(local <= rank, 1.0, 0.0).astype(BF16)
            pos = jnp.dot(le, ones, preferred_element_type=F32) + jnp.asarray(j * RANK_TILE, F32)
            bufs[e][pl.ds(start, RANK_TILE), :] = pos
        return 0

    lax.fori_loop(0, n_tiles, tile, 0)
    lane = lax.broadcasted_iota(I32, (1, LANES), 1)
    out = jnp.zeros((cap, LANES), F32)
    for e in range(N_EXPERTS):
        out = jnp.where(lane == e, bufs[e][0:cap, :], out)
    idx_ref[0] = out.astype(I32)


def _route_rank(cend_flat, cnt, n_req, seq, cap):
    n_tiles = seq // RANK_TILE
    grid_spec = pltpu.PrefetchScalarGridSpec(
        num_scalar_prefetch=1,
        grid=(n_req,),
        in_specs=[pl.BlockSpec((n_tiles, N_EXPERTS, RANK_TILE), lambda b, c: (b, 0, 0))],
        out_specs=pl.BlockSpec((1, cap, LANES), lambda b, c: (b, 0, 0)),
        scratch_shapes=[pltpu.VMEM((cap + RANK_TILE, LANES), F32)] * N_EXPERTS,
    )
    return pl.pallas_call(
        functools.partial(_route_rank_body, seq=seq, cap=cap),
        grid_spec=grid_spec,
        out_shape=jax.ShapeDtypeStruct((n_req, cap, LANES), I32),
        compiler_params=_params(1),
        name=f"route_rank_{seq}",
    )(cend_flat, cnt)


def _gather_rows(idx_ref, base, u2p_ref, aff_ref, tile_ref, gate_ref, r0, n):
    for i in range(n):
        r = r0 + i
        t = idx_ref[base + r]
        slab = u2p_ref[pl.ds(pl.multiple_of(t * WORDS, WORDS), WORDS), :]
        tile_ref[pl.ds(r, WORDS, stride=TILE_PITCH), :] = slab
        gate_ref[pl.ds(r, 1), :] = aff_ref[pl.ds(t, 1), :]


def _scatter_rows(idx_ref, base, y_ref, acc_ref, r0, n):
    for g0 in range(0, n, SCATTER_UNROLL):
        rows = [r0 + g0 + i for i in range(SCATTER_UNROLL)]
        dst = [pl.ds(pl.multiple_of(idx_ref[base + r] * CHUNKS, CHUNKS), CHUNKS) for r in rows]
        vals = [acc_ref[d, :] + y_ref[pl.ds(pl.multiple_of(r * CHUNKS, CHUNKS), CHUNKS), :]
                for d, r in zip(dst, rows)]
        for d, v in zip(dst, vals):
            acc_ref[d, :] = v


def _expert_rows(e, tile_ref, gate_ref, wg, wu, wd, y_ref, r0, n):
    halves = [[], []]
    for c in range(WORDS):
        w = tile_ref[pl.ds(c * TILE_PITCH + r0, n), :]
        for k in range(2):
            halves[k].append(pltpu.unpack_elementwise(w, index=k, packed_dtype=BF16,
                                                      unpacked_dtype=F32))
    x = jnp.concatenate(halves[0] + halves[1], axis=-1).astype(BF16)
    lane = lax.broadcasted_iota(I32, (1, LANES), 1)
    gate = jnp.sum(jnp.where(lane == e, gate_ref[pl.ds(r0, n), :], 0.0), axis=-1, keepdims=True)
    hg = jnp.dot(x, wg, preferred_element_type=F32)
    hu = jnp.dot(x, wu, preferred_element_type=F32)
    hidden = (hg * jax.nn.sigmoid(hg) * hu).astype(BF16)
    y = jnp.dot(hidden, wd, preferred_element_type=F32) * gate
    for c in range(CHUNKS):
        y_ref[pl.ds(r0 * CHUNKS + c, n, stride=CHUNKS), :] = y[:, c * LANES:(c + 1) * LANES]


def _moe_body(idx_ref, u2p_ref, aff_ref, wg_ref, wu_ref, wd_ref, acc_ref,
              tile_a, tile_b, gate_a, gate_b, y_a, y_b):
    p = pl.program_id(0)
    e = pl.program_id(1)
    step = p * N_EXPERTS + e
    base = step * ROWS
    base_next = jnp.minimum(step + 1, N_PASS * N_EXPERTS - 1) * ROWS
    base_prev = jnp.where(e == 0, base, base - ROWS)

    @pl.when(e == 0)
    def _():
        acc_ref[...] = jnp.zeros_like(acc_ref)
        y_b[...] = jnp.zeros_like(y_b)

        def gather(c, _):
            _gather_rows(idx_ref, base, u2p_ref, aff_ref, tile_a, gate_a, c * SUBLANES, SUBLANES)
            return 0

        lax.fori_loop(0, ROWS // SUBLANES, gather, 0)

    def run(tile_cur, gate_cur, y_cur, tile_nxt, gate_nxt, y_prv):
        wg = wg_ref[0, 0].astype(BF16)
        wu = wu_ref[0, 0].astype(BF16)
        wd = wd_ref[0, 0].astype(BF16)
        for blk in range(ROWS // ROW_BLOCK):
            r0 = blk * ROW_BLOCK
            _gather_rows(idx_ref, base_next, u2p_ref, aff_ref, tile_nxt, gate_nxt, r0, ROW_BLOCK)
            _expert_rows(e, tile_cur, gate_cur, wg, wu, wd, y_cur, r0, ROW_BLOCK)
            _scatter_rows(idx_ref, base_prev, y_prv, acc_ref, r0, ROW_BLOCK)

    @pl.when(e % 2 == 0)
    def _():
        run(tile_a, gate_a, y_a, tile_b, gate_b, y_b)

    @pl.when(e % 2 == 1)
    def _():
        run(tile_b, gate_b, y_b, tile_a, gate_a, y_a)

    @pl.when(e == N_EXPERTS - 1)
    def _():
        def scatter(c, _):
            _scatter_rows(idx_ref, base, y_b, acc_ref, c * SCATTER_UNROLL, SCATTER_UNROLL)
            return 0

        lax.fori_loop(0, ROWS // SCATTER_UNROLL, scatter, 0)


def _moe(idx_flat, u2p, aff, w_gate, w_up, w_down, layer):
    one = pl.Buffered(1)
    grid_spec = pltpu.PrefetchScalarGridSpec(
        num_scalar_prefetch=1,
        grid=(N_PASS, N_EXPERTS),
        in_specs=[
            pl.BlockSpec((PASS_TOK * WORDS, LANES), lambda p, e, idx: (p, 0)),
            pl.BlockSpec((PASS_TOK, LANES), lambda p, e, idx: (p, 0), pipeline_mode=one),
            pl.BlockSpec((1, 1, D_MODEL, EXPERT_DIM), lambda p, e, idx: (layer, e, 0, 0)),
            pl.BlockSpec((1, 1, D_MODEL, EXPERT_DIM), lambda p, e, idx: (layer, e, 0, 0)),
            pl.BlockSpec((1, 1, EXPERT_DIM, D_MODEL), lambda p, e, idx: (layer, e, 0, 0)),
        ],
        out_specs=pl.BlockSpec((PASS_TOK * CHUNKS, LANES), lambda p, e, idx: (p, 0),
                               pipeline_mode=one),
        scratch_shapes=[
            pltpu.VMEM((WORDS * TILE_PITCH, LANES), jnp.uint32),
            pltpu.VMEM((WORDS * TILE_PITCH, LANES), jnp.uint32),
            pltpu.VMEM((ROWS, LANES), F32),
            pltpu.VMEM((ROWS, LANES), F32),
            pltpu.VMEM((ROWS * CHUNKS, LANES), F32),
            pltpu.VMEM((ROWS * CHUNKS, LANES), F32),
        ],
    )
    return pl.pallas_call(
        _moe_body,
        grid_spec=grid_spec,
        out_shape=jax.ShapeDtypeStruct((N_TOK * CHUNKS, LANES), F32),
        compiler_params=_params(2, arbitrary=True),
        name="moe",
    )(idx_flat, u2p, aff, w_gate, w_up, w_down)


def _final_body(x1_ref, ff_ref, g2_ref, lg_ref, lb_ref, o_ref):
    ffn = jnp.concatenate([ff_ref[pl.ds(c, TM, stride=CHUNKS), :] for c in range(CHUNKS)], axis=-1)
    o_ref[...] = _layer_norm(ALPHA * x1_ref[...] + g2_ref[0] * ffn, lg_ref[...], lb_ref[...])


def _final(x1, ff, mod3, ln_g, ln_b, row0=0, n_rows=N_TOK):
    t0 = row0 // TM
    row = lambda i: (i + t0, 0)
    return pl.pallas_call(
        _final_body,
        grid=(n_rows // TM,),
        in_specs=[
            pl.BlockSpec((TM, D_MODEL), row),
            pl.BlockSpec((TM * CHUNKS, LANES), row),
            pl.BlockSpec((1, 1, D_MODEL), lambda i: (_mod_row(i + t0, TM), 0, 5)),
            _full((1, D_MODEL)), _full((1, D_MODEL)),
        ],
        out_specs=pl.BlockSpec((TM, D_MODEL), lambda i: (i, 0)),
        out_shape=jax.ShapeDtypeStruct((n_rows, D_MODEL), F32),
        compiler_params=_params(1),
        name="final_ln",
    )(x1, ff, mod3, ln_g, ln_b)


def _rot_cols(w):
    half = ROPE // 2
    return jnp.concatenate([-w[..., half:], w[..., :half]], axis=-1)


def _pad_lanes(w):
    pad = [(0, 0)] * (w.ndim - 1) + [(0, LANES - w.shape[-1])]
    return jnp.pad(w, pad)


def _rope_tables():
    rows_n = LAT_T // GRID_W
    r, cl = jnp.meshgrid(jnp.arange(rows_n, dtype=F32), jnp.arange(GRID_W, dtype=F32), indexing="ij")
    inv = ROPE_THETA ** (-jnp.arange(0, ROPE // 2, 2, dtype=F32) / (ROPE // 2))
    ang = jnp.concatenate([r.reshape(-1)[:, None] * inv, cl.reshape(-1)[:, None] * inv], axis=-1)
    cos, sin = jnp.cos(ang), jnp.sin(ang)
    cos_lat = _pad_lanes(jnp.concatenate([cos, cos], axis=-1))
    sin_lat = _pad_lanes(jnp.concatenate([sin, sin], axis=-1))
    cos_ctx = _pad_lanes(jnp.ones((N_CTX, ROPE), F32))
    sin_ctx = jnp.zeros((N_CTX, LANES), F32)
    cos_a = jnp.concatenate([cos_ctx] + [cos_lat] * LAT_B, axis=0)
    sin_a = jnp.concatenate([sin_ctx] + [sin_lat] * LAT_B, axis=0)
    return cos_a, sin_a


def kernel(x_prompt, x_sample, cache_ckv, cache_krope, c, c_ctx, w_in, q_norm, w_uq, kv_norm, w_uk, w_uv,
           pool_w, pool_scale, conv_w, w_out, w_ada, b_ada, ln1_g, ln1_b, ln2_g, ln2_b, w_router,
           w_gate, w_up, w_down):
    w_in_ext = _prep_win(w_in)
    hw = NOPE + ROPE
    uq_nope = [w_uq[:, :, h * hw:h * hw + NOPE] for h in range(HEADS)]
    uq_rope = [w_uq[:, :, h * hw + NOPE:(h + 1) * hw] for h in range(HEADS)]
    w_uq_ext = jnp.concatenate(
        uq_nope + [blk for w in uq_rope for blk in (w, _rot_cols(w))], axis=-1).astype(BF16)
    w_uk_b = w_uk.astype(BF16)
    w_uv_b = w_uv.astype(BF16)
    eye = jnp.eye(len(POOL_WINDOWS), dtype=F32)
    pool_bd = (pool_w[:, :, :, None, :] * eye[None, :, None, :, None]).reshape(DEPTH, MIXW, MIXW).astype(BF16)
    w_out_b = w_out.astype(BF16)
    w_router_pad = _pad_lanes(w_router).astype(BF16)
    cos_a, sin_a = _rope_tables()

    cond = jnp.concatenate([c_ctx[None, :], c, jnp.zeros((SUBLANES - 1 - LAT_B, D_MODEL), F32)], axis=0)
    mod = _ada(cond, w_ada, b_ada)
    kc, vc = _cachekv(cache_ckv, _pad_lanes(cache_krope), w_uk_b, w_uv_b)

    prev = (x_prompt.reshape(N_CTX, D_MODEL), x_sample.reshape(N_LAT, D_MODEL))
    xs = prev + (0,)
    ckv_layers, kr_layers = [], []
    ctx_off = (jnp.arange(CTX_B, dtype=I32) * CTX_T)[:, None, None]
    for l in range(DEPTH):
        mod3 = mod[l].reshape(SUBLANES, 1, 6 * D_MODEL)
        outs = _front(prev, mod3, l, w_in_ext, q_norm[l][None], kv_norm[l][None],
                      w_uq_ext[l], w_uk_b[l], w_uv_b[l], cos_a, sin_a,
                      pool_bd[l], pool_scale[l][None], conv_w[l])
        ckv, kr, q, k, v, bc = outs[:6]
        if l > 0:
            xs = (outs[6], outs[6], N_CTX)
        ckv_layers.append(ckv[:N_CTX].reshape(CTX_B, CTX_T, KV_RANK))
        kr_layers.append(kr[:N_CTX, :ROPE].reshape(CTX_B, CTX_T, ROPE))
        a_ctx = _attn_ctx(q, k, v)
        a_lat = _attn_lat(q, k, v, kc, vc, l)
        x1, u2p, aff = _postmix(a_ctx, a_lat, bc, *xs, mod3, l, w_out_b,
                                ln1_g[l][None], ln1_b[l][None], w_router_pad[l])
        idx_ctx = _route_ctx(aff)
        cnt_lat, cend = _route_count(aff, N_CTX // LAT_T, LAT_B, LAT_T, LAT_CAP)
        cend_flat = cend[:, :, :N_EXPERTS].transpose(0, 2, 1).reshape(-1)
        idx_lat = _route_rank(cend_flat, cnt_lat, LAT_B, LAT_T, LAT_CAP)
        idx_lat = idx_lat[:, :, :N_EXPERTS].transpose(0, 2, 1)
        idx_ctx = (idx_ctx + ctx_off).transpose(1, 0, 2).reshape(1, N_EXPERTS, ROWS)
        idx_flat = jnp.concatenate([idx_ctx, idx_lat], axis=0).reshape(-1)
        ff = _moe(idx_flat, u2p, aff, w_gate, w_up, w_down, l)
        if l + 1 < DEPTH:
            prev = (x1, ff, mod3, ln2_g[l][None], ln2_b[l][None])
        else:
            y_ctx = _final(x1, ff, mod3, ln2_g[l][None], ln2_b[l][None], 0, N_CTX)
            y_lat = _final(x1, ff, mod3, ln2_g[l][None], ln2_b[l][None], N_CTX, N_LAT)

    y_prompt = y_ctx.reshape(CTX_B, CTX_T, D_MODEL)
    y_sample = y_lat.reshape(LAT_B, LAT_T, D_MODEL)
    new_ckv = jnp.stack(ckv_layers, axis=1)
    new_krope = jnp.stack(kr_layers, axis=1)
    return (y_prompt, y_sample, new_ckv, new_krope)
```

```python
import functools

import jax
import jax.numpy as jnp
from jax import lax
from jax.experimental import pallas as pl
from jax.experimental.pallas import tpu as pltpu

F32 = jnp.float32
BF16 = jnp.bfloat16
I32 = jnp.int32

D_MODEL = 1024
CTX_B, CTX_T = 16, 256
LAT_B, LAT_T = 2, 4096
DEPTH = 4
PAST_LEN = 512
GRID_W = 64
HEADS = 4
NOPE, ROPE, VDIM = 128, 64, 128
Q_RANK, KV_RANK = 384, 256
POOL_WINDOWS = (2, 4, 8, 16)
POOL_GROUP = 64
MIXW = 256
N_EXPERTS = 16
EXPERT_DIM = 512
ROPE_THETA = 10000.0
ATTN_SCALE = (NOPE + ROPE) ** -0.5
LOG2E = 1.4426950408889634
Q_SCALE = ATTN_SCALE * LOG2E
ALPHA = (2 * DEPTH) ** 0.25
RMS_EPS = 1e-6
LN_EPS = 1e-5

N_CTX = CTX_B * CTX_T
N_LAT = LAT_B * LAT_T
N_TOK = N_CTX + N_LAT
CTX_CAP = 2 * CTX_T // N_EXPERTS
LAT_CAP = 2 * LAT_T // N_EXPERTS

LANES = 128
SUBLANES = 8
CHUNKS = D_MODEL // LANES
HEAD_BLOCK = 2 * LANES
VMEM_LIMIT = 56 * 1024 * 1024

TM = 512
SEQ_TILE = 256
HALO = 8
TQ = 512
KV_CHUNK = 512
CHUNK_UNROLL = 8
PASS_TOK = 4096
N_PASS = N_TOK // PASS_TOK
ROWS = 512
TILE_PITCH = ROWS + 8
SCATTER_UNROLL = 8
WORDS = CHUNKS // 2
ROW_BLOCK = 256
POST_BLOCK = 256


def _params(n_axes, arbitrary=False):
    sem = ("arbitrary" if arbitrary else "parallel",) * n_axes
    return pltpu.CompilerParams(dimension_semantics=sem, vmem_limit_bytes=VMEM_LIMIT)


def _mod_row(i, tile):
    start = i * tile
    return jnp.where(start < N_CTX, 0, 1 + (start - N_CTX) // LAT_T)


def _full(shape):
    nd = len(shape)
    return pl.BlockSpec(shape, lambda *_: (0,) * nd)


def _pair_specs(tile, width, lat_row0):
    nct = N_CTX // tile
    off = lat_row0 // tile
    return [pl.BlockSpec((tile, width), lambda i: (jnp.minimum(i, nct - 1), 0)),
            pl.BlockSpec((tile, width), lambda i: (jnp.maximum(i - nct, 0) + off, 0))]


def _pair_load(tile, ctx_ref, lat_ref):
    return jnp.where(pl.program_id(0) < N_CTX // tile, ctx_ref[...], lat_ref[...])


ADA_TN = 1536


def _ada_body(cond_ref, w_ref, b_ref, o_ref):
    c = cond_ref[...]
    a = c * jax.nn.sigmoid(c)
    a_hi = a.astype(BF16)
    a_lo = (a - a_hi.astype(F32)).astype(BF16)
    w = w_ref[0]
    w_hi = w.astype(BF16)
    w_lo = (w - w_hi.astype(F32)).astype(BF16)
    acc = jnp.dot(a_hi, w_hi, preferred_element_type=F32)
    acc += jnp.dot(a_lo, w_hi, preferred_element_type=F32)
    acc += jnp.dot(a_hi, w_lo, preferred_element_type=F32)
    o_ref[0] = acc + b_ref[0]


def _ada(cond, w_ada, b_ada):
    n = 6 * D_MODEL
    return pl.pallas_call(
        _ada_body,
        grid=(DEPTH, n // ADA_TN),
        in_specs=[
            _full((SUBLANES, D_MODEL)),
            pl.BlockSpec((1, D_MODEL, ADA_TN), lambda l, j: (l, 0, j)),
            pl.BlockSpec((1, 1, ADA_TN), lambda l, j: (l, 0, j)),
        ],
        out_specs=pl.BlockSpec((1, SUBLANES, ADA_TN), lambda l, j: (l, 0, j)),
        out_shape=jax.ShapeDtypeStruct((DEPTH, SUBLANES, n), F32),
        compiler_params=_params(2),
        name="ada_mod",
    )(cond, w_ada, b_ada.reshape(DEPTH, 1, n))


D_IN = Q_RANK + KV_RANK + ROPE + 4 * MIXW
IN_EXT = Q_RANK + KV_RANK + 4 * MIXW + LANES
UQ_EXT = HEADS * NOPE + HEADS * LANES


def _prep_win_body(w_ref, o_ref):
    w = w_ref[0]
    kr0 = Q_RANK + KV_RANK
    kr = w[:, kr0:kr0 + ROPE]
    half = ROPE // 2
    rot = jnp.concatenate([-kr[:, half:], kr[:, :half]], axis=-1)
    out = jnp.concatenate([w[:, :kr0], w[:, kr0 + ROPE:], kr, rot], axis=-1)
    o_ref[0] = out.astype(BF16)


def _rope_block(blk, cos_a, sin_a):
    return blk * cos_a + pltpu.roll(blk, ROPE, axis=1) * sin_a


def _prep_win(w_in):
    lyr = lambda l: (l, 0, 0)
    return pl.pallas_call(
        _prep_win_body,
        grid=(DEPTH,),
        in_specs=[pl.BlockSpec((1, D_MODEL, D_IN), lyr)],
        out_specs=pl.BlockSpec((1, D_MODEL, IN_EXT), lyr),
        out_shape=jax.ShapeDtypeStruct((DEPTH, D_MODEL, IN_EXT), BF16),
        compiler_params=_params(1),
        name="prep_w_in",
    )(w_in)


def _rms(x, g):
    return x * lax.rsqrt(jnp.mean(x * x, axis=-1, keepdims=True) + RMS_EPS) * g


MIX_COL0 = Q_RANK + KV_RANK


def _ffn_rows(ff_ref, n, tok0=0):
    return jnp.concatenate([ff_ref[pl.ds(tok0 * CHUNKS + c, n, stride=CHUNKS), :]
                            for c in range(CHUNKS)], axis=-1)


def _front_body(*refs, fused_ln):
    if fused_ln:
        (x1_ref, ff_ref, x1p_ref, x1n_ref, ffp_ref, ffn_ref, g2_ref, lg_ref, lb_ref,
         sh_ref, sc_ref, win_ref, qn_ref, kvn_ref, wuq_ref, wuk_ref, wuv_ref, cos_ref, sin_ref,
         pw_ref, ps_ref, cw_ref,
         ckv_ref, kr_ref, q_ref, k_ref, v_ref, bc_ref, x_ref) = refs

        def norm2(x1, ffn):
            return _layer_norm(ALPHA * x1 + g2_ref[0] * ffn, lg_ref[...], lb_ref[...])

    else:
        (xc_ref, xl_ref, xp_ref, xn_ref,
         sh_ref, sc_ref, win_ref, qn_ref, kvn_ref, wuq_ref, wuk_ref, wuv_ref, cos_ref, sin_ref,
         pw_ref, ps_ref, cw_ref,
         ckv_ref, kr_ref, q_ref, k_ref, v_ref, bc_ref) = refs
        is_ctx = pl.program_id(0) < N_CTX // TM

    n_sub = TM // SEQ_TILE
    subs = [slice(s * SEQ_TILE, (s + 1) * SEQ_TILE) for s in range(n_sub)]

    xs = []
    for s, rows in enumerate(subs):
        if fused_ln:
            x = norm2(x1_ref[rows, :], _ffn_rows(ff_ref, SEQ_TILE, s * SEQ_TILE))
            x_ref[rows, :] = x
        else:
            x = jnp.where(is_ctx, xc_ref[rows, :], xl_ref[rows, :])
        xs.append(x)
    if fused_ln:
        x_halo = [norm2(x1p_ref[...], _ffn_rows(ffp_ref, HALO)),
                  norm2(x1n_ref[...], _ffn_rows(ffn_ref, HALO))]
    else:
        x_halo = [xp_ref[...], xn_ref[...]]
    xs[-1] = jnp.concatenate([xs[-1]] + x_halo, axis=0)

    hs = []
    for x in xs:
        u = x * (1.0 + sc_ref[0]) + sh_ref[0]
        hs.append(jnp.dot(u.astype(BF16), win_ref[0], preferred_element_type=F32))
    mix_cols = slice(MIX_COL0, MIX_COL0 + 4 * MIXW)
    mix_halo = hs[-1][SEQ_TILE:, mix_cols]
    mixes = [h[:SEQ_TILE, mix_cols] for h in hs]

    for s, rows in enumerate(subs):
        prev = mix_halo[:HALO] if s == 0 else mixes[s - 1][SEQ_TILE - HALO:]
        nxt = mix_halo[HALO:] if s == n_sub - 1 else mixes[s + 1][:HALO]
        bc_ref[rows, :] = _seqmix_math(pl.program_id(0) * n_sub + s, mixes[s], prev, nxt,
                                       pw_ref[...], ps_ref[...], cw_ref[...])

    kr0 = MIX_COL0 + 4 * MIXW
    ckv_bs = []
    for s, rows in enumerate(subs):
        ckv = _rms(hs[s][:SEQ_TILE, Q_RANK:Q_RANK + KV_RANK], kvn_ref[...])
        ckv_ref[rows, :] = ckv
        ckv_bs.append(ckv.astype(BF16))
    for s, rows in enumerate(subs):
        k_nope = jnp.dot(ckv_bs[s], wuk_ref[...], preferred_element_type=F32)
        v_ref[rows, :] = jnp.dot(ckv_bs[s], wuv_ref[...], preferred_element_type=F32).astype(BF16)
        kr_blk = hs[s][:SEQ_TILE, kr0:kr0 + LANES]
        kr_ref[rows, :] = kr_blk
        k_rope = _rope_block(kr_blk, cos_ref[rows, :], sin_ref[rows, :])
        parts = []
        for hd in range(HEADS):
            parts += [k_nope[:, hd * NOPE:(hd + 1) * NOPE], k_rope]
        k_ref[rows, :] = jnp.concatenate(parts, axis=-1).astype(BF16)
    qns = [_rms(hs[s][:SEQ_TILE, :Q_RANK], qn_ref[...]).astype(BF16) for s in range(n_sub)]
    ra = HEADS * NOPE
    for s, rows in enumerate(subs):
        qq = jnp.dot(qns[s], wuq_ref[...], preferred_element_type=F32)
        parts = []
        for hd in range(HEADS):
            q_rope = _rope_block(qq[:, ra + hd * LANES:ra + (hd + 1) * LANES],
                                 cos_ref[rows, :], sin_ref[rows, :])
            parts += [qq[:, hd * NOPE:(hd + 1) * NOPE] * Q_SCALE, q_rope * Q_SCALE]
        q_ref[rows, :] = jnp.concatenate(parts, axis=-1).astype(BF16)


def _halo_specs(rows_per_token, width, lat_row0, n_rows):
    nct = N_CTX // TM
    nblk = n_rows // HALO

    def first_block(i):
        return (jnp.maximum(i - nct, 0) * TM + lat_row0) // HALO

    shape = (HALO * rows_per_token, width)
    return [pl.BlockSpec(shape, lambda i: (jnp.maximum(first_block(i) - 1, 0), 0)),
            pl.BlockSpec(shape, lambda i: (jnp.minimum(first_block(i) + TM // HALO, nblk - 1), 0))]


def _front(prev, mod3, layer, w_in_ext, q_norm, kv_norm, w_uq_ext, w_uk, w_uv, cos_a, sin_a,
           pool_bd, pool_scale, conv_w):
    row = lambda i: (i, 0)
    lyr = lambda i: (layer, 0, 0)
    modspec = lambda k: pl.BlockSpec((1, 1, D_MODEL), lambda i: (_mod_row(i, TM), 0, k))
    fused_ln = len(prev) == 5
    outs = [
        jax.ShapeDtypeStruct((N_TOK, KV_RANK), F32),
        jax.ShapeDtypeStruct((N_TOK, LANES), F32),
        jax.ShapeDtypeStruct((N_TOK, HEADS * HEAD_BLOCK), BF16),
        jax.ShapeDtypeStruct((N_TOK, HEADS * HEAD_BLOCK), BF16),
        jax.ShapeDtypeStruct((N_TOK, HEADS * VDIM), BF16),
        jax.ShapeDtypeStruct((N_TOK, 2 * MIXW), BF16),
    ]
    if fused_ln:
        x1, ff, mod3_prev, ln_g, ln_b = prev
        outs.append(jax.ShapeDtypeStruct((N_TOK, D_MODEL), F32))
        lead_specs = ([pl.BlockSpec((TM, D_MODEL), row), pl.BlockSpec((TM * CHUNKS, LANES), row)]
                      + _halo_specs(1, D_MODEL, N_CTX, N_TOK)
                      + _halo_specs(CHUNKS, LANES, N_CTX, N_TOK)
                      + [pl.BlockSpec((1, 1, D_MODEL), lambda i: (_mod_row(i, TM), 0, 5)),
                         _full((1, D_MODEL)), _full((1, D_MODEL))])
        lead_args = (x1, ff, x1, x1, ff, ff, mod3_prev, ln_g, ln_b)
    else:
        xc, xl = prev
        lead_specs = _pair_specs(TM, D_MODEL, 0) + _halo_specs(1, D_MODEL, 0, N_LAT)
        lead_args = (xc, xl, xl, xl)
    return pl.pallas_call(
        functools.partial(_front_body, fused_ln=fused_ln),
        grid=(N_TOK // TM,),
        in_specs=lead_specs + [
            modspec(0), modspec(1),
            pl.BlockSpec((1, D_MODEL, IN_EXT), lyr),
            _full((1, Q_RANK)), _full((1, KV_RANK)),
            _full((Q_RANK, UQ_EXT)),
            _full((KV_RANK, HEADS * NOPE)), _full((KV_RANK, HEADS * VDIM)),
            pl.BlockSpec((TM, LANES), row), pl.BlockSpec((TM, LANES), row),
            _full((MIXW, MIXW)), _full((1, MIXW)), _full((3, MIXW)),
        ],
        out_specs=tuple(pl.BlockSpec((TM, s.shape[1]), row) for s in outs),
        out_shape=tuple(outs),
        compiler_params=_params(1),
        name="front_ln" if fused_ln else "front",
    )(*lead_args, mod3, mod3, w_in_ext, q_norm, kv_norm, w_uq_ext, w_uk, w_uv, cos_a, sin_a,
      pool_bd, pool_scale, conv_w)


def _cachekv_body(ckv_ref, kr_ref, wuk_ref, wuv_ref, k_ref, v_ref):
    c = ckv_ref[0, 0].astype(BF16)
    k_nope = jnp.dot(c, wuk_ref[0], preferred_element_type=F32)
    v_ref[0, 0] = jnp.dot(c, wuv_ref[0], preferred_element_type=F32).astype(BF16)
    kr = kr_ref[0, 0]
    parts = []
    for hd in range(HEADS):
        parts += [k_nope[:, hd * NOPE:(hd + 1) * NOPE], kr]
    k_ref[0, 0] = jnp.concatenate(parts, axis=-1).astype(BF16)


def _cachekv(cache_ckv, kr_pad, w_uk, w_uv):
    bl = lambda b, l: (b, l, 0, 0)
    wl = lambda b, l: (l, 0, 0)
    return pl.pallas_call(
        _cachekv_body,
        grid=(LAT_B, DEPTH),
        in_specs=[
            pl.BlockSpec((1, 1, PAST_LEN, KV_RANK), bl),
            pl.BlockSpec((1, 1, PAST_LEN, LANES), bl),
            pl.BlockSpec((1, KV_RANK, HEADS * NOPE), wl),
            pl.BlockSpec((1, KV_RANK, HEADS * VDIM), wl),
        ],
        out_specs=(pl.BlockSpec((1, 1, PAST_LEN, HEADS * HEAD_BLOCK), bl),
                   pl.BlockSpec((1, 1, PAST_LEN, HEADS * VDIM), bl)),
        out_shape=(jax.ShapeDtypeStruct((LAT_B, DEPTH, PAST_LEN, HEADS * HEAD_BLOCK), BF16),
                   jax.ShapeDtypeStruct((LAT_B, DEPTH, PAST_LEN, HEADS * VDIM), BF16)),
        compiler_params=_params(2),
        name="cache_kv",
    )(cache_ckv, kr_pad, w_uk, w_uv)


EXT = SEQ_TILE + 2 * HALO


def _shift_up(x, k):
    return pltpu.roll(x, x.shape[0] - k, axis=0)


def _shift_down(x, k):
    return pltpu.roll(x, k, axis=0)


def _seqmix_math(j, main, prev, nxt, pw, ps, cw):
    n_ctx_tiles = N_CTX // SEQ_TILE
    tiles_per_lat = LAT_T // SEQ_TILE
    is_ctx = j < n_ctx_tiles
    jj = jnp.where(is_ctx, 0, (j - n_ctx_tiles) % tiles_per_lat)
    first = jj == 0
    last = jnp.where(is_ctx, True, jj == tiles_per_lat - 1)
    t_seq = jnp.where(is_ctx, CTX_T, LAT_T)

    prev = jnp.where(first, 0.0, prev)
    nxt = jnp.where(last, 0.0, nxt)
    ext = jnp.concatenate([prev, main, nxt], axis=0)
    p = ext[:, 0:MIXW]
    g_b = main[:, MIXW:2 * MIXW]
    g_c = ext[:, 2 * MIXW:3 * MIXW]
    h_in = ext[:, 3 * MIXW:4 * MIXW]

    sums = {}
    b = p
    for w in POOL_WINDOWS:
        b = b + _shift_up(b, w // 2)
        sums[w] = _shift_down(b, w // 2)[HALO:HALO + SEQ_TILE]
    tpos = jj * SEQ_TILE + lax.broadcasted_iota(I32, (SEQ_TILE, 1), 0)
    lane = lax.broadcasted_iota(I32, (1, MIXW), 1)
    num = None
    den = None
    for g, w in enumerate(POOL_WINDOWS):
        lo = jnp.maximum(tpos - w // 2, 0)
        hi = jnp.minimum(tpos + w - w // 2, t_seq)
        cnt = (hi - lo).astype(F32)
        if num is None:
            num, den = sums[w], jnp.broadcast_to(cnt, (SEQ_TILE, MIXW))
        else:
            sel = lane >= g * POOL_GROUP
            num = jnp.where(sel, sums[w], num)
            den = jnp.where(sel, cnt, den)
    pooled = num / den - p[HALO:HALO + SEQ_TILE]
    b_out = jnp.dot(pooled.astype(BF16), pw, preferred_element_type=F32) * ps

    y = g_c * h_in
    conv = (_shift_down(y, 1) * cw[0:1] + y * cw[1:2] + _shift_up(y, 1) * cw[2:3])
    c_out = g_b * conv[HALO:HALO + SEQ_TILE]
    return jnp.concatenate([b_out, c_out], axis=-1).astype(BF16)


_NT = (((1,), (1,)), ((), ()))


def _v_ext(v):
    return jnp.concatenate([v, jnp.ones(v.shape, v.dtype)], axis=-1)


def _attn_ctx_body(q_ref, k_ref, v_ref, o_ref):
    outs = []
    for hd in range(HEADS):
        qh = q_ref[:, hd * HEAD_BLOCK:(hd + 1) * HEAD_BLOCK]
        kh = k_ref[:, hd * HEAD_BLOCK:(hd + 1) * HEAD_BLOCK]
        vh = v_ref[:, hd * VDIM:(hd + 1) * VDIM]
        s = lax.dot_general(qh, kh, _NT, preferred_element_type=F32)
        m = jnp.max(s, axis=-1, keepdims=True)
        p = jnp.exp2(s - m)
        acc = jnp.dot(p.astype(BF16), _v_ext(vh), preferred_element_type=F32)
        outs.append(acc[:, :VDIM] / acc[:, VDIM:])
    o_ref[...] = jnp.concatenate(outs, axis=-1).astype(BF16)


def _attn_ctx(q, k, v):
    row = lambda b: (b, 0)
    return pl.pallas_call(
        _attn_ctx_body,
        grid=(CTX_B,),
        in_specs=[pl.BlockSpec((CTX_T, HEADS * HEAD_BLOCK), row),
                  pl.BlockSpec((CTX_T, HEADS * HEAD_BLOCK), row),
                  pl.BlockSpec((CTX_T, HEADS * VDIM), row)],
        out_specs=pl.BlockSpec((CTX_T, HEADS * VDIM), row),
        out_shape=jax.ShapeDtypeStruct((N_CTX, HEADS * VDIM), BF16),
        compiler_params=_params(1),
        name="attn_ctx",
    )(q, k, v)


def _attn_lat_body(q_ref, kc_ref, vc_ref, ko_ref, vo_ref, o_ref, m_ref, acc_ref):
    reps = KV_CHUNK // LANES

    def update(hd, k, v, first):
        qh = q_ref[:, hd * HEAD_BLOCK:(hd + 1) * HEAD_BLOCK]
        s = lax.dot_general(qh, k, _NT, preferred_element_type=F32)
        mx = jnp.max(s, axis=-1, keepdims=True)
        if first:
            m_new = jnp.broadcast_to(mx, (TQ, LANES))
        else:
            m_old = m_ref[hd]
            m_new = jnp.maximum(m_old, mx)
        p = jnp.exp2(s - jnp.concatenate([m_new] * reps, axis=-1))
        pv = jnp.dot(p.astype(BF16), _v_ext(v), preferred_element_type=F32)
        if first:
            acc_ref[hd] = pv
        else:
            a = jnp.exp2(m_old - m_new)
            acc_ref[hd] = acc_ref[hd] * jnp.concatenate([a, a], axis=-1) + pv
        m_ref[hd] = m_new

    def head_slices(hd):
        return slice(hd * HEAD_BLOCK, (hd + 1) * HEAD_BLOCK), slice(hd * VDIM, (hd + 1) * VDIM)

    for hd in range(HEADS):
        ks, vs = head_slices(hd)
        update(hd, kc_ref[0, 0, :, ks], vc_ref[0, 0, :, vs], True)

    def body(c, _):
        for j in range(CHUNK_UNROLL):
            rows = pl.ds(pl.multiple_of((c * CHUNK_UNROLL + j) * KV_CHUNK, KV_CHUNK), KV_CHUNK)
            for hd in range(HEADS):
                ks, vs = head_slices(hd)
                update(hd, ko_ref[rows, ks], vo_ref[rows, vs], False)
        return 0

    lax.fori_loop(0, LAT_T // (KV_CHUNK * CHUNK_UNROLL), body, 0)
    outs = [acc_ref[hd][:, :VDIM] / acc_ref[hd][:, VDIM:] for hd in range(HEADS)]
    o_ref[...] = jnp.concatenate(outs, axis=-1).astype(BF16)


def _attn_lat(q, k, v, kc, vc, layer):
    qt = LAT_T // TQ
    ctx_tiles = N_CTX // TQ
    ctx_blocks = N_CTX // LAT_T
    return pl.pallas_call(
        _attn_lat_body,
        grid=(LAT_B, qt),
        in_specs=[
            pl.BlockSpec((TQ, HEADS * HEAD_BLOCK), lambda b, i: (ctx_tiles + b * qt + i, 0)),
            pl.BlockSpec((1, 1, PAST_LEN, HEADS * HEAD_BLOCK), lambda b, i: (b, layer, 0, 0)),
            pl.BlockSpec((1, 1, PAST_LEN, HEADS * VDIM), lambda b, i: (b, layer, 0, 0)),
            pl.BlockSpec((LAT_T, HEADS * HEAD_BLOCK), lambda b, i: (ctx_blocks + b, 0)),
            pl.BlockSpec((LAT_T, HEADS * VDIM), lambda b, i: (ctx_blocks + b, 0)),
        ],
        out_specs=pl.BlockSpec((TQ, HEADS * VDIM), lambda b, i: (b * qt + i, 0)),
        out_shape=jax.ShapeDtypeStruct((N_LAT, HEADS * VDIM), BF16),
        scratch_shapes=[pltpu.VMEM((HEADS, TQ, LANES), F32),
                        pltpu.VMEM((HEADS, TQ, 2 * VDIM), F32)],
        compiler_params=_params(2),
        name="attn_lat",
    )(q, kc, vc, k, v)


def _layer_norm(y, g, b):
    mu = jnp.mean(y, axis=-1, keepdims=True)
    d = y - mu
    var = jnp.mean(d * d, axis=-1, keepdims=True)
    return d * lax.rsqrt(var + LN_EPS) * g + b


def _postmix_body(actx_ref, alat_ref, bc_ref, xc_ref, xl_ref, g1_ref, sh2_ref, sc2_ref, wa_ref, wbc_ref,
                  lg_ref, lb_ref, wr_ref, x1_ref, u2p_ref, aff_ref):
    is_ctx = pl.program_id(0) < N_CTX // TM
    lane = lax.broadcasted_iota(I32, (1, LANES), 1)
    blocks = [slice(b * POST_BLOCK, (b + 1) * POST_BLOCK) for b in range(TM // POST_BLOCK)]
    mixes = []
    for rows in blocks:
        a = jnp.where(is_ctx, actx_ref[rows, :], alat_ref[rows, :])
        mix = jnp.dot(a, wa_ref[0], preferred_element_type=F32)
        mixes.append(mix + jnp.dot(bc_ref[rows, :], wbc_ref[0], preferred_element_type=F32))
    u2s = []
    for b, rows in enumerate(blocks):
        x = jnp.where(is_ctx, xc_ref[rows, :], xl_ref[rows, :])
        x1 = _layer_norm(ALPHA * x + g1_ref[0] * mixes[b], lg_ref[...], lb_ref[...])
        x1_ref[rows, :] = x1
        u2 = x1 * (1.0 + sc2_ref[0]) + sh2_ref[0]
        for c in range(WORDS):
            lo = u2[:, c * LANES:(c + 1) * LANES]
            hi = u2[:, (c + WORDS) * LANES:(c + WORDS + 1) * LANES]
            u2p_ref[pl.ds(b * POST_BLOCK * WORDS + c, POST_BLOCK, stride=WORDS), :] = (
                pltpu.pack_elementwise([lo, hi], packed_dtype=BF16))
        u2s.append(u2.astype(BF16))
    for b, rows in enumerate(blocks):
        logits = jnp.dot(u2s[b], wr_ref[...], preferred_element_type=F32)
        logits = jnp.where(lane < N_EXPERTS, logits, -jnp.inf)
        m = jnp.max(logits, axis=-1, keepdims=True)
        e = jnp.exp(logits - m)
        aff_ref[rows, :] = e / jnp.sum(e, axis=-1, keepdims=True)


def _postmix(a_ctx, a_lat, bc, xc, xl, lat_row0, mod3, layer, w_out_b, ln_g, ln_b, w_router_pad):
    row = lambda i: (i, 0)
    modspec = lambda k: pl.BlockSpec((1, 1, D_MODEL), lambda i: (_mod_row(i, TM), 0, k))
    return pl.pallas_call(
        _postmix_body,
        grid=(N_TOK // TM,),
        in_specs=_pair_specs(TM, HEADS * VDIM, 0) + [
            pl.BlockSpec((TM, 2 * MIXW), row),
        ] + _pair_specs(TM, D_MODEL, lat_row0) + [
            modspec(2), modspec(3), modspec(4),
            pl.BlockSpec((1, HEADS * VDIM, D_MODEL), lambda i: (layer, 0, 0)),
            pl.BlockSpec((1, 2 * MIXW, D_MODEL), lambda i: (layer, 1, 0)),
            _full((1, D_MODEL)), _full((1, D_MODEL)),
            _full((D_MODEL, LANES)),
        ],
        out_specs=(pl.BlockSpec((TM, D_MODEL), row),
                   pl.BlockSpec((TM * WORDS, LANES), row),
                   pl.BlockSpec((TM, LANES), row)),
        out_shape=(jax.ShapeDtypeStruct((N_TOK, D_MODEL), F32),
                   jax.ShapeDtypeStruct((N_TOK * WORDS, LANES), jnp.uint32),
                   jax.ShapeDtypeStruct((N_TOK, LANES), F32)),
        compiler_params=_params(1),
        name="postmix",
    )(a_ctx, a_lat, bc, xc, xl, mod3, mod3, mod3, w_out_b, w_out_b, ln_g, ln_b, w_router_pad)


PREFIX_CHUNK = 256


def _route_ctx_body(aff_ref, idx_ref):
    n_rows = CTX_B * N_EXPERTS
    dense = jnp.concatenate(
        [aff_ref[b * CTX_T:(b + 1) * CTX_T, :].T[:N_EXPERTS] for b in range(CTX_B)], axis=0)

    def search(i, thr):
        cand = thr | jnp.left_shift(jnp.int32(1), 30 - i)
        n = jnp.sum((dense >= pltpu.bitcast(cand, F32)).astype(I32), axis=1, keepdims=True)
        return jnp.where(n >= CTX_CAP, cand, thr)

    thr_bits = lax.fori_loop(0, 31, search, jnp.zeros((n_rows, 1), I32))
    thr = pltpu.bitcast(thr_bits, F32)
    above = pltpu.bitcast(thr_bits + 1, F32)
    gt = (dense > thr) & (dense >= above)
    eq = (dense >= thr) & jnp.logical_not(gt)
    n_gt = jnp.sum(gt.astype(I32), axis=1, keepdims=True)
    need = (CTX_CAP - n_gt).astype(F32)

    ri = lax.broadcasted_iota(I32, (CTX_T, CTX_T), 0)
    ci = lax.broadcasted_iota(I32, (CTX_T, CTX_T), 1)
    tri = jnp.where(ri <= ci, 1.0, 0.0).astype(BF16)
    p_eq = jnp.dot(jnp.where(eq, 1.0, 0.0).astype(BF16), tri, preferred_element_type=F32)
    sel = jnp.where(gt, 1.0, jnp.where(eq & (p_eq <= need), 1.0, 0.0))
    cnt = jnp.dot(sel.astype(BF16), tri, preferred_element_type=F32)

    ones = jnp.ones((CTX_T, LANES), BF16)
    lane = lax.broadcasted_iota(I32, (1, LANES), 1)
    out = jnp.zeros((n_rows, LANES), F32)
    for r in range(CTX_CAP):
        le = jnp.where(cnt <= float(r), 1.0, 0.0).astype(BF16)
        out = jnp.where(lane == r, jnp.dot(le, ones, preferred_element_type=F32), out)
    idx_ref[...] = out.astype(I32)


def _route_ctx(aff):
    idx = pl.pallas_call(
        _route_ctx_body,
        grid=(1,),
        in_specs=[pl.BlockSpec((N_CTX, LANES), lambda i: (0, 0))],
        out_specs=_full((CTX_B * N_EXPERTS, LANES)),
        out_shape=jax.ShapeDtypeStruct((CTX_B * N_EXPERTS, LANES), I32),
        compiler_params=_params(1),
        name="route_ctx",
    )(aff)
    return idx[:, :CTX_CAP].reshape(CTX_B, N_EXPERTS, CTX_CAP)


RANK_TILE = LANES


def _expert_row(col):
    full = jnp.concatenate([jnp.broadcast_to(col, (N_EXPERTS, LANES)),
                            jnp.zeros((LANES - N_EXPERTS, LANES), col.dtype)], axis=0)
    return full.T[0:1]


def _route_count_body(aff_ref, cnt_ref, cend_ref, *, seq, cap):
    n_chunks = seq // PREFIX_CHUNK
    dense = aff_ref[...].T[:N_EXPERTS]

    def search(i, thr):
        cand = thr | jnp.left_shift(jnp.int32(1), 30 - i)
        n = jnp.sum((dense >= pltpu.bitcast(cand, F32)).astype(I32), axis=1, keepdims=True)
        return jnp.where(n >= cap, cand, thr)

    thr_bits = lax.fori_loop(0, 31, search, jnp.zeros((N_EXPERTS, 1), I32))
    thr_col = pltpu.bitcast(thr_bits, F32)
    above_col = pltpu.bitcast(thr_bits + 1, F32)
    n_gt = jnp.sum(((dense > thr_col) & (dense >= above_col)).astype(I32), axis=1, keepdims=True)
    thr = _expert_row(thr_col)
    above = _expert_row(above_col)
    need = _expert_row((cap - n_gt).astype(F32))

    ri = lax.broadcasted_iota(I32, (PREFIX_CHUNK, PREFIX_CHUNK), 0)
    ci = lax.broadcasted_iota(I32, (PREFIX_CHUNK, PREFIX_CHUNK), 1)
    tri = jnp.where(ci <= ri, 1.0, 0.0).astype(BF16)
    carry_eq = jnp.zeros((1, LANES), F32)
    carry_sel = jnp.zeros((1, LANES), F32)
    per = PREFIX_CHUNK // RANK_TILE
    for c in range(n_chunks):
        rows = slice(c * PREFIX_CHUNK, (c + 1) * PREFIX_CHUNK)
        a = aff_ref[rows, :]
        gt = (a > thr) & (a >= above)
        eq = (a >= thr) & jnp.logical_not(gt)
        p_eq = jnp.dot(tri, jnp.where(eq, 1.0, 0.0).astype(BF16),
                       preferred_element_type=F32) + carry_eq
        sel = jnp.where(gt, 1.0, jnp.where(eq & (p_eq <= need), 1.0, 0.0))
        p_sel = jnp.dot(tri, sel.astype(BF16), preferred_element_type=F32) + carry_sel
        dense_cnt = p_sel.T[:N_EXPERTS]
        for k in range(per):
            last = (k + 1) * RANK_TILE - 1
            cnt_ref[c * per + k] = dense_cnt[:, k * RANK_TILE:(k + 1) * RANK_TILE]
            cend_ref[0, c * per + k:c * per + k + 1, :] = p_sel[last:last + 1, :].astype(I32)
        carry_eq = p_eq[PREFIX_CHUNK - 1:PREFIX_CHUNK, :]
        carry_sel = p_sel[PREFIX_CHUNK - 1:PREFIX_CHUNK, :]


def _route_count(aff, first_block, n_req, seq, cap):
    n_tiles = seq // RANK_TILE
    return pl.pallas_call(
        functools.partial(_route_count_body, seq=seq, cap=cap),
        grid=(n_req,),
        in_specs=[pl.BlockSpec((seq, LANES), lambda b: (first_block + b, 0))],
        out_specs=(pl.BlockSpec((n_tiles, N_EXPERTS, RANK_TILE), lambda b: (b, 0, 0)),
                   pl.BlockSpec((1, n_tiles, LANES), lambda b: (b, 0, 0))),
        out_shape=(jax.ShapeDtypeStruct((n_req * n_tiles, N_EXPERTS, RANK_TILE), F32),
                   jax.ShapeDtypeStruct((n_req, n_tiles, LANES), I32)),
        compiler_params=_params(1),
        name=f"route_count_{seq}",
    )(aff)


def _route_rank_body(cend_ref, cnt_ref, idx_ref, *bufs, seq, cap):
    b = pl.program_id(0)
    n_tiles = seq // RANK_TILE
    rank = lax.broadcasted_iota(I32, (RANK_TILE, RANK_TILE), 0).astype(F32)
    ones = jnp.ones((RANK_TILE, LANES), BF16)

    def tile(j, _):
        counts = cnt_ref[j]
        for e in range(N_EXPERTS):
            prev = (b * N_EXPERTS + e) * n_tiles + j - 1
            start = jnp.where(j > 0, cend_ref[jnp.maximum(prev, 0)], 0)
            local = counts[e:e + 1, :] - jnp.asarray(start, F32)
            le = jnp.where(local <= rank, 1.0, 0.0).astype(BF16)
            pos = jnp.dot(le, ones, preferred_element_type=F32) + jnp.asarray(j * RANK_TILE, F32)
            bufs[e][pl.ds(start, RANK_TILE), :] = pos
        return 0

    lax.fori_loop(0, n_tiles, tile, 0)
    lane = lax.broadcasted_iota(I32, (1, LANES), 1)
    out = jnp.zeros((cap, LANES), F32)
    for e in range(N_EXPERTS):
        out = jnp.where(lane == e, bufs[e][0:cap, :], out)
    idx_ref[0] = out.astype(I32)


def _route_rank(cend_flat, cnt, n_req, seq, cap):
    n_tiles = seq // RANK_TILE
    grid_spec = pltpu.PrefetchScalarGridSpec(
        num_scalar_prefetch=1,
        grid=(n_req,),
        in_specs=[pl.BlockSpec((n_tiles, N_EXPERTS, RANK_TILE), lambda b, c: (b, 0, 0))],
        out_specs=pl.BlockSpec((1, cap, LANES), lambda b, c: (b, 0, 0)),
        scratch_shapes=[pltpu.VMEM((cap + RANK_TILE, LANES), F32)] * N_EXPERTS,
    )
    return pl.pallas_call(
        functools.partial(_route_rank_body, seq=seq, cap=cap),
        grid_spec=grid_spec,
        out_shape=jax.ShapeDtypeStruct((n_req, cap, LANES), I32),
        compiler_params=_params(1),
        name=f"route_rank_{seq}",
    )(cend_flat, cnt)


def _gather_rows(idx_ref, base, u2p_ref, aff_ref, tile_ref, gate_ref, r0, n):
    for i in range(n):
        r = r0 + i
        t = idx_ref[base + r]
        slab = u2p_ref[pl.ds(pl.multiple_of(t * WORDS, WORDS), WORDS), :]
        tile_ref[pl.ds(r, WORDS, stride=TILE_PITCH), :] = slab
        gate_ref[pl.ds(r, 1), :] = aff_ref[pl.ds(t, 1), :]


def _scatter_rows(idx_ref, base, y_ref, acc_ref, r0, n):
    for g0 in range(0, n, SCATTER_UNROLL):
        rows = [r0 + g0 + i for i in range(SCATTER_UNROLL)]
        dst = [pl.ds(pl.multiple_of(idx_ref[base + r] * CHUNKS, CHUNKS), CHUNKS) for r in rows]
        vals = [acc_ref[d, :] + y_ref[pl.ds(pl.multiple_of(r * CHUNKS, CHUNKS), CHUNKS), :]
                for d, r in zip(dst, rows)]
        for d, v in zip(dst, vals):
            acc_ref[d, :] = v


def _expert_rows(e, tile_ref, gate_ref, wg, wu, wd, y_ref, r0, n):
    halves = [[], []]
    for c in range(WORDS):
        w = tile_ref[pl.ds(c * TILE_PITCH + r0, n), :]
        for k in range(2):
            halves[k].append(pltpu.unpack_elementwise(w, index=k, packed_dtype=BF16,
                                                      unpacked_dtype=F32))
    x = jnp.concatenate(halves[0] + halves[1], axis=-1).astype(BF16)
    lane = lax.broadcasted_iota(I32, (1, LANES), 1)
    gate = jnp.sum(jnp.where(lane == e, gate_ref[pl.ds(r0, n), :], 0.0), axis=-1, keepdims=True)
    hg = jnp.dot(x, wg, preferred_element_type=F32)
    hu = jnp.dot(x, wu, preferred_element_type=F32)
    hidden = (hg * jax.nn.sigmoid(hg) * hu).astype(BF16)
    y = jnp.dot(hidden, wd, preferred_element_type=F32) * gate
    for c in range(CHUNKS):
        y_ref[pl.ds(r0 * CHUNKS + c, n, stride=CHUNKS), :] = y[:, c * LANES:(c + 1) * LANES]


def _moe_body(idx_ref, u2p_ref, aff_ref, wg_ref, wu_ref, wd_ref, *rest, first_pass, n_pass,
              emit_bf16):
    if emit_bf16:
        acc_ref, wgb_ref, wub_ref, wdb_ref, tile_a, tile_b, gate_a, gate_b, y_a, y_b = rest
    else:
        _, acc_ref, tile_a, tile_b, gate_a, gate_b, y_a, y_b = rest
    e = pl.program_id(1)
    step = (first_pass + pl.program_id(0)) * N_EXPERTS + e
    base = step * ROWS
    base_next = jnp.minimum(step + 1, (first_pass + n_pass) * N_EXPERTS - 1) * ROWS
    base_prev = jnp.where(e == 0, base, base - ROWS)

    @pl.when(e == 0)
    def _():
        acc_ref[...] = jnp.zeros_like(acc_ref)
        y_b[...] = jnp.zeros_like(y_b)

        def gather(c, _):
            _gather_rows(idx_ref, base, u2p_ref, aff_ref, tile_a, gate_a, c * SUBLANES, SUBLANES)
            return 0

        lax.fori_loop(0, ROWS // SUBLANES, gather, 0)

    def run(tile_cur, gate_cur, y_cur, tile_nxt, gate_nxt, y_prv):
        if emit_bf16:
            wg = wg_ref[0, 0].astype(BF16)
            wu = wu_ref[0, 0].astype(BF16)
            wd = wd_ref[0, 0].astype(BF16)
            wgb_ref[0] = wg
            wub_ref[0] = wu
            wdb_ref[0] = wd
        else:
            wg, wu, wd = wg_ref[0], wu_ref[0], wd_ref[0]
        for blk in range(ROWS // ROW_BLOCK):
            r0 = blk * ROW_BLOCK
            _gather_rows(idx_ref, base_next, u2p_ref, aff_ref, tile_nxt, gate_nxt, r0, ROW_BLOCK)
            _expert_rows(e, tile_cur, gate_cur, wg, wu, wd, y_cur, r0, ROW_BLOCK)
            _scatter_rows(idx_ref, base_prev, y_prv, acc_ref, r0, ROW_BLOCK)

    @pl.when(e % 2 == 0)
    def _():
        run(tile_a, gate_a, y_a, tile_b, gate_b, y_b)

    @pl.when(e % 2 == 1)
    def _():
        run(tile_b, gate_b, y_b, tile_a, gate_a, y_a)

    @pl.when(e == N_EXPERTS - 1)
    def _():
        def scatter(c, _):
            _scatter_rows(idx_ref, base, y_b, acc_ref, c * SCATTER_UNROLL, SCATTER_UNROLL)
            return 0

        lax.fori_loop(0, ROWS // SCATTER_UNROLL, scatter, 0)


_MOE_SCRATCH = [
    pltpu.VMEM((WORDS * TILE_PITCH, LANES), jnp.uint32),
    pltpu.VMEM((WORDS * TILE_PITCH, LANES), jnp.uint32),
    pltpu.VMEM((ROWS, LANES), F32),
    pltpu.VMEM((ROWS, LANES), F32),
    pltpu.VMEM((ROWS * CHUNKS, LANES), F32),
    pltpu.VMEM((ROWS * CHUNKS, LANES), F32),
]


def _moe(idx_flat, u2p, aff, w_gate, w_up, w_down, layer):
    one = pl.Buffered(1)
    ff_shape = jax.ShapeDtypeStruct((N_TOK * CHUNKS, LANES), F32)
    wide = jax.ShapeDtypeStruct((N_EXPERTS, D_MODEL, EXPERT_DIM), BF16)
    tall = jax.ShapeDtypeStruct((N_EXPERTS, EXPERT_DIM, D_MODEL), BF16)
    per_e = lambda p, e, idx: (e, 0, 0)

    def tok_specs(first_pass, u2p_mode):
        return [pl.BlockSpec((PASS_TOK * WORDS, LANES), lambda p, e, idx: (first_pass + p, 0),
                             pipeline_mode=u2p_mode),
                pl.BlockSpec((PASS_TOK, LANES), lambda p, e, idx: (first_pass + p, 0),
                             pipeline_mode=one)]

    def ff_spec(first_pass):
        return pl.BlockSpec((PASS_TOK * CHUNKS, LANES), lambda p, e, idx: (first_pass + p, 0),
                            pipeline_mode=one)

    ff, wg_b, wu_b, wd_b = pl.pallas_call(
        functools.partial(_moe_body, first_pass=0, n_pass=1, emit_bf16=True),
        grid_spec=pltpu.PrefetchScalarGridSpec(
            num_scalar_prefetch=1,
            grid=(1, N_EXPERTS),
            in_specs=tok_specs(0, one) + [
                pl.BlockSpec((1, 1, D_MODEL, EXPERT_DIM), lambda p, e, idx: (layer, e, 0, 0)),
                pl.BlockSpec((1, 1, D_MODEL, EXPERT_DIM), lambda p, e, idx: (layer, e, 0, 0)),
                pl.BlockSpec((1, 1, EXPERT_DIM, D_MODEL), lambda p, e, idx: (layer, e, 0, 0)),
            ],
            out_specs=(ff_spec(0),
                       pl.BlockSpec((1, D_MODEL, EXPERT_DIM), per_e),
                       pl.BlockSpec((1, D_MODEL, EXPERT_DIM), per_e),
                       pl.BlockSpec((1, EXPERT_DIM, D_MODEL), per_e)),
            scratch_shapes=_MOE_SCRATCH,
        ),
        out_shape=(ff_shape, wide, wide, tall),
        compiler_params=_params(2, arbitrary=True),
        name="moe_first",
    )(idx_flat, u2p, aff, w_gate, w_up, w_down)

    return pl.pallas_call(
        functools.partial(_moe_body, first_pass=1, n_pass=N_PASS - 1, emit_bf16=False),
        grid_spec=pltpu.PrefetchScalarGridSpec(
            num_scalar_prefetch=1,
            grid=(N_PASS - 1, N_EXPERTS),
            in_specs=tok_specs(1, None) + [
                pl.BlockSpec((1, D_MODEL, EXPERT_DIM), per_e),
                pl.BlockSpec((1, D_MODEL, EXPERT_DIM), per_e),
                pl.BlockSpec((1, EXPERT_DIM, D_MODEL), per_e),
                pl.BlockSpec(memory_space=pl.ANY),
            ],
            out_specs=ff_spec(1),
            scratch_shapes=_MOE_SCRATCH,
        ),
        out_shape=ff_shape,
        input_output_aliases={6: 0},
        compiler_params=_params(2, arbitrary=True),
        name="moe_rest",
    )(idx_flat, u2p, aff, wg_b, wu_b, wd_b, ff)


def _final_body(x1_ref, ff_ref, g2_ref, lg_ref, lb_ref, o_ref):
    ffn = jnp.concatenate([ff_ref[pl.ds(c, TM, stride=CHUNKS), :] for c in range(CHUNKS)], axis=-1)
    o_ref[...] = _layer_norm(ALPHA * x1_ref[...] + g2_ref[0] * ffn, lg_ref[...], lb_ref[...])


def _final(x1, ff, mod3, ln_g, ln_b, row0=0, n_rows=N_TOK):
    t0 = row0 // TM
    row = lambda i: (i + t0, 0)
    return pl.pallas_call(
        _final_body,
        grid=(n_rows // TM,),
        in_specs=[
            pl.BlockSpec((TM, D_MODEL), row),
            pl.BlockSpec((TM * CHUNKS, LANES), row),
            pl.BlockSpec((1, 1, D_MODEL), lambda i: (_mod_row(i + t0, TM), 0, 5)),
            _full((1, D_MODEL)), _full((1, D_MODEL)),
        ],
        out_specs=pl.BlockSpec((TM, D_MODEL), lambda i: (i, 0)),
        out_shape=jax.ShapeDtypeStruct((n_rows, D_MODEL), F32),
        compiler_params=_params(1),
        name="final_ln",
    )(x1, ff, mod3, ln_g, ln_b)


def _rot_cols(w):
    half = ROPE // 2
    return jnp.concatenate([-w[..., half:], w[..., :half]], axis=-1)


def _pad_lanes(w):
    pad = [(0, 0)] * (w.ndim - 1) + [(0, LANES - w.shape[-1])]
    return jnp.pad(w, pad)


def _rope_tables():
    rows_n = LAT_T // GRID_W
    r, cl = jnp.meshgrid(jnp.arange(rows_n, dtype=F32), jnp.arange(GRID_W, dtype=F32), indexing="ij")
    inv = ROPE_THETA ** (-jnp.arange(0, ROPE // 2, 2, dtype=F32) / (ROPE // 2))
    ang = jnp.concatenate([r.reshape(-1)[:, None] * inv, cl.reshape(-1)[:, None] * inv], axis=-1)
    cos, sin = jnp.cos(ang), jnp.sin(ang)
    cos_lat = _pad_lanes(jnp.concatenate([cos, cos], axis=-1))
    sin_lat = _pad_lanes(jnp.concatenate([sin, sin], axis=-1))
    cos_ctx = _pad_lanes(jnp.ones((N_CTX, ROPE), F32))
    sin_ctx = jnp.zeros((N_CTX, LANES), F32)
    cos_a = jnp.concatenate([cos_ctx] + [cos_lat] * LAT_B, axis=0)
    sin_a = jnp.concatenate([sin_ctx] + [sin_lat] * LAT_B, axis=0)
    return cos_a, sin_a


def kernel(x_prompt, x_sample, cache_ckv, cache_krope, c, c_ctx, w_in, q_norm, w_uq, kv_norm, w_uk, w_uv,
           pool_w, pool_scale, conv_w, w_out, w_ada, b_ada, ln1_g, ln1_b, ln2_g, ln2_b, w_router,
           w_gate, w_up, w_down):
    w_in_ext = _prep_win(w_in)
    hw = NOPE + ROPE
    uq_nope = [w_uq[:, :, h * hw:h * hw + NOPE] for h in range(HEADS)]
    uq_rope = [w_uq[:, :, h * hw + NOPE:(h + 1) * hw] for h in range(HEADS)]
    w_uq_ext = jnp.concatenate(
        uq_nope + [blk for w in uq_rope for blk in (w, _rot_cols(w))], axis=-1).astype(BF16)
    w_uk_b = w_uk.astype(BF16)
    w_uv_b = w_uv.astype(BF16)
    eye = jnp.eye(len(POOL_WINDOWS), dtype=F32)
    pool_bd = (pool_w[:, :, :, None, :] * eye[None, :, None, :, None]).reshape(DEPTH, MIXW, MIXW).astype(BF16)
    w_out_b = w_out.astype(BF16)
    w_router_pad = _pad_lanes(w_router).astype(BF16)
    cos_a, sin_a = _rope_tables()

    cond = jnp.concatenate([c_ctx[None, :], c, jnp.zeros((SUBLANES - 1 - LAT_B, D_MODEL), F32)], axis=0)
    mod = _ada(cond, w_ada, b_ada)
    kc, vc = _cachekv(cache_ckv, _pad_lanes(cache_krope), w_uk_b, w_uv_b)

    prev = (x_prompt.reshape(N_CTX, D_MODEL), x_sample.reshape(N_LAT, D_MODEL))
    xs = prev + (0,)
    ckv_layers, kr_layers = [], []
    ctx_off = (jnp.arange(CTX_B, dtype=I32) * CTX_T)[:, None, None]
    for l in range(DEPTH):
        mod3 = mod[l].reshape(SUBLANES, 1, 6 * D_MODEL)
        outs = _front(prev, mod3, l, w_in_ext, q_norm[l][None], kv_norm[l][None],
                      w_uq_ext[l], w_uk_b[l], w_uv_b[l], cos_a, sin_a,
                      pool_bd[l], pool_scale[l][None], conv_w[l])
        ckv, kr, q, k, v, bc = outs[:6]
        if l > 0:
            xs = (outs[6], outs[6], N_CTX)
        ckv_layers.append(ckv[:N_CTX].reshape(CTX_B, CTX_T, KV_RANK))
        kr_layers.append(kr[:N_CTX, :ROPE].reshape(CTX_B, CTX_T, ROPE))
        a_ctx = _attn_ctx(q, k, v)
        a_lat = _attn_lat(q, k, v, kc, vc, l)
        x1, u2p, aff = _postmix(a_ctx, a_lat, bc, *xs, mod3, l, w_out_b,
                                ln1_g[l][None], ln1_b[l][None], w_router_pad[l])
        idx_ctx = _route_ctx(aff)
        cnt_lat, cend = _route_count(aff, N_CTX // LAT_T, LAT_B, LAT_T, LAT_CAP)
        cend_flat = cend[:, :, :N_EXPERTS].transpose(0, 2, 1).reshape(-1)
        idx_lat = _route_rank(cend_flat, cnt_lat, LAT_B, LAT_T, LAT_CAP)
        idx_lat = idx_lat[:, :, :N_EXPERTS].transpose(0, 2, 1)
        idx_ctx = (idx_ctx + ctx_off).transpose(1, 0, 2).reshape(1, N_EXPERTS, ROWS)
        idx_flat = jnp.concatenate([idx_ctx, idx_lat], axis=0).reshape(-1)
        ff = _moe(idx_flat, u2p, aff, w_gate, w_up, w_down, l)
        if l + 1 < DEPTH:
            prev = (x1, ff, mod3, ln2_g[l][None], ln2_b[l][None])
        else:
            y_ctx = _final(x1, ff, mod3, ln2_g[l][None], ln2_b[l][None], 0, N_CTX)
            y_lat = _final(x1, ff, mod3, ln2_g[l][None], ln2_b[l][None], N_CTX, N_LAT)

    y_prompt = y_ctx.reshape(CTX_B, CTX_T, D_MODEL)
    y_sample = y_lat.reshape(LAT_B, LAT_T, D_MODEL)
    new_ckv = jnp.stack(ckv_layers, axis=1)
    new_krope = jnp.stack(kr_layers, axis=1)
    return (y_prompt, y_sample, new_ckv, new_krope)
```

```python
import functools

import jax
import jax.numpy as jnp
from jax import lax
from jax.experimental import pallas as pl
from jax.experimental.pallas import tpu as pltpu

F32 = jnp.float32
BF16 = jnp.bfloat16
I32 = jnp.int32

D_MODEL = 1024
CTX_B, CTX_T = 16, 256
LAT_B, LAT_T = 2, 4096
DEPTH = 4
PAST_LEN = 512
GRID_W = 64
HEADS = 4
NOPE, ROPE, VDIM = 128, 64, 128
Q_RANK, KV_RANK = 384, 256
POOL_WINDOWS = (2, 4, 8, 16)
POOL_GROUP = 64
MIXW = 256
N_EXPERTS = 16
EXPERT_DIM = 512
ROPE_THETA = 10000.0
ATTN_SCALE = (NOPE + ROPE) ** -0.5
LOG2E = 1.4426950408889634
Q_SCALE = ATTN_SCALE * LOG2E
ALPHA = (2 * DEPTH) ** 0.25
RMS_EPS = 1e-6
LN_EPS = 1e-5

N_CTX = CTX_B * CTX_T
N_LAT = LAT_B * LAT_T
N_TOK = N_CTX + N_LAT
CTX_CAP = 2 * CTX_T // N_EXPERTS
LAT_CAP = 2 * LAT_T // N_EXPERTS

LANES = 128
SUBLANES = 8
CHUNKS = D_MODEL // LANES
HEAD_BLOCK = 2 * LANES
VMEM_LIMIT = 56 * 1024 * 1024

TM = 512
SEQ_TILE = 256
HALO = 8
TQ = 512
KV_CHUNK = 512
CHUNK_UNROLL = 8
PASS_TOK = 4096
N_PASS = N_TOK // PASS_TOK
ROWS = 512
TILE_PITCH = ROWS + 8
SCATTER_UNROLL = 8
WORDS = CHUNKS // 2
ROW_BLOCK = 512
POST_BLOCK = 256


def _params(n_axes, arbitrary=False):
    sem = ("arbitrary" if arbitrary else "parallel",) * n_axes
    return pltpu.CompilerParams(dimension_semantics=sem, vmem_limit_bytes=VMEM_LIMIT)


def _mod_row(i, tile):
    start = i * tile
    return jnp.where(start < N_CTX, 0, 1 + (start - N_CTX) // LAT_T)


def _full(shape):
    nd = len(shape)
    return pl.BlockSpec(shape, lambda *_: (0,) * nd)


def _pair_specs(tile, width, lat_row0):
    nct = N_CTX // tile
    off = lat_row0 // tile
    return [pl.BlockSpec((tile, width), lambda i: (jnp.minimum(i, nct - 1), 0)),
            pl.BlockSpec((tile, width), lambda i: (jnp.maximum(i - nct, 0) + off, 0))]


def _pair_load(tile, ctx_ref, lat_ref):
    return jnp.where(pl.program_id(0) < N_CTX // tile, ctx_ref[...], lat_ref[...])


ADA_TN = 1536


def _ada_body(cond_ref, w_ref, b_ref, o_ref):
    c = cond_ref[...]
    a = c * jax.nn.sigmoid(c)
    a_hi = a.astype(BF16)
    a_lo = (a - a_hi.astype(F32)).astype(BF16)
    w = w_ref[0]
    w_hi = w.astype(BF16)
    w_lo = (w - w_hi.astype(F32)).astype(BF16)
    acc = jnp.dot(a_hi, w_hi, preferred_element_type=F32)
    acc += jnp.dot(a_lo, w_hi, preferred_element_type=F32)
    acc += jnp.dot(a_hi, w_lo, preferred_element_type=F32)
    o_ref[0] = acc + b_ref[0]


def _ada(cond, w_ada, b_ada):
    n = 6 * D_MODEL
    return pl.pallas_call(
        _ada_body,
        grid=(DEPTH, n // ADA_TN),
        in_specs=[
            _full((SUBLANES, D_MODEL)),
            pl.BlockSpec((1, D_MODEL, ADA_TN), lambda l, j: (l, 0, j)),
            pl.BlockSpec((1, 1, ADA_TN), lambda l, j: (l, 0, j)),
        ],
        out_specs=pl.BlockSpec((1, SUBLANES, ADA_TN), lambda l, j: (l, 0, j)),
        out_shape=jax.ShapeDtypeStruct((DEPTH, SUBLANES, n), F32),
        compiler_params=_params(2),
        name="ada_mod",
    )(cond, w_ada, b_ada.reshape(DEPTH, 1, n))


D_IN = Q_RANK + KV_RANK + ROPE + 4 * MIXW
IN_EXT = Q_RANK + KV_RANK + 4 * MIXW + LANES
UQ_EXT = HEADS * NOPE + HEADS * LANES


def _prep_win_body(w_ref, o_ref):
    w = w_ref[0]
    kr0 = Q_RANK + KV_RANK
    kr = w[:, kr0:kr0 + ROPE]
    half = ROPE // 2
    rot = jnp.concatenate([-kr[:, half:], kr[:, :half]], axis=-1)
    out = jnp.concatenate([w[:, :kr0], w[:, kr0 + ROPE:], kr, rot], axis=-1)
    o_ref[0] = out.astype(BF16)


def _rope_block(blk, cos_a, sin_a):
    return blk * cos_a + pltpu.roll(blk, ROPE, axis=1) * sin_a


def _prep_win(w_in):
    lyr = lambda l: (l, 0, 0)
    return pl.pallas_call(
        _prep_win_body,
        grid=(DEPTH,),
        in_specs=[pl.BlockSpec((1, D_MODEL, D_IN), lyr)],
        out_specs=pl.BlockSpec((1, D_MODEL, IN_EXT), lyr),
        out_shape=jax.ShapeDtypeStruct((DEPTH, D_MODEL, IN_EXT), BF16),
        compiler_params=_params(1),
        name="prep_w_in",
    )(w_in)


def _rms(x, g):
    return x * lax.rsqrt(jnp.mean(x * x, axis=-1, keepdims=True) + RMS_EPS) * g


MIX_COL0 = Q_RANK + KV_RANK


def _ffn_rows(ff_ref, n, tok0=0):
    return jnp.concatenate([ff_ref[pl.ds(tok0 * CHUNKS + c, n, stride=CHUNKS), :]
                            for c in range(CHUNKS)], axis=-1)


def _front_body(*refs, fused_ln):
    if fused_ln:
        (x1_ref, ff_ref, x1p_ref, x1n_ref, ffp_ref, ffn_ref, g2_ref, lg_ref, lb_ref,
         sh_ref, sc_ref, win_ref, qn_ref, kvn_ref, wuq_ref, wuk_ref, wuv_ref, cos_ref, sin_ref,
         pw_ref, ps_ref, cw_ref,
         ckv_ref, kr_ref, q_ref, k_ref, v_ref, bc_ref, x_ref) = refs

        def norm2(x1, ffn):
            return _layer_norm(ALPHA * x1 + g2_ref[0] * ffn, lg_ref[...], lb_ref[...])

    else:
        (xc_ref, xl_ref, xp_ref, xn_ref,
         sh_ref, sc_ref, win_ref, qn_ref, kvn_ref, wuq_ref, wuk_ref, wuv_ref, cos_ref, sin_ref,
         pw_ref, ps_ref, cw_ref,
         ckv_ref, kr_ref, q_ref, k_ref, v_ref, bc_ref) = refs
        is_ctx = pl.program_id(0) < N_CTX // TM

    n_sub = TM // SEQ_TILE
    subs = [slice(s * SEQ_TILE, (s + 1) * SEQ_TILE) for s in range(n_sub)]

    xs = []
    for s, rows in enumerate(subs):
        if fused_ln:
            x = norm2(x1_ref[rows, :], _ffn_rows(ff_ref, SEQ_TILE, s * SEQ_TILE))
            x_ref[rows, :] = x
        else:
            x = jnp.where(is_ctx, xc_ref[rows, :], xl_ref[rows, :])
        xs.append(x)
    if fused_ln:
        x_halo = [norm2(x1p_ref[...], _ffn_rows(ffp_ref, HALO)),
                  norm2(x1n_ref[...], _ffn_rows(ffn_ref, HALO))]
    else:
        x_halo = [xp_ref[...], xn_ref[...]]
    xs[-1] = jnp.concatenate([xs[-1]] + x_halo, axis=0)

    hs = []
    for x in xs:
        u = x * (1.0 + sc_ref[0]) + sh_ref[0]
        hs.append(jnp.dot(u.astype(BF16), win_ref[0], preferred_element_type=F32))
    mix_cols = slice(MIX_COL0, MIX_COL0 + 4 * MIXW)
    mix_halo = hs[-1][SEQ_TILE:, mix_cols]
    mixes = [h[:SEQ_TILE, mix_cols] for h in hs]

    for s, rows in enumerate(subs):
        prev = mix_halo[:HALO] if s == 0 else mixes[s - 1][SEQ_TILE - HALO:]
        nxt = mix_halo[HALO:] if s == n_sub - 1 else mixes[s + 1][:HALO]
        bc_ref[rows, :] = _seqmix_math(pl.program_id(0) * n_sub + s, mixes[s], prev, nxt,
                                       pw_ref[...], ps_ref[...], cw_ref[...])

    kr0 = MIX_COL0 + 4 * MIXW
    ckv_bs = []
    for s, rows in enumerate(subs):
        ckv = _rms(hs[s][:SEQ_TILE, Q_RANK:Q_RANK + KV_RANK], kvn_ref[...])
        ckv_ref[rows, :] = ckv
        ckv_bs.append(ckv.astype(BF16))
    for s, rows in enumerate(subs):
        k_nope = jnp.dot(ckv_bs[s], wuk_ref[...], preferred_element_type=F32)
        v_ref[rows, :] = jnp.dot(ckv_bs[s], wuv_ref[...], preferred_element_type=F32).astype(BF16)
        kr_blk = hs[s][:SEQ_TILE, kr0:kr0 + LANES]
        kr_ref[rows, :] = kr_blk
        k_rope = _rope_block(kr_blk, cos_ref[rows, :], sin_ref[rows, :])
        parts = []
        for hd in range(HEADS):
            parts += [k_nope[:, hd * NOPE:(hd + 1) * NOPE], k_rope]
        k_ref[rows, :] = jnp.concatenate(parts, axis=-1).astype(BF16)
    qns = [_rms(hs[s][:SEQ_TILE, :Q_RANK], qn_ref[...]).astype(BF16) for s in range(n_sub)]
    ra = HEADS * NOPE
    for s, rows in enumerate(subs):
        qq = jnp.dot(qns[s], wuq_ref[...], preferred_element_type=F32)
        parts = []
        for hd in range(HEADS):
            q_rope = _rope_block(qq[:, ra + hd * LANES:ra + (hd + 1) * LANES],
                                 cos_ref[rows, :], sin_ref[rows, :])
            parts += [qq[:, hd * NOPE:(hd + 1) * NOPE] * Q_SCALE, q_rope * Q_SCALE]
        q_ref[rows, :] = jnp.concatenate(parts, axis=-1).astype(BF16)


def _halo_specs(rows_per_token, width, lat_row0, n_rows):
    nct = N_CTX // TM
    nblk = n_rows // HALO

    def first_block(i):
        return (jnp.maximum(i - nct, 0) * TM + lat_row0) // HALO

    shape = (HALO * rows_per_token, width)
    return [pl.BlockSpec(shape, lambda i: (jnp.maximum(first_block(i) - 1, 0), 0)),
            pl.BlockSpec(shape, lambda i: (jnp.minimum(first_block(i) + TM // HALO, nblk - 1), 0))]


def _front(prev, mod3, layer, w_in_ext, q_norm, kv_norm, w_uq_ext, w_uk, w_uv, cos_a, sin_a,
           pool_bd, pool_scale, conv_w):
    row = lambda i: (i, 0)
    lyr = lambda i: (layer, 0, 0)
    modspec = lambda k: pl.BlockSpec((1, 1, D_MODEL), lambda i: (_mod_row(i, TM), 0, k))
    fused_ln = len(prev) == 5
    outs = [
        jax.ShapeDtypeStruct((N_TOK, KV_RANK), F32),
        jax.ShapeDtypeStruct((N_TOK, LANES), F32),
        jax.ShapeDtypeStruct((N_TOK, HEADS * HEAD_BLOCK), BF16),
        jax.ShapeDtypeStruct((N_TOK, HEADS * HEAD_BLOCK), BF16),
        jax.ShapeDtypeStruct((N_TOK, HEADS * VDIM), BF16),
        jax.ShapeDtypeStruct((N_TOK, 2 * MIXW), BF16),
    ]
    if fused_ln:
        x1, ff, mod3_prev, ln_g, ln_b = prev
        outs.append(jax.ShapeDtypeStruct((N_TOK, D_MODEL), F32))
        lead_specs = ([pl.BlockSpec((TM, D_MODEL), row), pl.BlockSpec((TM * CHUNKS, LANES), row)]
                      + _halo_specs(1, D_MODEL, N_CTX, N_TOK)
                      + _halo_specs(CHUNKS, LANES, N_CTX, N_TOK)
                      + [pl.BlockSpec((1, 1, D_MODEL), lambda i: (_mod_row(i, TM), 0, 5)),
                         _full((1, D_MODEL)), _full((1, D_MODEL))])
        lead_args = (x1, ff, x1, x1, ff, ff, mod3_prev, ln_g, ln_b)
    else:
        xc, xl = prev
        lead_specs = _pair_specs(TM, D_MODEL, 0) + _halo_specs(1, D_MODEL, 0, N_LAT)
        lead_args = (xc, xl, xl, xl)
    return pl.pallas_call(
        functools.partial(_front_body, fused_ln=fused_ln),
        grid=(N_TOK // TM,),
        in_specs=lead_specs + [
            modspec(0), modspec(1),
            pl.BlockSpec((1, D_MODEL, IN_EXT), lyr),
            _full((1, Q_RANK)), _full((1, KV_RANK)),
            _full((Q_RANK, UQ_EXT)),
            _full((KV_RANK, HEADS * NOPE)), _full((KV_RANK, HEADS * VDIM)),
            pl.BlockSpec((TM, LANES), row), pl.BlockSpec((TM, LANES), row),
            _full((MIXW, MIXW)), _full((1, MIXW)), _full((3, MIXW)),
        ],
        out_specs=tuple(pl.BlockSpec((TM, s.shape[1]), row) for s in outs),
        out_shape=tuple(outs),
        compiler_params=_params(1),
        name="front_ln" if fused_ln else "front",
    )(*lead_args, mod3, mod3, w_in_ext, q_norm, kv_norm, w_uq_ext, w_uk, w_uv, cos_a, sin_a,
      pool_bd, pool_scale, conv_w)


def _cachekv_body(ckv_ref, kr_ref, wuk_ref, wuv_ref, k_ref, v_ref):
    c = ckv_ref[0, 0].astype(BF16)
    k_nope = jnp.dot(c, wuk_ref[0], preferred_element_type=F32)
    v_ref[0, 0] = jnp.dot(c, wuv_ref[0], preferred_element_type=F32).astype(BF16)
    kr = kr_ref[0, 0]
    parts = []
    for hd in range(HEADS):
        parts += [k_nope[:, hd * NOPE:(hd + 1) * NOPE], kr]
    k_ref[0, 0] = jnp.concatenate(parts, axis=-1).astype(BF16)


def _cachekv(cache_ckv, kr_pad, w_uk, w_uv):
    bl = lambda b, l: (b, l, 0, 0)
    wl = lambda b, l: (l, 0, 0)
    return pl.pallas_call(
        _cachekv_body,
        grid=(LAT_B, DEPTH),
        in_specs=[
            pl.BlockSpec((1, 1, PAST_LEN, KV_RANK), bl),
            pl.BlockSpec((1, 1, PAST_LEN, LANES), bl),
            pl.BlockSpec((1, KV_RANK, HEADS * NOPE), wl),
            pl.BlockSpec((1, KV_RANK, HEADS * VDIM), wl),
        ],
        out_specs=(pl.BlockSpec((1, 1, PAST_LEN, HEADS * HEAD_BLOCK), bl),
                   pl.BlockSpec((1, 1, PAST_LEN, HEADS * VDIM), bl)),
        out_shape=(jax.ShapeDtypeStruct((LAT_B, DEPTH, PAST_LEN, HEADS * HEAD_BLOCK), BF16),
                   jax.ShapeDtypeStruct((LAT_B, DEPTH, PAST_LEN, HEADS * VDIM), BF16)),
        compiler_params=_params(2),
        name="cache_kv",
    )(cache_ckv, kr_pad, w_uk, w_uv)


EXT = SEQ_TILE + 2 * HALO


def _shift_up(x, k):
    return pltpu.roll(x, x.shape[0] - k, axis=0)


def _shift_down(x, k):
    return pltpu.roll(x, k, axis=0)


def _seqmix_math(j, main, prev, nxt, pw, ps, cw):
    n_ctx_tiles = N_CTX // SEQ_TILE
    tiles_per_lat = LAT_T // SEQ_TILE
    is_ctx = j < n_ctx_tiles
    jj = jnp.where(is_ctx, 0, (j - n_ctx_tiles) % tiles_per_lat)
    first = jj == 0
    last = jnp.where(is_ctx, True, jj == tiles_per_lat - 1)
    t_seq = jnp.where(is_ctx, CTX_T, LAT_T)

    prev = jnp.where(first, 0.0, prev)
    nxt = jnp.where(last, 0.0, nxt)
    ext = jnp.concatenate([prev, main, nxt], axis=0)
    p = ext[:, 0:MIXW]
    g_b = main[:, MIXW:2 * MIXW]
    g_c = ext[:, 2 * MIXW:3 * MIXW]
    h_in = ext[:, 3 * MIXW:4 * MIXW]

    sums = {}
    b = p
    for w in POOL_WINDOWS:
        b = b + _shift_up(b, w // 2)
        sums[w] = _shift_down(b, w // 2)[HALO:HALO + SEQ_TILE]
    tpos = jj * SEQ_TILE + lax.broadcasted_iota(I32, (SEQ_TILE, 1), 0)
    lane = lax.broadcasted_iota(I32, (1, MIXW), 1)
    num = None
    den = None
    for g, w in enumerate(POOL_WINDOWS):
        lo = jnp.maximum(tpos - w // 2, 0)
        hi = jnp.minimum(tpos + w - w // 2, t_seq)
        cnt = (hi - lo).astype(F32)
        if num is None:
            num, den = sums[w], jnp.broadcast_to(cnt, (SEQ_TILE, MIXW))
        else:
            sel = lane >= g * POOL_GROUP
            num = jnp.where(sel, sums[w], num)
            den = jnp.where(sel, cnt, den)
    pooled = num / den - p[HALO:HALO + SEQ_TILE]
    b_out = jnp.dot(pooled.astype(BF16), pw, preferred_element_type=F32) * ps

    y = g_c * h_in
    conv = (_shift_down(y, 1) * cw[0:1] + y * cw[1:2] + _shift_up(y, 1) * cw[2:3])
    c_out = g_b * conv[HALO:HALO + SEQ_TILE]
    return jnp.concatenate([b_out, c_out], axis=-1).astype(BF16)


_NT = (((1,), (1,)), ((), ()))


def _v_ext(v):
    return jnp.concatenate([v, jnp.ones(v.shape, v.dtype)], axis=-1)


def _attn_ctx_body(q_ref, k_ref, v_ref, o_ref):
    outs = []
    for hd in range(HEADS):
        qh = q_ref[:, hd * HEAD_BLOCK:(hd + 1) * HEAD_BLOCK]
        kh = k_ref[:, hd * HEAD_BLOCK:(hd + 1) * HEAD_BLOCK]
        vh = v_ref[:, hd * VDIM:(hd + 1) * VDIM]
        s = lax.dot_general(qh, kh, _NT, preferred_element_type=F32)
        m = jnp.max(s, axis=-1, keepdims=True)
        p = jnp.exp2(s - m)
        acc = jnp.dot(p.astype(BF16), _v_ext(vh), preferred_element_type=F32)
        outs.append(acc[:, :VDIM] / acc[:, VDIM:])
    o_ref[...] = jnp.concatenate(outs, axis=-1).astype(BF16)


def _attn_ctx(q, k, v):
    row = lambda b: (b, 0)
    return pl.pallas_call(
        _attn_ctx_body,
        grid=(CTX_B,),
        in_specs=[pl.BlockSpec((CTX_T, HEADS * HEAD_BLOCK), row),
                  pl.BlockSpec((CTX_T, HEADS * HEAD_BLOCK), row),
                  pl.BlockSpec((CTX_T, HEADS * VDIM), row)],
        out_specs=pl.BlockSpec((CTX_T, HEADS * VDIM), row),
        out_shape=jax.ShapeDtypeStruct((N_CTX, HEADS * VDIM), BF16),
        compiler_params=_params(1),
        name="attn_ctx",
    )(q, k, v)


def _attn_lat_body(q_ref, kc_ref, vc_ref, ko_ref, vo_ref, o_ref, m_ref, acc_ref):
    reps = KV_CHUNK // LANES

    def update(hd, k, v, first):
        qh = q_ref[:, hd * HEAD_BLOCK:(hd + 1) * HEAD_BLOCK]
        s = lax.dot_general(qh, k, _NT, preferred_element_type=F32)
        mx = jnp.max(s, axis=-1, keepdims=True)
        if first:
            m_new = jnp.broadcast_to(mx, (TQ, LANES))
        else:
            m_old = m_ref[hd]
            m_new = jnp.maximum(m_old, mx)
        p = jnp.exp2(s - jnp.concatenate([m_new] * reps, axis=-1))
        pv = jnp.dot(p.astype(BF16), _v_ext(v), preferred_element_type=F32)
        if first:
            acc_ref[hd] = pv
        else:
            a = jnp.exp2(m_old - m_new)
            acc_ref[hd] = acc_ref[hd] * jnp.concatenate([a, a], axis=-1) + pv
        m_ref[hd] = m_new

    def head_slices(hd):
        return slice(hd * HEAD_BLOCK, (hd + 1) * HEAD_BLOCK), slice(hd * VDIM, (hd + 1) * VDIM)

    for hd in range(HEADS):
        ks, vs = head_slices(hd)
        update(hd, kc_ref[0, 0, :, ks], vc_ref[0, 0, :, vs], True)

    def body(c, _):
        for j in range(CHUNK_UNROLL):
            rows = pl.ds(pl.multiple_of((c * CHUNK_UNROLL + j) * KV_CHUNK, KV_CHUNK), KV_CHUNK)
            for hd in range(HEADS):
                ks, vs = head_slices(hd)
                update(hd, ko_ref[rows, ks], vo_ref[rows, vs], False)
        return 0

    lax.fori_loop(0, LAT_T // (KV_CHUNK * CHUNK_UNROLL), body, 0)
    outs = [acc_ref[hd][:, :VDIM] / acc_ref[hd][:, VDIM:] for hd in range(HEADS)]
    o_ref[...] = jnp.concatenate(outs, axis=-1).astype(BF16)


def _attn_lat(q, k, v, kc, vc, layer):
    qt = LAT_T // TQ
    ctx_tiles = N_CTX // TQ
    ctx_blocks = N_CTX // LAT_T
    return pl.pallas_call(
        _attn_lat_body,
        grid=(LAT_B, qt),
        in_specs=[
            pl.BlockSpec((TQ, HEADS * HEAD_BLOCK), lambda b, i: (ctx_tiles + b * qt + i, 0)),
            pl.BlockSpec((1, 1, PAST_LEN, HEADS * HEAD_BLOCK), lambda b, i: (b, layer, 0, 0)),
            pl.BlockSpec((1, 1, PAST_LEN, HEADS * VDIM), lambda b, i: (b, layer, 0, 0)),
            pl.BlockSpec((LAT_T, HEADS * HEAD_BLOCK), lambda b, i: (ctx_blocks + b, 0)),
            pl.BlockSpec((LAT_T, HEADS * VDIM), lambda b, i: (ctx_blocks + b, 0)),
        ],
        out_specs=pl.BlockSpec((TQ, HEADS * VDIM), lambda b, i: (b * qt + i, 0)),
        out_shape=jax.ShapeDtypeStruct((N_LAT, HEADS * VDIM), BF16),
        scratch_shapes=[pltpu.VMEM((HEADS, TQ, LANES), F32),
                        pltpu.VMEM((HEADS, TQ, 2 * VDIM), F32)],
        compiler_params=_params(2),
        name="attn_lat",
    )(q, kc, vc, k, v)


def _layer_norm(y, g, b):
    mu = jnp.mean(y, axis=-1, keepdims=True)
    d = y - mu
    var = jnp.mean(d * d, axis=-1, keepdims=True)
    return d * lax.rsqrt(var + LN_EPS) * g + b


def _postmix_body(actx_ref, alat_ref, bc_ref, xc_ref, xl_ref, g1_ref, sh2_ref, sc2_ref, wa_ref, wbc_ref,
                  lg_ref, lb_ref, wr_ref, x1_ref, u2p_ref, aff_ref):
    is_ctx = pl.program_id(0) < N_CTX // TM
    lane = lax.broadcasted_iota(I32, (1, LANES), 1)
    blocks = [slice(b * POST_BLOCK, (b + 1) * POST_BLOCK) for b in range(TM // POST_BLOCK)]
    mixes = []
    for rows in blocks:
        a = jnp.where(is_ctx, actx_ref[rows, :], alat_ref[rows, :])
        mix = jnp.dot(a, wa_ref[0], preferred_element_type=F32)
        mixes.append(mix + jnp.dot(bc_ref[rows, :], wbc_ref[0], preferred_element_type=F32))
    u2s = []
    for b, rows in enumerate(blocks):
        x = jnp.where(is_ctx, xc_ref[rows, :], xl_ref[rows, :])
        x1 = _layer_norm(ALPHA * x + g1_ref[0] * mixes[b], lg_ref[...], lb_ref[...])
        x1_ref[rows, :] = x1
        u2 = x1 * (1.0 + sc2_ref[0]) + sh2_ref[0]
        for c in range(WORDS):
            lo = u2[:, c * LANES:(c + 1) * LANES]
            hi = u2[:, (c + WORDS) * LANES:(c + WORDS + 1) * LANES]
            u2p_ref[pl.ds(b * POST_BLOCK * WORDS + c, POST_BLOCK, stride=WORDS), :] = (
                pltpu.pack_elementwise([lo, hi], packed_dtype=BF16))
        u2s.append(u2.astype(BF16))
    for b, rows in enumerate(blocks):
        logits = jnp.dot(u2s[b], wr_ref[...], preferred_element_type=F32)
        logits = jnp.where(lane < N_EXPERTS, logits, -jnp.inf)
        m = jnp.max(logits, axis=-1, keepdims=True)
        e = jnp.exp(logits - m)
        aff_ref[rows, :] = e / jnp.sum(e, axis=-1, keepdims=True)


def _postmix(a_ctx, a_lat, bc, xc, xl, lat_row0, mod3, layer, w_out_b, ln_g, ln_b, w_router_pad):
    row = lambda i: (i, 0)
    modspec = lambda k: pl.BlockSpec((1, 1, D_MODEL), lambda i: (_mod_row(i, TM), 0, k))
    return pl.pallas_call(
        _postmix_body,
        grid=(N_TOK // TM,),
        in_specs=_pair_specs(TM, HEADS * VDIM, 0) + [
            pl.BlockSpec((TM, 2 * MIXW), row),
        ] + _pair_specs(TM, D_MODEL, lat_row0) + [
            modspec(2), modspec(3), modspec(4),
            pl.BlockSpec((1, HEADS * VDIM, D_MODEL), lambda i: (layer, 0, 0)),
            pl.BlockSpec((1, 2 * MIXW, D_MODEL), lambda i: (layer, 1, 0)),
            _full((1, D_MODEL)), _full((1, D_MODEL)),
            _full((D_MODEL, LANES)),
        ],
        out_specs=(pl.BlockSpec((TM, D_MODEL), row),
                   pl.BlockSpec((TM * WORDS, LANES), row),
                   pl.BlockSpec((TM, LANES), row)),
        out_shape=(jax.ShapeDtypeStruct((N_TOK, D_MODEL), F32),
                   jax.ShapeDtypeStruct((N_TOK * WORDS, LANES), jnp.uint32),
                   jax.ShapeDtypeStruct((N_TOK, LANES), F32)),
        compiler_params=_params(1),
        name="postmix",
    )(a_ctx, a_lat, bc, xc, xl, mod3, mod3, mod3, w_out_b, w_out_b, ln_g, ln_b, w_router_pad)


PREFIX_CHUNK = 256


def _route_ctx_body(aff_ref, idx_ref):
    n_rows = CTX_B * N_EXPERTS
    dense = jnp.concatenate(
        [aff_ref[b * CTX_T:(b + 1) * CTX_T, :].T[:N_EXPERTS] for b in range(CTX_B)], axis=0)

    def search(i, thr):
        cand = thr | jnp.left_shift(jnp.int32(1), 30 - i)
        n = jnp.sum((dense >= pltpu.bitcast(cand, F32)).astype(I32), axis=1, keepdims=True)
        return jnp.where(n >= CTX_CAP, cand, thr)

    thr_bits = lax.fori_loop(0, 31, search, jnp.zeros((n_rows, 1), I32))
    thr = pltpu.bitcast(thr_bits, F32)
    above = pltpu.bitcast(thr_bits + 1, F32)
    gt = (dense > thr) & (dense >= above)
    eq = (dense >= thr) & jnp.logical_not(gt)
    n_gt = jnp.sum(gt.astype(I32), axis=1, keepdims=True)
    need = (CTX_CAP - n_gt).astype(F32)

    ri = lax.broadcasted_iota(I32, (CTX_T, CTX_T), 0)
    ci = lax.broadcasted_iota(I32, (CTX_T, CTX_T), 1)
    tri = jnp.where(ri <= ci, 1.0, 0.0).astype(BF16)
    p_eq = jnp.dot(jnp.where(eq, 1.0, 0.0).astype(BF16), tri, preferred_element_type=F32)
    sel = jnp.where(gt, 1.0, jnp.where(eq & (p_eq <= need), 1.0, 0.0))
    cnt = jnp.dot(sel.astype(BF16), tri, preferred_element_type=F32)

    ones = jnp.ones((CTX_T, LANES), BF16)
    lane = lax.broadcasted_iota(I32, (1, LANES), 1)
    out = jnp.zeros((n_rows, LANES), F32)
    for r in range(CTX_CAP):
        le = jnp.where(cnt <= float(r), 1.0, 0.0).astype(BF16)
        out = jnp.where(lane == r, jnp.dot(le, ones, preferred_element_type=F32), out)
    idx_ref[...] = out.astype(I32)


def _route_ctx(aff):
    idx = pl.pallas_call(
        _route_ctx_body,
        grid=(1,),
        in_specs=[pl.BlockSpec((N_CTX, LANES), lambda i: (0, 0))],
        out_specs=_full((CTX_B * N_EXPERTS, LANES)),
        out_shape=jax.ShapeDtypeStruct((CTX_B * N_EXPERTS, LANES), I32),
        compiler_params=_params(1),
        name="route_ctx",
    )(aff)
    return idx[:, :CTX_CAP].reshape(CTX_B, N_EXPERTS, CTX_CAP)


RANK_TILE = LANES


def _expert_row(col):
    full = jnp.concatenate([jnp.broadcast_to(col, (N_EXPERTS, LANES)),
                            jnp.zeros((LANES - N_EXPERTS, LANES), col.dtype)], axis=0)
    return full.T[0:1]


def _route_count_body(aff_ref, cnt_ref, cend_ref, *, seq, cap):
    n_chunks = seq // PREFIX_CHUNK
    dense = aff_ref[...].T[:N_EXPERTS]

    def search(i, thr):
        cand = thr | jnp.left_shift(jnp.int32(1), 30 - i)
        n = jnp.sum((dense >= pltpu.bitcast(cand, F32)).astype(I32), axis=1, keepdims=True)
        return jnp.where(n >= cap, cand, thr)

    thr_bits = lax.fori_loop(0, 31, search, jnp.zeros((N_EXPERTS, 1), I32))
    thr_col = pltpu.bitcast(thr_bits, F32)
    above_col = pltpu.bitcast(thr_bits + 1, F32)
    n_gt = jnp.sum(((dense > thr_col) & (dense >= above_col)).astype(I32), axis=1, keepdims=True)
    thr = _expert_row(thr_col)
    above = _expert_row(above_col)
    need = _expert_row((cap - n_gt).astype(F32))

    ri = lax.broadcasted_iota(I32, (PREFIX_CHUNK, PREFIX_CHUNK), 0)
    ci = lax.broadcasted_iota(I32, (PREFIX_CHUNK, PREFIX_CHUNK), 1)
    tri = jnp.where(ci <= ri, 1.0, 0.0).astype(BF16)
    carry_eq = jnp.zeros((1, LANES), F32)
    carry_sel = jnp.zeros((1, LANES), F32)
    per = PREFIX_CHUNK // RANK_TILE
    for c in range(n_chunks):
        rows = slice(c * PREFIX_CHUNK, (c + 1) * PREFIX_CHUNK)
        a = aff_ref[rows, :]
        gt = (a > thr) & (a >= above)
        eq = (a >= thr) & jnp.logical_not(gt)
        p_eq = jnp.dot(tri, jnp.where(eq, 1.0, 0.0).astype(BF16),
                       preferred_element_type=F32) + carry_eq
        sel = jnp.where(gt, 1.0, jnp.where(eq & (p_eq <= need), 1.0, 0.0))
        p_sel = jnp.dot(tri, sel.astype(BF16), preferred_element_type=F32) + carry_sel
        dense_cnt = p_sel.T[:N_EXPERTS]
        for k in range(per):
            last = (k + 1) * RANK_TILE - 1
            cnt_ref[c * per + k] = dense_cnt[:, k * RANK_TILE:(k + 1) * RANK_TILE]
            cend_ref[0, c * per + k:c * per + k + 1, :] = p_sel[last:last + 1, :].astype(I32)
        carry_eq = p_eq[PREFIX_CHUNK - 1:PREFIX_CHUNK, :]
        carry_sel = p_sel[PREFIX_CHUNK - 1:PREFIX_CHUNK, :]


def _route_count(aff, first_block, n_req, seq, cap):
    n_tiles = seq // RANK_TILE
    return pl.pallas_call(
        functools.partial(_route_count_body, seq=seq, cap=cap),
        grid=(n_req,),
        in_specs=[pl.BlockSpec((seq, LANES), lambda b: (first_block + b, 0))],
        out_specs=(pl.BlockSpec((n_tiles, N_EXPERTS, RANK_TILE), lambda b: (b, 0, 0)),
                   pl.BlockSpec((1, n_tiles, LANES), lambda b: (b, 0, 0))),
        out_shape=(jax.ShapeDtypeStruct((n_req * n_tiles, N_EXPERTS, RANK_TILE), F32),
                   jax.ShapeDtypeStruct((n_req, n_tiles, LANES), I32)),
        compiler_params=_params(1),
        name=f"route_count_{seq}",
    )(aff)


def _route_rank_body(cend_ref, cnt_ref, idx_ref, *bufs, seq, cap):
    b = pl.program_id(0)
    n_tiles = seq // RANK_TILE
    rank = lax.broadcasted_iota(I32, (RANK_TILE, RANK_TILE), 0).astype(F32)
    ones = jnp.ones((RANK_TILE, LANES), BF16)

    def tile(j, _):
        counts = cnt_ref[j]
        for e in range(N_EXPERTS):
            prev = (b * N_EXPERTS + e) * n_tiles + j - 1
            start = jnp.where(j > 0, cend_ref[jnp.maximum(prev, 0)], 0)
            local = counts[e:e + 1, :] - jnp.asarray(start, F32)
            le = jnp.where(local <= rank, 1.0, 0.0).astype(BF16)
            pos = jnp.dot(le, ones, preferred_element_type=F32) + jnp.asarray(j * RANK_TILE, F32)
            bufs[e][pl.ds(start, RANK_TILE), :] = pos
        return 0

    lax.fori_loop(0, n_tiles, tile, 0)
    lane = lax.broadcasted_iota(I32, (1, LANES), 1)
    out = jnp.zeros((cap, LANES), F32)
    for e in range(N_EXPERTS):
        out = jnp.where(lane == e, bufs[e][0:cap, :], out)
    idx_ref[0] = out.astype(I32)


def _route_rank(cend_flat, cnt, n_req, seq, cap):
    n_tiles = seq // RANK_TILE
    grid_spec = pltpu.PrefetchScalarGridSpec(
        num_scalar_prefetch=1,
        grid=(n_req,),
        in_specs=[pl.BlockSpec((n_tiles, N_EXPERTS, RANK_TILE), lambda b, c: (b, 0, 0))],
        out_specs=pl.BlockSpec((1, cap, LANES), lambda b, c: (b, 0, 0)),
        scratch_shapes=[pltpu.VMEM((cap + RANK_TILE, LANES), F32)] * N_EXPERTS,
    )
    return pl.pallas_call(
        functools.partial(_route_rank_body, seq=seq, cap=cap),
        grid_spec=grid_spec,
        out_shape=jax.ShapeDtypeStruct((n_req, cap, LANES), I32),
        compiler_params=_params(1),
        name=f"route_rank_{seq}",
    )(cend_flat, cnt)


def _gather_rows(idx_ref, base, u2p_ref, aff_ref, tile_ref, gate_ref, r0, n):
    for i in range(n):
        r = r0 + i
        t = idx_ref[base + r]
        slab = u2p_ref[pl.ds(pl.multiple_of(t * WORDS, WORDS), WORDS), :]
        tile_ref[pl.ds(r, WORDS, stride=TILE_PITCH), :] = slab
        gate_ref[pl.ds(r, 1), :] = aff_ref[pl.ds(t, 1), :]


def _scatter_rows(idx_ref, base, y_ref, acc_ref, r0, n):
    for g0 in range(0, n, SCATTER_UNROLL):
        rows = [r0 + g0 + i for i in range(SCATTER_UNROLL)]
        dst = [pl.ds(pl.multiple_of(idx_ref[base + r] * CHUNKS, CHUNKS), CHUNKS) for r in rows]
        vals = [acc_ref[d, :] + y_ref[pl.ds(pl.multiple_of(r * CHUNKS, CHUNKS), CHUNKS), :]
                for d, r in zip(dst, rows)]
        for d, v in zip(dst, vals):
            acc_ref[d, :] = v


def _expert_rows(e, tile_ref, gate_ref, wg, wu, wd, y_ref, r0, n):
    halves = [[], []]
    for c in range(WORDS):
        w = tile_ref[pl.ds(c * TILE_PITCH + r0, n), :]
        for k in range(2):
            halves[k].append(pltpu.unpack_elementwise(w, index=k, packed_dtype=BF16,
                                                      unpacked_dtype=F32))
    x = jnp.concatenate(halves[0] + halves[1], axis=-1).astype(BF16)
    lane = lax.broadcasted_iota(I32, (1, LANES), 1)
    gate = jnp.sum(jnp.where(lane == e, gate_ref[pl.ds(r0, n), :], 0.0), axis=-1, keepdims=True)
    hg = jnp.dot(x, wg, preferred_element_type=F32)
    hu = jnp.dot(x, wu, preferred_element_type=F32)
    hidden = (hg * jax.nn.sigmoid(hg) * hu).astype(BF16)
    y = jnp.dot(hidden, wd, preferred_element_type=F32) * gate
    for c in range(CHUNKS):
        y_ref[pl.ds(r0 * CHUNKS + c, n, stride=CHUNKS), :] = y[:, c * LANES:(c + 1) * LANES]


def _moe_body(idx_ref, u2p_ref, aff_ref, wg_ref, wu_ref, wd_ref, acc_ref,
              tile_a, tile_b, gate_a, gate_b, y_a, y_b):
    p = pl.program_id(0)
    e = pl.program_id(1)
    step = p * N_EXPERTS + e
    base = step * ROWS
    base_next = jnp.minimum(step + 1, N_PASS * N_EXPERTS - 1) * ROWS
    base_prev = jnp.where(e == 0, base, base - ROWS)

    @pl.when(e == 0)
    def _():
        acc_ref[...] = jnp.zeros_like(acc_ref)
        y_b[...] = jnp.zeros_like(y_b)

        def gather(c, _):
            _gather_rows(idx_ref, base, u2p_ref, aff_ref, tile_a, gate_a, c * SUBLANES, SUBLANES)
            return 0

        lax.fori_loop(0, ROWS // SUBLANES, gather, 0)

    def run(tile_cur, gate_cur, y_cur, tile_nxt, gate_nxt, y_prv):
        wg = wg_ref[0, 0].astype(BF16)
        wu = wu_ref[0, 0].astype(BF16)
        wd = wd_ref[0, 0].astype(BF16)
        for blk in range(ROWS // ROW_BLOCK):
            r0 = blk * ROW_BLOCK
            _gather_rows(idx_ref, base_next, u2p_ref, aff_ref, tile_nxt, gate_nxt, r0, ROW_BLOCK)
            _expert_rows(e, tile_cur, gate_cur, wg, wu, wd, y_cur, r0, ROW_BLOCK)
            _scatter_rows(idx_ref, base_prev, y_prv, acc_ref, r0, ROW_BLOCK)

    @pl.when(e % 2 == 0)
    def _():
        run(tile_a, gate_a, y_a, tile_b, gate_b, y_b)

    @pl.when(e % 2 == 1)
    def _():
        run(tile_b, gate_b, y_b, tile_a, gate_a, y_a)

    @pl.when(e == N_EXPERTS - 1)
    def _():
        def scatter(c, _):
            _scatter_rows(idx_ref, base, y_b, acc_ref, c * SCATTER_UNROLL, SCATTER_UNROLL)
            return 0

        lax.fori_loop(0, ROWS // SCATTER_UNROLL, scatter, 0)


_MOE_SCRATCH = [
    pltpu.VMEM((WORDS * TILE_PITCH, LANES), jnp.uint32),
    pltpu.VMEM((WORDS * TILE_PITCH, LANES), jnp.uint32),
    pltpu.VMEM((ROWS, LANES), F32),
    pltpu.VMEM((ROWS, LANES), F32),
    pltpu.VMEM((ROWS * CHUNKS, LANES), F32),
    pltpu.VMEM((ROWS * CHUNKS, LANES), F32),
]


def _moe(idx_flat, u2p, aff, w_gate, w_up, w_down, layer):
    one = pl.Buffered(1)
    grid_spec = pltpu.PrefetchScalarGridSpec(
        num_scalar_prefetch=1,
        grid=(N_PASS, N_EXPERTS),
        in_specs=[
            pl.BlockSpec((PASS_TOK * WORDS, LANES), lambda p, e, idx: (p, 0)),
            pl.BlockSpec((PASS_TOK, LANES), lambda p, e, idx: (p, 0), pipeline_mode=one),
            pl.BlockSpec((1, 1, D_MODEL, EXPERT_DIM), lambda p, e, idx: (layer, e, 0, 0)),
            pl.BlockSpec((1, 1, D_MODEL, EXPERT_DIM), lambda p, e, idx: (layer, e, 0, 0)),
            pl.BlockSpec((1, 1, EXPERT_DIM, D_MODEL), lambda p, e, idx: (layer, e, 0, 0)),
        ],
        out_specs=pl.BlockSpec((PASS_TOK * CHUNKS, LANES), lambda p, e, idx: (p, 0),
                               pipeline_mode=one),
        scratch_shapes=_MOE_SCRATCH,
    )
    return pl.pallas_call(
        _moe_body,
        grid_spec=grid_spec,
        out_shape=jax.ShapeDtypeStruct((N_TOK * CHUNKS, LANES), F32),
        compiler_params=_params(2, arbitrary=True),
        name="moe",
    )(idx_flat, u2p, aff, w_gate, w_up, w_down)


def _final_body(x1_ref, ff_ref, g2_ref, lg_ref, lb_ref, o_ref):
    ffn = jnp.concatenate([ff_ref[pl.ds(c, TM, stride=CHUNKS), :] for c in range(CHUNKS)], axis=-1)
    o_ref[...] = _layer_norm(ALPHA * x1_ref[...] + g2_ref[0] * ffn, lg_ref[...], lb_ref[...])


def _final(x1, ff, mod3, ln_g, ln_b, row0=0, n_rows=N_TOK):
    t0 = row0 // TM
    row = lambda i: (i + t0, 0)
    return pl.pallas_call(
        _final_body,
        grid=(n_rows // TM,),
        in_specs=[
            pl.BlockSpec((TM, D_MODEL), row),
            pl.BlockSpec((TM * CHUNKS, LANES), row),
            pl.BlockSpec((1, 1, D_MODEL), lambda i: (_mod_row(i + t0, TM), 0, 5)),
            _full((1, D_MODEL)), _full((1, D_MODEL)),
        ],
        out_specs=pl.BlockSpec((TM, D_MODEL), lambda i: (i, 0)),
        out_shape=jax.ShapeDtypeStruct((n_rows, D_MODEL), F32),
        compiler_params=_params(1),
        name="final_ln",
    )(x1, ff, mod3, ln_g, ln_b)


def _rot_cols(w):
    half = ROPE // 2
    return jnp.concatenate([-w[..., half:], w[..., :half]], axis=-1)


def _pad_lanes(w):
    pad = [(0, 0)] * (w.ndim - 1) + [(0, LANES - w.shape[-1])]
    return jnp.pad(w, pad)


def _rope_tables():
    rows_n = LAT_T // GRID_W
    r, cl = jnp.meshgrid(jnp.arange(rows_n, dtype=F32), jnp.arange(GRID_W, dtype=F32), indexing="ij")
    inv = ROPE_THETA ** (-jnp.arange(0, ROPE // 2, 2, dtype=F32) / (ROPE // 2))
    ang = jnp.concatenate([r.reshape(-1)[:, None] * inv, cl.reshape(-1)[:, None] * inv], axis=-1)
    cos, sin = jnp.cos(ang), jnp.sin(ang)
    cos_lat = _pad_lanes(jnp.concatenate([cos, cos], axis=-1))
    sin_lat = _pad_lanes(jnp.concatenate([sin, sin], axis=-1))
    cos_ctx = _pad_lanes(jnp.ones((N_CTX, ROPE), F32))
    sin_ctx = jnp.zeros((N_CTX, LANES), F32)
    cos_a = jnp.concatenate([cos_ctx] + [cos_lat] * LAT_B, axis=0)
    sin_a = jnp.concatenate([sin_ctx] + [sin_lat] * LAT_B, axis=0)
    return cos_a, sin_a


def kernel(x_prompt, x_sample, cache_ckv, cache_krope, c, c_ctx, w_in, q_norm, w_uq, kv_norm, w_uk, w_uv,
           pool_w, pool_scale, conv_w, w_out, w_ada, b_ada, ln1_g, ln1_b, ln2_g, ln2_b, w_router,
           w_gate, w_up, w_down):
    w_in_ext = _prep_win(w_in)
    hw = NOPE + ROPE
    uq_nope = [w_uq[:, :, h * hw:h * hw + NOPE] for h in range(HEADS)]
    uq_rope = [w_uq[:, :, h * hw + NOPE:(h + 1) * hw] for h in range(HEADS)]
    w_uq_ext = jnp.concatenate(
        uq_nope + [blk for w in uq_rope for blk in (w, _rot_cols(w))], axis=-1).astype(BF16)
    w_uk_b = w_uk.astype(BF16)
    w_uv_b = w_uv.astype(BF16)
    eye = jnp.eye(len(POOL_WINDOWS), dtype=F32)
    pool_bd = (pool_w[:, :, :, None, :] * eye[None, :, None, :, None]).reshape(DEPTH, MIXW, MIXW).astype(BF16)
    w_out_b = w_out.astype(BF16)
    w_router_pad = _pad_lanes(w_router).astype(BF16)
    cos_a, sin_a = _rope_tables()

    cond = jnp.concatenate([c_ctx[None, :], c, jnp.zeros((SUBLANES - 1 - LAT_B, D_MODEL), F32)], axis=0)
    mod = _ada(cond, w_ada, b_ada)
    kc, vc = _cachekv(cache_ckv, _pad_lanes(cache_krope), w_uk_b, w_uv_b)

    prev = (x_prompt.reshape(N_CTX, D_MODEL), x_sample.reshape(N_LAT, D_MODEL))
    xs = prev + (0,)
    ckv_layers, kr_layers = [], []
    ctx_off = (jnp.arange(CTX_B, dtype=I32) * CTX_T)[:, None, None]
    for l in range(DEPTH):
        mod3 = mod[l].reshape(SUBLANES, 1, 6 * D_MODEL)
        outs = _front(prev, mod3, l, w_in_ext, q_norm[l][None], kv_norm[l][None],
                      w_uq_ext[l], w_uk_b[l], w_uv_b[l], cos_a, sin_a,
                      pool_bd[l], pool_scale[l][None], conv_w[l])
        ckv, kr, q, k, v, bc = outs[:6]
        if l > 0:
            xs = (outs[6], outs[6], N_CTX)
        ckv_layers.append(ckv[:N_CTX].reshape(CTX_B, CTX_T, KV_RANK))
        kr_layers.append(kr[:N_CTX, :ROPE].reshape(CTX_B, CTX_T, ROPE))
        a_ctx = _attn_ctx(q, k, v)
        a_lat = _attn_lat(q, k, v, kc, vc, l)
        x1, u2p, aff = _postmix(a_ctx, a_lat, bc, *xs, mod3, l, w_out_b,
                                ln1_g[l][None], ln1_b[l][None], w_router_pad[l])
        idx_ctx = _route_ctx(aff)
        cnt_lat, cend = _route_count(aff, N_CTX // LAT_T, LAT_B, LAT_T, LAT_CAP)
        cend_flat = cend[:, :, :N_EXPERTS].transpose(0, 2, 1).reshape(-1)
        idx_lat = _route_rank(cend_flat, cnt_lat, LAT_B, LAT_T, LAT_CAP)
        idx_lat = idx_lat[:, :, :N_EXPERTS].transpose(0, 2, 1)
        idx_ctx = (idx_ctx + ctx_off).transpose(1, 0, 2).reshape(1, N_EXPERTS, ROWS)
        idx_flat = jnp.concatenate([idx_ctx, idx_lat], axis=0).reshape(-1)
        ff = _moe(idx_flat, u2p, aff, w_gate, w_up, w_down, l)
        if l + 1 < DEPTH:
            prev = (x1, ff, mod3, ln2_g[l][None], ln2_b[l][None])
        else:
            y_ctx = _final(x1, ff, mod3, ln2_g[l][None], ln2_b[l][None], 0, N_CTX)
            y_lat = _final(x1, ff, mod3, ln2_g[l][None], ln2_b[l][None], N_CTX, N_LAT)

    y_prompt = y_ctx.reshape(CTX_B, CTX_T, D_MODEL)
    y_sample = y_lat.reshape(LAT_B, LAT_T, D_MODEL)
    new_ckv = jnp.stack(ckv_layers, axis=1)
    new_krope = jnp.stack(kr_layers, axis=1)
    return (y_prompt, y_sample, new_ckv, new_krope)
```

```python
import functools

import jax
import jax.numpy as jnp
from jax import lax
from jax.experimental import pallas as pl
from jax.experimental.pallas import tpu as pltpu

F32 = jnp.float32
BF16 = jnp.bfloat16
I32 = jnp.int32

D_MODEL = 1024
CTX_B, CTX_T = 16, 256
LAT_B, LAT_T = 2, 4096
DEPTH = 4
PAST_LEN = 512
GRID_W = 64
HEADS = 4
NOPE, ROPE, VDIM = 128, 64, 128
Q_RANK, KV_RANK = 384, 256
POOL_WINDOWS = (2, 4, 8, 16)
POOL_GROUP = 64
MIXW = 256
N_EXPERTS = 16
EXPERT_DIM = 512
ROPE_THETA = 10000.0
ATTN_SCALE = (NOPE + ROPE) ** -0.5
LOG2E = 1.4426950408889634
Q_SCALE = ATTN_SCALE * LOG2E
ALPHA = (2 * DEPTH) ** 0.25
RMS_EPS = 1e-6
LN_EPS = 1e-5

N_CTX = CTX_B * CTX_T
N_LAT = LAT_B * LAT_T
N_TOK = N_CTX + N_LAT
CTX_CAP = 2 * CTX_T // N_EXPERTS
LAT_CAP = 2 * LAT_T // N_EXPERTS

LANES = 128
SUBLANES = 8
CHUNKS = D_MODEL // LANES
HEAD_BLOCK = 2 * LANES
VMEM_LIMIT = 56 * 1024 * 1024

TM = 512
SEQ_TILE = 256
HALO = 8
TQ = 512
KV_CHUNK = 512
CHUNK_UNROLL = 8
PASS_TOK = 4096
N_PASS = N_TOK // PASS_TOK
ROWS = 512
TILE_PITCH = ROWS + 8
SCATTER_UNROLL = 8
WORDS = CHUNKS // 2
ROW_BLOCK = 512
POST_BLOCK = 256


def _params(n_axes, arbitrary=False):
    sem = ("arbitrary" if arbitrary else "parallel",) * n_axes
    return pltpu.CompilerParams(dimension_semantics=sem, vmem_limit_bytes=VMEM_LIMIT)


def _mod_row(i, tile):
    start = i * tile
    return jnp.where(start < N_CTX, 0, 1 + (start - N_CTX) // LAT_T)


def _full(shape):
    nd = len(shape)
    return pl.BlockSpec(shape, lambda *_: (0,) * nd)


def _pair_specs(tile, width, lat_row0):
    nct = N_CTX // tile
    off = lat_row0 // tile
    return [pl.BlockSpec((tile, width), lambda i: (jnp.minimum(i, nct - 1), 0)),
            pl.BlockSpec((tile, width), lambda i: (jnp.maximum(i - nct, 0) + off, 0))]


def _pair_load(tile, ctx_ref, lat_ref):
    return jnp.where(pl.program_id(0) < N_CTX // tile, ctx_ref[...], lat_ref[...])


ADA_TN = 1536


def _ada_body(cond_ref, w_ref, b_ref, o_ref):
    c = cond_ref[...]
    a = c * jax.nn.sigmoid(c)
    a_hi = a.astype(BF16)
    a_lo = (a - a_hi.astype(F32)).astype(BF16)
    w = w_ref[0]
    w_hi = w.astype(BF16)
    w_lo = (w - w_hi.astype(F32)).astype(BF16)
    acc = jnp.dot(a_hi, w_hi, preferred_element_type=F32)
    acc += jnp.dot(a_lo, w_hi, preferred_element_type=F32)
    acc += jnp.dot(a_hi, w_lo, preferred_element_type=F32)
    o_ref[0] = acc + b_ref[0]


def _ada(cond, w_ada, b_ada):
    n = 6 * D_MODEL
    return pl.pallas_call(
        _ada_body,
        grid=(DEPTH, n // ADA_TN),
        in_specs=[
            _full((SUBLANES, D_MODEL)),
            pl.BlockSpec((1, D_MODEL, ADA_TN), lambda l, j: (l, 0, j)),
            pl.BlockSpec((1, 1, ADA_TN), lambda l, j: (l, 0, j)),
        ],
        out_specs=pl.BlockSpec((1, SUBLANES, ADA_TN), lambda l, j: (l, 0, j)),
        out_shape=jax.ShapeDtypeStruct((DEPTH, SUBLANES, n), F32),
        compiler_params=_params(2),
        name="ada_mod",
    )(cond, w_ada, b_ada.reshape(DEPTH, 1, n))


D_IN = Q_RANK + KV_RANK + ROPE + 4 * MIXW
IN_EXT = Q_RANK + KV_RANK + 4 * MIXW + LANES
UQ_EXT = HEADS * NOPE + HEADS * LANES


def _prep_win_body(w_ref, o_ref):
    w = w_ref[0]
    kr0 = Q_RANK + KV_RANK
    kr = w[:, kr0:kr0 + ROPE]
    half = ROPE // 2
    rot = jnp.concatenate([-kr[:, half:], kr[:, :half]], axis=-1)
    out = jnp.concatenate([w[:, :kr0], w[:, kr0 + ROPE:], kr, rot], axis=-1)
    o_ref[0] = out.astype(BF16)


def _rope_block(blk, cos_a, sin_a):
    return blk * cos_a + pltpu.roll(blk, ROPE, axis=1) * sin_a


def _prep_win(w_in):
    lyr = lambda l: (l, 0, 0)
    return pl.pallas_call(
        _prep_win_body,
        grid=(DEPTH,),
        in_specs=[pl.BlockSpec((1, D_MODEL, D_IN), lyr)],
        out_specs=pl.BlockSpec((1, D_MODEL, IN_EXT), lyr),
        out_shape=jax.ShapeDtypeStruct((DEPTH, D_MODEL, IN_EXT), BF16),
        compiler_params=_params(1),
        name="prep_w_in",
    )(w_in)


def _rms(x, g):
    return x * lax.rsqrt(jnp.mean(x * x, axis=-1, keepdims=True) + RMS_EPS) * g


MIX_COL0 = Q_RANK + KV_RANK


def _ffn_rows(ff_ref, n, tok0=0):
    return jnp.concatenate([ff_ref[pl.ds(tok0 * CHUNKS + c, n, stride=CHUNKS), :]
                            for c in range(CHUNKS)], axis=-1)


def _front_body(*refs, fused_ln):
    if fused_ln:
        (x1_ref, ff_ref, x1p_ref, x1n_ref, ffp_ref, ffn_ref, g2_ref, lg_ref, lb_ref,
         sh_ref, sc_ref, win_ref, qn_ref, kvn_ref, wuq_ref, wuk_ref, wuv_ref, cos_ref, sin_ref,
         pw_ref, ps_ref, cw_ref,
         ckv_ref, kr_ref, q_ref, k_ref, v_ref, bc_ref, x_ref) = refs

        def norm2(x1, ffn):
            return _layer_norm(ALPHA * x1 + g2_ref[0] * ffn, lg_ref[...], lb_ref[...])

    else:
        (xc_ref, xl_ref, xp_ref, xn_ref,
         sh_ref, sc_ref, win_ref, qn_ref, kvn_ref, wuq_ref, wuk_ref, wuv_ref, cos_ref, sin_ref,
         pw_ref, ps_ref, cw_ref,
         ckv_ref, kr_ref, q_ref, k_ref, v_ref, bc_ref) = refs
        is_ctx = pl.program_id(0) < N_CTX // TM

    n_sub = TM // SEQ_TILE
    subs = [slice(s * SEQ_TILE, (s + 1) * SEQ_TILE) for s in range(n_sub)]

    xs = []
    for s, rows in enumerate(subs):
        if fused_ln:
            x = norm2(x1_ref[rows, :], _ffn_rows(ff_ref, SEQ_TILE, s * SEQ_TILE))
            x_ref[rows, :] = x
        else:
            x = jnp.where(is_ctx, xc_ref[rows, :], xl_ref[rows, :])
        xs.append(x)
    if fused_ln:
        x_halo = [norm2(x1p_ref[...], _ffn_rows(ffp_ref, HALO)),
                  norm2(x1n_ref[...], _ffn_rows(ffn_ref, HALO))]
    else:
        x_halo = [xp_ref[...], xn_ref[...]]
    xs[-1] = jnp.concatenate([xs[-1]] + x_halo, axis=0)

    hs = []
    for x in xs:
        u = x * (1.0 + sc_ref[0]) + sh_ref[0]
        hs.append(jnp.dot(u.astype(BF16), win_ref[0], preferred_element_type=F32))
    mix_cols = slice(MIX_COL0, MIX_COL0 + 4 * MIXW)
    mix_halo = hs[-1][SEQ_TILE:, mix_cols]
    mixes = [h[:SEQ_TILE, mix_cols] for h in hs]

    for s, rows in enumerate(subs):
        prev = mix_halo[:HALO] if s == 0 else mixes[s - 1][SEQ_TILE - HALO:]
        nxt = mix_halo[HALO:] if s == n_sub - 1 else mixes[s + 1][:HALO]
        bc_ref[rows, :] = _seqmix_math(pl.program_id(0) * n_sub + s, mixes[s], prev, nxt,
                                       pw_ref[...], ps_ref[...], cw_ref[...])

    kr0 = MIX_COL0 + 4 * MIXW
    ckv_bs = []
    for s, rows in enumerate(subs):
        ckv = _rms(hs[s][:SEQ_TILE, Q_RANK:Q_RANK + KV_RANK], kvn_ref[...])
        ckv_ref[rows, :] = ckv
        ckv_bs.append(ckv.astype(BF16))
    for s, rows in enumerate(subs):
        k_nope = jnp.dot(ckv_bs[s], wuk_ref[...], preferred_element_type=F32)
        v_ref[rows, :] = jnp.dot(ckv_bs[s], wuv_ref[...], preferred_element_type=F32).astype(BF16)
        kr_blk = hs[s][:SEQ_TILE, kr0:kr0 + LANES]
        kr_ref[rows, :] = kr_blk
        k_rope = _rope_block(kr_blk, cos_ref[rows, :], sin_ref[rows, :])
        parts = []
        for hd in range(HEADS):
            parts += [k_nope[:, hd * NOPE:(hd + 1) * NOPE], k_rope]
        k_ref[rows, :] = jnp.concatenate(parts, axis=-1).astype(BF16)
    qns = [_rms(hs[s][:SEQ_TILE, :Q_RANK], qn_ref[...]).astype(BF16) for s in range(n_sub)]
    ra = HEADS * NOPE
    for s, rows in enumerate(subs):
        qq = jnp.dot(qns[s], wuq_ref[...], preferred_element_type=F32)
        parts = []
        for hd in range(HEADS):
            q_rope = _rope_block(qq[:, ra + hd * LANES:ra + (hd + 1) * LANES],
                                 cos_ref[rows, :], sin_ref[rows, :])
            parts += [qq[:, hd * NOPE:(hd + 1) * NOPE] * Q_SCALE, q_rope * Q_SCALE]
        q_ref[rows, :] = jnp.concatenate(parts, axis=-1).astype(BF16)


def _halo_specs(rows_per_token, width, lat_row0, n_rows):
    nct = N_CTX // TM
    nblk = n_rows // HALO

    def first_block(i):
        return (jnp.maximum(i - nct, 0) * TM + lat_row0) // HALO

    shape = (HALO * rows_per_token, width)
    return [pl.BlockSpec(shape, lambda i: (jnp.maximum(first_block(i) - 1, 0), 0)),
            pl.BlockSpec(shape, lambda i: (jnp.minimum(first_block(i) + TM // HALO, nblk - 1), 0))]


def _front(prev, mod3, layer, w_in_ext, q_norm, kv_norm, w_uq_ext, w_uk, w_uv, cos_a, sin_a,
           pool_bd, pool_scale, conv_w):
    row = lambda i: (i, 0)
    lyr = lambda i: (layer, 0, 0)
    nct = N_CTX // TM
    rope_row = lambda i: (jnp.where(i < nct, i, nct + (i - nct) % (LAT_T // TM)), 0)
    modspec = lambda k: pl.BlockSpec((1, 1, D_MODEL), lambda i: (_mod_row(i, TM), 0, k))
    fused_ln = len(prev) == 5
    outs = [
        jax.ShapeDtypeStruct((N_TOK, KV_RANK), F32),
        jax.ShapeDtypeStruct((N_TOK, LANES), F32),
        jax.ShapeDtypeStruct((N_TOK, HEADS * HEAD_BLOCK), BF16),
        jax.ShapeDtypeStruct((N_TOK, HEADS * HEAD_BLOCK), BF16),
        jax.ShapeDtypeStruct((N_TOK, HEADS * VDIM), BF16),
        jax.ShapeDtypeStruct((N_TOK, 2 * MIXW), BF16),
    ]
    if fused_ln:
        x1, ff, mod3_prev, ln_g, ln_b = prev
        outs.append(jax.ShapeDtypeStruct((N_TOK, D_MODEL), F32))
        lead_specs = ([pl.BlockSpec((TM, D_MODEL), row), pl.BlockSpec((TM * CHUNKS, LANES), row)]
                      + _halo_specs(1, D_MODEL, N_CTX, N_TOK)
                      + _halo_specs(CHUNKS, LANES, N_CTX, N_TOK)
                      + [pl.BlockSpec((1, 1, D_MODEL), lambda i: (_mod_row(i, TM), 0, 5)),
                         _full((1, D_MODEL)), _full((1, D_MODEL))])
        lead_args = (x1, ff, x1, x1, ff, ff, mod3_prev, ln_g, ln_b)
    else:
        xc, xl = prev
        lead_specs = _pair_specs(TM, D_MODEL, 0) + _halo_specs(1, D_MODEL, 0, N_LAT)
        lead_args = (xc, xl, xl, xl)
    return pl.pallas_call(
        functools.partial(_front_body, fused_ln=fused_ln),
        grid=(N_TOK // TM,),
        in_specs=lead_specs + [
            modspec(0), modspec(1),
            pl.BlockSpec((1, D_MODEL, IN_EXT), lyr),
            _full((1, Q_RANK)), _full((1, KV_RANK)),
            _full((Q_RANK, UQ_EXT)),
            _full((KV_RANK, HEADS * NOPE)), _full((KV_RANK, HEADS * VDIM)),
            pl.BlockSpec((TM, LANES), rope_row), pl.BlockSpec((TM, LANES), rope_row),
            _full((MIXW, MIXW)), _full((1, MIXW)), _full((3, MIXW)),
        ],
        out_specs=tuple(pl.BlockSpec((TM, s.shape[1]), row) for s in outs),
        out_shape=tuple(outs),
        compiler_params=_params(1),
        name="front_ln" if fused_ln else "front",
    )(*lead_args, mod3, mod3, w_in_ext, q_norm, kv_norm, w_uq_ext, w_uk, w_uv, cos_a, sin_a,
      pool_bd, pool_scale, conv_w)


def _cachekv_body(ckv_ref, kr_ref, wuk_ref, wuv_ref, k_ref, v_ref):
    c = ckv_ref[0, 0].astype(BF16)
    k_nope = jnp.dot(c, wuk_ref[0], preferred_element_type=F32)
    v_ref[0, 0] = jnp.dot(c, wuv_ref[0], preferred_element_type=F32).astype(BF16)
    kr = kr_ref[0, 0]
    parts = []
    for hd in range(HEADS):
        parts += [k_nope[:, hd * NOPE:(hd + 1) * NOPE], kr]
    k_ref[0, 0] = jnp.concatenate(parts, axis=-1).astype(BF16)


def _cachekv(cache_ckv, kr_pad, w_uk, w_uv):
    bl = lambda b, l: (b, l, 0, 0)
    wl = lambda b, l: (l, 0, 0)
    return pl.pallas_call(
        _cachekv_body,
        grid=(LAT_B, DEPTH),
        in_specs=[
            pl.BlockSpec((1, 1, PAST_LEN, KV_RANK), bl),
            pl.BlockSpec((1, 1, PAST_LEN, LANES), bl),
            pl.BlockSpec((1, KV_RANK, HEADS * NOPE), wl),
            pl.BlockSpec((1, KV_RANK, HEADS * VDIM), wl),
        ],
        out_specs=(pl.BlockSpec((1, 1, PAST_LEN, HEADS * HEAD_BLOCK), bl),
                   pl.BlockSpec((1, 1, PAST_LEN, HEADS * VDIM), bl)),
        out_shape=(jax.ShapeDtypeStruct((LAT_B, DEPTH, PAST_LEN, HEADS * HEAD_BLOCK), BF16),
                   jax.ShapeDtypeStruct((LAT_B, DEPTH, PAST_LEN, HEADS * VDIM), BF16)),
        compiler_params=_params(2),
        name="cache_kv",
    )(cache_ckv, kr_pad, w_uk, w_uv)


EXT = SEQ_TILE + 2 * HALO


def _shift_up(x, k):
    return pltpu.roll(x, x.shape[0] - k, axis=0)


def _shift_down(x, k):
    return pltpu.roll(x, k, axis=0)


def _seqmix_math(j, main, prev, nxt, pw, ps, cw):
    n_ctx_tiles = N_CTX // SEQ_TILE
    tiles_per_lat = LAT_T // SEQ_TILE
    is_ctx = j < n_ctx_tiles
    jj = jnp.where(is_ctx, 0, (j - n_ctx_tiles) % tiles_per_lat)
    first = jj == 0
    last = jnp.where(is_ctx, True, jj == tiles_per_lat - 1)
    t_seq = jnp.where(is_ctx, CTX_T, LAT_T)

    prev = jnp.where(first, 0.0, prev)
    nxt = jnp.where(last, 0.0, nxt)
    ext = jnp.concatenate([prev, main, nxt], axis=0)
    p = ext[:, 0:MIXW]
    g_b = main[:, MIXW:2 * MIXW]
    g_c = ext[:, 2 * MIXW:3 * MIXW]
    h_in = ext[:, 3 * MIXW:4 * MIXW]

    sums = {}
    b = p
    for w in POOL_WINDOWS:
        b = b + _shift_up(b, w // 2)
        sums[w] = _shift_down(b, w // 2)[HALO:HALO + SEQ_TILE]
    tpos = jj * SEQ_TILE + lax.broadcasted_iota(I32, (SEQ_TILE, 1), 0)
    lane = lax.broadcasted_iota(I32, (1, MIXW), 1)
    num = None
    den = None
    for g, w in enumerate(POOL_WINDOWS):
        lo = jnp.maximum(tpos - w // 2, 0)
        hi = jnp.minimum(tpos + w - w // 2, t_seq)
        cnt = (hi - lo).astype(F32)
        if num is None:
            num, den = sums[w], jnp.broadcast_to(cnt, (SEQ_TILE, MIXW))
        else:
            sel = lane >= g * POOL_GROUP
            num = jnp.where(sel, sums[w], num)
            den = jnp.where(sel, cnt, den)
    pooled = num / den - p[HALO:HALO + SEQ_TILE]
    b_out = jnp.dot(pooled.astype(BF16), pw, preferred_element_type=F32) * ps

    y = g_c * h_in
    conv = (_shift_down(y, 1) * cw[0:1] + y * cw[1:2] + _shift_up(y, 1) * cw[2:3])
    c_out = g_b * conv[HALO:HALO + SEQ_TILE]
    return jnp.concatenate([b_out, c_out], axis=-1).astype(BF16)


_NT = (((1,), (1,)), ((), ()))


def _v_ext(v):
    return jnp.concatenate([v, jnp.ones(v.shape, v.dtype)], axis=-1)


CTX_PER_STEP = 2


def _attn_ctx_body(q_ref, k_ref, v_ref, o_ref):
    chains = [(slice(r * CTX_T, (r + 1) * CTX_T), hd)
              for r in range(CTX_PER_STEP) for hd in range(HEADS)]
    scores = []
    for rows, hd in chains:
        cols = slice(hd * HEAD_BLOCK, (hd + 1) * HEAD_BLOCK)
        scores.append(lax.dot_general(q_ref[rows, cols], k_ref[rows, cols], _NT,
                                      preferred_element_type=F32))
    probs = [jnp.exp2(s - jnp.max(s, axis=-1, keepdims=True)).astype(BF16) for s in scores]
    for (rows, hd), p in zip(chains, probs):
        vcols = slice(hd * VDIM, (hd + 1) * VDIM)
        acc = jnp.dot(p, _v_ext(v_ref[rows, vcols]), preferred_element_type=F32)
        o_ref[rows, vcols] = (acc[:, :VDIM] / acc[:, VDIM:]).astype(BF16)


def _attn_ctx(q, k, v):
    row = lambda b: (b, 0)
    tile = CTX_PER_STEP * CTX_T
    return pl.pallas_call(
        _attn_ctx_body,
        grid=(CTX_B // CTX_PER_STEP,),
        in_specs=[pl.BlockSpec((tile, HEADS * HEAD_BLOCK), row),
                  pl.BlockSpec((tile, HEADS * HEAD_BLOCK), row),
                  pl.BlockSpec((tile, HEADS * VDIM), row)],
        out_specs=pl.BlockSpec((tile, HEADS * VDIM), row),
        out_shape=jax.ShapeDtypeStruct((N_CTX, HEADS * VDIM), BF16),
        compiler_params=_params(1),
        name="attn_ctx",
    )(q, k, v)


def _attn_lat_body(q_ref, kc_ref, vc_ref, ko_ref, vo_ref, o_ref, m_ref, acc_ref):
    reps = KV_CHUNK // LANES

    def update(hd, k, v, first):
        qh = q_ref[:, hd * HEAD_BLOCK:(hd + 1) * HEAD_BLOCK]
        s = lax.dot_general(qh, k, _NT, preferred_element_type=F32)
        mx = jnp.max(s, axis=-1, keepdims=True)
        if first:
            m_new = jnp.broadcast_to(mx, (TQ, LANES))
        else:
            m_old = m_ref[hd]
            m_new = jnp.maximum(m_old, mx)
        p = jnp.exp2(s - jnp.concatenate([m_new] * reps, axis=-1))
        pv = jnp.dot(p.astype(BF16), _v_ext(v), preferred_element_type=F32)
        if first:
            acc_ref[hd] = pv
        else:
            a = jnp.exp2(m_old - m_new)
            acc_ref[hd] = acc_ref[hd] * jnp.concatenate([a, a], axis=-1) + pv
        m_ref[hd] = m_new

    def head_slices(hd):
        return slice(hd * HEAD_BLOCK, (hd + 1) * HEAD_BLOCK), slice(hd * VDIM, (hd + 1) * VDIM)

    for hd in range(HEADS):
        ks, vs = head_slices(hd)
        update(hd, kc_ref[0, 0, :, ks], vc_ref[0, 0, :, vs], True)

    def body(c, _):
        for j in range(CHUNK_UNROLL):
            rows = pl.ds(pl.multiple_of((c * CHUNK_UNROLL + j) * KV_CHUNK, KV_CHUNK), KV_CHUNK)
            for hd in range(HEADS):
                ks, vs = head_slices(hd)
                update(hd, ko_ref[rows, ks], vo_ref[rows, vs], False)
        return 0

    lax.fori_loop(0, LAT_T // (KV_CHUNK * CHUNK_UNROLL), body, 0)
    outs = [acc_ref[hd][:, :VDIM] / acc_ref[hd][:, VDIM:] for hd in range(HEADS)]
    o_ref[...] = jnp.concatenate(outs, axis=-1).astype(BF16)


def _attn_lat(q, k, v, kc, vc, layer):
    qt = LAT_T // TQ
    ctx_tiles = N_CTX // TQ
    ctx_blocks = N_CTX // LAT_T
    return pl.pallas_call(
        _attn_lat_body,
        grid=(LAT_B, qt),
        in_specs=[
            pl.BlockSpec((TQ, HEADS * HEAD_BLOCK), lambda b, i: (ctx_tiles + b * qt + i, 0)),
            pl.BlockSpec((1, 1, PAST_LEN, HEADS * HEAD_BLOCK), lambda b, i: (b, layer, 0, 0)),
            pl.BlockSpec((1, 1, PAST_LEN, HEADS * VDIM), lambda b, i: (b, layer, 0, 0)),
            pl.BlockSpec((LAT_T, HEADS * HEAD_BLOCK), lambda b, i: (ctx_blocks + b, 0)),
            pl.BlockSpec((LAT_T, HEADS * VDIM), lambda b, i: (ctx_blocks + b, 0)),
        ],
        out_specs=pl.BlockSpec((TQ, HEADS * VDIM), lambda b, i: (b * qt + i, 0)),
        out_shape=jax.ShapeDtypeStruct((N_LAT, HEADS * VDIM), BF16),
        scratch_shapes=[pltpu.VMEM((HEADS, TQ, LANES), F32),
                        pltpu.VMEM((HEADS, TQ, 2 * VDIM), F32)],
        compiler_params=_params(2),
        name="attn_lat",
    )(q, kc, vc, k, v)


def _layer_norm(y, g, b):
    mu = jnp.mean(y, axis=-1, keepdims=True)
    d = y - mu
    var = jnp.mean(d * d, axis=-1, keepdims=True)
    return d * lax.rsqrt(var + LN_EPS) * g + b


def _postmix_body(actx_ref, alat_ref, bc_ref, xc_ref, xl_ref, g1_ref, sh2_ref, sc2_ref, wa_ref, wbc_ref,
                  lg_ref, lb_ref, wr_ref, x1_ref, u2p_ref, aff_ref):
    is_ctx = pl.program_id(0) < N_CTX // TM
    lane = lax.broadcasted_iota(I32, (1, LANES), 1)
    blocks = [slice(b * POST_BLOCK, (b + 1) * POST_BLOCK) for b in range(TM // POST_BLOCK)]
    mixes = []
    for rows in blocks:
        a = jnp.where(is_ctx, actx_ref[rows, :], alat_ref[rows, :])
        mix = jnp.dot(a, wa_ref[0], preferred_element_type=F32)
        mixes.append(mix + jnp.dot(bc_ref[rows, :], wbc_ref[0], preferred_element_type=F32))
    u2s = []
    for b, rows in enumerate(blocks):
        x = jnp.where(is_ctx, xc_ref[rows, :], xl_ref[rows, :])
        x1 = _layer_norm(ALPHA * x + g1_ref[0] * mixes[b], lg_ref[...], lb_ref[...])
        x1_ref[rows, :] = x1
        u2 = x1 * (1.0 + sc2_ref[0]) + sh2_ref[0]
        for c in range(WORDS):
            lo = u2[:, c * LANES:(c + 1) * LANES]
            hi = u2[:, (c + WORDS) * LANES:(c + WORDS + 1) * LANES]
            u2p_ref[pl.ds(b * POST_BLOCK * WORDS + c, POST_BLOCK, stride=WORDS), :] = (
                pltpu.pack_elementwise([lo, hi], packed_dtype=BF16))
        u2s.append(u2.astype(BF16))
    for b, rows in enumerate(blocks):
        logits = jnp.dot(u2s[b], wr_ref[...], preferred_element_type=F32)
        logits = jnp.where(lane < N_EXPERTS, logits, -jnp.inf)
        m = jnp.max(logits, axis=-1, keepdims=True)
        e = jnp.exp(logits - m)
        aff_ref[rows, :] = e / jnp.sum(e, axis=-1, keepdims=True)


def _postmix(a_ctx, a_lat, bc, xc, xl, lat_row0, mod3, layer, w_out_b, ln_g, ln_b, w_router_pad):
    row = lambda i: (i, 0)
    modspec = lambda k: pl.BlockSpec((1, 1, D_MODEL), lambda i: (_mod_row(i, TM), 0, k))
    return pl.pallas_call(
        _postmix_body,
        grid=(N_TOK // TM,),
        in_specs=_pair_specs(TM, HEADS * VDIM, 0) + [
            pl.BlockSpec((TM, 2 * MIXW), row),
        ] + _pair_specs(TM, D_MODEL, lat_row0) + [
            modspec(2), modspec(3), modspec(4),
            pl.BlockSpec((1, HEADS * VDIM, D_MODEL), lambda i: (layer, 0, 0)),
            pl.BlockSpec((1, 2 * MIXW, D_MODEL), lambda i: (layer, 1, 0)),
            _full((1, D_MODEL)), _full((1, D_MODEL)),
            _full((D_MODEL, LANES)),
        ],
        out_specs=(pl.BlockSpec((TM, D_MODEL), row),
                   pl.BlockSpec((TM * WORDS, LANES), row),
                   pl.BlockSpec((TM, LANES), row)),
        out_shape=(jax.ShapeDtypeStruct((N_TOK, D_MODEL), F32),
                   jax.ShapeDtypeStruct((N_TOK * WORDS, LANES), jnp.uint32),
                   jax.ShapeDtypeStruct((N_TOK, LANES), F32)),
        compiler_params=_params(1),
        name="postmix",
    )(a_ctx, a_lat, bc, xc, xl, mod3, mod3, mod3, w_out_b, w_out_b, ln_g, ln_b, w_router_pad)


PREFIX_CHUNK = 256


def _route_ctx_body(aff_ref, idx_ref):
    n_rows = CTX_B * N_EXPERTS
    dense = jnp.concatenate(
        [aff_ref[b * CTX_T:(b + 1) * CTX_T, :].T[:N_EXPERTS] for b in range(CTX_B)], axis=0)

    def search(i, thr):
        cand = thr | jnp.left_shift(jnp.int32(1), 30 - i)
        n = jnp.sum((dense >= pltpu.bitcast(cand, F32)).astype(I32), axis=1, keepdims=True)
        return jnp.where(n >= CTX_CAP, cand, thr)

    thr_bits = lax.fori_loop(0, 31, search, jnp.zeros((n_rows, 1), I32))
    thr = pltpu.bitcast(thr_bits, F32)
    above = pltpu.bitcast(thr_bits + 1, F32)
    gt = (dense > thr) & (dense >= above)
    eq = (dense >= thr) & jnp.logical_not(gt)
    n_gt = jnp.sum(gt.astype(I32), axis=1, keepdims=True)
    need = (CTX_CAP - n_gt).astype(F32)

    ri = lax.broadcasted_iota(I32, (CTX_T, CTX_T), 0)
    ci = lax.broadcasted_iota(I32, (CTX_T, CTX_T), 1)
    tri = jnp.where(ri <= ci, 1.0, 0.0).astype(BF16)
    p_eq = jnp.dot(jnp.where(eq, 1.0, 0.0).astype(BF16), tri, preferred_element_type=F32)
    sel = jnp.where(gt, 1.0, jnp.where(eq & (p_eq <= need), 1.0, 0.0))
    cnt = jnp.dot(sel.astype(BF16), tri, preferred_element_type=F32)

    ones = jnp.ones((CTX_T, LANES), BF16)
    lane = lax.broadcasted_iota(I32, (1, LANES), 1)
    out = jnp.zeros((n_rows, LANES), F32)
    for r in range(CTX_CAP):
        le = jnp.where(cnt <= float(r), 1.0, 0.0).astype(BF16)
        out = jnp.where(lane == r, jnp.dot(le, ones, preferred_element_type=F32), out)
    idx_ref[...] = out.astype(I32)


def _route_ctx(aff):
    idx = pl.pallas_call(
        _route_ctx_body,
        grid=(1,),
        in_specs=[pl.BlockSpec((N_CTX, LANES), lambda i: (0, 0))],
        out_specs=_full((CTX_B * N_EXPERTS, LANES)),
        out_shape=jax.ShapeDtypeStruct((CTX_B * N_EXPERTS, LANES), I32),
        compiler_params=_params(1),
        name="route_ctx",
    )(aff)
    return idx[:, :CTX_CAP].reshape(CTX_B, N_EXPERTS, CTX_CAP)


RANK_TILE = LANES


def _expert_row(col):
    full = jnp.concatenate([jnp.broadcast_to(col, (N_EXPERTS, LANES)),
                            jnp.zeros((LANES - N_EXPERTS, LANES), col.dtype)], axis=0)
    return full.T[0:1]


def _route_count_body(aff_ref, cnt_ref, cend_ref, *, seq, cap):
    n_chunks = seq // PREFIX_CHUNK
    dense = aff_ref[...].T[:N_EXPERTS]

    def search(i, thr):
        cand = thr | jnp.left_shift(jnp.int32(1), 30 - i)
        n = jnp.sum((dense >= pltpu.bitcast(cand, F32)).astype(I32), axis=1, keepdims=True)
        return jnp.where(n >= cap, cand, thr)

    thr_bits = lax.fori_loop(0, 31, search, jnp.zeros((N_EXPERTS, 1), I32))
    thr_col = pltpu.bitcast(thr_bits, F32)
    above_col = pltpu.bitcast(thr_bits + 1, F32)
    n_gt = jnp.sum(((dense > thr_col) & (dense >= above_col)).astype(I32), axis=1, keepdims=True)
    thr = _expert_row(thr_col)
    above = _expert_row(above_col)
    need = _expert_row((cap - n_gt).astype(F32))

    ri = lax.broadcasted_iota(I32, (PREFIX_CHUNK, PREFIX_CHUNK), 0)
    ci = lax.broadcasted_iota(I32, (PREFIX_CHUNK, PREFIX_CHUNK), 1)
    tri = jnp.where(ci <= ri, 1.0, 0.0).astype(BF16)
    carry_eq = jnp.zeros((1, LANES), F32)
    carry_sel = jnp.zeros((1, LANES), F32)
    per = PREFIX_CHUNK // RANK_TILE
    for c in range(n_chunks):
        rows = slice(c * PREFIX_CHUNK, (c + 1) * PREFIX_CHUNK)
        a = aff_ref[rows, :]
        gt = (a > thr) & (a >= above)
        eq = (a >= thr) & jnp.logical_not(gt)
        p_eq = jnp.dot(tri, jnp.where(eq, 1.0, 0.0).astype(BF16),
                       preferred_element_type=F32) + carry_eq
        sel = jnp.where(gt, 1.0, jnp.where(eq & (p_eq <= need), 1.0, 0.0))
        p_sel = jnp.dot(tri, sel.astype(BF16), preferred_element_type=F32) + carry_sel
        dense_cnt = p_sel.T[:N_EXPERTS]
        for k in range(per):
            last = (k + 1) * RANK_TILE - 1
            cnt_ref[c * per + k] = dense_cnt[:, k * RANK_TILE:(k + 1) * RANK_TILE]
            cend_ref[0, c * per + k:c * per + k + 1, :] = p_sel[last:last + 1, :].astype(I32)
        carry_eq = p_eq[PREFIX_CHUNK - 1:PREFIX_CHUNK, :]
        carry_sel = p_sel[PREFIX_CHUNK - 1:PREFIX_CHUNK, :]


def _route_count(aff, first_block, n_req, seq, cap):
    n_tiles = seq // RANK_TILE
    return pl.pallas_call(
        functools.partial(_route_count_body, seq=seq, cap=cap),
        grid=(n_req,),
        in_specs=[pl.BlockSpec((seq, LANES), lambda b: (first_block + b, 0))],
        out_specs=(pl.BlockSpec((n_tiles, N_EXPERTS, RANK_TILE), lambda b: (b, 0, 0)),
                   pl.BlockSpec((1, n_tiles, LANES), lambda b: (b, 0, 0))),
        out_shape=(jax.ShapeDtypeStruct((n_req * n_tiles, N_EXPERTS, RANK_TILE), F32),
                   jax.ShapeDtypeStruct((n_req, n_tiles, LANES), I32)),
        compiler_params=_params(1),
        name=f"route_count_{seq}",
    )(aff)


def _route_rank_body(cend_ref, cnt_ref, idx_ref, *bufs, seq, cap):
    b = pl.program_id(0)
    n_tiles = seq // RANK_TILE
    rank = lax.broadcasted_iota(I32, (RANK_TILE, RANK_TILE), 0).astype(F32)
    ones = jnp.ones((RANK_TILE, LANES), BF16)

    def tile(j, _):
        counts = cnt_ref[j]
        for e in range(N_EXPERTS):
            prev = (b * N_EXPERTS + e) * n_tiles + j - 1
            start = jnp.where(j > 0, cend_ref[jnp.maximum(prev, 0)], 0)
            local = counts[e:e + 1, :] - jnp.asarray(start, F32)
            le = jnp.where(local <= rank, 1.0, 0.0).astype(BF16)
            pos = jnp.dot(le, ones, preferred_element_type=F32) + jnp.asarray(j * RANK_TILE, F32)
            bufs[e][pl.ds(start, RANK_TILE), :] = pos
        return 0

    lax.fori_loop(0, n_tiles, tile, 0)
    lane = lax.broadcasted_iota(I32, (1, LANES), 1)
    out = jnp.zeros((cap, LANES), F32)
    for e in range(N_EXPERTS):
        out = jnp.where(lane == e, bufs[e][0:cap, :], out)
    idx_ref[0] = out.astype(I32)


def _route_rank(cend_flat, cnt, n_req, seq, cap):
    n_tiles = seq // RANK_TILE
    grid_spec = pltpu.PrefetchScalarGridSpec(
        num_scalar_prefetch=1,
        grid=(n_req,),
        in_specs=[pl.BlockSpec((n_tiles, N_EXPERTS, RANK_TILE), lambda b, c: (b, 0, 0))],
        out_specs=pl.BlockSpec((1, cap, LANES), lambda b, c: (b, 0, 0)),
        scratch_shapes=[pltpu.VMEM((cap + RANK_TILE, LANES), F32)] * N_EXPERTS,
    )
    return pl.pallas_call(
        functools.partial(_route_rank_body, seq=seq, cap=cap),
        grid_spec=grid_spec,
        out_shape=jax.ShapeDtypeStruct((n_req, cap, LANES), I32),
        compiler_params=_params(1),
        name=f"route_rank_{seq}",
    )(cend_flat, cnt)


def _gather_rows(idx_ref, base, u2p_ref, aff_ref, tile_ref, gate_ref, r0, n):
    for i in range(n):
        r = r0 + i
        t = idx_ref[base + r]
        slab = u2p_ref[pl.ds(pl.multiple_of(t * WORDS, WORDS), WORDS), :]
        tile_ref[pl.ds(r, WORDS, stride=TILE_PITCH), :] = slab
        gate_ref[pl.ds(r, 1), :] = aff_ref[pl.ds(t, 1), :]


def _scatter_rows(idx_ref, base, y_ref, acc_ref, r0, n):
    for g0 in range(0, n, SCATTER_UNROLL):
        rows = [r0 + g0 + i for i in range(SCATTER_UNROLL)]
        dst = [pl.ds(pl.multiple_of(idx_ref[base + r] * CHUNKS, CHUNKS), CHUNKS) for r in rows]
        vals = [acc_ref[d, :] + y_ref[pl.ds(pl.multiple_of(r * CHUNKS, CHUNKS), CHUNKS), :]
                for d, r in zip(dst, rows)]
        for d, v in zip(dst, vals):
            acc_ref[d, :] = v


def _expert_rows(e, tile_ref, gate_ref, wg, wu, wd, y_ref, r0, n):
    halves = [[], []]
    for c in range(WORDS):
        w = tile_ref[pl.ds(c * TILE_PITCH + r0, n), :]
        for k in range(2):
            halves[k].append(pltpu.unpack_elementwise(w, index=k, packed_dtype=BF16,
                                                      unpacked_dtype=F32))
    x = jnp.concatenate(halves[0] + halves[1], axis=-1).astype(BF16)
    lane = lax.broadcasted_iota(I32, (1, LANES), 1)
    gate = jnp.sum(jnp.where(lane == e, gate_ref[pl.ds(r0, n), :], 0.0), axis=-1, keepdims=True)
    hg = jnp.dot(x, wg, preferred_element_type=F32)
    hu = jnp.dot(x, wu, preferred_element_type=F32)
    hidden = (hg * jax.nn.sigmoid(hg) * hu).astype(BF16)
    y = jnp.dot(hidden, wd, preferred_element_type=F32) * gate
    for c in range(CHUNKS):
        y_ref[pl.ds(r0 * CHUNKS + c, n, stride=CHUNKS), :] = y[:, c * LANES:(c + 1) * LANES]


def _moe_body(idx_ref, u2p_ref, aff_ref, wg_ref, wu_ref, wd_ref, acc_ref,
              tile_a, tile_b, gate_a, gate_b, y_a, y_b):
    p = pl.program_id(0)
    e = pl.program_id(1)
    step = p * N_EXPERTS + e
    base = step * ROWS
    base_next = jnp.minimum(step + 1, N_PASS * N_EXPERTS - 1) * ROWS
    base_prev = jnp.where(e == 0, base, base - ROWS)

    @pl.when(e == 0)
    def _():
        acc_ref[...] = jnp.zeros_like(acc_ref)
        y_b[...] = jnp.zeros_like(y_b)

        def gather(c, _):
            _gather_rows(idx_ref, base, u2p_ref, aff_ref, tile_a, gate_a, c * SUBLANES, SUBLANES)
            return 0

        lax.fori_loop(0, ROWS // SUBLANES, gather, 0)

    def run(tile_cur, gate_cur, y_cur, tile_nxt, gate_nxt, y_prv):
        wg = wg_ref[0, 0].astype(BF16)
        wu = wu_ref[0, 0].astype(BF16)
        wd = wd_ref[0, 0].astype(BF16)
        for blk in range(ROWS // ROW_BLOCK):
            r0 = blk * ROW_BLOCK
            _gather_rows(idx_ref, base_next, u2p_ref, aff_ref, tile_nxt, gate_nxt, r0, ROW_BLOCK)
            _expert_rows(e, tile_cur, gate_cur, wg, wu, wd, y_cur, r0, ROW_BLOCK)
            _scatter_rows(idx_ref, base_prev, y_prv, acc_ref, r0, ROW_BLOCK)

    @pl.when(e % 2 == 0)
    def _():
        run(tile_a, gate_a, y_a, tile_b, gate_b, y_b)

    @pl.when(e % 2 == 1)
    def _():
        run(tile_b, gate_b, y_b, tile_a, gate_a, y_a)

    @pl.when(e == N_EXPERTS - 1)
    def _():
        def scatter(c, _):
            _scatter_rows(idx_ref, base, y_b, acc_ref, c * SCATTER_UNROLL, SCATTER_UNROLL)
            return 0

        lax.fori_loop(0, ROWS // SCATTER_UNROLL, scatter, 0)


_MOE_SCRATCH = [
    pltpu.VMEM((WORDS * TILE_PITCH, LANES), jnp.uint32),
    pltpu.VMEM((WORDS * TILE_PITCH, LANES), jnp.uint32),
    pltpu.VMEM((ROWS, LANES), F32),
    pltpu.VMEM((ROWS, LANES), F32),
    pltpu.VMEM((ROWS * CHUNKS, LANES), F32),
    pltpu.VMEM((ROWS * CHUNKS, LANES), F32),
]


def _moe(idx_flat, u2p, aff, w_gate, w_up, w_down, layer):
    one = pl.Buffered(1)
    grid_spec = pltpu.PrefetchScalarGridSpec(
        num_scalar_prefetch=1,
        grid=(N_PASS, N_EXPERTS),
        in_specs=[
            pl.BlockSpec((PASS_TOK * WORDS, LANES), lambda p, e, idx: (p, 0)),
            pl.BlockSpec((PASS_TOK, LANES), lambda p, e, idx: (p, 0), pipeline_mode=one),
            pl.BlockSpec((1, 1, D_MODEL, EXPERT_DIM), lambda p, e, idx: (layer, e, 0, 0)),
            pl.BlockSpec((1, 1, D_MODEL, EXPERT_DIM), lambda p, e, idx: (layer, e, 0, 0)),
            pl.BlockSpec((1, 1, EXPERT_DIM, D_MODEL), lambda p, e, idx: (layer, e, 0, 0)),
        ],
        out_specs=pl.BlockSpec((PASS_TOK * CHUNKS, LANES), lambda p, e, idx: (p, 0),
                               pipeline_mode=one),
        scratch_shapes=_MOE_SCRATCH,
    )
    return pl.pallas_call(
        _moe_body,
        grid_spec=grid_spec,
        out_shape=jax.ShapeDtypeStruct((N_TOK * CHUNKS, LANES), F32),
        compiler_params=_params(2, arbitrary=True),
        name="moe",
    )(idx_flat, u2p, aff, w_gate, w_up, w_down)


def _final_body(x1_ref, ff_ref, g2_ref, lg_ref, lb_ref, o_ref):
    ffn = jnp.concatenate([ff_ref[pl.ds(c, TM, stride=CHUNKS), :] for c in range(CHUNKS)], axis=-1)
    o_ref[...] = _layer_norm(ALPHA * x1_ref[...] + g2_ref[0] * ffn, lg_ref[...], lb_ref[...])


def _final(x1, ff, mod3, ln_g, ln_b, row0=0, n_rows=N_TOK):
    t0 = row0 // TM
    row = lambda i: (i + t0, 0)
    return pl.pallas_call(
        _final_body,
        grid=(n_rows // TM,),
        in_specs=[
            pl.BlockSpec((TM, D_MODEL), row),
            pl.BlockSpec((TM * CHUNKS, LANES), row),
            pl.BlockSpec((1, 1, D_MODEL), lambda i: (_mod_row(i + t0, TM), 0, 5)),
            _full((1, D_MODEL)), _full((1, D_MODEL)),
        ],
        out_specs=pl.BlockSpec((TM, D_MODEL), lambda i: (i, 0)),
        out_shape=jax.ShapeDtypeStruct((n_rows, D_MODEL), F32),
        compiler_params=_params(1),
        name="final_ln",
    )(x1, ff, mod3, ln_g, ln_b)


def _rot_cols(w):
    half = ROPE // 2
    return jnp.concatenate([-w[..., half:], w[..., :half]], axis=-1)


def _pad_lanes(w):
    pad = [(0, 0)] * (w.ndim - 1) + [(0, LANES - w.shape[-1])]
    return jnp.pad(w, pad)


def _rope_tables():
    rows_n = LAT_T // GRID_W
    r, cl = jnp.meshgrid(jnp.arange(rows_n, dtype=F32), jnp.arange(GRID_W, dtype=F32), indexing="ij")
    inv = ROPE_THETA ** (-jnp.arange(0, ROPE // 2, 2, dtype=F32) / (ROPE // 2))
    ang = jnp.concatenate([r.reshape(-1)[:, None] * inv, cl.reshape(-1)[:, None] * inv], axis=-1)
    cos, sin = jnp.cos(ang), jnp.sin(ang)
    cos_lat = _pad_lanes(jnp.concatenate([cos, cos], axis=-1))
    sin_lat = _pad_lanes(jnp.concatenate([sin, sin], axis=-1))
    cos_ctx = _pad_lanes(jnp.ones((N_CTX, ROPE), F32))
    sin_ctx = jnp.zeros((N_CTX, LANES), F32)
    cos_a = jnp.concatenate([cos_ctx, cos_lat], axis=0)
    sin_a = jnp.concatenate([sin_ctx, sin_lat], axis=0)
    return cos_a, sin_a


def kernel(x_prompt, x_sample, cache_ckv, cache_krope, c, c_ctx, w_in, q_norm, w_uq, kv_norm, w_uk, w_uv,
           pool_w, pool_scale, conv_w, w_out, w_ada, b_ada, ln1_g, ln1_b, ln2_g, ln2_b, w_router,
           w_gate, w_up, w_down):
    w_in_ext = _prep_win(w_in)
    hw = NOPE + ROPE
    uq_nope = [w_uq[:, :, h * hw:h * hw + NOPE] for h in range(HEADS)]
    uq_rope = [w_uq[:, :, h * hw + NOPE:(h + 1) * hw] for h in range(HEADS)]
    w_uq_ext = jnp.concatenate(
        uq_nope + [blk for w in uq_rope for blk in (w, _rot_cols(w))], axis=-1).astype(BF16)
    w_uk_b = w_uk.astype(BF16)
    w_uv_b = w_uv.astype(BF16)
    eye = jnp.eye(len(POOL_WINDOWS), dtype=F32)
    pool_bd = (pool_w[:, :, :, None, :] * eye[None, :, None, :, None]).reshape(DEPTH, MIXW, MIXW).astype(BF16)
    w_out_b = w_out.astype(BF16)
    w_router_pad = _pad_lanes(w_router).astype(BF16)
    cos_a, sin_a = _rope_tables()

    cond = jnp.concatenate([c_ctx[None, :], c, jnp.zeros((SUBLANES - 1 - LAT_B, D_MODEL), F32)], axis=0)
    mod = _ada(cond, w_ada, b_ada)
    kc, vc = _cachekv(cache_ckv, _pad_lanes(cache_krope), w_uk_b, w_uv_b)

    prev = (x_prompt.reshape(N_CTX, D_MODEL), x_sample.reshape(N_LAT, D_MODEL))
    xs = prev + (0,)
    ckv_layers, kr_layers = [], []
    ctx_off = (jnp.arange(CTX_B, dtype=I32) * CTX_T)[:, None, None]
    for l in range(DEPTH):
        mod3 = mod[l].reshape(SUBLANES, 1, 6 * D_MODEL)
        outs = _front(prev, mod3, l, w_in_ext, q_norm[l][None], kv_norm[l][None],
                      w_uq_ext[l], w_uk_b[l], w_uv_b[l], cos_a, sin_a,
                      pool_bd[l], pool_scale[l][None], conv_w[l])
        ckv, kr, q, k, v, bc = outs[:6]
        if l > 0:
            xs = (outs[6], outs[6], N_CTX)
        ckv_layers.append(ckv[:N_CTX].reshape(CTX_B, CTX_T, KV_RANK))
        kr_layers.append(kr[:N_CTX, :ROPE].reshape(CTX_B, CTX_T, ROPE))
        a_ctx = _attn_ctx(q, k, v)
        a_lat = _attn_lat(q, k, v, kc, vc, l)
        x1, u2p, aff = _postmix(a_ctx, a_lat, bc, *xs, mod3, l, w_out_b,
                                ln1_g[l][None], ln1_b[l][None], w_router_pad[l])
        idx_ctx = _route_ctx(aff)
        cnt_lat, cend = _route_count(aff, N_CTX // LAT_T, LAT_B, LAT_T, LAT_CAP)
        cend_flat = cend[:, :, :N_EXPERTS].transpose(0, 2, 1).reshape(-1)
        idx_lat = _route_rank(cend_flat, cnt_lat, LAT_B, LAT_T, LAT_CAP)
        idx_lat = idx_lat[:, :, :N_EXPERTS].transpose(0, 2, 1)
        idx_ctx = (idx_ctx + ctx_off).transpose(1, 0, 2).reshape(1, N_EXPERTS, ROWS)
        idx_flat = jnp.concatenate([idx_ctx, idx_lat], axis=0).reshape(-1)
        ff = _moe(idx_flat, u2p, aff, w_gate, w_up, w_down, l)
        if l + 1 < DEPTH:
            prev = (x1, ff, mod3, ln2_g[l][None], ln2_b[l][None])
        else:
            y_ctx = _final(x1, ff, mod3, ln2_g[l][None], ln2_b[l][None], 0, N_CTX)
            y_lat = _final(x1, ff, mod3, ln2_g[l][None], ln2_b[l][None], N_CTX, N_LAT)

    y_prompt = y_ctx.reshape(CTX_B, CTX_T, D_MODEL)
    y_sample = y_lat.reshape(LAT_B, LAT_T, D_MODEL)
    new_ckv = jnp.stack(ckv_layers, axis=1)
    new_krope = jnp.stack(kr_layers, axis=1)
    return (y_prompt, y_sample, new_ckv, new_krope)
```

```python
import functools

import jax
import jax.numpy as jnp
from jax import lax
from jax.experimental import pallas as pl
from jax.experimental.pallas import tpu as pltpu

F32 = jnp.float32
BF16 = jnp.bfloat16
I32 = jnp.int32

D_MODEL = 1024
CTX_B, CTX_T = 16, 256
LAT_B, LAT_T = 2, 4096
DEPTH = 4
PAST_LEN = 512
GRID_W = 64
HEADS = 4
NOPE, ROPE, VDIM = 128, 64, 128
Q_RANK, KV_RANK = 384, 256
POOL_WINDOWS = (2, 4, 8, 16)
POOL_GROUP = 64
MIXW = 256
N_EXPERTS = 16
EXPERT_DIM = 512
ROPE_THETA = 10000.0
ATTN_SCALE = (NOPE + ROPE) ** -0.5
LOG2E = 1.4426950408889634
Q_SCALE = ATTN_SCALE * LOG2E
ALPHA = (2 * DEPTH) ** 0.25
RMS_EPS = 1e-6
LN_EPS = 1e-5

N_CTX = CTX_B * CTX_T
N_LAT = LAT_B * LAT_T
N_TOK = N_CTX + N_LAT
CTX_CAP = 2 * CTX_T // N_EXPERTS
LAT_CAP = 2 * LAT_T // N_EXPERTS

LANES = 128
SUBLANES = 8
CHUNKS = D_MODEL // LANES
HEAD_BLOCK = 2 * LANES
VMEM_LIMIT = 56 * 1024 * 1024

TM = 512
SEQ_TILE = 256
HALO = 8
TQ = 512
KV_CHUNK = 512
CHUNK_UNROLL = 8
PASS_TOK = 4096
N_PASS = N_TOK // PASS_TOK
ROWS = 512
TILE_PITCH = ROWS + 8
SCATTER_UNROLL = 8
WORDS = CHUNKS // 2
ROW_BLOCK = 512
POST_BLOCK = 256


def _params(n_axes, arbitrary=False):
    sem = ("arbitrary" if arbitrary else "parallel",) * n_axes
    return pltpu.CompilerParams(dimension_semantics=sem, vmem_limit_bytes=VMEM_LIMIT)


def _mod_row(i, tile):
    start = i * tile
    return jnp.where(start < N_CTX, 0, 1 + (start - N_CTX) // LAT_T)


def _full(shape):
    nd = len(shape)
    return pl.BlockSpec(shape, lambda *_: (0,) * nd)


def _pair_specs(tile, width, lat_row0):
    nct = N_CTX // tile
    off = lat_row0 // tile
    return [pl.BlockSpec((tile, width), lambda i: (jnp.minimum(i, nct - 1), 0)),
            pl.BlockSpec((tile, width), lambda i: (jnp.maximum(i - nct, 0) + off, 0))]


def _pair_load(tile, ctx_ref, lat_ref):
    return jnp.where(pl.program_id(0) < N_CTX // tile, ctx_ref[...], lat_ref[...])


ADA_TN = 1536


def _ada_body(cond_ref, w_ref, b_ref, o_ref):
    c = cond_ref[...]
    a = c * jax.nn.sigmoid(c)
    a_hi = a.astype(BF16)
    a_lo = (a - a_hi.astype(F32)).astype(BF16)
    w = w_ref[0]
    w_hi = w.astype(BF16)
    w_lo = (w - w_hi.astype(F32)).astype(BF16)
    acc = jnp.dot(a_hi, w_hi, preferred_element_type=F32)
    acc += jnp.dot(a_lo, w_hi, preferred_element_type=F32)
    acc += jnp.dot(a_hi, w_lo, preferred_element_type=F32)
    o_ref[0] = acc + b_ref[0]


def _ada(cond, w_ada, b_ada):
    n = 6 * D_MODEL
    return pl.pallas_call(
        _ada_body,
        grid=(DEPTH, n // ADA_TN),
        in_specs=[
            _full((SUBLANES, D_MODEL)),
            pl.BlockSpec((1, D_MODEL, ADA_TN), lambda l, j: (l, 0, j)),
            pl.BlockSpec((1, 1, ADA_TN), lambda l, j: (l, 0, j)),
        ],
        out_specs=pl.BlockSpec((1, SUBLANES, ADA_TN), lambda l, j: (l, 0, j)),
        out_shape=jax.ShapeDtypeStruct((DEPTH, SUBLANES, n), F32),
        compiler_params=_params(2),
        name="ada_mod",
    )(cond, w_ada, b_ada.reshape(DEPTH, 1, n))


D_IN = Q_RANK + KV_RANK + ROPE + 4 * MIXW
IN_EXT = Q_RANK + KV_RANK + 4 * MIXW + LANES
UQ_EXT = HEADS * NOPE + HEADS * LANES


def _prep_win_body(wt_ref, o_ref):
    kr0 = Q_RANK + KV_RANK
    head = wt_ref[0, :kr0, :].T
    mix = wt_ref[0, kr0 + ROPE:, :].T
    kr_rows = jnp.concatenate([wt_ref[0, kr0:kr0 + ROPE, :],
                               jnp.zeros((LANES - ROPE, D_MODEL), F32)], axis=0)
    kr = kr_rows.T[:, :ROPE]
    half = ROPE // 2
    rot = jnp.concatenate([-kr[:, half:], kr[:, :half]], axis=-1)
    o_ref[0] = jnp.concatenate([head, mix, kr, rot], axis=-1).astype(BF16)


def _rope_block(blk, cos_a, sin_a):
    return blk * cos_a + pltpu.roll(blk, ROPE, axis=1) * sin_a


def _prep_win(w_in):
    lyr = lambda l: (l, 0, 0)
    return pl.pallas_call(
        _prep_win_body,
        grid=(DEPTH,),
        in_specs=[pl.BlockSpec((1, D_IN, D_MODEL), lyr)],
        out_specs=pl.BlockSpec((1, D_MODEL, IN_EXT), lyr),
        out_shape=jax.ShapeDtypeStruct((DEPTH, D_MODEL, IN_EXT), BF16),
        compiler_params=_params(1),
        name="prep_w_in",
    )(jnp.swapaxes(w_in, 1, 2))


def _rms(x, g):
    return x * lax.rsqrt(jnp.mean(x * x, axis=-1, keepdims=True) + RMS_EPS) * g


MIX_COL0 = Q_RANK + KV_RANK


def _ffn_rows(ff_ref, n, tok0=0):
    return jnp.concatenate([ff_ref[pl.ds(tok0 * CHUNKS + c, n, stride=CHUNKS), :]
                            for c in range(CHUNKS)], axis=-1)


def _front_body(*refs, fused_ln):
    if fused_ln:
        (x1_ref, ff_ref, x1p_ref, x1n_ref, ffp_ref, ffn_ref, g2_ref, lg_ref, lb_ref,
         sh_ref, sc_ref, win_ref, qn_ref, kvn_ref, wuq_ref, wuk_ref, wuv_ref, cos_ref, sin_ref,
         pw_ref, ps_ref, cw_ref,
         ckv_ref, kr_ref, q_ref, k_ref, v_ref, bc_ref, x_ref) = refs

        def norm2(x1, ffn):
            return _layer_norm(ALPHA * x1 + g2_ref[0] * ffn, lg_ref[...], lb_ref[...])

    else:
        (xc_ref, xl_ref, xp_ref, xn_ref,
         sh_ref, sc_ref, win_ref, qn_ref, kvn_ref, wuq_ref, wuk_ref, wuv_ref, cos_ref, sin_ref,
         pw_ref, ps_ref, cw_ref,
         ckv_ref, kr_ref, q_ref, k_ref, v_ref, bc_ref) = refs
        is_ctx = pl.program_id(0) < N_CTX // TM

    n_sub = TM // SEQ_TILE
    subs = [slice(s * SEQ_TILE, (s + 1) * SEQ_TILE) for s in range(n_sub)]

    xs = []
    for s, rows in enumerate(subs):
        if fused_ln:
            x = norm2(x1_ref[rows, :], _ffn_rows(ff_ref, SEQ_TILE, s * SEQ_TILE))
            x_ref[rows, :] = x
        else:
            x = jnp.where(is_ctx, xc_ref[rows, :], xl_ref[rows, :])
        xs.append(x)
    if fused_ln:
        x_halo = [norm2(x1p_ref[...], _ffn_rows(ffp_ref, HALO)),
                  norm2(x1n_ref[...], _ffn_rows(ffn_ref, HALO))]
    else:
        x_halo = [xp_ref[...], xn_ref[...]]
    xs[-1] = jnp.concatenate([xs[-1]] + x_halo, axis=0)

    hs = []
    for x in xs:
        u = x * (1.0 + sc_ref[0]) + sh_ref[0]
        hs.append(jnp.dot(u.astype(BF16), win_ref[0], preferred_element_type=F32))
    mix_cols = slice(MIX_COL0, MIX_COL0 + 4 * MIXW)
    mix_halo = hs[-1][SEQ_TILE:, mix_cols]
    mixes = [h[:SEQ_TILE, mix_cols] for h in hs]

    for s, rows in enumerate(subs):
        prev = mix_halo[:HALO] if s == 0 else mixes[s - 1][SEQ_TILE - HALO:]
        nxt = mix_halo[HALO:] if s == n_sub - 1 else mixes[s + 1][:HALO]
        bc_ref[rows, :] = _seqmix_math(pl.program_id(0) * n_sub + s, mixes[s], prev, nxt,
                                       pw_ref[...], ps_ref[...], cw_ref[...])

    kr0 = MIX_COL0 + 4 * MIXW
    ckv_bs = []
    for s, rows in enumerate(subs):
        ckv = _rms(hs[s][:SEQ_TILE, Q_RANK:Q_RANK + KV_RANK], kvn_ref[...])
        ckv_ref[rows, :] = ckv
        ckv_bs.append(ckv.astype(BF16))
    for s, rows in enumerate(subs):
        k_nope = jnp.dot(ckv_bs[s], wuk_ref[...], preferred_element_type=F32)
        v_ref[rows, :] = jnp.dot(ckv_bs[s], wuv_ref[...], preferred_element_type=F32).astype(BF16)
        kr_blk = hs[s][:SEQ_TILE, kr0:kr0 + LANES]
        kr_ref[rows, :] = kr_blk
        k_rope = _rope_block(kr_blk, cos_ref[rows, :], sin_ref[rows, :])
        parts = []
        for hd in range(HEADS):
            parts += [k_nope[:, hd * NOPE:(hd + 1) * NOPE], k_rope]
        k_ref[rows, :] = jnp.concatenate(parts, axis=-1).astype(BF16)
    qns = [_rms(hs[s][:SEQ_TILE, :Q_RANK], qn_ref[...]).astype(BF16) for s in range(n_sub)]
    ra = HEADS * NOPE
    for s, rows in enumerate(subs):
        qq = jnp.dot(qns[s], wuq_ref[...], preferred_element_type=F32)
        parts = []
        for hd in range(HEADS):
            q_rope = _rope_block(qq[:, ra + hd * LANES:ra + (hd + 1) * LANES],
                                 cos_ref[rows, :], sin_ref[rows, :])
            parts += [qq[:, hd * NOPE:(hd + 1) * NOPE] * Q_SCALE, q_rope * Q_SCALE]
        q_ref[rows, :] = jnp.concatenate(parts, axis=-1).astype(BF16)


def _halo_specs(rows_per_token, width, lat_row0, n_rows):
    nct = N_CTX // TM
    nblk = n_rows // HALO

    def first_block(i):
        return (jnp.maximum(i - nct, 0) * TM + lat_row0) // HALO

    shape = (HALO * rows_per_token, width)
    return [pl.BlockSpec(shape, lambda i: (jnp.maximum(first_block(i) - 1, 0), 0)),
            pl.BlockSpec(shape, lambda i: (jnp.minimum(first_block(i) + TM // HALO, nblk - 1), 0))]


def _front(prev, mod3, layer, w_in_ext, q_norm, kv_norm, w_uq_ext, w_uk, w_uv, cos_a, sin_a,
           pool_bd, pool_scale, conv_w):
    row = lambda i: (i, 0)
    lyr = lambda i: (layer, 0, 0)
    nct = N_CTX // TM
    rope_row = lambda i: (jnp.where(i < nct, i, nct + (i - nct) % (LAT_T // TM)), 0)
    modspec = lambda k: pl.BlockSpec((1, 1, D_MODEL), lambda i: (_mod_row(i, TM), 0, k))
    fused_ln = len(prev) == 5
    outs = [
        jax.ShapeDtypeStruct((N_TOK, KV_RANK), F32),
        jax.ShapeDtypeStruct((N_TOK, LANES), F32),
        jax.ShapeDtypeStruct((N_TOK, HEADS * HEAD_BLOCK), BF16),
        jax.ShapeDtypeStruct((N_TOK, HEADS * HEAD_BLOCK), BF16),
        jax.ShapeDtypeStruct((N_TOK, HEADS * VDIM), BF16),
        jax.ShapeDtypeStruct((N_TOK, 2 * MIXW), BF16),
    ]
    if fused_ln:
        x1, ff, mod3_prev, ln_g, ln_b = prev
        outs.append(jax.ShapeDtypeStruct((N_TOK, D_MODEL), F32))
        lead_specs = ([pl.BlockSpec((TM, D_MODEL), row), pl.BlockSpec((TM * CHUNKS, LANES), row)]
                      + _halo_specs(1, D_MODEL, N_CTX, N_TOK)
                      + _halo_specs(CHUNKS, LANES, N_CTX, N_TOK)
                      + [pl.BlockSpec((1, 1, D_MODEL), lambda i: (_mod_row(i, TM), 0, 5)),
                         _full((1, D_MODEL)), _full((1, D_MODEL))])
        lead_args = (x1, ff, x1, x1, ff, ff, mod3_prev, ln_g, ln_b)
    else:
        xc, xl = prev
        lead_specs = _pair_specs(TM, D_MODEL, 0) + _halo_specs(1, D_MODEL, 0, N_LAT)
        lead_args = (xc, xl, xl, xl)
    return pl.pallas_call(
        functools.partial(_front_body, fused_ln=fused_ln),
        grid=(N_TOK // TM,),
        in_specs=lead_specs + [
            modspec(0), modspec(1),
            pl.BlockSpec((1, D_MODEL, IN_EXT), lyr),
            _full((1, Q_RANK)), _full((1, KV_RANK)),
            _full((Q_RANK, UQ_EXT)),
            _full((KV_RANK, HEADS * NOPE)), _full((KV_RANK, HEADS * VDIM)),
            pl.BlockSpec((TM, LANES), rope_row), pl.BlockSpec((TM, LANES), rope_row),
            _full((MIXW, MIXW)), _full((1, MIXW)), _full((3, MIXW)),
        ],
        out_specs=tuple(pl.BlockSpec((TM, s.shape[1]), row) for s in outs),
        out_shape=tuple(outs),
        compiler_params=_params(1),
        name="front_ln" if fused_ln else "front",
    )(*lead_args, mod3, mod3, w_in_ext, q_norm, kv_norm, w_uq_ext, w_uk, w_uv, cos_a, sin_a,
      pool_bd, pool_scale, conv_w)


def _cachekv_body(ckv_ref, kr_ref, wuk_ref, wuv_ref, k_ref, v_ref):
    c = ckv_ref[0, 0].astype(BF16)
    k_nope = jnp.dot(c, wuk_ref[0], preferred_element_type=F32)
    v_ref[0, 0] = jnp.dot(c, wuv_ref[0], preferred_element_type=F32).astype(BF16)
    kr = kr_ref[0, 0]
    parts = []
    for hd in range(HEADS):
        parts += [k_nope[:, hd * NOPE:(hd + 1) * NOPE], kr]
    k_ref[0, 0] = jnp.concatenate(parts, axis=-1).astype(BF16)


def _cachekv(cache_ckv, kr_pad, w_uk, w_uv):
    bl = lambda b, l: (b, l, 0, 0)
    wl = lambda b, l: (l, 0, 0)
    return pl.pallas_call(
        _cachekv_body,
        grid=(LAT_B, DEPTH),
        in_specs=[
            pl.BlockSpec((1, 1, PAST_LEN, KV_RANK), bl),
            pl.BlockSpec((1, 1, PAST_LEN, LANES), bl),
            pl.BlockSpec((1, KV_RANK, HEADS * NOPE), wl),
            pl.BlockSpec((1, KV_RANK, HEADS * VDIM), wl),
        ],
        out_specs=(pl.BlockSpec((1, 1, PAST_LEN, HEADS * HEAD_BLOCK), bl),
                   pl.BlockSpec((1, 1, PAST_LEN, HEADS * VDIM), bl)),
        out_shape=(jax.ShapeDtypeStruct((LAT_B, DEPTH, PAST_LEN, HEADS * HEAD_BLOCK), BF16),
                   jax.ShapeDtypeStruct((LAT_B, DEPTH, PAST_LEN, HEADS * VDIM), BF16)),
        compiler_params=_params(2),
        name="cache_kv",
    )(cache_ckv, kr_pad, w_uk, w_uv)


EXT = SEQ_TILE + 2 * HALO


def _shift_up(x, k):
    return pltpu.roll(x, x.shape[0] - k, axis=0)


def _shift_down(x, k):
    return pltpu.roll(x, k, axis=0)


def _seqmix_math(j, main, prev, nxt, pw, ps, cw):
    n_ctx_tiles = N_CTX // SEQ_TILE
    tiles_per_lat = LAT_T // SEQ_TILE
    is_ctx = j < n_ctx_tiles
    jj = jnp.where(is_ctx, 0, (j - n_ctx_tiles) % tiles_per_lat)
    first = jj == 0
    last = jnp.where(is_ctx, True, jj == tiles_per_lat - 1)
    t_seq = jnp.where(is_ctx, CTX_T, LAT_T)

    prev = jnp.where(first, 0.0, prev)
    nxt = jnp.where(last, 0.0, nxt)
    ext = jnp.concatenate([prev, main, nxt], axis=0)
    p = ext[:, 0:MIXW]
    g_b = main[:, MIXW:2 * MIXW]
    g_c = ext[:, 2 * MIXW:3 * MIXW]
    h_in = ext[:, 3 * MIXW:4 * MIXW]

    sums = {}
    b = p
    for w in POOL_WINDOWS:
        b = b + _shift_up(b, w // 2)
        sums[w] = _shift_down(b, w // 2)[HALO:HALO + SEQ_TILE]
    tpos = jj * SEQ_TILE + lax.broadcasted_iota(I32, (SEQ_TILE, 1), 0)
    lane = lax.broadcasted_iota(I32, (1, MIXW), 1)
    num = None
    den = None
    for g, w in enumerate(POOL_WINDOWS):
        lo = jnp.maximum(tpos - w // 2, 0)
        hi = jnp.minimum(tpos + w - w // 2, t_seq)
        cnt = (hi - lo).astype(F32)
        if num is None:
            num, den = sums[w], jnp.broadcast_to(cnt, (SEQ_TILE, MIXW))
        else:
            sel = lane >= g * POOL_GROUP
            num = jnp.where(sel, sums[w], num)
            den = jnp.where(sel, cnt, den)
    pooled = num / den - p[HALO:HALO + SEQ_TILE]
    b_out = jnp.dot(pooled.astype(BF16), pw, preferred_element_type=F32) * ps

    y = g_c * h_in
    conv = (_shift_down(y, 1) * cw[0:1] + y * cw[1:2] + _shift_up(y, 1) * cw[2:3])
    c_out = g_b * conv[HALO:HALO + SEQ_TILE]
    return jnp.concatenate([b_out, c_out], axis=-1).astype(BF16)


_NT = (((1,), (1,)), ((), ()))


def _v_ext(v):
    return jnp.concatenate([v, jnp.ones(v.shape, v.dtype)], axis=-1)


CTX_PER_STEP = 2


def _attn_ctx_body(q_ref, k_ref, v_ref, o_ref):
    chains = [(slice(r * CTX_T, (r + 1) * CTX_T), hd)
              for r in range(CTX_PER_STEP) for hd in range(HEADS)]
    scores = []
    for rows, hd in chains:
        cols = slice(hd * HEAD_BLOCK, (hd + 1) * HEAD_BLOCK)
        scores.append(lax.dot_general(q_ref[rows, cols], k_ref[rows, cols], _NT,
                                      preferred_element_type=F32))
    probs = [jnp.exp2(s - jnp.max(s, axis=-1, keepdims=True)).astype(BF16) for s in scores]
    for (rows, hd), p in zip(chains, probs):
        vcols = slice(hd * VDIM, (hd + 1) * VDIM)
        acc = jnp.dot(p, _v_ext(v_ref[rows, vcols]), preferred_element_type=F32)
        o_ref[rows, vcols] = (acc[:, :VDIM] / acc[:, VDIM:]).astype(BF16)


def _attn_ctx(q, k, v):
    row = lambda b: (b, 0)
    tile = CTX_PER_STEP * CTX_T
    return pl.pallas_call(
        _attn_ctx_body,
        grid=(CTX_B // CTX_PER_STEP,),
        in_specs=[pl.BlockSpec((tile, HEADS * HEAD_BLOCK), row),
                  pl.BlockSpec((tile, HEADS * HEAD_BLOCK), row),
                  pl.BlockSpec((tile, HEADS * VDIM), row)],
        out_specs=pl.BlockSpec((tile, HEADS * VDIM), row),
        out_shape=jax.ShapeDtypeStruct((N_CTX, HEADS * VDIM), BF16),
        compiler_params=_params(1),
        name="attn_ctx",
    )(q, k, v)


def _attn_lat_body(q_ref, kc_ref, vc_ref, ko_ref, vo_ref, o_ref, m_ref, acc_ref):
    reps = KV_CHUNK // LANES

    def update(hd, k, v, first):
        qh = q_ref[:, hd * HEAD_BLOCK:(hd + 1) * HEAD_BLOCK]
        s = lax.dot_general(qh, k, _NT, preferred_element_type=F32)
        mx = jnp.max(s, axis=-1, keepdims=True)
        if first:
            m_new = jnp.broadcast_to(mx, (TQ, LANES))
        else:
            m_old = m_ref[hd]
            m_new = jnp.maximum(m_old, mx)
        p = jnp.exp2(s - jnp.concatenate([m_new] * reps, axis=-1))
        pv = jnp.dot(p.astype(BF16), _v_ext(v), preferred_element_type=F32)
        if first:
            acc_ref[hd] = pv
        else:
            a = jnp.exp2(m_old - m_new)
            acc_ref[hd] = acc_ref[hd] * jnp.concatenate([a, a], axis=-1) + pv
        m_ref[hd] = m_new

    def head_slices(hd):
        return slice(hd * HEAD_BLOCK, (hd + 1) * HEAD_BLOCK), slice(hd * VDIM, (hd + 1) * VDIM)

    for hd in range(HEADS):
        ks, vs = head_slices(hd)
        update(hd, kc_ref[0, 0, :, ks], vc_ref[0, 0, :, vs], True)

    def body(c, _):
        for j in range(CHUNK_UNROLL):
            rows = pl.ds(pl.multiple_of((c * CHUNK_UNROLL + j) * KV_CHUNK, KV_CHUNK), KV_CHUNK)
            for hd in range(HEADS):
                ks, vs = head_slices(hd)
                update(hd, ko_ref[rows, ks], vo_ref[rows, vs], False)
        return 0

    lax.fori_loop(0, LAT_T // (KV_CHUNK * CHUNK_UNROLL), body, 0)
    outs = [acc_ref[hd][:, :VDIM] / acc_ref[hd][:, VDIM:] for hd in range(HEADS)]
    o_ref[...] = jnp.concatenate(outs, axis=-1).astype(BF16)


def _attn_lat(q, k, v, kc, vc, layer):
    qt = LAT_T // TQ
    ctx_tiles = N_CTX // TQ
    ctx_blocks = N_CTX // LAT_T
    return pl.pallas_call(
        _attn_lat_body,
        grid=(LAT_B, qt),
        in_specs=[
            pl.BlockSpec((TQ, HEADS * HEAD_BLOCK), lambda b, i: (ctx_tiles + b * qt + i, 0)),
            pl.BlockSpec((1, 1, PAST_LEN, HEADS * HEAD_BLOCK), lambda b, i: (b, layer, 0, 0)),
            pl.BlockSpec((1, 1, PAST_LEN, HEADS * VDIM), lambda b, i: (b, layer, 0, 0)),
            pl.BlockSpec((LAT_T, HEADS * HEAD_BLOCK), lambda b, i: (ctx_blocks + b, 0)),
            pl.BlockSpec((LAT_T, HEADS * VDIM), lambda b, i: (ctx_blocks + b, 0)),
        ],
        out_specs=pl.BlockSpec((TQ, HEADS * VDIM), lambda b, i: (b * qt + i, 0)),
        out_shape=jax.ShapeDtypeStruct((N_LAT, HEADS * VDIM), BF16),
        scratch_shapes=[pltpu.VMEM((HEADS, TQ, LANES), F32),
                        pltpu.VMEM((HEADS, TQ, 2 * VDIM), F32)],
        compiler_params=_params(2),
        name="attn_lat",
    )(q, kc, vc, k, v)


def _layer_norm(y, g, b):
    mu = jnp.mean(y, axis=-1, keepdims=True)
    d = y - mu
    var = jnp.mean(d * d, axis=-1, keepdims=True)
    return d * lax.rsqrt(var + LN_EPS) * g + b


def _postmix_body(actx_ref, alat_ref, bc_ref, xc_ref, xl_ref, g1_ref, sh2_ref, sc2_ref, wa_ref, wbc_ref,
                  lg_ref, lb_ref, wr_ref, x1_ref, u2p_ref, aff_ref):
    is_ctx = pl.program_id(0) < N_CTX // TM
    lane = lax.broadcasted_iota(I32, (1, LANES), 1)
    blocks = [slice(b * POST_BLOCK, (b + 1) * POST_BLOCK) for b in range(TM // POST_BLOCK)]
    mixes = []
    for rows in blocks:
        a = jnp.where(is_ctx, actx_ref[rows, :], alat_ref[rows, :])
        mix = jnp.dot(a, wa_ref[0], preferred_element_type=F32)
        mixes.append(mix + jnp.dot(bc_ref[rows, :], wbc_ref[0], preferred_element_type=F32))
    u2s = []
    for b, rows in enumerate(blocks):
        x = jnp.where(is_ctx, xc_ref[rows, :], xl_ref[rows, :])
        x1 = _layer_norm(ALPHA * x + g1_ref[0] * mixes[b], lg_ref[...], lb_ref[...])
        x1_ref[rows, :] = x1
        u2 = x1 * (1.0 + sc2_ref[0]) + sh2_ref[0]
        for c in range(WORDS):
            lo = u2[:, c * LANES:(c + 1) * LANES]
            hi = u2[:, (c + WORDS) * LANES:(c + WORDS + 1) * LANES]
            u2p_ref[pl.ds(b * POST_BLOCK * WORDS + c, POST_BLOCK, stride=WORDS), :] = (
                pltpu.pack_elementwise([lo, hi], packed_dtype=BF16))
        u2s.append(u2.astype(BF16))
    for b, rows in enumerate(blocks):
        logits = jnp.dot(u2s[b], wr_ref[...], preferred_element_type=F32)
        logits = jnp.where(lane < N_EXPERTS, logits, -jnp.inf)
        m = jnp.max(logits, axis=-1, keepdims=True)
        e = jnp.exp(logits - m)
        aff_ref[rows, :] = e / jnp.sum(e, axis=-1, keepdims=True)


def _postmix(a_ctx, a_lat, bc, xc, xl, lat_row0, mod3, layer, w_out_b, ln_g, ln_b, w_router_pad):
    row = lambda i: (i, 0)
    modspec = lambda k: pl.BlockSpec((1, 1, D_MODEL), lambda i: (_mod_row(i, TM), 0, k))
    return pl.pallas_call(
        _postmix_body,
        grid=(N_TOK // TM,),
        in_specs=_pair_specs(TM, HEADS * VDIM, 0) + [
            pl.BlockSpec((TM, 2 * MIXW), row),
        ] + _pair_specs(TM, D_MODEL, lat_row0) + [
            modspec(2), modspec(3), modspec(4),
            pl.BlockSpec((1, HEADS * VDIM, D_MODEL), lambda i: (layer, 0, 0)),
            pl.BlockSpec((1, 2 * MIXW, D_MODEL), lambda i: (layer, 1, 0)),
            _full((1, D_MODEL)), _full((1, D_MODEL)),
            _full((D_MODEL, LANES)),
        ],
        out_specs=(pl.BlockSpec((TM, D_MODEL), row),
                   pl.BlockSpec((TM * WORDS, LANES), row),
                   pl.BlockSpec((TM, LANES), row)),
        out_shape=(jax.ShapeDtypeStruct((N_TOK, D_MODEL), F32),
                   jax.ShapeDtypeStruct((N_TOK * WORDS, LANES), jnp.uint32),
                   jax.ShapeDtypeStruct((N_TOK, LANES), F32)),
        compiler_params=_params(1),
        name="postmix",
    )(a_ctx, a_lat, bc, xc, xl, mod3, mod3, mod3, w_out_b, w_out_b, ln_g, ln_b, w_router_pad)


PREFIX_CHUNK = 256


def _route_ctx_body(aff_ref, idx_ref):
    n_rows = CTX_B * N_EXPERTS
    dense = jnp.concatenate(
        [aff_ref[b * CTX_T:(b + 1) * CTX_T, :].T[:N_EXPERTS] for b in range(CTX_B)], axis=0)

    def search(i, thr):
        cand = thr | jnp.left_shift(jnp.int32(1), 30 - i)
        n = jnp.sum((dense >= pltpu.bitcast(cand, F32)).astype(I32), axis=1, keepdims=True)
        return jnp.where(n >= CTX_CAP, cand, thr)

    thr_bits = lax.fori_loop(0, 31, search, jnp.zeros((n_rows, 1), I32))
    thr = pltpu.bitcast(thr_bits, F32)
    above = pltpu.bitcast(thr_bits + 1, F32)
    gt = (dense > thr) & (dense >= above)
    eq = (dense >= thr) & jnp.logical_not(gt)
    n_gt = jnp.sum(gt.astype(I32), axis=1, keepdims=True)
    need = (CTX_CAP - n_gt).astype(F32)

    ri = lax.broadcasted_iota(I32, (CTX_T, CTX_T), 0)
    ci = lax.broadcasted_iota(I32, (CTX_T, CTX_T), 1)
    tri = jnp.where(ri <= ci, 1.0, 0.0).astype(BF16)
    p_eq = jnp.dot(jnp.where(eq, 1.0, 0.0).astype(BF16), tri, preferred_element_type=F32)
    sel = jnp.where(gt, 1.0, jnp.where(eq & (p_eq <= need), 1.0, 0.0))
    cnt = jnp.dot(sel.astype(BF16), tri, preferred_element_type=F32)

    ones = jnp.ones((CTX_T, LANES), BF16)
    lane = lax.broadcasted_iota(I32, (1, LANES), 1)
    out = jnp.zeros((n_rows, LANES), F32)
    for r in range(CTX_CAP):
        le = jnp.where(cnt <= float(r), 1.0, 0.0).astype(BF16)
        out = jnp.where(lane == r, jnp.dot(le, ones, preferred_element_type=F32), out)
    idx_ref[...] = out.astype(I32)


def _route_ctx(aff):
    idx = pl.pallas_call(
        _route_ctx_body,
        grid=(1,),
        in_specs=[pl.BlockSpec((N_CTX, LANES), lambda i: (0, 0))],
        out_specs=_full((CTX_B * N_EXPERTS, LANES)),
        out_shape=jax.ShapeDtypeStruct((CTX_B * N_EXPERTS, LANES), I32),
        compiler_params=_params(1),
        name="route_ctx",
    )(aff)
    return idx[:, :CTX_CAP].reshape(CTX_B, N_EXPERTS, CTX_CAP)


RANK_TILE = LANES


def _expert_row(col):
    full = jnp.concatenate([jnp.broadcast_to(col, (N_EXPERTS, LANES)),
                            jnp.zeros((LANES - N_EXPERTS, LANES), col.dtype)], axis=0)
    return full.T[0:1]


def _route_count_body(aff_ref, cnt_ref, cend_ref, *, seq, cap):
    n_chunks = seq // PREFIX_CHUNK
    dense = aff_ref[...].T[:N_EXPERTS]

    def search(i, thr):
        cand = thr | jnp.left_shift(jnp.int32(1), 30 - i)
        n = jnp.sum((dense >= pltpu.bitcast(cand, F32)).astype(I32), axis=1, keepdims=True)
        return jnp.where(n >= cap, cand, thr)

    thr_bits = lax.fori_loop(0, 31, search, jnp.zeros((N_EXPERTS, 1), I32))
    thr_col = pltpu.bitcast(thr_bits, F32)
    above_col = pltpu.bitcast(thr_bits + 1, F32)
    n_gt = jnp.sum(((dense > thr_col) & (dense >= above_col)).astype(I32), axis=1, keepdims=True)
    thr = _expert_row(thr_col)
    above = _expert_row(above_col)
    need = _expert_row((cap - n_gt).astype(F32))

    ri = lax.broadcasted_iota(I32, (PREFIX_CHUNK, PREFIX_CHUNK), 0)
    ci = lax.broadcasted_iota(I32, (PREFIX_CHUNK, PREFIX_CHUNK), 1)
    tri = jnp.where(ci <= ri, 1.0, 0.0).astype(BF16)
    carry_eq = jnp.zeros((1, LANES), F32)
    carry_sel = jnp.zeros((1, LANES), F32)
    per = PREFIX_CHUNK // RANK_TILE
    for c in range(n_chunks):
        rows = slice(c * PREFIX_CHUNK, (c + 1) * PREFIX_CHUNK)
        a = aff_ref[rows, :]
        gt = (a > thr) & (a >= above)
        eq = (a >= thr) & jnp.logical_not(gt)
        p_eq = jnp.dot(tri, jnp.where(eq, 1.0, 0.0).astype(BF16),
                       preferred_element_type=F32) + carry_eq
        sel = jnp.where(gt, 1.0, jnp.where(eq & (p_eq <= need), 1.0, 0.0))
        p_sel = jnp.dot(tri, sel.astype(BF16), preferred_element_type=F32) + carry_sel
        dense_cnt = p_sel.T[:N_EXPERTS]
        for k in range(per):
            last = (k + 1) * RANK_TILE - 1
            cnt_ref[c * per + k] = dense_cnt[:, k * RANK_TILE:(k + 1) * RANK_TILE]
            cend_ref[0, c * per + k:c * per + k + 1, :] = p_sel[last:last + 1, :].astype(I32)
        carry_eq = p_eq[PREFIX_CHUNK - 1:PREFIX_CHUNK, :]
        carry_sel = p_sel[PREFIX_CHUNK - 1:PREFIX_CHUNK, :]


def _route_count(aff, first_block, n_req, seq, cap):
    n_tiles = seq // RANK_TILE
    return pl.pallas_call(
        functools.partial(_route_count_body, seq=seq, cap=cap),
        grid=(n_req,),
        in_specs=[pl.BlockSpec((seq, LANES), lambda b: (first_block + b, 0))],
        out_specs=(pl.BlockSpec((n_tiles, N_EXPERTS, RANK_TILE), lambda b: (b, 0, 0)),
                   pl.BlockSpec((1, n_tiles, LANES), lambda b: (b, 0, 0))),
        out_shape=(jax.ShapeDtypeStruct((n_req * n_tiles, N_EXPERTS, RANK_TILE), F32),
                   jax.ShapeDtypeStruct((n_req, n_tiles, LANES), I32)),
        compiler_params=_params(1),
        name=f"route_count_{seq}",
    )(aff)


def _route_rank_body(cend_ref, cnt_ref, idx_ref, *bufs, seq, cap):
    b = pl.program_id(0)
    n_tiles = seq // RANK_TILE
    rank = lax.broadcasted_iota(I32, (RANK_TILE, RANK_TILE), 0).astype(F32)
    ones = jnp.ones((RANK_TILE, LANES), BF16)

    def tile(j, _):
        counts = cnt_ref[j]
        for e in range(N_EXPERTS):
            prev = (b * N_EXPERTS + e) * n_tiles + j - 1
            start = jnp.where(j > 0, cend_ref[jnp.maximum(prev, 0)], 0)
            local = counts[e:e + 1, :] - jnp.asarray(start, F32)
            le = jnp.where(local <= rank, 1.0, 0.0).astype(BF16)
            pos = jnp.dot(le, ones, preferred_element_type=F32) + jnp.asarray(j * RANK_TILE, F32)
            bufs[e][pl.ds(start, RANK_TILE), :] = pos
        return 0

    lax.fori_loop(0, n_tiles, tile, 0)
    lane = lax.broadcasted_iota(I32, (1, LANES), 1)
    out = jnp.zeros((cap, LANES), F32)
    for e in range(N_EXPERTS):
        out = jnp.where(lane == e, bufs[e][0:cap, :], out)
    idx_ref[0] = out.astype(I32)


def _route_rank(cend_flat, cnt, n_req, seq, cap):
    n_tiles = seq // RANK_TILE
    grid_spec = pltpu.PrefetchScalarGridSpec(
        num_scalar_prefetch=1,
        grid=(n_req,),
        in_specs=[pl.BlockSpec((n_tiles, N_EXPERTS, RANK_TILE), lambda b, c: (b, 0, 0))],
        out_specs=pl.BlockSpec((1, cap, LANES), lambda b, c: (b, 0, 0)),
        scratch_shapes=[pltpu.VMEM((cap + RANK_TILE, LANES), F32)] * N_EXPERTS,
    )
    return pl.pallas_call(
        functools.partial(_route_rank_body, seq=seq, cap=cap),
        grid_spec=grid_spec,
        out_shape=jax.ShapeDtypeStruct((n_req, cap, LANES), I32),
        compiler_params=_params(1),
        name=f"route_rank_{seq}",
    )(cend_flat, cnt)


def _gather_rows(idx_ref, base, u2p_ref, aff_ref, tile_ref, gate_ref, r0, n):
    for i in range(n):
        r = r0 + i
        t = idx_ref[base + r]
        slab = u2p_ref[pl.ds(pl.multiple_of(t * WORDS, WORDS), WORDS), :]
        tile_ref[pl.ds(r, WORDS, stride=TILE_PITCH), :] = slab
        gate_ref[pl.ds(r, 1), :] = aff_ref[pl.ds(t, 1), :]


def _scatter_rows(idx_ref, base, y_ref, acc_ref, r0, n):
    for g0 in range(0, n, SCATTER_UNROLL):
        rows = [r0 + g0 + i for i in range(SCATTER_UNROLL)]
        dst = [pl.ds(pl.multiple_of(idx_ref[base + r] * CHUNKS, CHUNKS), CHUNKS) for r in rows]
        vals = [acc_ref[d, :] + y_ref[pl.ds(pl.multiple_of(r * CHUNKS, CHUNKS), CHUNKS), :]
                for d, r in zip(dst, rows)]
        for d, v in zip(dst, vals):
            acc_ref[d, :] = v


def _expert_rows(e, tile_ref, gate_ref, wg, wu, wd, y_ref, r0, n):
    halves = [[], []]
    for c in range(WORDS):
        w = tile_ref[pl.ds(c * TILE_PITCH + r0, n), :]
        for k in range(2):
            halves[k].append(pltpu.unpack_elementwise(w, index=k, packed_dtype=BF16,
                                                      unpacked_dtype=F32))
    x = jnp.concatenate(halves[0] + halves[1], axis=-1).astype(BF16)
    lane = lax.broadcasted_iota(I32, (1, LANES), 1)
    gate = jnp.sum(jnp.where(lane == e, gate_ref[pl.ds(r0, n), :], 0.0), axis=-1, keepdims=True)
    hg = jnp.dot(x, wg, preferred_element_type=F32)
    hu = jnp.dot(x, wu, preferred_element_type=F32)
    hidden = (hg * jax.nn.sigmoid(hg) * hu).astype(BF16)
    y = jnp.dot(hidden, wd, preferred_element_type=F32) * gate
    for c in range(CHUNKS):
        y_ref[pl.ds(r0 * CHUNKS + c, n, stride=CHUNKS), :] = y[:, c * LANES:(c + 1) * LANES]


def _moe_body(idx_ref, u2p_ref, aff_ref, wg_ref, wu_ref, wd_ref, acc_ref,
              tile_a, tile_b, gate_a, gate_b, y_a, y_b):
    p = pl.program_id(0)
    e = pl.program_id(1)
    step = p * N_EXPERTS + e
    base = step * ROWS
    base_next = jnp.minimum(step + 1, N_PASS * N_EXPERTS - 1) * ROWS
    base_prev = jnp.where(e == 0, base, base - ROWS)

    @pl.when(e == 0)
    def _():
        acc_ref[...] = jnp.zeros_like(acc_ref)
        y_b[...] = jnp.zeros_like(y_b)

        def gather(c, _):
            _gather_rows(idx_ref, base, u2p_ref, aff_ref, tile_a, gate_a, c * SUBLANES, SUBLANES)
            return 0

        lax.fori_loop(0, ROWS // SUBLANES, gather, 0)

    def run(tile_cur, gate_cur, y_cur, tile_nxt, gate_nxt, y_prv):
        wg = wg_ref[0, 0].astype(BF16)
        wu = wu_ref[0, 0].astype(BF16)
        wd = wd_ref[0, 0].astype(BF16)
        for blk in range(ROWS // ROW_BLOCK):
            r0 = blk * ROW_BLOCK
            _gather_rows(idx_ref, base_next, u2p_ref, aff_ref, tile_nxt, gate_nxt, r0, ROW_BLOCK)
            _expert_rows(e, tile_cur, gate_cur, wg, wu, wd, y_cur, r0, ROW_BLOCK)
            _scatter_rows(idx_ref, base_prev, y_prv, acc_ref, r0, ROW_BLOCK)

    @pl.when(e % 2 == 0)
    def _():
        run(tile_a, gate_a, y_a, tile_b, gate_b, y_b)

    @pl.when(e % 2 == 1)
    def _():
        run(tile_b, gate_b, y_b, tile_a, gate_a, y_a)

    @pl.when(e == N_EXPERTS - 1)
    def _():
        def scatter(c, _):
            _scatter_rows(idx_ref, base, y_b, acc_ref, c * SCATTER_UNROLL, SCATTER_UNROLL)
            return 0

        lax.fori_loop(0, ROWS // SCATTER_UNROLL, scatter, 0)


_MOE_SCRATCH = [
    pltpu.VMEM((WORDS * TILE_PITCH, LANES), jnp.uint32),
    pltpu.VMEM((WORDS * TILE_PITCH, LANES), jnp.uint32),
    pltpu.VMEM((ROWS, LANES), F32),
    pltpu.VMEM((ROWS, LANES), F32),
    pltpu.VMEM((ROWS * CHUNKS, LANES), F32),
    pltpu.VMEM((ROWS * CHUNKS, LANES), F32),
]


def _moe(idx_flat, u2p, aff, w_gate, w_up, w_down, layer):
    one = pl.Buffered(1)
    grid_spec = pltpu.PrefetchScalarGridSpec(
        num_scalar_prefetch=1,
        grid=(N_PASS, N_EXPERTS),
        in_specs=[
            pl.BlockSpec((PASS_TOK * WORDS, LANES), lambda p, e, idx: (p, 0)),
            pl.BlockSpec((PASS_TOK, LANES), lambda p, e, idx: (p, 0), pipeline_mode=one),
            pl.BlockSpec((1, 1, D_MODEL, EXPERT_DIM), lambda p, e, idx: (layer, e, 0, 0)),
            pl.BlockSpec((1, 1, D_MODEL, EXPERT_DIM), lambda p, e, idx: (layer, e, 0, 0)),
            pl.BlockSpec((1, 1, EXPERT_DIM, D_MODEL), lambda p, e, idx: (layer, e, 0, 0)),
        ],
        out_specs=pl.BlockSpec((PASS_TOK * CHUNKS, LANES), lambda p, e, idx: (p, 0),
                               pipeline_mode=one),
        scratch_shapes=_MOE_SCRATCH,
    )
    return pl.pallas_call(
        _moe_body,
        grid_spec=grid_spec,
        out_shape=jax.ShapeDtypeStruct((N_TOK * CHUNKS, LANES), F32),
        compiler_params=_params(2, arbitrary=True),
        name="moe",
    )(idx_flat, u2p, aff, w_gate, w_up, w_down)


def _final_body(x1_ref, ff_ref, g2_ref, lg_ref, lb_ref, o_ref):
    ffn = jnp.concatenate([ff_ref[pl.ds(c, TM, stride=CHUNKS), :] for c in range(CHUNKS)], axis=-1)
    o_ref[...] = _layer_norm(ALPHA * x1_ref[...] + g2_ref[0] * ffn, lg_ref[...], lb_ref[...])


def _final(x1, ff, mod3, ln_g, ln_b, row0=0, n_rows=N_TOK):
    t0 = row0 // TM
    row = lambda i: (i + t0, 0)
    return pl.pallas_call(
        _final_body,
        grid=(n_rows // TM,),
        in_specs=[
            pl.BlockSpec((TM, D_MODEL), row),
            pl.BlockSpec((TM * CHUNKS, LANES), row),
            pl.BlockSpec((1, 1, D_MODEL), lambda i: (_mod_row(i + t0, TM), 0, 5)),
            _full((1, D_MODEL)), _full((1, D_MODEL)),
        ],
        out_specs=pl.BlockSpec((TM, D_MODEL), lambda i: (i, 0)),
        out_shape=jax.ShapeDtypeStruct((n_rows, D_MODEL), F32),
        compiler_params=_params(1),
        name="final_ln",
    )(x1, ff, mod3, ln_g, ln_b)


def _rot_cols(w):
    half = ROPE // 2
    return jnp.concatenate([-w[..., half:], w[..., :half]], axis=-1)


def _pad_lanes(w):
    pad = [(0, 0)] * (w.ndim - 1) + [(0, LANES - w.shape[-1])]
    return jnp.pad(w, pad)


def _rope_tables():
    rows_n = LAT_T // GRID_W
    r, cl = jnp.meshgrid(jnp.arange(rows_n, dtype=F32), jnp.arange(GRID_W, dtype=F32), indexing="ij")
    inv = ROPE_THETA ** (-jnp.arange(0, ROPE // 2, 2, dtype=F32) / (ROPE // 2))
    ang = jnp.concatenate([r.reshape(-1)[:, None] * inv, cl.reshape(-1)[:, None] * inv], axis=-1)
    cos, sin = jnp.cos(ang), jnp.sin(ang)
    cos_lat = _pad_lanes(jnp.concatenate([cos, cos], axis=-1))
    sin_lat = _pad_lanes(jnp.concatenate([sin, sin], axis=-1))
    cos_ctx = _pad_lanes(jnp.ones((N_CTX, ROPE), F32))
    sin_ctx = jnp.zeros((N_CTX, LANES), F32)
    cos_a = jnp.concatenate([cos_ctx, cos_lat], axis=0)
    sin_a = jnp.concatenate([sin_ctx, sin_lat], axis=0)
    return cos_a, sin_a


def kernel(x_prompt, x_sample, cache_ckv, cache_krope, c, c_ctx, w_in, q_norm, w_uq, kv_norm, w_uk, w_uv,
           pool_w, pool_scale, conv_w, w_out, w_ada, b_ada, ln1_g, ln1_b, ln2_g, ln2_b, w_router,
           w_gate, w_up, w_down):
    w_in_ext = _prep_win(w_in)
    hw = NOPE + ROPE
    uq_nope = [w_uq[:, :, h * hw:h * hw + NOPE] for h in range(HEADS)]
    uq_rope = [w_uq[:, :, h * hw + NOPE:(h + 1) * hw] for h in range(HEADS)]
    w_uq_ext = jnp.concatenate(
        uq_nope + [blk for w in uq_rope for blk in (w, _rot_cols(w))], axis=-1).astype(BF16)
    w_uk_b = w_uk.astype(BF16)
    w_uv_b = w_uv.astype(BF16)
    eye = jnp.eye(len(POOL_WINDOWS), dtype=F32)
    pool_bd = (pool_w[:, :, :, None, :] * eye[None, :, None, :, None]).reshape(DEPTH, MIXW, MIXW).astype(BF16)
    w_out_b = w_out.astype(BF16)
    w_router_pad = _pad_lanes(w_router).astype(BF16)
    cos_a, sin_a = _rope_tables()

    cond = jnp.concatenate([c_ctx[None, :], c, jnp.zeros((SUBLANES - 1 - LAT_B, D_MODEL), F32)], axis=0)
    mod = _ada(cond, w_ada, b_ada)
    kc, vc = _cachekv(cache_ckv, _pad_lanes(cache_krope), w_uk_b, w_uv_b)

    prev = (x_prompt.reshape(N_CTX, D_MODEL), x_sample.reshape(N_LAT, D_MODEL))
    xs = prev + (0,)
    ckv_layers, kr_layers = [], []
    ctx_off = (jnp.arange(CTX_B, dtype=I32) * CTX_T)[:, None, None]
    for l in range(DEPTH):
        mod3 = mod[l].reshape(SUBLANES, 1, 6 * D_MODEL)
        outs = _front(prev, mod3, l, w_in_ext, q_norm[l][None], kv_norm[l][None],
                      w_uq_ext[l], w_uk_b[l], w_uv_b[l], cos_a, sin_a,
                      pool_bd[l], pool_scale[l][None], conv_w[l])
        ckv, kr, q, k, v, bc = outs[:6]
        if l > 0:
            xs = (outs[6], outs[6], N_CTX)
        ckv_layers.append(ckv[:N_CTX].reshape(CTX_B, CTX_T, KV_RANK))
        kr_layers.append(kr[:N_CTX, :ROPE].reshape(CTX_B, CTX_T, ROPE))
        a_ctx = _attn_ctx(q, k, v)
        a_lat = _attn_lat(q, k, v, kc, vc, l)
        x1, u2p, aff = _postmix(a_ctx, a_lat, bc, *xs, mod3, l, w_out_b,
                                ln1_g[l][None], ln1_b[l][None], w_router_pad[l])
        idx_ctx = _route_ctx(aff)
        cnt_lat, cend = _route_count(aff, N_CTX // LAT_T, LAT_B, LAT_T, LAT_CAP)
        cend_flat = cend[:, :, :N_EXPERTS].transpose(0, 2, 1).reshape(-1)
        idx_lat = _route_rank(cend_flat, cnt_lat, LAT_B, LAT_T, LAT_CAP)
        idx_lat = idx_lat[:, :, :N_EXPERTS].transpose(0, 2, 1)
        idx_ctx = (idx_ctx + ctx_off).transpose(1, 0, 2).reshape(1, N_EXPERTS, ROWS)
        idx_flat = jnp.concatenate([idx_ctx, idx_lat], axis=0).reshape(-1)
        ff = _moe(idx_flat, u2p, aff, w_gate, w_up, w_down, l)
        if l + 1 < DEPTH:
            prev = (x1, ff, mod3, ln2_g[l][None], ln2_b[l][None])
        else:
            y_ctx = _final(x1, ff, mod3, ln2_g[l][None], ln2_b[l][None], 0, N_CTX)
            y_lat = _final(x1, ff, mod3, ln2_g[l][None], ln2_b[l][None], N_CTX, N_LAT)

    y_prompt = y_ctx.reshape(CTX_B, CTX_T, D_MODEL)
    y_sample = y_lat.reshape(LAT_B, LAT_T, D_MODEL)
    new_ckv = jnp.stack(ckv_layers, axis=1)
    new_krope = jnp.stack(kr_layers, axis=1)
    return (y_prompt, y_sample, new_ckv, new_krope)
```

```python
import functools

import jax
import jax.numpy as jnp
from jax import lax
from jax.experimental import pallas as pl
from jax.experimental.pallas import tpu as pltpu

F32 = jnp.float32
BF16 = jnp.bfloat16
I32 = jnp.int32

D_MODEL = 1024
CTX_B, CTX_T = 16, 256
LAT_B, LAT_T = 2, 4096
DEPTH = 4
PAST_LEN = 512
GRID_W = 64
HEADS = 4
NOPE, ROPE, VDIM = 128, 64, 128
Q_RANK, KV_RANK = 384, 256
POOL_WINDOWS = (2, 4, 8, 16)
POOL_GROUP = 64
MIXW = 256
N_EXPERTS = 16
EXPERT_DIM = 512
ROPE_THETA = 10000.0
ATTN_SCALE = (NOPE + ROPE) ** -0.5
LOG2E = 1.4426950408889634
Q_SCALE = ATTN_SCALE * LOG2E
ALPHA = (2 * DEPTH) ** 0.25
RMS_EPS = 1e-6
LN_EPS = 1e-5

N_CTX = CTX_B * CTX_T
N_LAT = LAT_B * LAT_T
N_TOK = N_CTX + N_LAT
CTX_CAP = 2 * CTX_T // N_EXPERTS
LAT_CAP = 2 * LAT_T // N_EXPERTS

LANES = 128
SUBLANES = 8
CHUNKS = D_MODEL // LANES
HEAD_BLOCK = 2 * LANES
VMEM_LIMIT = 56 * 1024 * 1024

TM = 512
SEQ_TILE = 256
HALO = 8
TQ = 512
KV_CHUNK = 1024
CHUNK_UNROLL = 4
PASS_TOK = 4096
N_PASS = N_TOK // PASS_TOK
ROWS = 512
TILE_PITCH = ROWS + 8
SCATTER_UNROLL = 8
WORDS = CHUNKS // 2
ROW_BLOCK = 512
POST_BLOCK = 256


def _params(n_axes, arbitrary=False):
    sem = ("arbitrary" if arbitrary else "parallel",) * n_axes
    return pltpu.CompilerParams(dimension_semantics=sem, vmem_limit_bytes=VMEM_LIMIT)


def _mod_row(i, tile):
    start = i * tile
    return jnp.where(start < N_CTX, 0, 1 + (start - N_CTX) // LAT_T)


def _full(shape):
    nd = len(shape)
    return pl.BlockSpec(shape, lambda *_: (0,) * nd)


def _pair_specs(tile, width, lat_row0):
    nct = N_CTX // tile
    off = lat_row0 // tile
    return [pl.BlockSpec((tile, width), lambda i: (jnp.minimum(i, nct - 1), 0)),
            pl.BlockSpec((tile, width), lambda i: (jnp.maximum(i - nct, 0) + off, 0))]


def _pair_load(tile, ctx_ref, lat_ref):
    return jnp.where(pl.program_id(0) < N_CTX // tile, ctx_ref[...], lat_ref[...])


ADA_TN = 1536


def _ada_body(cond_ref, w_ref, b_ref, o_ref):
    c = cond_ref[...]
    a = c * jax.nn.sigmoid(c)
    a_hi = a.astype(BF16)
    a_lo = (a - a_hi.astype(F32)).astype(BF16)
    w = w_ref[0]
    w_hi = w.astype(BF16)
    w_lo = (w - w_hi.astype(F32)).astype(BF16)
    acc = jnp.dot(a_hi, w_hi, preferred_element_type=F32)
    acc += jnp.dot(a_lo, w_hi, preferred_element_type=F32)
    acc += jnp.dot(a_hi, w_lo, preferred_element_type=F32)
    o_ref[0] = acc + b_ref[0]


def _ada(cond, w_ada, b_ada):
    n = 6 * D_MODEL
    return pl.pallas_call(
        _ada_body,
        grid=(DEPTH, n // ADA_TN),
        in_specs=[
            _full((SUBLANES, D_MODEL)),
            pl.BlockSpec((1, D_MODEL, ADA_TN), lambda l, j: (l, 0, j)),
            pl.BlockSpec((1, 1, ADA_TN), lambda l, j: (l, 0, j)),
        ],
        out_specs=pl.BlockSpec((1, SUBLANES, ADA_TN), lambda l, j: (l, 0, j)),
        out_shape=jax.ShapeDtypeStruct((DEPTH, SUBLANES, n), F32),
        compiler_params=_params(2),
        name="ada_mod",
    )(cond, w_ada, b_ada.reshape(DEPTH, 1, n))


D_IN = Q_RANK + KV_RANK + ROPE + 4 * MIXW
IN_EXT = Q_RANK + KV_RANK + 4 * MIXW + LANES
UQ_EXT = HEADS * NOPE + HEADS * LANES


def _prep_win_body(wt_ref, o_ref):
    kr0 = Q_RANK + KV_RANK
    head = wt_ref[0, :kr0, :].T
    mix = wt_ref[0, kr0 + ROPE:, :].T
    kr_rows = jnp.concatenate([wt_ref[0, kr0:kr0 + ROPE, :],
                               jnp.zeros((LANES - ROPE, D_MODEL), F32)], axis=0)
    kr = kr_rows.T[:, :ROPE]
    half = ROPE // 2
    rot = jnp.concatenate([-kr[:, half:], kr[:, :half]], axis=-1)
    o_ref[0] = jnp.concatenate([head, mix, kr, rot], axis=-1).astype(BF16)


def _rope_block(blk, cos_a, sin_a):
    return blk * cos_a + pltpu.roll(blk, ROPE, axis=1) * sin_a


def _prep_win(w_in):
    lyr = lambda l: (l, 0, 0)
    return pl.pallas_call(
        _prep_win_body,
        grid=(DEPTH,),
        in_specs=[pl.BlockSpec((1, D_IN, D_MODEL), lyr)],
        out_specs=pl.BlockSpec((1, D_MODEL, IN_EXT), lyr),
        out_shape=jax.ShapeDtypeStruct((DEPTH, D_MODEL, IN_EXT), BF16),
        compiler_params=_params(1),
        name="prep_w_in",
    )(jnp.swapaxes(w_in, 1, 2))


def _rms(x, g):
    return x * lax.rsqrt(jnp.mean(x * x, axis=-1, keepdims=True) + RMS_EPS) * g


MIX_COL0 = Q_RANK + KV_RANK


def _ffn_rows(ff_ref, n, tok0=0):
    return jnp.concatenate([ff_ref[pl.ds(tok0 * CHUNKS + c, n, stride=CHUNKS), :]
                            for c in range(CHUNKS)], axis=-1)


def _front_body(*refs, fused_ln):
    if fused_ln:
        (x1_ref, ff_ref, x1p_ref, x1n_ref, ffp_ref, ffn_ref, g2_ref, lg_ref, lb_ref,
         sh_ref, sc_ref, win_ref, qn_ref, kvn_ref, wuq_ref, wuk_ref, wuv_ref, cos_ref, sin_ref,
         pw_ref, ps_ref, cw_ref,
         ckv_ref, kr_ref, q_ref, k_ref, v_ref, bc_ref, x_ref) = refs

        def norm2(x1, ffn):
            return _layer_norm(ALPHA * x1 + g2_ref[0] * ffn, lg_ref[...], lb_ref[...])

    else:
        (xc_ref, xl_ref, xp_ref, xn_ref,
         sh_ref, sc_ref, win_ref, qn_ref, kvn_ref, wuq_ref, wuk_ref, wuv_ref, cos_ref, sin_ref,
         pw_ref, ps_ref, cw_ref,
         ckv_ref, kr_ref, q_ref, k_ref, v_ref, bc_ref) = refs
        is_ctx = pl.program_id(0) < N_CTX // TM

    n_sub = TM // SEQ_TILE
    subs = [slice(s * SEQ_TILE, (s + 1) * SEQ_TILE) for s in range(n_sub)]

    xs = []
    for s, rows in enumerate(subs):
        if fused_ln:
            x = norm2(x1_ref[rows, :], _ffn_rows(ff_ref, SEQ_TILE, s * SEQ_TILE))
            x_ref[rows, :] = x
        else:
            x = jnp.where(is_ctx, xc_ref[rows, :], xl_ref[rows, :])
        xs.append(x)
    if fused_ln:
        x_halo = [norm2(x1p_ref[...], _ffn_rows(ffp_ref, HALO)),
                  norm2(x1n_ref[...], _ffn_rows(ffn_ref, HALO))]
    else:
        x_halo = [xp_ref[...], xn_ref[...]]
    xs[-1] = jnp.concatenate([xs[-1]] + x_halo, axis=0)

    hs = []
    for x in xs:
        u = x * (1.0 + sc_ref[0]) + sh_ref[0]
        hs.append(jnp.dot(u.astype(BF16), win_ref[0], preferred_element_type=F32))
    mix_cols = slice(MIX_COL0, MIX_COL0 + 4 * MIXW)
    mix_halo = hs[-1][SEQ_TILE:, mix_cols]
    mixes = [h[:SEQ_TILE, mix_cols] for h in hs]

    for s, rows in enumerate(subs):
        prev = mix_halo[:HALO] if s == 0 else mixes[s - 1][SEQ_TILE - HALO:]
        nxt = mix_halo[HALO:] if s == n_sub - 1 else mixes[s + 1][:HALO]
        bc_ref[rows, :] = _seqmix_math(pl.program_id(0) * n_sub + s, mixes[s], prev, nxt,
                                       pw_ref[...], ps_ref[...], cw_ref[...])

    kr0 = MIX_COL0 + 4 * MIXW
    ckv_bs = []
    for s, rows in enumerate(subs):
        ckv = _rms(hs[s][:SEQ_TILE, Q_RANK:Q_RANK + KV_RANK], kvn_ref[...])
        ckv_ref[rows, :] = ckv
        ckv_bs.append(ckv.astype(BF16))
    for s, rows in enumerate(subs):
        k_nope = jnp.dot(ckv_bs[s], wuk_ref[...], preferred_element_type=F32)
        v_ref[rows, :] = jnp.dot(ckv_bs[s], wuv_ref[...], preferred_element_type=F32).astype(BF16)
        kr_blk = hs[s][:SEQ_TILE, kr0:kr0 + LANES]
        kr_ref[rows, :] = kr_blk
        k_rope = _rope_block(kr_blk, cos_ref[rows, :], sin_ref[rows, :])
        parts = []
        for hd in range(HEADS):
            parts += [k_nope[:, hd * NOPE:(hd + 1) * NOPE], k_rope]
        k_ref[rows, :] = jnp.concatenate(parts, axis=-1).astype(BF16)
    qns = [_rms(hs[s][:SEQ_TILE, :Q_RANK], qn_ref[...]).astype(BF16) for s in range(n_sub)]
    ra = HEADS * NOPE
    for s, rows in enumerate(subs):
        qq = jnp.dot(qns[s], wuq_ref[...], preferred_element_type=F32)
        parts = []
        for hd in range(HEADS):
            q_rope = _rope_block(qq[:, ra + hd * LANES:ra + (hd + 1) * LANES],
                                 cos_ref[rows, :], sin_ref[rows, :])
            parts += [qq[:, hd * NOPE:(hd + 1) * NOPE] * Q_SCALE, q_rope * Q_SCALE]
        q_ref[rows, :] = jnp.concatenate(parts, axis=-1).astype(BF16)


def _halo_specs(rows_per_token, width, lat_row0, n_rows):
    nct = N_CTX // TM
    nblk = n_rows // HALO

    def first_block(i):
        return (jnp.maximum(i - nct, 0) * TM + lat_row0) // HALO

    shape = (HALO * rows_per_token, width)
    return [pl.BlockSpec(shape, lambda i: (jnp.maximum(first_block(i) - 1, 0), 0)),
            pl.BlockSpec(shape, lambda i: (jnp.minimum(first_block(i) + TM // HALO, nblk - 1), 0))]


def _front(prev, mod3, layer, w_in_ext, q_norm, kv_norm, w_uq_ext, w_uk, w_uv, cos_a, sin_a,
           pool_bd, pool_scale, conv_w):
    row = lambda i: (i, 0)
    lyr = lambda i: (layer, 0, 0)
    nct = N_CTX // TM
    rope_row = lambda i: (jnp.where(i < nct, i, nct + (i - nct) % (LAT_T // TM)), 0)
    modspec = lambda k: pl.BlockSpec((1, 1, D_MODEL), lambda i: (_mod_row(i, TM), 0, k))
    fused_ln = len(prev) == 5
    outs = [
        jax.ShapeDtypeStruct((N_TOK, KV_RANK), F32),
        jax.ShapeDtypeStruct((N_TOK, LANES), F32),
        jax.ShapeDtypeStruct((N_TOK, HEADS * HEAD_BLOCK), BF16),
        jax.ShapeDtypeStruct((N_TOK, HEADS * HEAD_BLOCK), BF16),
        jax.ShapeDtypeStruct((N_TOK, HEADS * VDIM), BF16),
        jax.ShapeDtypeStruct((N_TOK, 2 * MIXW), BF16),
    ]
    if fused_ln:
        x1, ff, mod3_prev, ln_g, ln_b = prev
        outs.append(jax.ShapeDtypeStruct((N_TOK, D_MODEL), F32))
        lead_specs = ([pl.BlockSpec((TM, D_MODEL), row), pl.BlockSpec((TM * CHUNKS, LANES), row)]
                      + _halo_specs(1, D_MODEL, N_CTX, N_TOK)
                      + _halo_specs(CHUNKS, LANES, N_CTX, N_TOK)
                      + [pl.BlockSpec((1, 1, D_MODEL), lambda i: (_mod_row(i, TM), 0, 5)),
                         _full((1, D_MODEL)), _full((1, D_MODEL))])
        lead_args = (x1, ff, x1, x1, ff, ff, mod3_prev, ln_g, ln_b)
    else:
        xc, xl = prev
        lead_specs = _pair_specs(TM, D_MODEL, 0) + _halo_specs(1, D_MODEL, 0, N_LAT)
        lead_args = (xc, xl, xl, xl)
    return pl.pallas_call(
        functools.partial(_front_body, fused_ln=fused_ln),
        grid=(N_TOK // TM,),
        in_specs=lead_specs + [
            modspec(0), modspec(1),
            pl.BlockSpec((1, D_MODEL, IN_EXT), lyr),
            _full((1, Q_RANK)), _full((1, KV_RANK)),
            _full((Q_RANK, UQ_EXT)),
            _full((KV_RANK, HEADS * NOPE)), _full((KV_RANK, HEADS * VDIM)),
            pl.BlockSpec((TM, LANES), rope_row), pl.BlockSpec((TM, LANES), rope_row),
            _full((MIXW, MIXW)), _full((1, MIXW)), _full((3, MIXW)),
        ],
        out_specs=tuple(pl.BlockSpec((TM, s.shape[1]), row) for s in outs),
        out_shape=tuple(outs),
        compiler_params=_params(1),
        name="front_ln" if fused_ln else "front",
    )(*lead_args, mod3, mod3, w_in_ext, q_norm, kv_norm, w_uq_ext, w_uk, w_uv, cos_a, sin_a,
      pool_bd, pool_scale, conv_w)


def _cachekv_body(ckv_ref, kr_ref, wuk_ref, wuv_ref, k_ref, v_ref):
    c = ckv_ref[0, 0].astype(BF16)
    k_nope = jnp.dot(c, wuk_ref[0], preferred_element_type=F32)
    v_ref[0, 0] = jnp.dot(c, wuv_ref[0], preferred_element_type=F32).astype(BF16)
    kr = kr_ref[0, 0]
    parts = []
    for hd in range(HEADS):
        parts += [k_nope[:, hd * NOPE:(hd + 1) * NOPE], kr]
    k_ref[0, 0] = jnp.concatenate(parts, axis=-1).astype(BF16)


def _cachekv(cache_ckv, kr_pad, w_uk, w_uv):
    bl = lambda b, l: (b, l, 0, 0)
    wl = lambda b, l: (l, 0, 0)
    return pl.pallas_call(
        _cachekv_body,
        grid=(LAT_B, DEPTH),
        in_specs=[
            pl.BlockSpec((1, 1, PAST_LEN, KV_RANK), bl),
            pl.BlockSpec((1, 1, PAST_LEN, LANES), bl),
            pl.BlockSpec((1, KV_RANK, HEADS * NOPE), wl),
            pl.BlockSpec((1, KV_RANK, HEADS * VDIM), wl),
        ],
        out_specs=(pl.BlockSpec((1, 1, PAST_LEN, HEADS * HEAD_BLOCK), bl),
                   pl.BlockSpec((1, 1, PAST_LEN, HEADS * VDIM), bl)),
        out_shape=(jax.ShapeDtypeStruct((LAT_B, DEPTH, PAST_LEN, HEADS * HEAD_BLOCK), BF16),
                   jax.ShapeDtypeStruct((LAT_B, DEPTH, PAST_LEN, HEADS * VDIM), BF16)),
        compiler_params=_params(2),
        name="cache_kv",
    )(cache_ckv, kr_pad, w_uk, w_uv)


EXT = SEQ_TILE + 2 * HALO


def _shift_up(x, k):
    return pltpu.roll(x, x.shape[0] - k, axis=0)


def _shift_down(x, k):
    return pltpu.roll(x, k, axis=0)


def _seqmix_math(j, main, prev, nxt, pw, ps, cw):
    n_ctx_tiles = N_CTX // SEQ_TILE
    tiles_per_lat = LAT_T // SEQ_TILE
    is_ctx = j < n_ctx_tiles
    jj = jnp.where(is_ctx, 0, (j - n_ctx_tiles) % tiles_per_lat)
    first = jj == 0
    last = jnp.where(is_ctx, True, jj == tiles_per_lat - 1)
    t_seq = jnp.where(is_ctx, CTX_T, LAT_T)

    prev = jnp.where(first, 0.0, prev)
    nxt = jnp.where(last, 0.0, nxt)
    ext = jnp.concatenate([prev, main, nxt], axis=0)
    p = ext[:, 0:MIXW]
    g_b = main[:, MIXW:2 * MIXW]
    g_c = ext[:, 2 * MIXW:3 * MIXW]
    h_in = ext[:, 3 * MIXW:4 * MIXW]

    sums = {}
    b = p
    for w in POOL_WINDOWS:
        b = b + _shift_up(b, w // 2)
        sums[w] = _shift_down(b, w // 2)[HALO:HALO + SEQ_TILE]
    tpos = jj * SEQ_TILE + lax.broadcasted_iota(I32, (SEQ_TILE, 1), 0)
    lane = lax.broadcasted_iota(I32, (1, MIXW), 1)
    num = None
    den = None
    for g, w in enumerate(POOL_WINDOWS):
        lo = jnp.maximum(tpos - w // 2, 0)
        hi = jnp.minimum(tpos + w - w // 2, t_seq)
        cnt = (hi - lo).astype(F32)
        if num is None:
            num, den = sums[w], jnp.broadcast_to(cnt, (SEQ_TILE, MIXW))
        else:
            sel = lane >= g * POOL_GROUP
            num = jnp.where(sel, sums[w], num)
            den = jnp.where(sel, cnt, den)
    pooled = num / den - p[HALO:HALO + SEQ_TILE]
    b_out = jnp.dot(pooled.astype(BF16), pw, preferred_element_type=F32) * ps

    y = g_c * h_in
    conv = (_shift_down(y, 1) * cw[0:1] + y * cw[1:2] + _shift_up(y, 1) * cw[2:3])
    c_out = g_b * conv[HALO:HALO + SEQ_TILE]
    return jnp.concatenate([b_out, c_out], axis=-1).astype(BF16)


_NT = (((1,), (1,)), ((), ()))


def _v_ext(v):
    return jnp.concatenate([v, jnp.ones(v.shape, v.dtype)], axis=-1)


CTX_PER_STEP = 2


def _attn_ctx_body(q_ref, k_ref, v_ref, o_ref):
    chains = [(slice(r * CTX_T, (r + 1) * CTX_T), hd)
              for r in range(CTX_PER_STEP) for hd in range(HEADS)]
    scores = []
    for rows, hd in chains:
        cols = slice(hd * HEAD_BLOCK, (hd + 1) * HEAD_BLOCK)
        scores.append(lax.dot_general(q_ref[rows, cols], k_ref[rows, cols], _NT,
                                      preferred_element_type=F32))
    probs = [jnp.exp2(s - jnp.max(s, axis=-1, keepdims=True)).astype(BF16) for s in scores]
    for (rows, hd), p in zip(chains, probs):
        vcols = slice(hd * VDIM, (hd + 1) * VDIM)
        acc = jnp.dot(p, _v_ext(v_ref[rows, vcols]), preferred_element_type=F32)
        o_ref[rows, vcols] = (acc[:, :VDIM] / acc[:, VDIM:]).astype(BF16)


def _attn_ctx(q, k, v):
    row = lambda b: (b, 0)
    tile = CTX_PER_STEP * CTX_T
    return pl.pallas_call(
        _attn_ctx_body,
        grid=(CTX_B // CTX_PER_STEP,),
        in_specs=[pl.BlockSpec((tile, HEADS * HEAD_BLOCK), row),
                  pl.BlockSpec((tile, HEADS * HEAD_BLOCK), row),
                  pl.BlockSpec((tile, HEADS * VDIM), row)],
        out_specs=pl.BlockSpec((tile, HEADS * VDIM), row),
        out_shape=jax.ShapeDtypeStruct((N_CTX, HEADS * VDIM), BF16),
        compiler_params=_params(1),
        name="attn_ctx",
    )(q, k, v)


def _attn_lat_body(q_ref, kc_ref, vc_ref, ko_ref, vo_ref, o_ref, m_ref, acc_ref):
    def update(hd, k, v, first):
        reps = k.shape[0] // LANES
        qh = q_ref[:, hd * HEAD_BLOCK:(hd + 1) * HEAD_BLOCK]
        s = lax.dot_general(qh, k, _NT, preferred_element_type=F32)
        mx = jnp.max(s, axis=-1, keepdims=True)
        if first:
            m_new = jnp.broadcast_to(mx, (TQ, LANES))
        else:
            m_old = m_ref[hd]
            m_new = jnp.maximum(m_old, mx)
        p = jnp.exp2(s - jnp.concatenate([m_new] * reps, axis=-1))
        pv = jnp.dot(p.astype(BF16), _v_ext(v), preferred_element_type=F32)
        if first:
            acc_ref[hd] = pv
        else:
            a = jnp.exp2(m_old - m_new)
            acc_ref[hd] = acc_ref[hd] * jnp.concatenate([a, a], axis=-1) + pv
        m_ref[hd] = m_new

    def head_slices(hd):
        return slice(hd * HEAD_BLOCK, (hd + 1) * HEAD_BLOCK), slice(hd * VDIM, (hd + 1) * VDIM)

    for hd in range(HEADS):
        ks, vs = head_slices(hd)
        update(hd, kc_ref[0, 0, :, ks], vc_ref[0, 0, :, vs], True)

    def body(c, _):
        for j in range(CHUNK_UNROLL):
            rows = pl.ds(pl.multiple_of((c * CHUNK_UNROLL + j) * KV_CHUNK, KV_CHUNK), KV_CHUNK)
            for hd in range(HEADS):
                ks, vs = head_slices(hd)
                update(hd, ko_ref[rows, ks], vo_ref[rows, vs], False)
        return 0

    lax.fori_loop(0, LAT_T // (KV_CHUNK * CHUNK_UNROLL), body, 0)
    outs = [acc_ref[hd][:, :VDIM] / acc_ref[hd][:, VDIM:] for hd in range(HEADS)]
    o_ref[...] = jnp.concatenate(outs, axis=-1).astype(BF16)


def _attn_lat(q, k, v, kc, vc, layer):
    qt = LAT_T // TQ
    ctx_tiles = N_CTX // TQ
    ctx_blocks = N_CTX // LAT_T
    return pl.pallas_call(
        _attn_lat_body,
        grid=(LAT_B, qt),
        in_specs=[
            pl.BlockSpec((TQ, HEADS * HEAD_BLOCK), lambda b, i: (ctx_tiles + b * qt + i, 0)),
            pl.BlockSpec((1, 1, PAST_LEN, HEADS * HEAD_BLOCK), lambda b, i: (b, layer, 0, 0)),
            pl.BlockSpec((1, 1, PAST_LEN, HEADS * VDIM), lambda b, i: (b, layer, 0, 0)),
            pl.BlockSpec((LAT_T, HEADS * HEAD_BLOCK), lambda b, i: (ctx_blocks + b, 0)),
            pl.BlockSpec((LAT_T, HEADS * VDIM), lambda b, i: (ctx_blocks + b, 0)),
        ],
        out_specs=pl.BlockSpec((TQ, HEADS * VDIM), lambda b, i: (b * qt + i, 0)),
        out_shape=jax.ShapeDtypeStruct((N_LAT, HEADS * VDIM), BF16),
        scratch_shapes=[pltpu.VMEM((HEADS, TQ, LANES), F32),
                        pltpu.VMEM((HEADS, TQ, 2 * VDIM), F32)],
        compiler_params=_params(2),
        name="attn_lat",
    )(q, kc, vc, k, v)


def _layer_norm(y, g, b):
    mu = jnp.mean(y, axis=-1, keepdims=True)
    d = y - mu
    var = jnp.mean(d * d, axis=-1, keepdims=True)
    return d * lax.rsqrt(var + LN_EPS) * g + b


def _postmix_body(actx_ref, alat_ref, bc_ref, xc_ref, xl_ref, g1_ref, sh2_ref, sc2_ref, wa_ref, wbc_ref,
                  lg_ref, lb_ref, wr_ref, x1_ref, u2p_ref, aff_ref):
    is_ctx = pl.program_id(0) < N_CTX // TM
    lane = lax.broadcasted_iota(I32, (1, LANES), 1)
    blocks = [slice(b * POST_BLOCK, (b + 1) * POST_BLOCK) for b in range(TM // POST_BLOCK)]
    mixes = []
    for rows in blocks:
        a = jnp.where(is_ctx, actx_ref[rows, :], alat_ref[rows, :])
        mix = jnp.dot(a, wa_ref[0], preferred_element_type=F32)
        mixes.append(mix + jnp.dot(bc_ref[rows, :], wbc_ref[0], preferred_element_type=F32))
    u2s = []
    for b, rows in enumerate(blocks):
        x = jnp.where(is_ctx, xc_ref[rows, :], xl_ref[rows, :])
        x1 = _layer_norm(ALPHA * x + g1_ref[0] * mixes[b], lg_ref[...], lb_ref[...])
        x1_ref[rows, :] = x1
        u2 = x1 * (1.0 + sc2_ref[0]) + sh2_ref[0]
        for c in range(WORDS):
            lo = u2[:, c * LANES:(c + 1) * LANES]
            hi = u2[:, (c + WORDS) * LANES:(c + WORDS + 1) * LANES]
            u2p_ref[pl.ds(b * POST_BLOCK * WORDS + c, POST_BLOCK, stride=WORDS), :] = (
                pltpu.pack_elementwise([lo, hi], packed_dtype=BF16))
        u2s.append(u2.astype(BF16))
    for b, rows in enumerate(blocks):
        logits = jnp.dot(u2s[b], wr_ref[...], preferred_element_type=F32)
        logits = jnp.where(lane < N_EXPERTS, logits, -jnp.inf)
        m = jnp.max(logits, axis=-1, keepdims=True)
        e = jnp.exp(logits - m)
        aff_ref[rows, :] = e / jnp.sum(e, axis=-1, keepdims=True)


def _postmix(a_ctx, a_lat, bc, xc, xl, lat_row0, mod3, layer, w_out_b, ln_g, ln_b, w_router_pad):
    row = lambda i: (i, 0)
    modspec = lambda k: pl.BlockSpec((1, 1, D_MODEL), lambda i: (_mod_row(i, TM), 0, k))
    return pl.pallas_call(
        _postmix_body,
        grid=(N_TOK // TM,),
        in_specs=_pair_specs(TM, HEADS * VDIM, 0) + [
            pl.BlockSpec((TM, 2 * MIXW), row),
        ] + _pair_specs(TM, D_MODEL, lat_row0) + [
            modspec(2), modspec(3), modspec(4),
            pl.BlockSpec((1, HEADS * VDIM, D_MODEL), lambda i: (layer, 0, 0)),
            pl.BlockSpec((1, 2 * MIXW, D_MODEL), lambda i: (layer, 1, 0)),
            _full((1, D_MODEL)), _full((1, D_MODEL)),
            _full((D_MODEL, LANES)),
        ],
        out_specs=(pl.BlockSpec((TM, D_MODEL), row),
                   pl.BlockSpec((TM * WORDS, LANES), row),
                   pl.BlockSpec((TM, LANES), row)),
        out_shape=(jax.ShapeDtypeStruct((N_TOK, D_MODEL), F32),
                   jax.ShapeDtypeStruct((N_TOK * WORDS, LANES), jnp.uint32),
                   jax.ShapeDtypeStruct((N_TOK, LANES), F32)),
        compiler_params=_params(1),
        name="postmix",
    )(a_ctx, a_lat, bc, xc, xl, mod3, mod3, mod3, w_out_b, w_out_b, ln_g, ln_b, w_router_pad)


PREFIX_CHUNK = 256


def _route_ctx_body(aff_ref, idx_ref):
    n_rows = CTX_B * N_EXPERTS
    dense = jnp.concatenate(
        [aff_ref[b * CTX_T:(b + 1) * CTX_T, :].T[:N_EXPERTS] for b in range(CTX_B)], axis=0)

    def search(i, thr):
        cand = thr | jnp.left_shift(jnp.int32(1), 30 - i)
        n = jnp.sum((dense >= pltpu.bitcast(cand, F32)).astype(I32), axis=1, keepdims=True)
        return jnp.where(n >= CTX_CAP, cand, thr)

    thr_bits = lax.fori_loop(0, 31, search, jnp.zeros((n_rows, 1), I32))
    thr = pltpu.bitcast(thr_bits, F32)
    above = pltpu.bitcast(thr_bits + 1, F32)
    gt = (dense > thr) & (dense >= above)
    eq = (dense >= thr) & jnp.logical_not(gt)
    n_gt = jnp.sum(gt.astype(I32), axis=1, keepdims=True)
    need = (CTX_CAP - n_gt).astype(F32)

    ri = lax.broadcasted_iota(I32, (CTX_T, CTX_T), 0)
    ci = lax.broadcasted_iota(I32, (CTX_T, CTX_T), 1)
    tri = jnp.where(ri <= ci, 1.0, 0.0).astype(BF16)
    p_eq = jnp.dot(jnp.where(eq, 1.0, 0.0).astype(BF16), tri, preferred_element_type=F32)
    sel = jnp.where(gt, 1.0, jnp.where(eq & (p_eq <= need), 1.0, 0.0))
    cnt = jnp.dot(sel.astype(BF16), tri, preferred_element_type=F32)

    ones = jnp.ones((CTX_T, LANES), BF16)
    lane = lax.broadcasted_iota(I32, (1, LANES), 1)
    out = jnp.zeros((n_rows, LANES), F32)
    for r in range(CTX_CAP):
        le = jnp.where(cnt <= float(r), 1.0, 0.0).astype(BF16)
        out = jnp.where(lane == r, jnp.dot(le, ones, preferred_element_type=F32), out)
    idx_ref[...] = out.astype(I32)


def _route_ctx(aff):
    idx = pl.pallas_call(
        _route_ctx_body,
        grid=(1,),
        in_specs=[pl.BlockSpec((N_CTX, LANES), lambda i: (0, 0))],
        out_specs=_full((CTX_B * N_EXPERTS, LANES)),
        out_shape=jax.ShapeDtypeStruct((CTX_B * N_EXPERTS, LANES), I32),
        compiler_params=_params(1),
        name="route_ctx",
    )(aff)
    return idx[:, :CTX_CAP].reshape(CTX_B, N_EXPERTS, CTX_CAP)


RANK_TILE = LANES


def _expert_row(col):
    full = jnp.concatenate([jnp.broadcast_to(col, (N_EXPERTS, LANES)),
                            jnp.zeros((LANES - N_EXPERTS, LANES), col.dtype)], axis=0)
    return full.T[0:1]


def _route_count_body(aff_ref, cnt_ref, cend_ref, *, seq, cap):
    n_chunks = seq // PREFIX_CHUNK
    dense = aff_ref[...].T[:N_EXPERTS]

    def search(i, thr):
        cand = thr | jnp.left_shift(jnp.int32(1), 30 - i)
        n = jnp.sum((dense >= pltpu.bitcast(cand, F32)).astype(I32), axis=1, keepdims=True)
        return jnp.where(n >= cap, cand, thr)

    thr_bits = lax.fori_loop(0, 31, search, jnp.zeros((N_EXPERTS, 1), I32))
    thr_col = pltpu.bitcast(thr_bits, F32)
    above_col = pltpu.bitcast(thr_bits + 1, F32)
    n_gt = jnp.sum(((dense > thr_col) & (dense >= above_col)).astype(I32), axis=1, keepdims=True)
    thr = _expert_row(thr_col)
    above = _expert_row(above_col)
    need = _expert_row((cap - n_gt).astype(F32))

    ri = lax.broadcasted_iota(I32, (PREFIX_CHUNK, PREFIX_CHUNK), 0)
    ci = lax.broadcasted_iota(I32, (PREFIX_CHUNK, PREFIX_CHUNK), 1)
    tri = jnp.where(ci <= ri, 1.0, 0.0).astype(BF16)
    carry_eq = jnp.zeros((1, LANES), F32)
    carry_sel = jnp.zeros((1, LANES), F32)
    per = PREFIX_CHUNK // RANK_TILE
    for c in range(n_chunks):
        rows = slice(c * PREFIX_CHUNK, (c + 1) * PREFIX_CHUNK)
        a = aff_ref[rows, :]
        gt = (a > thr) & (a >= above)
        eq = (a >= thr) & jnp.logical_not(gt)
        p_eq = jnp.dot(tri, jnp.where(eq, 1.0, 0.0).astype(BF16),
                       preferred_element_type=F32) + carry_eq
        sel = jnp.where(gt, 1.0, jnp.where(eq & (p_eq <= need), 1.0, 0.0))
        p_sel = jnp.dot(tri, sel.astype(BF16), preferred_element_type=F32) + carry_sel
        dense_cnt = p_sel.T[:N_EXPERTS]
        for k in range(per):
            last = (k + 1) * RANK_TILE - 1
            cnt_ref[c * per + k] = dense_cnt[:, k * RANK_TILE:(k + 1) * RANK_TILE]
            cend_ref[0, c * per + k:c * per + k + 1, :] = p_sel[last:last + 1, :].astype(I32)
        carry_eq = p_eq[PREFIX_CHUNK - 1:PREFIX_CHUNK, :]
        carry_sel = p_sel[PREFIX_CHUNK - 1:PREFIX_CHUNK, :]


def _route_count(aff, first_block, n_req, seq, cap):
    n_tiles = seq // RANK_TILE
    return pl.pallas_call(
        functools.partial(_route_count_body, seq=seq, cap=cap),
        grid=(n_req,),
        in_specs=[pl.BlockSpec((seq, LANES), lambda b: (first_block + b, 0))],
        out_specs=(pl.BlockSpec((n_tiles, N_EXPERTS, RANK_TILE), lambda b: (b, 0, 0)),
                   pl.BlockSpec((1, n_tiles, LANES), lambda b: (b, 0, 0))),
        out_shape=(jax.ShapeDtypeStruct((n_req * n_tiles, N_EXPERTS, RANK_TILE), F32),
                   jax.ShapeDtypeStruct((n_req, n_tiles, LANES), I32)),
        compiler_params=_params(1),
        name=f"route_count_{seq}",
    )(aff)


def _route_rank_body(cend_ref, cnt_ref, idx_ref, *bufs, seq, cap):
    b = pl.program_id(0)
    n_tiles = seq // RANK_TILE
    rank = lax.broadcasted_iota(I32, (RANK_TILE, RANK_TILE), 0).astype(F32)
    ones = jnp.ones((RANK_TILE, LANES), BF16)

    def tile(j, _):
        counts = cnt_ref[j]
        for e in range(N_EXPERTS):
            prev = (b * N_EXPERTS + e) * n_tiles + j - 1
            start = jnp.where(j > 0, cend_ref[jnp.maximum(prev, 0)], 0)
            local = counts[e:e + 1, :] - jnp.asarray(start, F32)
            le = jnp.where(local <= rank, 1.0, 0.0).astype(BF16)
            pos = jnp.dot(le, ones, preferred_element_type=F32) + jnp.asarray(j * RANK_TILE, F32)
            bufs[e][pl.ds(start, RANK_TILE), :] = pos
        return 0

    lax.fori_loop(0, n_tiles, tile, 0)
    lane = lax.broadcasted_iota(I32, (1, LANES), 1)
    out = jnp.zeros((cap, LANES), F32)
    for e in range(N_EXPERTS):
        out = jnp.where(lane == e, bufs[e][0:cap, :], out)
    idx_ref[0] = out.astype(I32)


def _route_rank(cend_flat, cnt, n_req, seq, cap):
    n_tiles = seq // RANK_TILE
    grid_spec = pltpu.PrefetchScalarGridSpec(
        num_scalar_prefetch=1,
        grid=(n_req,),
        in_specs=[pl.BlockSpec((n_tiles, N_EXPERTS, RANK_TILE), lambda b, c: (b, 0, 0))],
        out_specs=pl.BlockSpec((1, cap, LANES), lambda b, c: (b, 0, 0)),
        scratch_shapes=[pltpu.VMEM((cap + RANK_TILE, LANES), F32)] * N_EXPERTS,
    )
    return pl.pallas_call(
        functools.partial(_route_rank_body, seq=seq, cap=cap),
        grid_spec=grid_spec,
        out_shape=jax.ShapeDtypeStruct((n_req, cap, LANES), I32),
        compiler_params=_params(1),
        name=f"route_rank_{seq}",
    )(cend_flat, cnt)


def _gather_rows(idx_ref, base, u2p_ref, aff_ref, tile_ref, gate_ref, r0, n):
    for i in range(n):
        r = r0 + i
        t = idx_ref[base + r]
        slab = u2p_ref[pl.ds(pl.multiple_of(t * WORDS, WORDS), WORDS), :]
        tile_ref[pl.ds(r, WORDS, stride=TILE_PITCH), :] = slab
        gate_ref[pl.ds(r, 1), :] = aff_ref[pl.ds(t, 1), :]


def _scatter_rows(idx_ref, base, y_ref, acc_ref, r0, n):
    for g0 in range(0, n, SCATTER_UNROLL):
        rows = [r0 + g0 + i for i in range(SCATTER_UNROLL)]
        dst = [pl.ds(pl.multiple_of(idx_ref[base + r] * CHUNKS, CHUNKS), CHUNKS) for r in rows]
        vals = [acc_ref[d, :] + y_ref[pl.ds(pl.multiple_of(r * CHUNKS, CHUNKS), CHUNKS), :]
                for d, r in zip(dst, rows)]
        for d, v in zip(dst, vals):
            acc_ref[d, :] = v


def _expert_rows(e, tile_ref, gate_ref, wg, wu, wd, y_ref, r0, n):
    halves = [[], []]
    for c in range(WORDS):
        w = tile_ref[pl.ds(c * TILE_PITCH + r0, n), :]
        for k in range(2):
            halves[k].append(pltpu.unpack_elementwise(w, index=k, packed_dtype=BF16,
                                                      unpacked_dtype=F32))
    x = jnp.concatenate(halves[0] + halves[1], axis=-1).astype(BF16)
    lane = lax.broadcasted_iota(I32, (1, LANES), 1)
    gate = jnp.sum(jnp.where(lane == e, gate_ref[pl.ds(r0, n), :], 0.0), axis=-1, keepdims=True)
    hg = jnp.dot(x, wg, preferred_element_type=F32)
    hu = jnp.dot(x, wu, preferred_element_type=F32)
    hidden = (hg * jax.nn.sigmoid(hg) * hu).astype(BF16)
    y = jnp.dot(hidden, wd, preferred_element_type=F32) * gate
    for c in range(CHUNKS):
        y_ref[pl.ds(r0 * CHUNKS + c, n, stride=CHUNKS), :] = y[:, c * LANES:(c + 1) * LANES]


def _moe_body(idx_ref, u2p_ref, aff_ref, wg_ref, wu_ref, wd_ref, acc_ref,
              tile_a, tile_b, gate_a, gate_b, y_a, y_b):
    p = pl.program_id(0)
    e = pl.program_id(1)
    step = p * N_EXPERTS + e
    base = step * ROWS
    base_next = jnp.minimum(step + 1, N_PASS * N_EXPERTS - 1) * ROWS
    base_prev = jnp.where(e == 0, base, base - ROWS)

    @pl.when(e == 0)
    def _():
        acc_ref[...] = jnp.zeros_like(acc_ref)
        y_b[...] = jnp.zeros_like(y_b)

        def gather(c, _):
            _gather_rows(idx_ref, base, u2p_ref, aff_ref, tile_a, gate_a, c * SUBLANES, SUBLANES)
            return 0

        lax.fori_loop(0, ROWS // SUBLANES, gather, 0)

    def run(tile_cur, gate_cur, y_cur, tile_nxt, gate_nxt, y_prv):
        wg = wg_ref[0, 0].astype(BF16)
        wu = wu_ref[0, 0].astype(BF16)
        wd = wd_ref[0, 0].astype(BF16)
        for blk in range(ROWS // ROW_BLOCK):
            r0 = blk * ROW_BLOCK
            _gather_rows(idx_ref, base_next, u2p_ref, aff_ref, tile_nxt, gate_nxt, r0, ROW_BLOCK)
            _expert_rows(e, tile_cur, gate_cur, wg, wu, wd, y_cur, r0, ROW_BLOCK)
            _scatter_rows(idx_ref, base_prev, y_prv, acc_ref, r0, ROW_BLOCK)

    @pl.when(e % 2 == 0)
    def _():
        run(tile_a, gate_a, y_a, tile_b, gate_b, y_b)

    @pl.when(e % 2 == 1)
    def _():
        run(tile_b, gate_b, y_b, tile_a, gate_a, y_a)

    @pl.when(e == N_EXPERTS - 1)
    def _():
        def scatter(c, _):
            _scatter_rows(idx_ref, base, y_b, acc_ref, c * SCATTER_UNROLL, SCATTER_UNROLL)
            return 0

        lax.fori_loop(0, ROWS // SCATTER_UNROLL, scatter, 0)


_MOE_SCRATCH = [
    pltpu.VMEM((WORDS * TILE_PITCH, LANES), jnp.uint32),
    pltpu.VMEM((WORDS * TILE_PITCH, LANES), jnp.uint32),
    pltpu.VMEM((ROWS, LANES), F32),
    pltpu.VMEM((ROWS, LANES), F32),
    pltpu.VMEM((ROWS * CHUNKS, LANES), F32),
    pltpu.VMEM((ROWS * CHUNKS, LANES), F32),
]


def _moe(idx_flat, u2p, aff, w_gate, w_up, w_down, layer):
    one = pl.Buffered(1)
    grid_spec = pltpu.PrefetchScalarGridSpec(
        num_scalar_prefetch=1,
        grid=(N_PASS, N_EXPERTS),
        in_specs=[
            pl.BlockSpec((PASS_TOK * WORDS, LANES), lambda p, e, idx: (p, 0)),
            pl.BlockSpec((PASS_TOK, LANES), lambda p, e, idx: (p, 0), pipeline_mode=one),
            pl.BlockSpec((1, 1, D_MODEL, EXPERT_DIM), lambda p, e, idx: (layer, e, 0, 0)),
            pl.BlockSpec((1, 1, D_MODEL, EXPERT_DIM), lambda p, e, idx: (layer, e, 0, 0)),
            pl.BlockSpec((1, 1, EXPERT_DIM, D_MODEL), lambda p, e, idx: (layer, e, 0, 0)),
        ],
        out_specs=pl.BlockSpec((PASS_TOK * CHUNKS, LANES), lambda p, e, idx: (p, 0),
                               pipeline_mode=one),
        scratch_shapes=_MOE_SCRATCH,
    )
    return pl.pallas_call(
        _moe_body,
        grid_spec=grid_spec,
        out_shape=jax.ShapeDtypeStruct((N_TOK * CHUNKS, LANES), F32),
        compiler_params=_params(2, arbitrary=True),
        name="moe",
    )(idx_flat, u2p, aff, w_gate, w_up, w_down)


def _final_body(x1_ref, ff_ref, g2_ref, lg_ref, lb_ref, o_ref):
    ffn = jnp.concatenate([ff_ref[pl.ds(c, TM, stride=CHUNKS), :] for c in range(CHUNKS)], axis=-1)
    o_ref[...] = _layer_norm(ALPHA * x1_ref[...] + g2_ref[0] * ffn, lg_ref[...], lb_ref[...])


def _final(x1, ff, mod3, ln_g, ln_b, row0=0, n_rows=N_TOK):
    t0 = row0 // TM
    row = lambda i: (i + t0, 0)
    return pl.pallas_call(
        _final_body,
        grid=(n_rows // TM,),
        in_specs=[
            pl.BlockSpec((TM, D_MODEL), row),
            pl.BlockSpec((TM * CHUNKS, LANES), row),
            pl.BlockSpec((1, 1, D_MODEL), lambda i: (_mod_row(i + t0, TM), 0, 5)),
            _full((1, D_MODEL)), _full((1, D_MODEL)),
        ],
        out_specs=pl.BlockSpec((TM, D_MODEL), lambda i: (i, 0)),
        out_shape=jax.ShapeDtypeStruct((n_rows, D_MODEL), F32),
        compiler_params=_params(1),
        name="final_ln",
    )(x1, ff, mod3, ln_g, ln_b)


def _rot_cols(w):
    half = ROPE // 2
    return jnp.concatenate([-w[..., half:], w[..., :half]], axis=-1)


def _pad_lanes(w):
    pad = [(0, 0)] * (w.ndim - 1) + [(0, LANES - w.shape[-1])]
    return jnp.pad(w, pad)


def _rope_tables():
    rows_n = LAT_T // GRID_W
    r, cl = jnp.meshgrid(jnp.arange(rows_n, dtype=F32), jnp.arange(GRID_W, dtype=F32), indexing="ij")
    inv = ROPE_THETA ** (-jnp.arange(0, ROPE // 2, 2, dtype=F32) / (ROPE // 2))
    ang = jnp.concatenate([r.reshape(-1)[:, None] * inv, cl.reshape(-1)[:, None] * inv], axis=-1)
    cos, sin = jnp.cos(ang), jnp.sin(ang)
    cos_lat = _pad_lanes(jnp.concatenate([cos, cos], axis=-1))
    sin_lat = _pad_lanes(jnp.concatenate([sin, sin], axis=-1))
    cos_ctx = _pad_lanes(jnp.ones((N_CTX, ROPE), F32))
    sin_ctx = jnp.zeros((N_CTX, LANES), F32)
    cos_a = jnp.concatenate([cos_ctx, cos_lat], axis=0)
    sin_a = jnp.concatenate([sin_ctx, sin_lat], axis=0)
    return cos_a, sin_a


def kernel(x_prompt, x_sample, cache_ckv, cache_krope, c, c_ctx, w_in, q_norm, w_uq, kv_norm, w_uk, w_uv,
           pool_w, pool_scale, conv_w, w_out, w_ada, b_ada, ln1_g, ln1_b, ln2_g, ln2_b, w_router,
           w_gate, w_up, w_down):
    w_in_ext = _prep_win(w_in)
    hw = NOPE + ROPE
    uq_nope = [w_uq[:, :, h * hw:h * hw + NOPE] for h in range(HEADS)]
    uq_rope = [w_uq[:, :, h * hw + NOPE:(h + 1) * hw] for h in range(HEADS)]
    w_uq_ext = jnp.concatenate(
        uq_nope + [blk for w in uq_rope for blk in (w, _rot_cols(w))], axis=-1).astype(BF16)
    w_uk_b = w_uk.astype(BF16)
    w_uv_b = w_uv.astype(BF16)
    eye = jnp.eye(len(POOL_WINDOWS), dtype=F32)
    pool_bd = (pool_w[:, :, :, None, :] * eye[None, :, None, :, None]).reshape(DEPTH, MIXW, MIXW).astype(BF16)
    w_out_b = w_out.astype(BF16)
    w_router_pad = _pad_lanes(w_router).astype(BF16)
    cos_a, sin_a = _rope_tables()

    cond = jnp.concatenate([c_ctx[None, :], c, jnp.zeros((SUBLANES - 1 - LAT_B, D_MODEL), F32)], axis=0)
    mod = _ada(cond, w_ada, b_ada)
    kc, vc = _cachekv(cache_ckv, _pad_lanes(cache_krope), w_uk_b, w_uv_b)

    prev = (x_prompt.reshape(N_CTX, D_MODEL), x_sample.reshape(N_LAT, D_MODEL))
    xs = prev + (0,)
    ckv_layers, kr_layers = [], []
    ctx_off = (jnp.arange(CTX_B, dtype=I32) * CTX_T)[:, None, None]
    for l in range(DEPTH):
        mod3 = mod[l].reshape(SUBLANES, 1, 6 * D_MODEL)
        outs = _front(prev, mod3, l, w_in_ext, q_norm[l][None], kv_norm[l][None],
                      w_uq_ext[l], w_uk_b[l], w_uv_b[l], cos_a, sin_a,
                      pool_bd[l], pool_scale[l][None], conv_w[l])
        ckv, kr, q, k, v, bc = outs[:6]
        if l > 0:
            xs = (outs[6], outs[6], N_CTX)
        ckv_layers.append(ckv[:N_CTX].reshape(CTX_B, CTX_T, KV_RANK))
        kr_layers.append(kr[:N_CTX, :ROPE].reshape(CTX_B, CTX_T, ROPE))
        a_ctx = _attn_ctx(q, k, v)
        a_lat = _attn_lat(q, k, v, kc, vc, l)
        x1, u2p, aff = _postmix(a_ctx, a_lat, bc, *xs, mod3, l, w_out_b,
                                ln1_g[l][None], ln1_b[l][None], w_router_pad[l])
        idx_ctx = _route_ctx(aff)
        cnt_lat, cend = _route_count(aff, N_CTX // LAT_T, LAT_B, LAT_T, LAT_CAP)
        cend_flat = cend[:, :, :N_EXPERTS].transpose(0, 2, 1).reshape(-1)
        idx_lat = _route_rank(cend_flat, cnt_lat, LAT_B, LAT_T, LAT_CAP)
        idx_lat = idx_lat[:, :, :N_EXPERTS].transpose(0, 2, 1)
        idx_ctx = (idx_ctx + ctx_off).transpose(1, 0, 2).reshape(1, N_EXPERTS, ROWS)
        idx_flat = jnp.concatenate([idx_ctx, idx_lat], axis=0).reshape(-1)
        ff = _moe(idx_flat, u2p, aff, w_gate, w_up, w_down, l)
        if l + 1 < DEPTH:
            prev = (x1, ff, mod3, ln2_g[l][None], ln2_b[l][None])
        else:
            y_ctx = _final(x1, ff, mod3, ln2_g[l][None], ln2_b[l][None], 0, N_CTX)
            y_lat = _final(x1, ff, mod3, ln2_g[l][None], ln2_b[l][None], N_CTX, N_LAT)

    y_prompt = y_ctx.reshape(CTX_B, CTX_T, D_MODEL)
    y_sample = y_lat.reshape(LAT_B, LAT_T, D_MODEL)
    new_ckv = jnp.stack(ckv_layers, axis=1)
    new_krope = jnp.stack(kr_layers, axis=1)
    return (y_prompt, y_sample, new_ckv, new_krope)
```

```python
import functools

import jax
import jax.numpy as jnp
from jax import lax
from jax.experimental import pallas as pl
from jax.experimental.pallas import tpu as pltpu

F32 = jnp.float32
BF16 = jnp.bfloat16
I32 = jnp.int32

D_MODEL = 1024
CTX_B, CTX_T = 16, 256
LAT_B, LAT_T = 2, 4096
DEPTH = 4
PAST_LEN = 512
GRID_W = 64
HEADS = 4
NOPE, ROPE, VDIM = 128, 64, 128
Q_RANK, KV_RANK = 384, 256
POOL_WINDOWS = (2, 4, 8, 16)
POOL_GROUP = 64
MIXW = 256
N_EXPERTS = 16
EXPERT_DIM = 512
ROPE_THETA = 10000.0
ATTN_SCALE = (NOPE + ROPE) ** -0.5
LOG2E = 1.4426950408889634
Q_SCALE = ATTN_SCALE * LOG2E
ALPHA = (2 * DEPTH) ** 0.25
RMS_EPS = 1e-6
LN_EPS = 1e-5

N_CTX = CTX_B * CTX_T
N_LAT = LAT_B * LAT_T
N_TOK = N_CTX + N_LAT
CTX_CAP = 2 * CTX_T // N_EXPERTS
LAT_CAP = 2 * LAT_T // N_EXPERTS

LANES = 128
SUBLANES = 8
CHUNKS = D_MODEL // LANES
HEAD_BLOCK = 2 * LANES
VMEM_LIMIT = 56 * 1024 * 1024

TM = 512
SEQ_TILE = 256
HALO = 8
TQ = 1024
KV_CHUNK = 1024
CHUNK_UNROLL = 4
PASS_TOK = 4096
N_PASS = N_TOK // PASS_TOK
ROWS = 512
TILE_PITCH = ROWS + 8
SCATTER_UNROLL = 8
WORDS = CHUNKS // 2
ROW_BLOCK = 512
POST_BLOCK = 256


def _params(n_axes, arbitrary=False):
    sem = ("arbitrary" if arbitrary else "parallel",) * n_axes
    return pltpu.CompilerParams(dimension_semantics=sem, vmem_limit_bytes=VMEM_LIMIT)


def _mod_row(i, tile):
    start = i * tile
    return jnp.where(start < N_CTX, 0, 1 + (start - N_CTX) // LAT_T)


def _full(shape):
    nd = len(shape)
    return pl.BlockSpec(shape, lambda *_: (0,) * nd)


def _pair_specs(tile, width, lat_row0):
    nct = N_CTX // tile
    off = lat_row0 // tile
    return [pl.BlockSpec((tile, width), lambda i: (jnp.minimum(i, nct - 1), 0)),
            pl.BlockSpec((tile, width), lambda i: (jnp.maximum(i - nct, 0) + off, 0))]


def _pair_load(tile, ctx_ref, lat_ref):
    return jnp.where(pl.program_id(0) < N_CTX // tile, ctx_ref[...], lat_ref[...])


ADA_TN = 1536


def _ada_body(cond_ref, w_ref, b_ref, o_ref):
    c = cond_ref[...]
    a = c * jax.nn.sigmoid(c)
    a_hi = a.astype(BF16)
    a_lo = (a - a_hi.astype(F32)).astype(BF16)
    w = w_ref[0]
    w_hi = w.astype(BF16)
    w_lo = (w - w_hi.astype(F32)).astype(BF16)
    acc = jnp.dot(a_hi, w_hi, preferred_element_type=F32)
    acc += jnp.dot(a_lo, w_hi, preferred_element_type=F32)
    acc += jnp.dot(a_hi, w_lo, preferred_element_type=F32)
    o_ref[0] = acc + b_ref[0]


def _ada(cond, w_ada, b_ada):
    n = 6 * D_MODEL
    return pl.pallas_call(
        _ada_body,
        grid=(DEPTH, n // ADA_TN),
        in_specs=[
            _full((SUBLANES, D_MODEL)),
            pl.BlockSpec((1, D_MODEL, ADA_TN), lambda l, j: (l, 0, j)),
            pl.BlockSpec((1, 1, ADA_TN), lambda l, j: (l, 0, j)),
        ],
        out_specs=pl.BlockSpec((1, SUBLANES, ADA_TN), lambda l, j: (l, 0, j)),
        out_shape=jax.ShapeDtypeStruct((DEPTH, SUBLANES, n), F32),
        compiler_params=_params(2),
        name="ada_mod",
    )(cond, w_ada, b_ada.reshape(DEPTH, 1, n))


D_IN = Q_RANK + KV_RANK + ROPE + 4 * MIXW
IN_EXT = Q_RANK + KV_RANK + 4 * MIXW + LANES
UQ_EXT = HEADS * NOPE + HEADS * LANES


def _prep_win_body(wt_ref, o_ref):
    kr0 = Q_RANK + KV_RANK
    head = wt_ref[0, :kr0, :].T
    mix = wt_ref[0, kr0 + ROPE:, :].T
    kr_rows = jnp.concatenate([wt_ref[0, kr0:kr0 + ROPE, :],
                               jnp.zeros((LANES - ROPE, D_MODEL), F32)], axis=0)
    kr = kr_rows.T[:, :ROPE]
    half = ROPE // 2
    rot = jnp.concatenate([-kr[:, half:], kr[:, :half]], axis=-1)
    o_ref[0] = jnp.concatenate([head, mix, kr, rot], axis=-1).astype(BF16)


def _rope_block(blk, cos_a, sin_a):
    return blk * cos_a + pltpu.roll(blk, ROPE, axis=1) * sin_a


def _prep_win(w_in):
    lyr = lambda l: (l, 0, 0)
    return pl.pallas_call(
        _prep_win_body,
        grid=(DEPTH,),
        in_specs=[pl.BlockSpec((1, D_IN, D_MODEL), lyr)],
        out_specs=pl.BlockSpec((1, D_MODEL, IN_EXT), lyr),
        out_shape=jax.ShapeDtypeStruct((DEPTH, D_MODEL, IN_EXT), BF16),
        compiler_params=_params(1),
        name="prep_w_in",
    )(jnp.swapaxes(w_in, 1, 2))


def _rms(x, g):
    return x * lax.rsqrt(jnp.mean(x * x, axis=-1, keepdims=True) + RMS_EPS) * g


MIX_COL0 = Q_RANK + KV_RANK


def _ffn_rows(ff_ref, n, tok0=0):
    return jnp.concatenate([ff_ref[pl.ds(tok0 * CHUNKS + c, n, stride=CHUNKS), :]
                            for c in range(CHUNKS)], axis=-1)


def _front_body(*refs, fused_ln):
    if fused_ln:
        (x1_ref, ff_ref, x1p_ref, x1n_ref, ffp_ref, ffn_ref, g2_ref, lg_ref, lb_ref,
         sh_ref, sc_ref, win_ref, qn_ref, kvn_ref, wuq_ref, wuk_ref, wuv_ref, cos_ref, sin_ref,
         pw_ref, ps_ref, cw_ref,
         ckv_ref, kr_ref, q_ref, k_ref, v_ref, bc_ref, x_ref) = refs

        def norm2(x1, ffn):
            return _layer_norm(ALPHA * x1 + g2_ref[0] * ffn, lg_ref[...], lb_ref[...])

    else:
        (xc_ref, xl_ref, xp_ref, xn_ref,
         sh_ref, sc_ref, win_ref, qn_ref, kvn_ref, wuq_ref, wuk_ref, wuv_ref, cos_ref, sin_ref,
         pw_ref, ps_ref, cw_ref,
         ckv_ref, kr_ref, q_ref, k_ref, v_ref, bc_ref) = refs
        is_ctx = pl.program_id(0) < N_CTX // TM

    n_sub = TM // SEQ_TILE
    subs = [slice(s * SEQ_TILE, (s + 1) * SEQ_TILE) for s in range(n_sub)]

    xs = []
    for s, rows in enumerate(subs):
        if fused_ln:
            x = norm2(x1_ref[rows, :], _ffn_rows(ff_ref, SEQ_TILE, s * SEQ_TILE))
            x_ref[rows, :] = x
        else:
            x = jnp.where(is_ctx, xc_ref[rows, :], xl_ref[rows, :])
        xs.append(x)
    if fused_ln:
        x_halo = [norm2(x1p_ref[...], _ffn_rows(ffp_ref, HALO)),
                  norm2(x1n_ref[...], _ffn_rows(ffn_ref, HALO))]
    else:
        x_halo = [xp_ref[...], xn_ref[...]]
    xs[-1] = jnp.concatenate([xs[-1]] + x_halo, axis=0)

    hs = []
    for x in xs:
        u = x * (1.0 + sc_ref[0]) + sh_ref[0]
        hs.append(jnp.dot(u.astype(BF16), win_ref[0], preferred_element_type=F32))
    mix_cols = slice(MIX_COL0, MIX_COL0 + 4 * MIXW)
    mix_halo = hs[-1][SEQ_TILE:, mix_cols]
    mixes = [h[:SEQ_TILE, mix_cols] for h in hs]

    for s, rows in enumerate(subs):
        prev = mix_halo[:HALO] if s == 0 else mixes[s - 1][SEQ_TILE - HALO:]
        nxt = mix_halo[HALO:] if s == n_sub - 1 else mixes[s + 1][:HALO]
        bc_ref[rows, :] = _seqmix_math(pl.program_id(0) * n_sub + s, mixes[s], prev, nxt,
                                       pw_ref[...], ps_ref[...], cw_ref[...])

    kr0 = MIX_COL0 + 4 * MIXW
    ckv_bs = []
    for s, rows in enumerate(subs):
        ckv = _rms(hs[s][:SEQ_TILE, Q_RANK:Q_RANK + KV_RANK], kvn_ref[...])
        ckv_ref[rows, :] = ckv
        ckv_bs.append(ckv.astype(BF16))
    for s, rows in enumerate(subs):
        k_nope = jnp.dot(ckv_bs[s], wuk_ref[...], preferred_element_type=F32)
        v_ref[rows, :] = jnp.dot(ckv_bs[s], wuv_ref[...], preferred_element_type=F32).astype(BF16)
        kr_blk = hs[s][:SEQ_TILE, kr0:kr0 + LANES]
        kr_ref[rows, :] = kr_blk
        k_rope = _rope_block(kr_blk, cos_ref[rows, :], sin_ref[rows, :])
        parts = []
        for hd in range(HEADS):
            parts += [k_nope[:, hd * NOPE:(hd + 1) * NOPE], k_rope]
        k_ref[rows, :] = jnp.concatenate(parts, axis=-1).astype(BF16)
    qns = [_rms(hs[s][:SEQ_TILE, :Q_RANK], qn_ref[...]).astype(BF16) for s in range(n_sub)]
    ra = HEADS * NOPE
    for s, rows in enumerate(subs):
        qq = jnp.dot(qns[s], wuq_ref[...], preferred_element_type=F32)
        parts = []
        for hd in range(HEADS):
            q_rope = _rope_block(qq[:, ra + hd * LANES:ra + (hd + 1) * LANES],
                                 cos_ref[rows, :], sin_ref[rows, :])
            parts += [qq[:, hd * NOPE:(hd + 1) * NOPE] * Q_SCALE, q_rope * Q_SCALE]
        q_ref[rows, :] = jnp.concatenate(parts, axis=-1).astype(BF16)


def _halo_specs(rows_per_token, width, lat_row0, n_rows):
    nct = N_CTX // TM
    nblk = n_rows // HALO

    def first_block(i):
        return (jnp.maximum(i - nct, 0) * TM + lat_row0) // HALO

    shape = (HALO * rows_per_token, width)
    return [pl.BlockSpec(shape, lambda i: (jnp.maximum(first_block(i) - 1, 0), 0)),
            pl.BlockSpec(shape, lambda i: (jnp.minimum(first_block(i) + TM // HALO, nblk - 1), 0))]


def _front(prev, mod3, layer, w_in_ext, q_norm, kv_norm, w_uq_ext, w_uk, w_uv, cos_a, sin_a,
           pool_bd, pool_scale, conv_w):
    row = lambda i: (i, 0)
    lyr = lambda i: (layer, 0, 0)
    nct = N_CTX // TM
    rope_row = lambda i: (jnp.where(i < nct, i, nct + (i - nct) % (LAT_T // TM)), 0)
    modspec = lambda k: pl.BlockSpec((1, 1, D_MODEL), lambda i: (_mod_row(i, TM), 0, k))
    fused_ln = len(prev) == 5
    outs = [
        jax.ShapeDtypeStruct((N_TOK, KV_RANK), F32),
        jax.ShapeDtypeStruct((N_TOK, LANES), F32),
        jax.ShapeDtypeStruct((N_TOK, HEADS * HEAD_BLOCK), BF16),
        jax.ShapeDtypeStruct((N_TOK, HEADS * HEAD_BLOCK), BF16),
        jax.ShapeDtypeStruct((N_TOK, HEADS * VDIM), BF16),
        jax.ShapeDtypeStruct((N_TOK, 2 * MIXW), BF16),
    ]
    if fused_ln:
        x1, ff, mod3_prev, ln_g, ln_b = prev
        outs.append(jax.ShapeDtypeStruct((N_TOK, D_MODEL), F32))
        lead_specs = ([pl.BlockSpec((TM, D_MODEL), row), pl.BlockSpec((TM * CHUNKS, LANES), row)]
                      + _halo_specs(1, D_MODEL, N_CTX, N_TOK)
                      + _halo_specs(CHUNKS, LANES, N_CTX, N_TOK)
                      + [pl.BlockSpec((1, 1, D_MODEL), lambda i: (_mod_row(i, TM), 0, 5)),
                         _full((1, D_MODEL)), _full((1, D_MODEL))])
        lead_args = (x1, ff, x1, x1, ff, ff, mod3_prev, ln_g, ln_b)
    else:
        xc, xl = prev
        lead_specs = _pair_specs(TM, D_MODEL, 0) + _halo_specs(1, D_MODEL, 0, N_LAT)
        lead_args = (xc, xl, xl, xl)
    return pl.pallas_call(
        functools.partial(_front_body, fused_ln=fused_ln),
        grid=(N_TOK // TM,),
        in_specs=lead_specs + [
            modspec(0), modspec(1),
            pl.BlockSpec((1, D_MODEL, IN_EXT), lyr),
            _full((1, Q_RANK)), _full((1, KV_RANK)),
            _full((Q_RANK, UQ_EXT)),
            _full((KV_RANK, HEADS * NOPE)), _full((KV_RANK, HEADS * VDIM)),
            pl.BlockSpec((TM, LANES), rope_row), pl.BlockSpec((TM, LANES), rope_row),
            _full((MIXW, MIXW)), _full((1, MIXW)), _full((3, MIXW)),
        ],
        out_specs=tuple(pl.BlockSpec((TM, s.shape[1]), row) for s in outs),
        out_shape=tuple(outs),
        compiler_params=_params(1),
        name="front_ln" if fused_ln else "front",
    )(*lead_args, mod3, mod3, w_in_ext, q_norm, kv_norm, w_uq_ext, w_uk, w_uv, cos_a, sin_a,
      pool_bd, pool_scale, conv_w)


def _cachekv_body(ckv_ref, kr_ref, wuk_ref, wuv_ref, k_ref, v_ref):
    c = ckv_ref[0, 0].astype(BF16)
    k_nope = jnp.dot(c, wuk_ref[0], preferred_element_type=F32)
    v_ref[0, 0] = jnp.dot(c, wuv_ref[0], preferred_element_type=F32).astype(BF16)
    kr = kr_ref[0, 0]
    parts = []
    for hd in range(HEADS):
        parts += [k_nope[:, hd * NOPE:(hd + 1) * NOPE], kr]
    k_ref[0, 0] = jnp.concatenate(parts, axis=-1).astype(BF16)


def _cachekv(cache_ckv, kr_pad, w_uk, w_uv):
    bl = lambda b, l: (b, l, 0, 0)
    wl = lambda b, l: (l, 0, 0)
    return pl.pallas_call(
        _cachekv_body,
        grid=(LAT_B, DEPTH),
        in_specs=[
            pl.BlockSpec((1, 1, PAST_LEN, KV_RANK), bl),
            pl.BlockSpec((1, 1, PAST_LEN, LANES), bl),
            pl.BlockSpec((1, KV_RANK, HEADS * NOPE), wl),
            pl.BlockSpec((1, KV_RANK, HEADS * VDIM), wl),
        ],
        out_specs=(pl.BlockSpec((1, 1, PAST_LEN, HEADS * HEAD_BLOCK), bl),
                   pl.BlockSpec((1, 1, PAST_LEN, HEADS * VDIM), bl)),
        out_shape=(jax.ShapeDtypeStruct((LAT_B, DEPTH, PAST_LEN, HEADS * HEAD_BLOCK), BF16),
                   jax.ShapeDtypeStruct((LAT_B, DEPTH, PAST_LEN, HEADS * VDIM), BF16)),
        compiler_params=_params(2),
        name="cache_kv",
    )(cache_ckv, kr_pad, w_uk, w_uv)


EXT = SEQ_TILE + 2 * HALO


def _shift_up(x, k):
    return pltpu.roll(x, x.shape[0] - k, axis=0)


def _shift_down(x, k):
    return pltpu.roll(x, k, axis=0)


def _seqmix_math(j, main, prev, nxt, pw, ps, cw):
    n_ctx_tiles = N_CTX // SEQ_TILE
    tiles_per_lat = LAT_T // SEQ_TILE
    is_ctx = j < n_ctx_tiles
    jj = jnp.where(is_ctx, 0, (j - n_ctx_tiles) % tiles_per_lat)
    first = jj == 0
    last = jnp.where(is_ctx, True, jj == tiles_per_lat - 1)
    t_seq = jnp.where(is_ctx, CTX_T, LAT_T)

    prev = jnp.where(first, 0.0, prev)
    nxt = jnp.where(last, 0.0, nxt)
    ext = jnp.concatenate([prev, main, nxt], axis=0)
    p = ext[:, 0:MIXW]
    g_b = main[:, MIXW:2 * MIXW]
    g_c = ext[:, 2 * MIXW:3 * MIXW]
    h_in = ext[:, 3 * MIXW:4 * MIXW]

    sums = {}
    b = p
    for w in POOL_WINDOWS:
        b = b + _shift_up(b, w // 2)
        sums[w] = _shift_down(b, w // 2)[HALO:HALO + SEQ_TILE]
    tpos = jj * SEQ_TILE + lax.broadcasted_iota(I32, (SEQ_TILE, 1), 0)
    lane = lax.broadcasted_iota(I32, (1, MIXW), 1)
    num = None
    den = None
    for g, w in enumerate(POOL_WINDOWS):
        lo = jnp.maximum(tpos - w // 2, 0)
        hi = jnp.minimum(tpos + w - w // 2, t_seq)
        cnt = (hi - lo).astype(F32)
        if num is None:
            num, den = sums[w], jnp.broadcast_to(cnt, (SEQ_TILE, MIXW))
        else:
            sel = lane >= g * POOL_GROUP
            num = jnp.where(sel, sums[w], num)
            den = jnp.where(sel, cnt, den)
    pooled = num / den - p[HALO:HALO + SEQ_TILE]
    b_out = jnp.dot(pooled.astype(BF16), pw, preferred_element_type=F32) * ps

    y = g_c * h_in
    conv = (_shift_down(y, 1) * cw[0:1] + y * cw[1:2] + _shift_up(y, 1) * cw[2:3])
    c_out = g_b * conv[HALO:HALO + SEQ_TILE]
    return jnp.concatenate([b_out, c_out], axis=-1).astype(BF16)


_NT = (((1,), (1,)), ((), ()))


def _v_ext(v):
    return jnp.concatenate([v, jnp.ones(v.shape, v.dtype)], axis=-1)


CTX_PER_STEP = 2


def _attn_ctx_body(q_ref, k_ref, v_ref, o_ref):
    chains = [(slice(r * CTX_T, (r + 1) * CTX_T), hd)
              for r in range(CTX_PER_STEP) for hd in range(HEADS)]
    scores = []
    for rows, hd in chains:
        cols = slice(hd * HEAD_BLOCK, (hd + 1) * HEAD_BLOCK)
        scores.append(lax.dot_general(q_ref[rows, cols], k_ref[rows, cols], _NT,
                                      preferred_element_type=F32))
    probs = [jnp.exp2(s - jnp.max(s, axis=-1, keepdims=True)).astype(BF16) for s in scores]
    for (rows, hd), p in zip(chains, probs):
        vcols = slice(hd * VDIM, (hd + 1) * VDIM)
        acc = jnp.dot(p, _v_ext(v_ref[rows, vcols]), preferred_element_type=F32)
        o_ref[rows, vcols] = (acc[:, :VDIM] / acc[:, VDIM:]).astype(BF16)


def _attn_ctx(q, k, v):
    row = lambda b: (b, 0)
    tile = CTX_PER_STEP * CTX_T
    return pl.pallas_call(
        _attn_ctx_body,
        grid=(CTX_B // CTX_PER_STEP,),
        in_specs=[pl.BlockSpec((tile, HEADS * HEAD_BLOCK), row),
                  pl.BlockSpec((tile, HEADS * HEAD_BLOCK), row),
                  pl.BlockSpec((tile, HEADS * VDIM), row)],
        out_specs=pl.BlockSpec((tile, HEADS * VDIM), row),
        out_shape=jax.ShapeDtypeStruct((N_CTX, HEADS * VDIM), BF16),
        compiler_params=_params(1),
        name="attn_ctx",
    )(q, k, v)


def _attn_lat_body(q_ref, kc_ref, vc_ref, ko_ref, vo_ref, o_ref, m_ref, acc_ref):
    def update(hd, k, v, first):
        reps = k.shape[0] // LANES
        qh = q_ref[:, hd * HEAD_BLOCK:(hd + 1) * HEAD_BLOCK]
        s = lax.dot_general(qh, k, _NT, preferred_element_type=F32)
        mx = jnp.max(s, axis=-1, keepdims=True)
        if first:
            m_new = jnp.broadcast_to(mx, (TQ, LANES))
        else:
            m_old = m_ref[hd]
            m_new = jnp.maximum(m_old, mx)
        p = jnp.exp2(s - jnp.concatenate([m_new] * reps, axis=-1))
        pv = jnp.dot(p.astype(BF16), _v_ext(v), preferred_element_type=F32)
        if first:
            acc_ref[hd] = pv
        else:
            a = jnp.exp2(m_old - m_new)
            acc_ref[hd] = acc_ref[hd] * jnp.concatenate([a, a], axis=-1) + pv
        m_ref[hd] = m_new

    def head_slices(hd):
        return slice(hd * HEAD_BLOCK, (hd + 1) * HEAD_BLOCK), slice(hd * VDIM, (hd + 1) * VDIM)

    for hd in range(HEADS):
        ks, vs = head_slices(hd)
        update(hd, kc_ref[0, 0, :, ks], vc_ref[0, 0, :, vs], True)

    def body(c, _):
        for j in range(CHUNK_UNROLL):
            rows = pl.ds(pl.multiple_of((c * CHUNK_UNROLL + j) * KV_CHUNK, KV_CHUNK), KV_CHUNK)
            for hd in range(HEADS):
                ks, vs = head_slices(hd)
                update(hd, ko_ref[rows, ks], vo_ref[rows, vs], False)
        return 0

    lax.fori_loop(0, LAT_T // (KV_CHUNK * CHUNK_UNROLL), body, 0)
    outs = [acc_ref[hd][:, :VDIM] / acc_ref[hd][:, VDIM:] for hd in range(HEADS)]
    o_ref[...] = jnp.concatenate(outs, axis=-1).astype(BF16)


def _attn_lat(q, k, v, kc, vc, layer):
    qt = LAT_T // TQ
    ctx_tiles = N_CTX // TQ
    ctx_blocks = N_CTX // LAT_T
    return pl.pallas_call(
        _attn_lat_body,
        grid=(LAT_B, qt),
        in_specs=[
            pl.BlockSpec((TQ, HEADS * HEAD_BLOCK), lambda b, i: (ctx_tiles + b * qt + i, 0)),
            pl.BlockSpec((1, 1, PAST_LEN, HEADS * HEAD_BLOCK), lambda b, i: (b, layer, 0, 0)),
            pl.BlockSpec((1, 1, PAST_LEN, HEADS * VDIM), lambda b, i: (b, layer, 0, 0)),
            pl.BlockSpec((LAT_T, HEADS * HEAD_BLOCK), lambda b, i: (ctx_blocks + b, 0)),
            pl.BlockSpec((LAT_T, HEADS * VDIM), lambda b, i: (ctx_blocks + b, 0)),
        ],
        out_specs=pl.BlockSpec((TQ, HEADS * VDIM), lambda b, i: (b * qt + i, 0)),
        out_shape=jax.ShapeDtypeStruct((N_LAT, HEADS * VDIM), BF16),
        scratch_shapes=[pltpu.VMEM((HEADS, TQ, LANES), F32),
                        pltpu.VMEM((HEADS, TQ, 2 * VDIM), F32)],
        compiler_params=_params(2),
        name="attn_lat",
    )(q, kc, vc, k, v)


def _layer_norm(y, g, b):
    mu = jnp.mean(y, axis=-1, keepdims=True)
    d = y - mu
    var = jnp.mean(d * d, axis=-1, keepdims=True)
    return d * lax.rsqrt(var + LN_EPS) * g + b


def _postmix_body(actx_ref, alat_ref, bc_ref, xc_ref, xl_ref, g1_ref, sh2_ref, sc2_ref, wa_ref, wbc_ref,
                  lg_ref, lb_ref, wr_ref, x1_ref, u2p_ref, aff_ref):
    is_ctx = pl.program_id(0) < N_CTX // TM
    lane = lax.broadcasted_iota(I32, (1, LANES), 1)
    blocks = [slice(b * POST_BLOCK, (b + 1) * POST_BLOCK) for b in range(TM // POST_BLOCK)]
    mixes = []
    for rows in blocks:
        a = jnp.where(is_ctx, actx_ref[rows, :], alat_ref[rows, :])
        mix = jnp.dot(a, wa_ref[0], preferred_element_type=F32)
        mixes.append(mix + jnp.dot(bc_ref[rows, :], wbc_ref[0], preferred_element_type=F32))
    u2s = []
    for b, rows in enumerate(blocks):
        x = jnp.where(is_ctx, xc_ref[rows, :], xl_ref[rows, :])
        x1 = _layer_norm(ALPHA * x + g1_ref[0] * mixes[b], lg_ref[...], lb_ref[...])
        x1_ref[rows, :] = x1
        u2 = x1 * (1.0 + sc2_ref[0]) + sh2_ref[0]
        for c in range(WORDS):
            lo = u2[:, c * LANES:(c + 1) * LANES]
            hi = u2[:, (c + WORDS) * LANES:(c + WORDS + 1) * LANES]
            u2p_ref[pl.ds(b * POST_BLOCK * WORDS + c, POST_BLOCK, stride=WORDS), :] = (
                pltpu.pack_elementwise([lo, hi], packed_dtype=BF16))
        u2s.append(u2.astype(BF16))
    for b, rows in enumerate(blocks):
        logits = jnp.dot(u2s[b], wr_ref[...], preferred_element_type=F32)
        logits = jnp.where(lane < N_EXPERTS, logits, -jnp.inf)
        m = jnp.max(logits, axis=-1, keepdims=True)
        e = jnp.exp(logits - m)
        aff_ref[rows, :] = e / jnp.sum(e, axis=-1, keepdims=True)


def _postmix(a_ctx, a_lat, bc, xc, xl, lat_row0, mod3, layer, w_out_b, ln_g, ln_b, w_router_pad):
    row = lambda i: (i, 0)
    modspec = lambda k: pl.BlockSpec((1, 1, D_MODEL), lambda i: (_mod_row(i, TM), 0, k))
    return pl.pallas_call(
        _postmix_body,
        grid=(N_TOK // TM,),
        in_specs=_pair_specs(TM, HEADS * VDIM, 0) + [
            pl.BlockSpec((TM, 2 * MIXW), row),
        ] + _pair_specs(TM, D_MODEL, lat_row0) + [
            modspec(2), modspec(3), modspec(4),
            pl.BlockSpec((1, HEADS * VDIM, D_MODEL), lambda i: (layer, 0, 0)),
            pl.BlockSpec((1, 2 * MIXW, D_MODEL), lambda i: (layer, 1, 0)),
            _full((1, D_MODEL)), _full((1, D_MODEL)),
            _full((D_MODEL, LANES)),
        ],
        out_specs=(pl.BlockSpec((TM, D_MODEL), row),
                   pl.BlockSpec((TM * WORDS, LANES), row),
                   pl.BlockSpec((TM, LANES), row)),
        out_shape=(jax.ShapeDtypeStruct((N_TOK, D_MODEL), F32),
                   jax.ShapeDtypeStruct((N_TOK * WORDS, LANES), jnp.uint32),
                   jax.ShapeDtypeStruct((N_TOK, LANES), F32)),
        compiler_params=_params(1),
        name="postmix",
    )(a_ctx, a_lat, bc, xc, xl, mod3, mod3, mod3, w_out_b, w_out_b, ln_g, ln_b, w_router_pad)


PREFIX_CHUNK = 256


def _route_ctx_body(aff_ref, idx_ref):
    n_rows = CTX_B * N_EXPERTS
    dense = jnp.concatenate(
        [aff_ref[b * CTX_T:(b + 1) * CTX_T, :].T[:N_EXPERTS] for b in range(CTX_B)], axis=0)

    def search(i, thr):
        cand = thr | jnp.left_shift(jnp.int32(1), 30 - i)
        n = jnp.sum((dense >= pltpu.bitcast(cand, F32)).astype(I32), axis=1, keepdims=True)
        return jnp.where(n >= CTX_CAP, cand, thr)

    thr_bits = lax.fori_loop(0, 31, search, jnp.zeros((n_rows, 1), I32))
    thr = pltpu.bitcast(thr_bits, F32)
    above = pltpu.bitcast(thr_bits + 1, F32)
    gt = (dense > thr) & (dense >= above)
    eq = (dense >= thr) & jnp.logical_not(gt)
    n_gt = jnp.sum(gt.astype(I32), axis=1, keepdims=True)
    need = (CTX_CAP - n_gt).astype(F32)

    ri = lax.broadcasted_iota(I32, (CTX_T, CTX_T), 0)
    ci = lax.broadcasted_iota(I32, (CTX_T, CTX_T), 1)
    tri = jnp.where(ri <= ci, 1.0, 0.0).astype(BF16)
    p_eq = jnp.dot(jnp.where(eq, 1.0, 0.0).astype(BF16), tri, preferred_element_type=F32)
    sel = jnp.where(gt, 1.0, jnp.where(eq & (p_eq <= need), 1.0, 0.0))
    cnt = jnp.dot(sel.astype(BF16), tri, preferred_element_type=F32)

    ones = jnp.ones((CTX_T, LANES), BF16)
    lane = lax.broadcasted_iota(I32, (1, LANES), 1)
    out = jnp.zeros((n_rows, LANES), F32)
    for r in range(CTX_CAP):
        le = jnp.where(cnt <= float(r), 1.0, 0.0).astype(BF16)
        out = jnp.where(lane == r, jnp.dot(le, ones, preferred_element_type=F32), out)
    idx_ref[...] = out.astype(I32)


def _route_ctx(aff):
    idx = pl.pallas_call(
        _route_ctx_body,
        grid=(1,),
        in_specs=[pl.BlockSpec((N_CTX, LANES), lambda i: (0, 0))],
        out_specs=_full((CTX_B * N_EXPERTS, LANES)),
        out_shape=jax.ShapeDtypeStruct((CTX_B * N_EXPERTS, LANES), I32),
        compiler_params=_params(1),
        name="route_ctx",
    )(aff)
    return idx[:, :CTX_CAP].reshape(CTX_B, N_EXPERTS, CTX_CAP)


RANK_TILE = LANES


def _expert_row(col):
    full = jnp.concatenate([jnp.broadcast_to(col, (N_EXPERTS, LANES)),
                            jnp.zeros((LANES - N_EXPERTS, LANES), col.dtype)], axis=0)
    return full.T[0:1]


def _route_count_body(aff_ref, cnt_ref, cend_ref, *, seq, cap):
    n_chunks = seq // PREFIX_CHUNK
    dense = aff_ref[...].T[:N_EXPERTS]

    def search(i, thr):
        cand = thr | jnp.left_shift(jnp.int32(1), 30 - i)
        n = jnp.sum((dense >= pltpu.bitcast(cand, F32)).astype(I32), axis=1, keepdims=True)
        return jnp.where(n >= cap, cand, thr)

    thr_bits = lax.fori_loop(0, 31, search, jnp.zeros((N_EXPERTS, 1), I32))
    thr_col = pltpu.bitcast(thr_bits, F32)
    above_col = pltpu.bitcast(thr_bits + 1, F32)
    n_gt = jnp.sum(((dense > thr_col) & (dense >= above_col)).astype(I32), axis=1, keepdims=True)
    thr = _expert_row(thr_col)
    above = _expert_row(above_col)
    need = _expert_row((cap - n_gt).astype(F32))

    ri = lax.broadcasted_iota(I32, (PREFIX_CHUNK, PREFIX_CHUNK), 0)
    ci = lax.broadcasted_iota(I32, (PREFIX_CHUNK, PREFIX_CHUNK), 1)
    tri = jnp.where(ci <= ri, 1.0, 0.0).astype(BF16)
    carry_eq = jnp.zeros((1, LANES), F32)
    carry_sel = jnp.zeros((1, LANES), F32)
    per = PREFIX_CHUNK // RANK_TILE
    for c in range(n_chunks):
        rows = slice(c * PREFIX_CHUNK, (c + 1) * PREFIX_CHUNK)
        a = aff_ref[rows, :]
        gt = (a > thr) & (a >= above)
        eq = (a >= thr) & jnp.logical_not(gt)
        p_eq = jnp.dot(tri, jnp.where(eq, 1.0, 0.0).astype(BF16),
                       preferred_element_type=F32) + carry_eq
        sel = jnp.where(gt, 1.0, jnp.where(eq & (p_eq <= need), 1.0, 0.0))
        p_sel = jnp.dot(tri, sel.astype(BF16), preferred_element_type=F32) + carry_sel
        dense_cnt = p_sel.T[:N_EXPERTS]
        for k in range(per):
            last = (k + 1) * RANK_TILE - 1
            cnt_ref[c * per + k] = dense_cnt[:, k * RANK_TILE:(k + 1) * RANK_TILE]
            cend_ref[0, c * per + k:c * per + k + 1, :] = p_sel[last:last + 1, :].astype(I32)
        carry_eq = p_eq[PREFIX_CHUNK - 1:PREFIX_CHUNK, :]
        carry_sel = p_sel[PREFIX_CHUNK - 1:PREFIX_CHUNK, :]


def _route_count(aff, first_block, n_req, seq, cap):
    n_tiles = seq // RANK_TILE
    return pl.pallas_call(
        functools.partial(_route_count_body, seq=seq, cap=cap),
        grid=(n_req,),
        in_specs=[pl.BlockSpec((seq, LANES), lambda b: (first_block + b, 0))],
        out_specs=(pl.BlockSpec((n_tiles, N_EXPERTS, RANK_TILE), lambda b: (b, 0, 0)),
                   pl.BlockSpec((1, n_tiles, LANES), lambda b: (b, 0, 0))),
        out_shape=(jax.ShapeDtypeStruct((n_req * n_tiles, N_EXPERTS, RANK_TILE), F32),
                   jax.ShapeDtypeStruct((n_req, n_tiles, LANES), I32)),
        compiler_params=_params(1),
        name=f"route_count_{seq}",
    )(aff)


def _route_rank_body(cend_ref, cnt_ref, idx_ref, *bufs, seq, cap):
    b = pl.program_id(0)
    n_tiles = seq // RANK_TILE
    rank = lax.broadcasted_iota(I32, (RANK_TILE, RANK_TILE), 0).astype(F32)
    ones = jnp.ones((RANK_TILE, LANES), BF16)

    def tile(j, _):
        counts = cnt_ref[j]
        for e in range(N_EXPERTS):
            prev = (b * N_EXPERTS + e) * n_tiles + j - 1
            start = jnp.where(j > 0, cend_ref[jnp.maximum(prev, 0)], 0)
            local = counts[e:e + 1, :] - jnp.asarray(start, F32)
            le = jnp.where(local <= rank, 1.0, 0.0).astype(BF16)
            pos = jnp.dot(le, ones, preferred_element_type=F32) + jnp.asarray(j * RANK_TILE, F32)
            bufs[e][pl.ds(start, RANK_TILE), :] = pos
        return 0

    lax.fori_loop(0, n_tiles, tile, 0)
    lane = lax.broadcasted_iota(I32, (1, LANES), 1)
    out = jnp.zeros((cap, LANES), F32)
    for e in range(N_EXPERTS):
        out = jnp.where(lane == e, bufs[e][0:cap, :], out)
    idx_ref[0] = out.astype(I32)


def _route_rank(cend_flat, cnt, n_req, seq, cap):
    n_tiles = seq // RANK_TILE
    grid_spec = pltpu.PrefetchScalarGridSpec(
        num_scalar_prefetch=1,
        grid=(n_req,),
        in_specs=[pl.BlockSpec((n_tiles, N_EXPERTS, RANK_TILE), lambda b, c: (b, 0, 0))],
        out_specs=pl.BlockSpec((1, cap, LANES), lambda b, c: (b, 0, 0)),
        scratch_shapes=[pltpu.VMEM((cap + RANK_TILE, LANES), F32)] * N_EXPERTS,
    )
    return pl.pallas_call(
        functools.partial(_route_rank_body, seq=seq, cap=cap),
        grid_spec=grid_spec,
        out_shape=jax.ShapeDtypeStruct((n_req, cap, LANES), I32),
        compiler_params=_params(1),
        name=f"route_rank_{seq}",
    )(cend_flat, cnt)


def _gather_rows(idx_ref, base, u2p_ref, aff_ref, tile_ref, gate_ref, r0, n):
    for i in range(n):
        r = r0 + i
        t = idx_ref[base + r]
        slab = u2p_ref[pl.ds(pl.multiple_of(t * WORDS, WORDS), WORDS), :]
        tile_ref[pl.ds(r, WORDS, stride=TILE_PITCH), :] = slab
        gate_ref[pl.ds(r, 1), :] = aff_ref[pl.ds(t, 1), :]


def _scatter_rows(idx_ref, base, y_ref, acc_ref, r0, n):
    for g0 in range(0, n, SCATTER_UNROLL):
        rows = [r0 + g0 + i for i in range(SCATTER_UNROLL)]
        dst = [pl.ds(pl.multiple_of(idx_ref[base + r] * CHUNKS, CHUNKS), CHUNKS) for r in rows]
        vals = [acc_ref[d, :] + y_ref[pl.ds(pl.multiple_of(r * CHUNKS, CHUNKS), CHUNKS), :]
                for d, r in zip(dst, rows)]
        for d, v in zip(dst, vals):
            acc_ref[d, :] = v


def _expert_rows(e, tile_ref, gate_ref, wg, wu, wd, y_ref, r0, n):
    halves = [[], []]
    for c in range(WORDS):
        w = tile_ref[pl.ds(c * TILE_PITCH + r0, n), :]
        for k in range(2):
            halves[k].append(pltpu.unpack_elementwise(w, index=k, packed_dtype=BF16,
                                                      unpacked_dtype=F32))
    x = jnp.concatenate(halves[0] + halves[1], axis=-1).astype(BF16)
    lane = lax.broadcasted_iota(I32, (1, LANES), 1)
    gate = jnp.sum(jnp.where(lane == e, gate_ref[pl.ds(r0, n), :], 0.0), axis=-1, keepdims=True)
    hg = jnp.dot(x, wg, preferred_element_type=F32)
    hu = jnp.dot(x, wu, preferred_element_type=F32)
    hidden = (hg * jax.nn.sigmoid(hg) * hu).astype(BF16)
    y = jnp.dot(hidden, wd, preferred_element_type=F32) * gate
    for c in range(CHUNKS):
        y_ref[pl.ds(r0 * CHUNKS + c, n, stride=CHUNKS), :] = y[:, c * LANES:(c + 1) * LANES]


def _moe_body(idx_ref, u2p_ref, aff_ref, wg_ref, wu_ref, wd_ref, acc_ref,
              tile_a, tile_b, gate_a, gate_b, y_a, y_b):
    p = pl.program_id(0)
    e = pl.program_id(1)
    step = p * N_EXPERTS + e
    base = step * ROWS
    base_next = jnp.minimum(step + 1, N_PASS * N_EXPERTS - 1) * ROWS
    base_prev = jnp.where(e == 0, base, base - ROWS)

    @pl.when(e == 0)
    def _():
        acc_ref[...] = jnp.zeros_like(acc_ref)
        y_b[...] = jnp.zeros_like(y_b)

        def gather(c, _):
            _gather_rows(idx_ref, base, u2p_ref, aff_ref, tile_a, gate_a, c * SUBLANES, SUBLANES)
            return 0

        lax.fori_loop(0, ROWS // SUBLANES, gather, 0)

    def run(tile_cur, gate_cur, y_cur, tile_nxt, gate_nxt, y_prv):
        wg = wg_ref[0, 0].astype(BF16)
        wu = wu_ref[0, 0].astype(BF16)
        wd = wd_ref[0, 0].astype(BF16)
        for blk in range(ROWS // ROW_BLOCK):
            r0 = blk * ROW_BLOCK
            _gather_rows(idx_ref, base_next, u2p_ref, aff_ref, tile_nxt, gate_nxt, r0, ROW_BLOCK)
            _expert_rows(e, tile_cur, gate_cur, wg, wu, wd, y_cur, r0, ROW_BLOCK)
            _scatter_rows(idx_ref, base_prev, y_prv, acc_ref, r0, ROW_BLOCK)

    @pl.when(e % 2 == 0)
    def _():
        run(tile_a, gate_a, y_a, tile_b, gate_b, y_b)

    @pl.when(e % 2 == 1)
    def _():
        run(tile_b, gate_b, y_b, tile_a, gate_a, y_a)

    @pl.when(e == N_EXPERTS - 1)
    def _():
        def scatter(c, _):
            _scatter_rows(idx_ref, base, y_b, acc_ref, c * SCATTER_UNROLL, SCATTER_UNROLL)
            return 0

        lax.fori_loop(0, ROWS // SCATTER_UNROLL, scatter, 0)


_MOE_SCRATCH = [
    pltpu.VMEM((WORDS * TILE_PITCH, LANES), jnp.uint32),
    pltpu.VMEM((WORDS * TILE_PITCH, LANES), jnp.uint32),
    pltpu.VMEM((ROWS, LANES), F32),
    pltpu.VMEM((ROWS, LANES), F32),
    pltpu.VMEM((ROWS * CHUNKS, LANES), F32),
    pltpu.VMEM((ROWS * CHUNKS, LANES), F32),
]


def _moe(idx_flat, u2p, aff, w_gate, w_up, w_down, layer):
    one = pl.Buffered(1)
    grid_spec = pltpu.PrefetchScalarGridSpec(
        num_scalar_prefetch=1,
        grid=(N_PASS, N_EXPERTS),
        in_specs=[
            pl.BlockSpec((PASS_TOK * WORDS, LANES), lambda p, e, idx: (p, 0)),
            pl.BlockSpec((PASS_TOK, LANES), lambda p, e, idx: (p, 0), pipeline_mode=one),
            pl.BlockSpec((1, 1, D_MODEL, EXPERT_DIM), lambda p, e, idx: (layer, e, 0, 0)),
            pl.BlockSpec((1, 1, D_MODEL, EXPERT_DIM), lambda p, e, idx: (layer, e, 0, 0)),
            pl.BlockSpec((1, 1, EXPERT_DIM, D_MODEL), lambda p, e, idx: (layer, e, 0, 0)),
        ],
        out_specs=pl.BlockSpec((PASS_TOK * CHUNKS, LANES), lambda p, e, idx: (p, 0),
                               pipeline_mode=one),
        scratch_shapes=_MOE_SCRATCH,
    )
    return pl.pallas_call(
        _moe_body,
        grid_spec=grid_spec,
        out_shape=jax.ShapeDtypeStruct((N_TOK * CHUNKS, LANES), F32),
        compiler_params=_params(2, arbitrary=True),
        name="moe",
    )(idx_flat, u2p, aff, w_gate, w_up, w_down)


def _final_body(x1_ref, ff_ref, g2_ref, lg_ref, lb_ref, o_ref):
    ffn = jnp.concatenate([ff_ref[pl.ds(c, TM, stride=CHUNKS), :] for c in range(CHUNKS)], axis=-1)
    o_ref[...] = _layer_norm(ALPHA * x1_ref[...] + g2_ref[0] * ffn, lg_ref[...], lb_ref[...])


def _final(x1, ff, mod3, ln_g, ln_b, row0=0, n_rows=N_TOK):
    t0 = row0 // TM
    row = lambda i: (i + t0, 0)
    return pl.pallas_call(
        _final_body,
        grid=(n_rows // TM,),
        in_specs=[
            pl.BlockSpec((TM, D_MODEL), row),
            pl.BlockSpec((TM * CHUNKS, LANES), row),
            pl.BlockSpec((1, 1, D_MODEL), lambda i: (_mod_row(i + t0, TM), 0, 5)),
            _full((1, D_MODEL)), _full((1, D_MODEL)),
        ],
        out_specs=pl.BlockSpec((TM, D_MODEL), lambda i: (i, 0)),
        out_shape=jax.ShapeDtypeStruct((n_rows, D_MODEL), F32),
        compiler_params=_params(1),
        name="final_ln",
    )(x1, ff, mod3, ln_g, ln_b)


def _rot_cols(w):
    half = ROPE // 2
    return jnp.concatenate([-w[..., half:], w[..., :half]], axis=-1)


def _pad_lanes(w):
    pad = [(0, 0)] * (w.ndim - 1) + [(0, LANES - w.shape[-1])]
    return jnp.pad(w, pad)


def _rope_tables():
    rows_n = LAT_T // GRID_W
    r, cl = jnp.meshgrid(jnp.arange(rows_n, dtype=F32), jnp.arange(GRID_W, dtype=F32), indexing="ij")
    inv = ROPE_THETA ** (-jnp.arange(0, ROPE // 2, 2, dtype=F32) / (ROPE // 2))
    ang = jnp.concatenate([r.reshape(-1)[:, None] * inv, cl.reshape(-1)[:, None] * inv], axis=-1)
    cos, sin = jnp.cos(ang), jnp.sin(ang)
    cos_lat = _pad_lanes(jnp.concatenate([cos, cos], axis=-1))
    sin_lat = _pad_lanes(jnp.concatenate([sin, sin], axis=-1))
    cos_ctx = _pad_lanes(jnp.ones((N_CTX, ROPE), F32))
    sin_ctx = jnp.zeros((N_CTX, LANES), F32)
    cos_a = jnp.concatenate([cos_ctx, cos_lat], axis=0)
    sin_a = jnp.concatenate([sin_ctx, sin_lat], axis=0)
    return cos_a, sin_a


def kernel(x_prompt, x_sample, cache_ckv, cache_krope, c, c_ctx, w_in, q_norm, w_uq, kv_norm, w_uk, w_uv,
           pool_w, pool_scale, conv_w, w_out, w_ada, b_ada, ln1_g, ln1_b, ln2_g, ln2_b, w_router,
           w_gate, w_up, w_down):
    w_in_ext = _prep_win(w_in)
    hw = NOPE + ROPE
    uq_nope = [w_uq[:, :, h * hw:h * hw + NOPE] for h in range(HEADS)]
    uq_rope = [w_uq[:, :, h * hw + NOPE:(h + 1) * hw] for h in range(HEADS)]
    w_uq_ext = jnp.concatenate(
        uq_nope + [blk for w in uq_rope for blk in (w, _rot_cols(w))], axis=-1).astype(BF16)
    w_uk_b = w_uk.astype(BF16)
    w_uv_b = w_uv.astype(BF16)
    eye = jnp.eye(len(POOL_WINDOWS), dtype=F32)
    pool_bd = (pool_w[:, :, :, None, :] * eye[None, :, None, :, None]).reshape(DEPTH, MIXW, MIXW).astype(BF16)
    w_out_b = w_out.astype(BF16)
    w_router_pad = _pad_lanes(w_router).astype(BF16)
    cos_a, sin_a = _rope_tables()

    cond = jnp.concatenate([c_ctx[None, :], c, jnp.zeros((SUBLANES - 1 - LAT_B, D_MODEL), F32)], axis=0)
    mod = _ada(cond, w_ada, b_ada)
    kc, vc = _cachekv(cache_ckv, _pad_lanes(cache_krope), w_uk_b, w_uv_b)

    prev = (x_prompt.reshape(N_CTX, D_MODEL), x_sample.reshape(N_LAT, D_MODEL))
    xs = prev + (0,)
    ckv_layers, kr_layers = [], []
    ctx_off = (jnp.arange(CTX_B, dtype=I32) * CTX_T)[:, None, None]
    for l in range(DEPTH):
        mod3 = mod[l].reshape(SUBLANES, 1, 6 * D_MODEL)
        outs = _front(prev, mod3, l, w_in_ext, q_norm[l][None], kv_norm[l][None],
                      w_uq_ext[l], w_uk_b[l], w_uv_b[l], cos_a, sin_a,
                      pool_bd[l], pool_scale[l][None], conv_w[l])
        ckv, kr, q, k, v, bc = outs[:6]
        if l > 0:
            xs = (outs[6], outs[6], N_CTX)
        ckv_layers.append(ckv[:N_CTX].reshape(CTX_B, CTX_T, KV_RANK))
        kr_layers.append(kr[:N_CTX, :ROPE].reshape(CTX_B, CTX_T, ROPE))
        a_ctx = _attn_ctx(q, k, v)
        a_lat = _attn_lat(q, k, v, kc, vc, l)
        x1, u2p, aff = _postmix(a_ctx, a_lat, bc, *xs, mod3, l, w_out_b,
                                ln1_g[l][None], ln1_b[l][None], w_router_pad[l])
        idx_ctx = _route_ctx(aff)
        cnt_lat, cend = _route_count(aff, N_CTX // LAT_T, LAT_B, LAT_T, LAT_CAP)
        cend_flat = cend[:, :, :N_EXPERTS].transpose(0, 2, 1).reshape(-1)
        idx_lat = _route_rank(cend_flat, cnt_lat, LAT_B, LAT_T, LAT_CAP)
        idx_lat = idx_lat[:, :, :N_EXPERTS].transpose(0, 2, 1)
        idx_ctx = (idx_ctx + ctx_off).transpose(1, 0, 2).reshape(1, N_EXPERTS, ROWS)
        idx_flat = jnp.concatenate([idx_ctx, idx_lat], axis=0).reshape(-1)
        ff = _moe(idx_flat, u2p, aff, w_gate, w_up, w_down, l)
        if l + 1 < DEPTH:
            prev = (x1, ff, mod3, ln2_g[l][None], ln2_b[l][None])
        else:
            y_ctx = _final(x1, ff, mod3, ln2_g[l][None], ln2_b[l][None], 0, N_CTX)
            y_lat = _final(x1, ff, mod3, ln2_g[l][None], ln2_b[l][None], N_CTX, N_LAT)

    y_prompt = y_ctx.reshape(CTX_B, CTX_T, D_MODEL)
    y_sample = y_lat.reshape(LAT_B, LAT_T, D_MODEL)
    new_ckv = jnp.stack(ckv_layers, axis=1)
    new_krope = jnp.stack(kr_layers, axis=1)
    return (y_prompt, y_sample, new_ckv, new_krope)
```
